```python
import math
import jax, jax.numpy as jnp
from jax import lax
import numpy as np

D_MODEL = 1024
BATCH = 1
SEQ = 16384
DEPTH = 1
DEC_BATCH = 16
DEC_SEQ = 2048
PAST_LEN = 128

CONV_WIDTH = 512
CONV_GROUPS = 8
CONV_K = 3
DN_HEADS = 4
DN_DK = 128
DN_DV = 128
DN_QK = DN_HEADS * DN_DK
DN_V = DN_HEADS * DN_DV
DN_CONV_K = 3
DN_CHUNK = 64
MIX_WIDTH = CONV_WIDTH + DN_V
PROJ_WIDTH = 3 * CONV_WIDTH + 2 * DN_QK + DN_V + DN_V + 4 * DN_HEADS
N_GROUPS = 4
EXPERTS_PER_GROUP = 8
N_EXPERTS = N_GROUPS * EXPERTS_PER_GROUP
TOP_K = 2
EXPERT_FF = 512
MOE_BLOCK = 128
EPS = 1e-6

kernel_name = 'hymba_conv_gdn_hmoe_encoder'


def rmsnorm(x, g):
    xf = x.astype(jnp.float32)
    y = xf * lax.rsqrt(jnp.mean(xf * xf, axis=-1, keepdims=True) + EPS)
    return (y * g.astype(jnp.float32)).astype(x.dtype)


def l2norm(x):
    return x * lax.rsqrt(jnp.sum(x * x, axis=-1, keepdims=True) + EPS)


def depthwise_conv_centred(u, w):
    k, c = w.shape
    return lax.conv_general_dilated(
        u, w[:, None, :].astype(u.dtype), window_strides=(1,),
        padding=[((k - 1) // 2, k // 2)],
        dimension_numbers=('NWC', 'WIO', 'NWC'), feature_group_count=c)


def conv_mixer(bch, conv_w, norm_g):
    b_gate, c_gate, h = jnp.split(bch, 3, axis=-1)
    y = b_gate * depthwise_conv_centred(c_gate * h, conv_w)
    bn, t, _ = y.shape
    y = rmsnorm(y.reshape(bn, t, CONV_GROUPS, CONV_WIDTH // CONV_GROUPS),
                norm_g.reshape(CONV_GROUPS, CONV_WIDTH // CONV_GROUPS))
    return y.reshape(bn, t, CONV_WIDTH)


def chunk_gated_delta(q, k, v, g, beta):
    bn, t, h, dk = q.shape
    dv = v.shape[-1]
    c = DN_CHUNK
    n = t // c

    def chunks(a):
        a = a.reshape((bn, n, c, h) + a.shape[3:])
        return jnp.moveaxis(a, 3, 1)

    q, k, v, g, beta = chunks(q), chunks(k), chunks(v), chunks(g), chunks(beta)
    gc = jnp.cumsum(g, axis=-1)
    incl = jnp.tril(jnp.ones((c, c), dtype=bool))
    strict = jnp.tril(jnp.ones((c, c), dtype=bool), -1)
    decay = jnp.exp(jnp.where(incl, gc[..., :, None] - gc[..., None, :], -jnp.inf))
    kb = k * beta[..., None]
    lmat = jnp.where(strict, jnp.einsum('bhnid,bhnjd->bhnij', kb, k) * decay, 0.0)
    eye = jnp.eye(c, dtype=lmat.dtype)
    tinv = lax.linalg.triangular_solve(lmat + eye, jnp.broadcast_to(eye, lmat.shape),
                                       left_side=True, lower=True, unit_diagonal=True)
    w = jnp.einsum('bhnij,bhnjd->bhnid', tinv, kb * jnp.exp(gc)[..., None])
    u = jnp.einsum('bhnij,bhnjd->bhnid', tinv, v * beta[..., None])
    qk = jnp.einsum('bhnid,bhnjd->bhnij', q, k) * decay
    glast = gc[..., -1]
    q_dec = q * jnp.exp(gc)[..., None]
    k_dec = k * jnp.exp(glast[..., None] - gc)[..., None]

    def step(s, xs):
        qd, kd, wn, un, qkn, gl = xs
        v_new = un - jnp.einsum('bhid,bhde->bhie', wn, s)
        o = jnp.einsum('bhid,bhde->bhie', qd, s) + jnp.einsum('bhij,bhje->bhie', qkn, v_new)
        s = s * jnp.exp(gl)[..., None, None] + jnp.einsum('bhid,bhie->bhde', kd, v_new)
        return s, o

    xs = tuple(jnp.moveaxis(a, 2, 0) for a in (q_dec, k_dec, w, u, qk, glast))
    s0 = jnp.zeros((bn, h, dk, dv), jnp.float32)
    _, o = lax.scan(step, s0, xs)
    return jnp.transpose(o, (1, 0, 3, 2, 4)).reshape(bn, t, h, dv)


def deltanet_mixer(qkv, z, ab, conv_w, a_log, dt_bias, norm_g):
    bn, t, _ = qkv.shape
    out_dtype = qkv.dtype
    qkv = jax.nn.silu(depthwise_conv_centred(qkv, conv_w)).astype(jnp.float32)
    q, k, v = jnp.split(qkv, [DN_QK, 2 * DN_QK], axis=-1)
    q = l2norm(q.reshape(bn, t, DN_HEADS, DN_DK)) * (DN_DK ** -0.5)
    k = l2norm(k.reshape(bn, t, DN_HEADS, DN_DK))
    v = v.reshape(bn, t, DN_HEADS, DN_DV)
    ab = ab.astype(jnp.float32).reshape(bn, t, 4, DN_HEADS)
    a_log = a_log.astype(jnp.float32)
    dt_bias = dt_bias.astype(jnp.float32)
    g_f = -jnp.exp(a_log[0]) * jax.nn.softplus(ab[:, :, 0] + dt_bias[0])
    g_b = -jnp.exp(a_log[1]) * jax.nn.softplus(ab[:, :, 1] + dt_bias[1])
    beta_f = jax.nn.sigmoid(ab[:, :, 2])
    beta_b = jax.nn.sigmoid(ab[:, :, 3])
    o_f = chunk_gated_delta(q, k, v, g_f, beta_f)
    rev = lambda a: jnp.flip(a, axis=1)
    o_b = rev(chunk_gated_delta(rev(q), rev(k), rev(v), rev(g_b), rev(beta_b)))
    o = o_f + o_b
    o = o * lax.rsqrt(jnp.mean(o * o, axis=-1, keepdims=True) + EPS) * norm_g.astype(jnp.float32)
    o = o * jax.nn.silu(z.astype(jnp.float32).reshape(bn, t, DN_HEADS, DN_DV))
    return o.reshape(bn, t, DN_V).astype(out_dtype)


def hier_moe(x, rg_w, rg_b, re_w, re_b, w1, w3, w2):
    bn, t, d = x.shape
    xt = x.reshape(-1, d)
    n_tok = xt.shape[0]
    g_logits = (xt @ rg_w).astype(jnp.float32) + rg_b.astype(jnp.float32)
    g_prob = jax.nn.softmax(g_logits, axis=-1)
    g_w, g_idx = lax.top_k(g_prob, 1)
    e_logits = ((xt @ re_w).astype(jnp.float32) + re_b.astype(jnp.float32)).reshape(
        n_tok, N_GROUPS, EXPERTS_PER_GROUP)
    e_sel = jnp.take_along_axis(e_logits, g_idx[:, :, None], axis=1)[:, 0]
    e_prob = jax.nn.softmax(e_sel, axis=-1)
    top_p, top_i = lax.top_k(e_prob, TOP_K)
    gate = g_w * top_p / jnp.sum(top_p, axis=-1, keepdims=True)
    expert = (g_idx * EXPERTS_PER_GROUP + top_i).astype(jnp.int32)

    n_assign = n_tok * TOP_K
    flat_e = expert.reshape(-1)
    flat_tok = jnp.repeat(jnp.arange(n_tok, dtype=jnp.int32), TOP_K)
    flat_w = gate.reshape(-1)
    order = jnp.argsort(flat_e)
    se, stok, sw = flat_e[order], flat_tok[order], flat_w[order]
    counts = jnp.bincount(flat_e, length=N_EXPERTS).astype(jnp.int32)
    starts = jnp.cumsum(counts) - counts
    pcounts = (counts + (MOE_BLOCK - 1)) // MOE_BLOCK * MOE_BLOCK
    pends = jnp.cumsum(pcounts)
    pstarts = pends - pcounts
    dest = pstarts[se] + (jnp.arange(n_assign, dtype=jnp.int32) - starts[se])
    n_blk = (n_assign + N_EXPERTS * (MOE_BLOCK - 1) + MOE_BLOCK - 1) // MOE_BLOCK
    rows = n_blk * MOE_BLOCK
    xbuf = jnp.zeros((rows, d), x.dtype).at[dest].set(xt[stok])
    blk_start = jnp.arange(n_blk, dtype=jnp.int32) * MOE_BLOCK
    blk_e = jnp.minimum(jnp.searchsorted(pends, blk_start, side='right'), N_EXPERTS - 1)

    def expert_block(args):
        xb, e = args
        hdn = jax.nn.silu(xb @ w1[e]) * (xb @ w3[e])
        return hdn @ w2[e]

    ybuf = lax.map(expert_block, (xbuf.reshape(n_blk, MOE_BLOCK, d), blk_e)).reshape(rows, d)
    y_assign = ybuf[dest] * sw[:, None].astype(ybuf.dtype)
    y = jax.ops.segment_sum(y_assign, stok, num_segments=n_tok)
    return y.reshape(bn, t, d).astype(x.dtype)


def encoder_layer(x, norm1_g, w_in, conv_a_w, conv_a_norm_g, dn_conv_w, dn_a_log, dn_dt_bias,
                  dn_norm_g, w_o, norm2_g, rg_w, rg_b, re_w, re_b, w1, w3, w2):
    h = rmsnorm(x, norm1_g)
    proj = h @ w_in
    c0 = 3 * CONV_WIDTH
    c1 = c0 + 2 * DN_QK + DN_V
    c2 = c1 + DN_V
    bch, qkv, z, ab = proj[..., :c0], proj[..., c0:c1], proj[..., c1:c2], proj[..., c2:]
    y_a = conv_mixer(bch, conv_a_w, conv_a_norm_g)
    y_b = deltanet_mixer(qkv, z, ab, dn_conv_w, dn_a_log, dn_dt_bias, dn_norm_g)
    x = x + jnp.concatenate([y_a, y_b], axis=-1) @ w_o
    x = x + hier_moe(rmsnorm(x, norm2_g), rg_w, rg_b, re_w, re_b, w1, w3, w2)
    return x


def setup_inputs(seed: int = 0) -> dict:
    key = jax.random.key(seed)
    ks = jax.random.split(key, 24)
    f32 = jnp.float32
    nrm = lambda k, shape, scale: jax.random.normal(k, shape, f32) * scale
    gain = lambda k, shape: 1.0 + 0.02 * jax.random.normal(k, shape, f32)
    dt = jnp.exp(jax.random.uniform(ks[8], (DEPTH, 2, DN_HEADS), f32,
                                    math.log(1e-3), math.log(0.1)))
    return {
        'x_prompt': jax.random.normal(ks[0], (BATCH, SEQ, D_MODEL), f32),
        'x_sample': jax.random.normal(ks[1], (DEC_BATCH, DEC_SEQ, D_MODEL), f32),
        'norm1_g': gain(ks[2], (DEPTH, D_MODEL)),
        'w_in': nrm(ks[3], (DEPTH, D_MODEL, PROJ_WIDTH), D_MODEL ** -0.5),
        'conv_a_w': nrm(ks[4], (DEPTH, CONV_K, CONV_WIDTH), CONV_K ** -0.5),
        'conv_a_norm_g': gain(ks[5], (DEPTH, CONV_WIDTH)),
        'dn_conv_w': nrm(ks[6], (DEPTH, DN_CONV_K, 2 * DN_QK + DN_V), DN_CONV_K ** -0.5),
        'dn_a_log': jnp.log(jax.random.uniform(ks[7], (DEPTH, 2, DN_HEADS), f32, 1.0, 16.0)),
        'dn_dt_bias': dt + jnp.log(-jnp.expm1(-dt)),
        'dn_norm_g': gain(ks[9], (DEPTH, DN_DV)),
        'w_o': nrm(ks[10], (DEPTH, MIX_WIDTH, D_MODEL), MIX_WIDTH ** -0.5),
        'norm2_g': gain(ks[11], (DEPTH, D_MODEL)),
        'router_group_w': nrm(ks[12], (DEPTH, D_MODEL, N_GROUPS), D_MODEL ** -0.5),
        'router_group_b': nrm(ks[13], (DEPTH, N_GROUPS), 0.01),
        'router_expert_w': nrm(ks[14], (DEPTH, D_MODEL, N_EXPERTS), D_MODEL ** -0.5),
        'router_expert_b': nrm(ks[15], (DEPTH, N_EXPERTS), 0.01),
        'w1': nrm(ks[16], (DEPTH, N_EXPERTS, D_MODEL, EXPERT_FF), D_MODEL ** -0.5),
        'w3': nrm(ks[17], (DEPTH, N_EXPERTS, D_MODEL, EXPERT_FF), D_MODEL ** -0.5),
        'w2': nrm(ks[18], (DEPTH, N_EXPERTS, EXPERT_FF, D_MODEL), EXPERT_FF ** -0.5),
        'final_norm_g': gain(ks[19], (D_MODEL,)),
    }


def reference(x_prompt, x_sample, norm1_g, w_in, conv_a_w, conv_a_norm_g, dn_conv_w, dn_a_log,
              dn_dt_bias, dn_norm_g, w_o, norm2_g, router_group_w, router_group_b,
              router_expert_w, router_expert_b, w1, w3, w2, final_norm_g):
    def trunk(x):
        for l in range(DEPTH):
            x = encoder_layer(x, norm1_g[l], w_in[l], conv_a_w[l], conv_a_norm_g[l], dn_conv_w[l],
                              dn_a_log[l], dn_dt_bias[l], dn_norm_g[l], w_o[l], norm2_g[l],
                              router_group_w[l], router_group_b[l], router_expert_w[l],
                              router_expert_b[l], w1[l], w3[l], w2[l])
        return rmsnorm(x, final_norm_g)

    y_prompt = trunk(x_prompt)
    y_sample = trunk(x_sample)
    return (y_prompt, y_sample)
```

```python
import functools

import jax
import jax.numpy as jnp
from jax import lax
from jax.experimental import pallas as pl
from jax.experimental.pallas import tpu as pltpu

F32 = jnp.float32
BF16 = jnp.bfloat16

EPS = 1e-6
CONV_WIDTH = 512
CONV_GROUPS = 8
DN_HEADS = 4
DN_DK = 128
DN_V = 512
CHUNK = 64
N_GROUPS = 4
EXPERTS_PER_GROUP = 8
N_EXPERTS = N_GROUPS * EXPERTS_PER_GROUP
EXPERT_FF = 512
LANES = 128
HALO = 8

TOKEN_TILE = 256
DN_BLOCK = 256
EXPERT_TILE = 256
VMEM_LIMIT = 56 * 1024 * 1024


def _cparams(sem):
    return pltpu.CompilerParams(dimension_semantics=sem, vmem_limit_bytes=VMEM_LIMIT)


def _sigmoid(x):
    return 1.0 / (1.0 + jnp.exp(-x))


def _rms_rows(x, g):
    return x * lax.rsqrt(jnp.mean(x * x, axis=-1, keepdims=True) + EPS) * g


def _dot(a, b):
    return jnp.dot(a, b, preferred_element_type=F32)


def _proj_kernel(x_ref, xp_ref, xn_ref, g1_ref, w_ref, cw_ref, cng_ref, gmat_ref,
                 nea_ref, dtb_ref,
                 ya_ref, q_ref, k_ref, v_ref, z_ref, gb_ref,
                 hs_ref, su_ref, *, tile, seq_len):
    i = pl.program_id(0)
    not_start = ((i * tile) % seq_len != 0).astype(F32)
    not_end = (((i + 1) * tile) % seq_len != 0).astype(F32)
    g1 = g1_ref[...]
    hs_ref[0:HALO, :] = _rms_rows(xp_ref[...], g1)
    hs_ref[HALO:HALO + tile, :] = _rms_rows(x_ref[...], g1)
    hs_ref[HALO + tile:, :] = _rms_rows(xn_ref[...], g1)
    hb = hs_ref[...].astype(BF16)
    proj = _dot(hb, w_ref[...])

    rows = lax.broadcasted_iota(jnp.int32, (tile + 2 * HALO, 1), 0)
    keep = jnp.where(rows < HALO, not_start, jnp.where(rows >= HALO + tile, not_end, 1.0))
    cw = CONV_WIDTH
    su_ref[:, 0:cw] = proj[:, cw:2 * cw] * proj[:, 2 * cw:3 * cw] * keep
    su_ref[:, cw:] = proj[:, 3 * cw:6 * cw] * keep
    conv = (su_ref[HALO - 1:HALO - 1 + tile, :] * cw_ref[0:1, :]
            + su_ref[HALO:HALO + tile, :] * cw_ref[1:2, :]
            + su_ref[HALO + 1:HALO + 1 + tile, :] * cw_ref[2:3, :])

    main = proj[HALO:HALO + tile, :]
    ya = main[:, 0:cw] * conv[:, 0:cw]
    sq = ya * ya
    sq_hi = sq.astype(BF16)
    sq_lo = (sq - sq_hi.astype(F32)).astype(BF16)
    ms = _dot(sq_hi, gmat_ref[...]) + _dot(sq_lo, gmat_ref[...])
    ya_ref[...] = (ya * lax.rsqrt(ms + EPS) * cng_ref[...]).astype(ya_ref.dtype)

    qkv = conv[:, cw:]
    qkv = qkv * _sigmoid(qkv)
    for h in range(DN_HEADS):
        sl = slice(h * DN_DK, (h + 1) * DN_DK)
        qh = qkv[:, sl]
        qh = qh * (lax.rsqrt(jnp.sum(qh * qh, axis=-1, keepdims=True) + EPS) * (DN_DK ** -0.5))
        q_ref[:, sl] = qh.astype(q_ref.dtype)
        kh = qkv[:, 512 + h * DN_DK:512 + (h + 1) * DN_DK]
        kh = kh * lax.rsqrt(jnp.sum(kh * kh, axis=-1, keepdims=True) + EPS)
        k_ref[:, sl] = kh.astype(k_ref.dtype)
    v_ref[...] = qkv[:, 1024:1536].astype(v_ref.dtype)
    z_ref[...] = main[:, 6 * cw:7 * cw].astype(z_ref.dtype)

    ab = main[:, 7 * cw:]
    xs = ab + dtb_ref[...]
    softplus = jnp.maximum(xs, 0.0) + jnp.log(1.0 + jnp.exp(-jnp.abs(xs)))
    g = nea_ref[...] * softplus
    beta = _sigmoid(ab)
    r = lax.broadcasted_iota(jnp.int32, (tile, LANES), 0) % CHUNK
    pre = g
    suf = g
    s = 1
    while s < CHUNK:
        pre = pre + jnp.where(r >= s, pltpu.roll(pre, s, axis=0), 0.0)
        suf = suf + jnp.where(r < CHUNK - s, pltpu.roll(suf, tile - s, axis=0), 0.0)
        s *= 2
    lane = lax.broadcasted_iota(jnp.int32, (tile, LANES), 1)
    gb_ref[...] = jnp.where(lane < DN_HEADS, pre, jnp.where(lane < 2 * DN_HEADS, suf, beta))


def _proj_call(x2d, seq_len, row_off, outs, wts, n_total):
    n = x2d.shape[0]
    tile = TOKEN_TILE
    nt = n // tile
    tb = tile // HALO
    off = row_off // tile
    kern = functools.partial(_proj_kernel, tile=tile, seq_len=seq_len)
    const = lambda i: (0, 0)
    in_specs = [
        pl.BlockSpec((tile, x2d.shape[1]), lambda i: (i, 0)),
        pl.BlockSpec((HALO, x2d.shape[1]), lambda i: (jnp.maximum(i * tb - 1, 0), 0)),
        pl.BlockSpec((HALO, x2d.shape[1]), lambda i: (jnp.minimum((i + 1) * tb, n // HALO - 1), 0)),
    ] + [pl.BlockSpec(w.shape, const) for w in wts]
    widths = (512, 512, 512, 512, 512, LANES)
    dtypes = (BF16, BF16, BF16, BF16, BF16, F32)
    out_shape = [jax.ShapeDtypeStruct((n_total, w), d) for w, d in zip(widths, dtypes)]
    out_specs = [pl.BlockSpec((tile, w), lambda i: (i + off, 0)) for w in widths]
    aliases = {}
    args = [x2d, x2d, x2d] + list(wts)
    if outs is not None:
        in_specs += [pl.BlockSpec(memory_space=pl.ANY)] * len(outs)
        aliases = {len(args) + j: j for j in range(len(outs))}
        args += list(outs)
        body = lambda *refs: kern(*refs[:len(refs) - 8 - len(outs)], *refs[len(refs) - 8:])
    else:
        body = kern
    return pl.pallas_call(
        body, grid=(nt,), in_specs=in_specs, out_specs=out_specs, out_shape=out_shape,
        scratch_shapes=[pltpu.VMEM((tile + 2 * HALO, x2d.shape[1]), F32),
                        pltpu.VMEM((tile + 2 * HALO, 4 * CONV_WIDTH), F32)],
        input_output_aliases=aliases, compiler_params=_cparams(("arbitrary",)),
        name="proj_conv")(*args)


def _dn_kernel(*refs, block, reverse, n_prompt, seq_len, nblk):
    if reverse:
        (q_ref, k_ref, v_ref, gb_ref, of_ref, z_ref, ng_ref, o_ref, s_ref) = refs
    else:
        (q_ref, k_ref, v_ref, gb_ref, o_ref, s_ref) = refs
    i = pl.program_id(0)
    b = nblk - 1 - i if reverse else i
    if reverse:
        edge = (b + 1) * block
        reset = (edge == n_prompt) | ((edge > n_prompt) & ((edge - n_prompt) % seq_len == 0))
    else:
        edge = b * block
        reset = (edge == 0) | ((edge >= n_prompt) & ((edge - n_prompt) % seq_len == 0))

    @pl.when(reset)
    def _():
        s_ref[...] = jnp.zeros_like(s_ref)

    nchunk = block // CHUNK
    c = CHUNK
    ri = lax.broadcasted_iota(jnp.int32, (c, c), 0)
    ci = lax.broadcasted_iota(jnp.int32, (c, c), 1)
    incl = (ri <= ci) if reverse else (ri >= ci)
    strict = (ri < ci) if reverse else (ri > ci)
    eye = (ri == ci).astype(F32)
    g_off = DN_HEADS if reverse else 0
    b_off = 2 * DN_HEADS + (DN_HEADS if reverse else 0)
    last = 0 if reverse else c - 1

    def chunk_body(j, carry):
        cc = nchunk - 1 - j if reverse else j
        r0 = pl.multiple_of(cc * c, c)
        rows = pl.ds(r0, c)
        gbc = gb_ref[rows, :]
        gbt = gbc.T
        for h in range(DN_HEADS):
            sl = slice(h * DN_DK, (h + 1) * DN_DK)
            kc = k_ref[rows, sl]
            qc = q_ref[rows, sl]
            vc = v_ref[rows, sl].astype(F32)
            gcol = gbc[:, g_off + h:g_off + h + 1]
            grow = gbt[g_off + h:g_off + h + 1, :]
            beta = gbc[:, b_off + h:b_off + h + 1]
            glast = grow[:, last:last + 1]
            decay = jnp.where(incl, jnp.exp(jnp.minimum(gcol - grow, 0.0)), 0.0)
            kq = lax.dot_general(jnp.concatenate([kc, qc], axis=0), kc,
                                 (((1,), (1,)), ((), ())), preferred_element_type=F32)
            lmat = jnp.where(strict, kq[:c] * decay, 0.0) * beta
            amat = kq[c:] * decay
            tinv = eye - lmat
            lp = lmat
            p = 2
            while p < c:
                lpb = lp.astype(BF16)
                lp = _dot(lpb, lpb)
                tinv = tinv + _dot(tinv.astype(BF16), lp.astype(BF16))
                p *= 2
            egc = jnp.exp(gcol)
            kf = kc.astype(F32)
            rhs = jnp.concatenate([kf * (beta * egc), vc * beta], axis=1).astype(BF16)
            wu = _dot(tinv.astype(BF16), rhs)
            w = wu[:, :DN_DK]
            u = wu[:, DN_DK:]
            qd = qc.astype(F32) * egc
            kd = kf * jnp.exp(glast - gcol)
            s = s_ref[h]
            wq = _dot(jnp.concatenate([w, qd], axis=0).astype(BF16), s.astype(BF16))
            vn = u - wq[:c]
            vnb = vn.astype(BF16)
            o = wq[c:] + _dot(amat.astype(BF16), vnb)
            s_ref[h] = s * jnp.exp(glast) + lax.dot_general(
                kd.astype(BF16), vnb, (((0,), (0,)), ((), ())), preferred_element_type=F32)
            if reverse:
                o = o + of_ref[rows, sl]
                o = o * lax.rsqrt(jnp.mean(o * o, axis=-1, keepdims=True) + EPS) * ng_ref[...]
                zc = z_ref[rows, sl].astype(F32)
                o = o * (zc * _sigmoid(zc))
            o_ref[rows, sl] = o.astype(o_ref.dtype)
        return carry

    lax.fori_loop(0, nchunk, chunk_body, 0)


def _dn_call(q, k, v, gb, extra, reverse, n_prompt, seq_len):
    n = q.shape[0]
    block = DN_BLOCK
    nblk = n // block
    imap = (lambda i: (nblk - 1 - i, 0)) if reverse else (lambda i: (i, 0))
    tok = lambda w: pl.BlockSpec((block, w), imap)
    in_specs = [tok(512), tok(512), tok(512), tok(LANES)]
    args = [q, k, v, gb]
    if reverse:
        o_f, z, ng = extra
        in_specs += [tok(512), tok(512), pl.BlockSpec(ng.shape, lambda i: (0, 0))]
        args += [o_f, z, ng]
    kern = functools.partial(_dn_kernel, block=block, reverse=reverse, n_prompt=n_prompt,
                             seq_len=seq_len, nblk=nblk)
    return pl.pallas_call(
        kern, grid=(nblk,), in_specs=in_specs, out_specs=tok(512),
        out_shape=jax.ShapeDtypeStruct((n, 512), BF16 if reverse else F32),
        scratch_shapes=[pltpu.VMEM((DN_HEADS, DN_DK, DN_DK), F32)],
        compiler_params=_cparams(("arbitrary",)),
        name="deltanet_bwd" if reverse else "deltanet_fwd")(*args)


def _route_kernel(x_ref, ya_ref, yb_ref, wo_ref, g2_ref, rwh_ref, rwl_ref, rb_ref, tri_ref,
                  x2_ref, xn_ref, info_ref, cnt_ref, *, tile):
    half = wo_ref.shape[0] // 2
    x2 = x_ref[...] + _dot(ya_ref[...], wo_ref[0:half, :]) + _dot(yb_ref[...], wo_ref[half:, :])
    x2_ref[...] = x2
    xn = _rms_rows(x2, g2_ref[...])
    xn_ref[...] = xn
    xh = xn.astype(BF16)
    xl = (xn - xh.astype(F32)).astype(BF16)
    logits = (_dot(xh, rwh_ref[...]) + _dot(xl, rwh_ref[...]) + _dot(xh, rwl_ref[...])
              + rb_ref[...])
    lane = lax.broadcasted_iota(jnp.int32, (tile, LANES), 1).astype(F32)
    neg = jnp.float32(-jnp.inf)

    def first_argmax(vals):
        m = jnp.max(vals, axis=-1, keepdims=True)
        idx = jnp.min(jnp.where(vals == m, lane, float(LANES)), axis=-1, keepdims=True)
        return m, idx

    gl = jnp.where(lane < N_GROUPS, logits, neg)
    gmax, gidx = first_argmax(gl)
    g_w = 1.0 / jnp.sum(jnp.exp(gl - gmax), axis=-1, keepdims=True)
    lo = N_GROUPS + gidx * EXPERTS_PER_GROUP
    el = jnp.where((lane >= lo) & (lane < lo + EXPERTS_PER_GROUP), logits, neg)
    m1, i1 = first_argmax(el)
    m2, i2 = first_argmax(jnp.where(lane == i1, neg, el))
    r = jnp.exp(m2 - m1)
    gate1 = g_w / (1.0 + r)
    gate2 = g_w * r / (1.0 + r)
    e1 = i1 - N_GROUPS
    e2 = i2 - N_GROUPS
    onehot = ((lane == e1) | (lane == e2)).astype(BF16)
    before = _dot(tri_ref[...], onehot)
    rank1 = jnp.sum(jnp.where(lane == e1, before, 0.0), axis=-1, keepdims=True)
    rank2 = jnp.sum(jnp.where(lane == e2, before, 0.0), axis=-1, keepdims=True)
    cnt_ref[...] = jnp.sum(onehot.astype(F32), axis=0, keepdims=True).reshape(1, 1, LANES)
    info = jnp.where(lane == 0, e1,
           jnp.where(lane == 1, e2,
           jnp.where(lane == 2, rank1,
           jnp.where(lane == 3, rank2,
           jnp.where(lane == 4, gate1,
           jnp.where(lane == 5, gate2, 0.0))))))
    info_ref[...] = info


def _route_call(x2d, ya, yb, row_off, outs, wts, n_total):
    n = x2d.shape[0]
    tile = TOKEN_TILE
    nt = n // tile
    off = row_off // tile
    d = x2d.shape[1]
    const = lambda i: (0, 0)
    in_specs = [pl.BlockSpec((tile, d), lambda i: (i, 0)),
                pl.BlockSpec((tile, 512), lambda i: (i + off, 0)),
                pl.BlockSpec((tile, 512), lambda i: (i + off, 0))]
    in_specs += [pl.BlockSpec(w.shape, const) for w in wts]
    out_shape = [jax.ShapeDtypeStruct((n_total, d), F32), jax.ShapeDtypeStruct((n_total, d), F32),
                 jax.ShapeDtypeStruct((n_total, LANES), F32),
                 jax.ShapeDtypeStruct((n_total // tile, 1, LANES), F32)]
    out_specs = [pl.BlockSpec((tile, d), lambda i: (i + off, 0)),
                 pl.BlockSpec((tile, d), lambda i: (i + off, 0)),
                 pl.BlockSpec((tile, LANES), lambda i: (i + off, 0)),
                 pl.BlockSpec((1, 1, LANES), lambda i: (i + off, 0, 0))]
    kern = functools.partial(_route_kernel, tile=tile)
    args = [x2d, ya, yb] + list(wts)
    aliases = {}
    if outs is not None:
        in_specs += [pl.BlockSpec(memory_space=pl.ANY)] * len(outs)
        aliases = {len(args) + j: j for j in range(len(outs))}
        args += list(outs)
        body = lambda *refs: kern(*refs[:len(refs) - 4 - len(outs)], *refs[len(refs) - 4:])
    else:
        body = kern
    return pl.pallas_call(
        body, grid=(nt,), in_specs=in_specs, out_specs=out_specs, out_shape=out_shape,
        input_output_aliases=aliases, compiler_params=_cparams(("arbitrary",)),
        name="oproj_router")(*args)


def _dispatch_kernel(dest_ref, xn_ref, buf_in_ref, buf_ref, sem, *, tile):
    del buf_in_ref

    def issue(t, carry):
        for kk in range(2):
            pltpu.make_async_copy(xn_ref.at[pl.ds(t, 1), :],
                                  buf_ref.at[pl.ds(dest_ref[2 * t + kk], 1), :], sem).start()
        return carry

    lax.fori_loop(0, tile, issue, 0)
    for _ in range(2):
        pltpu.make_async_copy(xn_ref, buf_ref.at[pl.ds(0, tile), :], sem).wait()


def _dispatch_call(dest, xn, xbuf):
    n, d = xn.shape
    tile = TOKEN_TILE
    kern = functools.partial(_dispatch_kernel, tile=tile)
    return pl.pallas_call(
        kern, grid=(n // tile,),
        in_specs=[pl.BlockSpec((2 * tile,), lambda i: (i,), memory_space=pltpu.SMEM),
                  pl.BlockSpec((tile, d), lambda i: (i, 0)),
                  pl.BlockSpec(memory_space=pl.ANY)],
        out_specs=pl.BlockSpec(memory_space=pl.ANY),
        out_shape=jax.ShapeDtypeStruct(xbuf.shape, xbuf.dtype),
        scratch_shapes=[pltpu.SemaphoreType.DMA(())],
        input_output_aliases={2: 0}, compiler_params=_cparams(("arbitrary",)),
        name="moe_dispatch")(dest, xn, xbuf)


def _expert_kernel(te_ref, nu_ref, x_ref, w1_ref, w3_ref, w2_ref, y_ref):
    j = pl.program_id(0)

    @pl.when(j < nu_ref[0])
    def _():
        xb = x_ref[...].astype(BF16)
        h1 = _dot(xb, w1_ref[0])
        h3 = _dot(xb, w3_ref[0])
        hdn = (h1 * _sigmoid(h1) * h3).astype(BF16)
        y_ref[...] = _dot(hdn, w2_ref[0])

    @pl.when(j >= nu_ref[0])
    def _():
        y_ref[...] = jnp.zeros_like(y_ref)


def _expert_call(tile_expert, n_used, xbuf, w1, w3, w2):
    rows, d = xbuf.shape
    tm = EXPERT_TILE
    ff = w1.shape[2]
    grid_spec = pltpu.PrefetchScalarGridSpec(
        num_scalar_prefetch=2, grid=(rows // tm,),
        in_specs=[pl.BlockSpec((tm, d), lambda j, te, nu: (j, 0)),
                  pl.BlockSpec((1, d, ff), lambda j, te, nu: (te[j], 0, 0)),
                  pl.BlockSpec((1, d, ff), lambda j, te, nu: (te[j], 0, 0)),
                  pl.BlockSpec((1, ff, d), lambda j, te, nu: (te[j], 0, 0))],
        out_specs=pl.BlockSpec((tm, d), lambda j, te, nu: (j, 0)))
    return pl.pallas_call(
        _expert_kernel, grid_spec=grid_spec,
        out_shape=jax.ShapeDtypeStruct((rows, d), F32),
        compiler_params=_cparams(("arbitrary",)), name="moe_experts")(
            tile_expert, n_used, xbuf, w1, w3, w2)


def _combine_kernel(dest_ref, x2_ref, info_ref, gf_ref, ybuf_ref, out_ref, rows_ref, sem, *, tile):
    def issue(t, carry):
        for kk in range(2):
            pltpu.make_async_copy(ybuf_ref.at[pl.ds(dest_ref[2 * t + kk], 1), :],
                                  rows_ref.at[kk, pl.ds(t, 1), :], sem).start()
        return carry

    lax.fori_loop(0, tile, issue, 0)
    for kk in range(2):
        pltpu.make_async_copy(ybuf_ref.at[pl.ds(0, tile), :], rows_ref.at[kk], sem).wait()
    info = info_ref[...]
    y = x2_ref[...] + info[:, 4:5] * rows_ref[0] + info[:, 5:6] * rows_ref[1]
    out_ref[...] = _rms_rows(y, gf_ref[...])


def _combine_call(dest, x2, info, gf, ybuf, row_off, n):
    d = x2.shape[1]
    tile = TOKEN_TILE
    off = row_off // tile
    kern = functools.partial(_combine_kernel, tile=tile)
    return pl.pallas_call(
        kern, grid=(n // tile,),
        in_specs=[pl.BlockSpec((2 * tile,), lambda i: (i + off,), memory_space=pltpu.SMEM),
                  pl.BlockSpec((tile, d), lambda i: (i + off, 0)),
                  pl.BlockSpec((tile, LANES), lambda i: (i + off, 0)),
                  pl.BlockSpec(gf.shape, lambda i: (0, 0)),
                  pl.BlockSpec(memory_space=pl.ANY)],
        out_specs=pl.BlockSpec((tile, d), lambda i: (i, 0)),
        out_shape=jax.ShapeDtypeStruct((n, d), F32),
        scratch_shapes=[pltpu.VMEM((2, tile, d), F32), pltpu.SemaphoreType.DMA(())],
        compiler_params=_cparams(("arbitrary",)), name="moe_combine")(dest, x2, info, gf, ybuf)


def _layer(xp, xs, seq_p, seq_s, norm1_g, w_in, conv_a_w, conv_a_norm_g, dn_conv_w, dn_a_log,
           dn_dt_bias, dn_norm_g, w_o, norm2_g, rg_w, rg_b, re_w, re_b, w1, w3, w2, final_g):
    n_p, d = xp.shape
    n_s = xs.shape[0]
    n = n_p + n_s
    row = lambda a: a.reshape(1, -1).astype(F32)

    w_main = jnp.pad(w_in, ((0, 0), (0, LANES - 4 * DN_HEADS))).astype(BF16)
    cw =jnp.concatenate([conv_a_w, dn_conv_w], axis=1).astype(F32)
    grp = jnp.arange(CONV_WIDTH) // (CONV_WIDTH // CONV_GROUPS)
    gmat = ((grp[:, None] == grp[None, :]).astype(F32) / (CONV_WIDTH // CONV_GROUPS)).astype(BF16)
    pad16 = lambda a: jnp.pad(a.reshape(1, -1).astype(F32), ((0, 0), (0, LANES - a.size)))
    nea = pad16(-jnp.exp(dn_a_log.astype(F32)))
    dtb = pad16(dn_dt_bias)
    proj_w = [row(norm1_g), w_main, cw, row(conv_a_norm_g), gmat, nea, dtb]

    outs = _proj_call(xp, seq_p, 0, None, proj_w, n)
    outs = _proj_call(xs, seq_s, n_p, outs, proj_w, n)
    ya, q, k, v, z, gb = outs

    o_f = _dn_call(q, k, v, gb, None, False, n_p, seq_s)
    yb = _dn_call(q, k, v, gb, (o_f, z, row(dn_norm_g)), True, n_p, seq_s)

    rw = jnp.pad(jnp.concatenate([rg_w, re_w], axis=1).astype(F32),
                 ((0, 0), (0, LANES - N_GROUPS - N_EXPERTS)))
    rwh = rw.astype(BF16)
    rwl = (rw - rwh.astype(F32)).astype(BF16)
    rb = jnp.pad(jnp.concatenate([rg_b, re_b]).reshape(1, -1).astype(F32),
                 ((0, 0), (0, LANES - N_GROUPS - N_EXPERTS)))
    tile = TOKEN_TILE
    tri = (jnp.arange(tile)[:, None] > jnp.arange(tile)[None, :]).astype(BF16)
    route_w = [w_o.astype(BF16), row(norm2_g), rwh, rwl, rb, tri]
    routs = _route_call(xp, ya, yb, 0, None, route_w, n)
    routs = _route_call(xs, ya, yb, n_p, routs, route_w, n)
    x2, xn, info, cnt = routs

    tm = EXPERT_TILE
    counts_t = cnt[:, 0, :N_EXPERTS].astype(jnp.int32)
    base_t = jnp.cumsum(counts_t, axis=0) - counts_t
    counts = jnp.sum(counts_t, axis=0)
    pcounts = (counts + tm - 1) // tm * tm
    pends = jnp.cumsum(pcounts)
    table = (pends - pcounts)[None, :] + base_t
    e12 = info[:, 0:2].astype(jnp.int32)
    rank12 = info[:, 2:4].astype(jnp.int32)
    tile_id = (jnp.arange(n, dtype=jnp.int32) // tile)[:, None]
    dest = (table[tile_id, e12] + rank12).reshape(-1)
    n_tiles = (2 * n + N_EXPERTS * (tm - 1) + tm - 1) // tm
    tile_start = jnp.arange(n_tiles, dtype=jnp.int32) * tm
    tile_expert = jnp.minimum(jnp.searchsorted(pends, tile_start, side='right'),
                              N_EXPERTS - 1).astype(jnp.int32)
    n_used = (pends[-1:] // tm).astype(jnp.int32)

    xbuf = _dispatch_call(dest, xn, jnp.zeros((n_tiles * tm, d), F32))
    ybuf = _expert_call(tile_expert, n_used, xbuf, w1.astype(BF16), w3.astype(BF16),
                        w2.astype(BF16))
    gf = row(final_g)
    y_p = _combine_call(dest, x2, info, gf, ybuf, 0, n_p)
    y_s = _combine_call(dest, x2, info, gf, ybuf, n_p, n_s)
    return y_p, y_s


def kernel(x_prompt, x_sample, norm1_g, w_in, conv_a_w, conv_a_norm_g, dn_conv_w, dn_a_log,
           dn_dt_bias, dn_norm_g, w_o, norm2_g, router_group_w, router_group_b, router_expert_w,
           router_expert_b, w1, w3, w2, final_norm_g):
    assert norm1_g.shape[0] == 1, "single-layer trunk"
    bp, sp, d = x_prompt.shape
    bs, ss, _ = x_sample.shape
    assert bp == 1 and sp % max(TOKEN_TILE, DN_BLOCK) == 0 and ss % max(TOKEN_TILE, DN_BLOCK) == 0
    y_p, y_s = _layer(
        x_prompt.reshape(bp * sp, d), x_sample.reshape(bs * ss, d), sp, ss,
        norm1_g[0], w_in[0], conv_a_w[0], conv_a_norm_g[0], dn_conv_w[0], dn_a_log[0],
        dn_dt_bias[0], dn_norm_g[0], w_o[0], norm2_g[0], router_group_w[0], router_group_b[0],
        router_expert_w[0], router_expert_b[0], w1[0], w3[0], w2[0], final_norm_g)
    return y_p.reshape(bp, sp, d), y_s.reshape(bs, ss, d)
```

```python
import functools

import jax
import jax.numpy as jnp
import numpy as np
from jax import lax
from jax.experimental import pallas as pl
from jax.experimental.pallas import tpu as pltpu

F32 = jnp.float32
BF16 = jnp.bfloat16

EPS = 1e-6
CONV_WIDTH = 512
CONV_GROUPS = 8
DN_HEADS = 4
DN_DK = 128
DN_V = 512
CHUNK = 64
N_GROUPS = 4
EXPERTS_PER_GROUP = 8
N_EXPERTS = N_GROUPS * EXPERTS_PER_GROUP
EXPERT_FF = 512
LANES = 128
HALO = 8

TOKEN_TILE = 256
DN_BLOCK = 256
EXPERT_TILE = 256
VMEM_LIMIT = 56 * 1024 * 1024

_GB_COLS = np.concatenate([np.arange(4 * DN_HEADS), np.arange(2 * DN_HEADS), np.arange(2 * DN_HEADS)])


def _cparams(sem):
    return pltpu.CompilerParams(dimension_semantics=sem, vmem_limit_bytes=VMEM_LIMIT)


def _sigmoid(x):
    return 1.0 / (1.0 + jnp.exp(-x))


def _rms_rows(x, g):
    return x * lax.rsqrt(jnp.mean(x * x, axis=-1, keepdims=True) + EPS) * g


def _dot(a, b):
    return jnp.dot(a, b, preferred_element_type=F32)


def _proj_kernel(x_ref, xp_ref, xn_ref, g1_ref, w_ref, cw_ref, cng_ref, gmat_ref,
                 nea_ref, dtb_ref,
                 ya_ref, q_ref, k_ref, v_ref, z_ref, gb_ref,
                 hs_ref, su_ref, *, tile, seq_len):
    i = pl.program_id(0)
    not_start = ((i * tile) % seq_len != 0).astype(F32)
    not_end = (((i + 1) * tile) % seq_len != 0).astype(F32)
    g1 = g1_ref[...]
    hs_ref[0:HALO, :] = _rms_rows(xp_ref[...], g1)
    hs_ref[HALO:HALO + tile, :] = _rms_rows(x_ref[...], g1)
    hs_ref[HALO + tile:, :] = _rms_rows(xn_ref[...], g1)
    hb = hs_ref[...].astype(BF16)
    proj = _dot(hb, w_ref[...])

    rows = lax.broadcasted_iota(jnp.int32, (tile + 2 * HALO, 1), 0)
    keep = jnp.where(rows < HALO, not_start, jnp.where(rows >= HALO + tile, not_end, 1.0))
    cw = CONV_WIDTH
    su_ref[:, 0:cw] = proj[:, cw:2 * cw] * proj[:, 2 * cw:3 * cw] * keep
    su_ref[:, cw:] = proj[:, 3 * cw:6 * cw] * keep
    conv = (su_ref[HALO - 1:HALO - 1 + tile, :] * cw_ref[0:1, :]
            + su_ref[HALO:HALO + tile, :] * cw_ref[1:2, :]
            + su_ref[HALO + 1:HALO + 1 + tile, :] * cw_ref[2:3, :])

    main = proj[HALO:HALO + tile, :]
    ya = main[:, 0:cw] * conv[:, 0:cw]
    sq = ya * ya
    sq_hi = sq.astype(BF16)
    sq_lo = (sq - sq_hi.astype(F32)).astype(BF16)
    ms = _dot(sq_hi, gmat_ref[...]) + _dot(sq_lo, gmat_ref[...])
    ya_ref[...] = (ya * lax.rsqrt(ms + EPS) * cng_ref[...]).astype(ya_ref.dtype)

    qkv = conv[:, cw:]
    qkv = qkv * _sigmoid(qkv)
    for h in range(DN_HEADS):
        sl = slice(h * DN_DK, (h + 1) * DN_DK)
        qh = qkv[:, sl]
        qh = qh * (lax.rsqrt(jnp.sum(qh * qh, axis=-1, keepdims=True) + EPS) * (DN_DK ** -0.5))
        q_ref[:, sl] = qh.astype(q_ref.dtype)
        kh = qkv[:, 512 + h * DN_DK:512 + (h + 1) * DN_DK]
        kh = kh * lax.rsqrt(jnp.sum(kh * kh, axis=-1, keepdims=True) + EPS)
        k_ref[:, sl] = kh.astype(k_ref.dtype)
    v_ref[...] = qkv[:, 1024:1536].astype(v_ref.dtype)
    z_ref[...] = main[:, 6 * cw:7 * cw].astype(z_ref.dtype)

    ab = main[:, 7 * cw:]
    xs = ab + dtb_ref[...]
    softplus = jnp.maximum(xs, 0.0) + jnp.log(1.0 + jnp.exp(-jnp.abs(xs)))
    g = nea_ref[...] * softplus
    beta = _sigmoid(ab)
    r = lax.broadcasted_iota(jnp.int32, (tile, LANES), 0) % CHUNK
    pre = g
    suf = g
    s = 1
    while s < CHUNK:
        pre = pre + jnp.where(r >= s, pltpu.roll(pre, s, axis=0), 0.0)
        suf = suf + jnp.where(r < CHUNK - s, pltpu.roll(suf, tile - s, axis=0), 0.0)
        s *= 2
    lane = lax.broadcasted_iota(jnp.int32, (tile, LANES), 1) // DN_HEADS
    gb_ref[...] = jnp.where(lane == 0, pre,
                  jnp.where(lane == 1, suf,
                  jnp.where(lane < 4, beta,
                  jnp.where(lane == 4, suf - g,
                  jnp.where(lane == 5, pre - g, pre + suf - g)))))


def _proj_call(x2d, seq_len, row_off, outs, wts, n_total):
    n = x2d.shape[0]
    tile = TOKEN_TILE
    nt = n // tile
    tb = tile // HALO
    off = row_off // tile
    kern = functools.partial(_proj_kernel, tile=tile, seq_len=seq_len)
    const = lambda i: (0, 0)
    in_specs = [
        pl.BlockSpec((tile, x2d.shape[1]), lambda i: (i, 0)),
        pl.BlockSpec((HALO, x2d.shape[1]), lambda i: (jnp.maximum(i * tb - 1, 0), 0)),
        pl.BlockSpec((HALO, x2d.shape[1]), lambda i: (jnp.minimum((i + 1) * tb, n // HALO - 1), 0)),
    ] + [pl.BlockSpec(w.shape, const) for w in wts]
    widths = (512, 512, 512, 512, 512, LANES)
    dtypes = (BF16, BF16, BF16, BF16, BF16, F32)
    out_shape = [jax.ShapeDtypeStruct((n_total, w), d) for w, d in zip(widths, dtypes)]
    out_specs = [pl.BlockSpec((tile, w), lambda i: (i + off, 0)) for w in widths]
    aliases = {}
    args = [x2d, x2d, x2d] + list(wts)
    if outs is not None:
        in_specs += [pl.BlockSpec(memory_space=pl.ANY)] * len(outs)
        aliases = {len(args) + j: j for j in range(len(outs))}
        args += list(outs)
        body = lambda *refs: kern(*refs[:len(refs) - 8 - len(outs)], *refs[len(refs) - 8:])
    else:
        body = kern
    return pl.pallas_call(
        body, grid=(nt,), in_specs=in_specs, out_specs=out_specs, out_shape=out_shape,
        scratch_shapes=[pltpu.VMEM((tile + 2 * HALO, x2d.shape[1]), F32),
                        pltpu.VMEM((tile + 2 * HALO, 4 * CONV_WIDTH), F32)],
        input_output_aliases=aliases, compiler_params=_cparams(("arbitrary",)),
        name="proj_conv")(*args)


def _dn_kernel(*refs, block, reverse, n_prompt, seq_len, nblk):
    if reverse:
        (q_ref, k_ref, v_ref, gb_ref, of_ref, z_ref, ng_ref, o_ref, s_ref) = refs
    else:
        (q_ref, k_ref, v_ref, gb_ref, o_ref, s_ref) = refs
    i = pl.program_id(0)
    b = nblk - 1 - i if reverse else i
    if reverse:
        edge = (b + 1) * block
        reset = (edge == n_prompt) | ((edge > n_prompt) & ((edge - n_prompt) % seq_len == 0))
    else:
        edge = b * block
        reset = (edge == 0) | ((edge >= n_prompt) & ((edge - n_prompt) % seq_len == 0))

    @pl.when(reset)
    def _():
        s_ref[...] = jnp.zeros_like(s_ref)

    nchunk = block // CHUNK
    c = CHUNK
    heads = range(DN_HEADS)
    ri = lax.broadcasted_iota(jnp.int32, (block, block), 0)
    ci = lax.broadcasted_iota(jnp.int32, (block, block), 1)
    same = (ri // c) == (ci // c)
    incl = same & ((ri <= ci) if reverse else (ri >= ci))
    strict = same & ((ri < ci) if reverse else (ri > ci))
    pr = lax.broadcasted_iota(jnp.int32, (c, block), 0)
    pc = lax.broadcasted_iota(jnp.int32, (c, block), 1)
    eye_p = (pr == pc % c).astype(F32)
    lane_chunk = pc // c

    def fold(bd):
        out = bd[0:c]
        for r in range(1, nchunk):
            out = out + bd[r * c:(r + 1) * c]
        return out

    def unfold(p):
        zero = jnp.zeros_like(p)
        return jnp.concatenate([jnp.where(lane_chunk == r, p, zero) for r in range(nchunk)], axis=0)

    grp = lambda g, h: slice(g * DN_HEADS + h, g * DN_HEADS + h + 1)
    g_cum, g_beta, g_rest, g_tot = (1, 3, 5, 7) if reverse else (0, 2, 4, 6)
    gbv = gb_ref[...]
    gbt = gbv.T
    sls = [slice(h * DN_DK, (h + 1) * DN_DK) for h in heads]
    kb = [k_ref[:, sl] for sl in sls]
    qb = [q_ref[:, sl] for sl in sls]
    gcol = [gbv[:, grp(g_cum, h)] for h in heads]
    beta = [gbv[:, grp(g_beta, h)] for h in heads]
    kq = [lax.dot_general(jnp.concatenate([kb[h], qb[h]], axis=0), kb[h],
                          (((1,), (1,)), ((), ())), preferred_element_type=F32) for h in heads]
    decay = [jnp.exp(jnp.minimum(gcol[h] - gbt[grp(g_cum, h), :], 0.0)) for h in heads]
    bdl = [jnp.where(strict, kq[h][:block] * decay[h], 0.0) * beta[h] for h in heads]
    bda = [jnp.where(incl, kq[h][block:] * decay[h], 0.0).astype(BF16) for h in heads]

    lp = [fold(bdl[h]) for h in heads]
    xp = [eye_p - lp[h] for h in heads]
    bd = [bdl[h].astype(BF16) for h in heads]
    p = 1
    while p < c:
        if p == 1:
            res = [_dot(lp[h].astype(BF16), bd[h]) for h in heads]
            lp = res
        elif 2 * p < c:
            res = [_dot(jnp.concatenate([lp[h], xp[h]], axis=0).astype(BF16), bd[h]) for h in heads]
            lp = [r[:c] for r in res]
            xp = [xp[h] + res[h][c:] for h in heads]
        else:
            xp = [xp[h] + _dot(xp[h].astype(BF16), bd[h]) for h in heads]
        p *= 2
        if p < c:
            bd = [unfold(lp[h].astype(BF16)) for h in heads]
    bdt = [unfold(xp[h].astype(BF16)) for h in heads]

    egc = [jnp.exp(gcol[h]) for h in heads]
    kf = [kb[h].astype(F32) for h in heads]
    rhs = [jnp.concatenate([kf[h] * (beta[h] * egc[h]), v_ref[:, sls[h]].astype(F32) * beta[h]],
                           axis=1).astype(BF16) for h in heads]
    wu = [_dot(bdt[h], rhs[h]) for h in heads]
    au = [_dot(bda[h], wu[h].astype(BF16)) for h in heads]
    qt = [qb[h].astype(F32) * egc[h] - au[h][:, :DN_DK] for h in heads]
    kd = [(kf[h] * jnp.exp(gbv[:, grp(g_rest, h)])).astype(BF16) for h in heads]
    wqt = [jnp.concatenate([wu[h][:, :DN_DK], qt[h]], axis=1).astype(BF16) for h in heads]

    s = [s_ref[h] for h in heads]
    order = range(nchunk - 1, -1, -1) if reverse else range(nchunk)
    for cc in order:
        rows = slice(cc * c, (cc + 1) * c)
        for h in heads:
            lhs = jnp.concatenate([wqt[h][rows, :DN_DK], wqt[h][rows, DN_DK:]], axis=0)
            wq = _dot(lhs, s[h].astype(BF16))
            vn = wu[h][rows, DN_DK:] - wq[:c]
            o = wq[c:] + au[h][rows, DN_DK:]
            tot = gbt[grp(g_tot, h), cc * c:cc * c + 1]
            s[h] = s[h] * jnp.exp(tot) + lax.dot_general(
                kd[h][rows], vn.astype(BF16), (((0,), (0,)), ((), ())), preferred_element_type=F32)
            if reverse:
                o = o + of_ref[rows, sls[h]]
                o = o * lax.rsqrt(jnp.mean(o * o, axis=-1, keepdims=True) + EPS) * ng_ref[...]
                zc = z_ref[rows, sls[h]].astype(F32)
                o = o * (zc * _sigmoid(zc))
            o_ref[rows, sls[h]] = o.astype(o_ref.dtype)
    for h in heads:
        s_ref[h] = s[h]


def _dn_call(q, k, v, gb, extra, reverse, n_prompt, seq_len):
    n = q.shape[0]
    block = DN_BLOCK
    nblk = n // block
    imap = (lambda i: (nblk - 1 - i, 0)) if reverse else (lambda i: (i, 0))
    tok = lambda w: pl.BlockSpec((block, w), imap)
    in_specs = [tok(512), tok(512), tok(512), tok(LANES)]
    args = [q, k, v, gb]
    if reverse:
        o_f, z, ng = extra
        in_specs += [tok(512), tok(512), pl.BlockSpec(ng.shape, lambda i: (0, 0))]
        args += [o_f, z, ng]
    kern = functools.partial(_dn_kernel, block=block, reverse=reverse, n_prompt=n_prompt,
                             seq_len=seq_len, nblk=nblk)
    return pl.pallas_call(
        kern, grid=(nblk,), in_specs=in_specs, out_specs=tok(512),
        out_shape=jax.ShapeDtypeStruct((n, 512), BF16 if reverse else F32),
        scratch_shapes=[pltpu.VMEM((DN_HEADS, DN_DK, DN_DK), F32)],
        compiler_params=_cparams(("arbitrary",)),
        name="deltanet_bwd" if reverse else "deltanet_fwd")(*args)


def _route_kernel(x_ref, ya_ref, yb_ref, wo_ref, g2_ref, rwh_ref, rwl_ref, rb_ref, tri_ref,
                  x2_ref, xn_ref, info_ref, cnt_ref, *, tile):
    half = wo_ref.shape[0] // 2
    x2 = x_ref[...] + _dot(ya_ref[...], wo_ref[0:half, :]) + _dot(yb_ref[...], wo_ref[half:, :])
    x2_ref[...] = x2
    xn = _rms_rows(x2, g2_ref[...])
    xn_ref[...] = xn
    xh = xn.astype(BF16)
    xl = (xn - xh.astype(F32)).astype(BF16)
    logits = (_dot(xh, rwh_ref[...]) + _dot(xl, rwh_ref[...]) + _dot(xh, rwl_ref[...])
              + rb_ref[...])
    lane = lax.broadcasted_iota(jnp.int32, (tile, LANES), 1).astype(F32)
    neg = jnp.float32(-jnp.inf)

    def first_argmax(vals):
        m = jnp.max(vals, axis=-1, keepdims=True)
        idx = jnp.min(jnp.where(vals == m, lane, float(LANES)), axis=-1, keepdims=True)
        return m, idx

    gl = jnp.where(lane < N_GROUPS, logits, neg)
    gmax, gidx = first_argmax(gl)
    g_w = 1.0 / jnp.sum(jnp.exp(gl - gmax), axis=-1, keepdims=True)
    lo = N_GROUPS + gidx * EXPERTS_PER_GROUP
    el = jnp.where((lane >= lo) & (lane < lo + EXPERTS_PER_GROUP), logits, neg)
    m1, i1 = first_argmax(el)
    m2, i2 = first_argmax(jnp.where(lane == i1, neg, el))
    r = jnp.exp(m2 - m1)
    gate1 = g_w / (1.0 + r)
    gate2 = g_w * r / (1.0 + r)
    e1 = i1 - N_GROUPS
    e2 = i2 - N_GROUPS
    onehot = ((lane == e1) | (lane == e2)).astype(BF16)
    before = _dot(tri_ref[...], onehot)
    rank1 = jnp.sum(jnp.where(lane == e1, before, 0.0), axis=-1, keepdims=True)
    rank2 = jnp.sum(jnp.where(lane == e2, before, 0.0), axis=-1, keepdims=True)
    cnt_ref[...] = jnp.sum(onehot.astype(F32), axis=0, keepdims=True).reshape(1, 1, LANES)
    info = jnp.where(lane == 0, e1,
           jnp.where(lane == 1, e2,
           jnp.where(lane == 2, rank1,
           jnp.where(lane == 3, rank2,
           jnp.where(lane == 4, gate1,
           jnp.where(lane == 5, gate2, 0.0))))))
    info_ref[...] = info


def _route_call(x2d, ya, yb, row_off, outs, wts, n_total):
    n = x2d.shape[0]
    tile = TOKEN_TILE
    nt = n // tile
    off = row_off // tile
    d = x2d.shape[1]
    const = lambda i: (0, 0)
    in_specs = [pl.BlockSpec((tile, d), lambda i: (i, 0)),
                pl.BlockSpec((tile, 512), lambda i: (i + off, 0)),
                pl.BlockSpec((tile, 512), lambda i: (i + off, 0))]
    in_specs += [pl.BlockSpec(w.shape, const) for w in wts]
    out_shape = [jax.ShapeDtypeStruct((n_total, d), F32), jax.ShapeDtypeStruct((n_total, d), F32),
                 jax.ShapeDtypeStruct((n_total, LANES), F32),
                 jax.ShapeDtypeStruct((n_total // tile, 1, LANES), F32)]
    out_specs = [pl.BlockSpec((tile, d), lambda i: (i + off, 0)),
                 pl.BlockSpec((tile, d), lambda i: (i + off, 0)),
                 pl.BlockSpec((tile, LANES), lambda i: (i + off, 0)),
                 pl.BlockSpec((1, 1, LANES), lambda i: (i + off, 0, 0))]
    kern = functools.partial(_route_kernel, tile=tile)
    args = [x2d, ya, yb] + list(wts)
    aliases = {}
    if outs is not None:
        in_specs += [pl.BlockSpec(memory_space=pl.ANY)] * len(outs)
        aliases = {len(args) + j: j for j in range(len(outs))}
        args += list(outs)
        body = lambda *refs: kern(*refs[:len(refs) - 4 - len(outs)], *refs[len(refs) - 4:])
    else:
        body = kern
    return pl.pallas_call(
        body, grid=(nt,), in_specs=in_specs, out_specs=out_specs, out_shape=out_shape,
        input_output_aliases=aliases, compiler_params=_cparams(("arbitrary",)),
        name="oproj_router")(*args)


def _dispatch_kernel(dest_ref, xn_ref, buf_in_ref, buf_ref, sem, *, tile):
    del buf_in_ref

    def issue(t, carry):
        for kk in range(2):
            pltpu.make_async_copy(xn_ref.at[pl.ds(t, 1), :],
                                  buf_ref.at[pl.ds(dest_ref[2 * t + kk], 1), :], sem).start()
        return carry

    lax.fori_loop(0, tile, issue, 0)
    for _ in range(2):
        pltpu.make_async_copy(xn_ref, buf_ref.at[pl.ds(0, tile), :], sem).wait()


def _dispatch_call(dest, xn, xbuf):
    n, d = xn.shape
    tile = TOKEN_TILE
    kern = functools.partial(_dispatch_kernel, tile=tile)
    return pl.pallas_call(
        kern, grid=(n // tile,),
        in_specs=[pl.BlockSpec((2 * tile,), lambda i: (i,), memory_space=pltpu.SMEM),
                  pl.BlockSpec((tile, d), lambda i: (i, 0)),
                  pl.BlockSpec(memory_space=pl.ANY)],
        out_specs=pl.BlockSpec(memory_space=pl.ANY),
        out_shape=jax.ShapeDtypeStruct(xbuf.shape, xbuf.dtype),
        scratch_shapes=[pltpu.SemaphoreType.DMA(())],
        input_output_aliases={2: 0}, compiler_params=_cparams(("arbitrary",)),
        name="moe_dispatch")(dest, xn, xbuf)


def _expert_kernel(te_ref, nu_ref, x_ref, w1_ref, w3_ref, w2_ref, y_ref):
    j = pl.program_id(0)

    @pl.when(j < nu_ref[0])
    def _():
        xb = x_ref[...].astype(BF16)
        h1 = _dot(xb, w1_ref[0])
        h3 = _dot(xb, w3_ref[0])
        hdn = (h1 * _sigmoid(h1) * h3).astype(BF16)
        y_ref[...] = _dot(hdn, w2_ref[0])

    @pl.when(j >= nu_ref[0])
    def _():
        y_ref[...] = jnp.zeros_like(y_ref)


def _expert_call(tile_expert, n_used, xbuf, w1, w3, w2):
    rows, d = xbuf.shape
    tm = EXPERT_TILE
    ff = w1.shape[2]
    grid_spec = pltpu.PrefetchScalarGridSpec(
        num_scalar_prefetch=2, grid=(rows // tm,),
        in_specs=[pl.BlockSpec((tm, d), lambda j, te, nu: (j, 0)),
                  pl.BlockSpec((1, d, ff), lambda j, te, nu: (te[j], 0, 0)),
                  pl.BlockSpec((1, d, ff), lambda j, te, nu: (te[j], 0, 0)),
                  pl.BlockSpec((1, ff, d), lambda j, te, nu: (te[j], 0, 0))],
        out_specs=pl.BlockSpec((tm, d), lambda j, te, nu: (j, 0)))
    return pl.pallas_call(
        _expert_kernel, grid_spec=grid_spec,
        out_shape=jax.ShapeDtypeStruct((rows, d), F32),
        compiler_params=_cparams(("arbitrary",)), name="moe_experts")(
            tile_expert, n_used, xbuf, w1, w3, w2)


def _combine_kernel(dest_ref, x2_ref, info_ref, gf_ref, ybuf_ref, out_ref, rows_ref, sem, *, tile):
    def issue(t, carry):
        for kk in range(2):
            pltpu.make_async_copy(ybuf_ref.at[pl.ds(dest_ref[2 * t + kk], 1), :],
                                  rows_ref.at[kk, pl.ds(t, 1), :], sem).start()
        return carry

    lax.fori_loop(0, tile, issue, 0)
    for kk in range(2):
        pltpu.make_async_copy(ybuf_ref.at[pl.ds(0, tile), :], rows_ref.at[kk], sem).wait()
    info = info_ref[...]
    y = x2_ref[...] + info[:, 4:5] * rows_ref[0] + info[:, 5:6] * rows_ref[1]
    out_ref[...] = _rms_rows(y, gf_ref[...])


def _combine_call(dest, x2, info, gf, ybuf, row_off, n):
    d = x2.shape[1]
    tile = TOKEN_TILE
    off = row_off // tile
    kern = functools.partial(_combine_kernel, tile=tile)
    return pl.pallas_call(
        kern, grid=(n // tile,),
        in_specs=[pl.BlockSpec((2 * tile,), lambda i: (i + off,), memory_space=pltpu.SMEM),
                  pl.BlockSpec((tile, d), lambda i: (i + off, 0)),
                  pl.BlockSpec((tile, LANES), lambda i: (i + off, 0)),
                  pl.BlockSpec(gf.shape, lambda i: (0, 0)),
                  pl.BlockSpec(memory_space=pl.ANY)],
        out_specs=pl.BlockSpec((tile, d), lambda i: (i, 0)),
        out_shape=jax.ShapeDtypeStruct((n, d), F32),
        scratch_shapes=[pltpu.VMEM((2, tile, d), F32), pltpu.SemaphoreType.DMA(())],
        compiler_params=_cparams(("arbitrary",)), name="moe_combine")(dest, x2, info, gf, ybuf)


def _layer(xp, xs, seq_p, seq_s, norm1_g, w_in, conv_a_w, conv_a_norm_g, dn_conv_w, dn_a_log,
           dn_dt_bias, dn_norm_g, w_o, norm2_g, rg_w, rg_b, re_w, re_b, w1, w3, w2, final_g):
    n_p, d = xp.shape
    n_s = xs.shape[0]
    n = n_p + n_s
    row = lambda a: a.reshape(1, -1).astype(F32)

    n_ab = len(_GB_COLS)
    w_main = jnp.concatenate([w_in[:, :7 * CONV_WIDTH], w_in[:, 7 * CONV_WIDTH + _GB_COLS]], axis=1)
    w_main = jnp.pad(w_main, ((0, 0), (0, LANES - n_ab))).astype(BF16)
    cw =jnp.concatenate([conv_a_w, dn_conv_w], axis=1).astype(F32)
    grp = jnp.arange(CONV_WIDTH) // (CONV_WIDTH // CONV_GROUPS)
    gmat = ((grp[:, None] == grp[None, :]).astype(F32) / (CONV_WIDTH // CONV_GROUPS)).astype(BF16)
    is_a = (_GB_COLS < 2 * DN_HEADS)
    a_idx = jnp.where(is_a, _GB_COLS, 0)
    padl = lambda a: jnp.pad(a.reshape(1, -1).astype(F32), ((0, 0), (0, LANES - n_ab)))
    nea = padl(jnp.where(is_a, -jnp.exp(dn_a_log.astype(F32)).reshape(-1)[a_idx], 0.0))
    dtb = padl(jnp.where(is_a, dn_dt_bias.astype(F32).reshape(-1)[a_idx], 0.0))
    proj_w = [row(norm1_g), w_main, cw, row(conv_a_norm_g), gmat, nea, dtb]

    outs = _proj_call(xp, seq_p, 0, None, proj_w, n)
    outs = _proj_call(xs, seq_s, n_p, outs, proj_w, n)
    ya, q, k, v, z, gb = outs

    o_f = _dn_call(q, k, v, gb, None, False, n_p, seq_s)
    yb = _dn_call(q, k, v, gb, (o_f, z, row(dn_norm_g)), True, n_p, seq_s)

    rw = jnp.pad(jnp.concatenate([rg_w, re_w], axis=1).astype(F32),
                 ((0, 0), (0, LANES - N_GROUPS - N_EXPERTS)))
    rwh = rw.astype(BF16)
    rwl = (rw - rwh.astype(F32)).astype(BF16)
    rb = jnp.pad(jnp.concatenate([rg_b, re_b]).reshape(1, -1).astype(F32),
                 ((0, 0), (0, LANES - N_GROUPS - N_EXPERTS)))
    tile = TOKEN_TILE
    tri = (jnp.arange(tile)[:, None] > jnp.arange(tile)[None, :]).astype(BF16)
    route_w = [w_o.astype(BF16), row(norm2_g), rwh, rwl, rb, tri]
    routs = _route_call(xp, ya, yb, 0, None, route_w, n)
    routs = _route_call(xs, ya, yb, n_p, routs, route_w, n)
    x2, xn, info, cnt = routs

    tm = EXPERT_TILE
    counts_t = cnt[:, 0, :N_EXPERTS].astype(jnp.int32)
    base_t = jnp.cumsum(counts_t, axis=0) - counts_t
    counts = jnp.sum(counts_t, axis=0)
    pcounts = (counts + tm - 1) // tm * tm
    pends = jnp.cumsum(pcounts)
    table = (pends - pcounts)[None, :] + base_t
    e12 = info[:, 0:2].astype(jnp.int32)
    rank12 = info[:, 2:4].astype(jnp.int32)
    table_tok = jnp.broadcast_to(table[:, None, :], (n // tile, tile, N_EXPERTS)).reshape(n, N_EXPERTS)
    eid = jnp.arange(N_EXPERTS, dtype=jnp.int32)[None, None, :]
    dest = (jnp.sum(jnp.where(e12[:, :, None] == eid, table_tok[:, None, :], 0), axis=-1)
            + rank12).reshape(-1)
    n_tiles = (2 * n + N_EXPERTS * (tm - 1) + tm - 1) // tm
    tile_start = jnp.arange(n_tiles, dtype=jnp.int32) * tm
    tile_expert = jnp.minimum(jnp.sum(pends[None, :] <= tile_start[:, None], axis=1),
                              N_EXPERTS - 1).astype(jnp.int32)
    n_used = (pends[-1:] // tm).astype(jnp.int32)

    xbuf = _dispatch_call(dest, xn, jnp.zeros((n_tiles * tm, d), F32))
    ybuf = _expert_call(tile_expert, n_used, xbuf, w1.astype(BF16), w3.astype(BF16),
                        w2.astype(BF16))
    gf = row(final_g)
    y_p = _combine_call(dest, x2, info, gf, ybuf, 0, n_p)
    y_s = _combine_call(dest, x2, info, gf, ybuf, n_p, n_s)
    return y_p, y_s


def kernel(x_prompt, x_sample, norm1_g, w_in, conv_a_w, conv_a_norm_g, dn_conv_w, dn_a_log,
           dn_dt_bias, dn_norm_g, w_o, norm2_g, router_group_w, router_group_b, router_expert_w,
           router_expert_b, w1, w3, w2, final_norm_g):
    assert norm1_g.shape[0] == 1, "single-layer trunk"
    bp, sp, d = x_prompt.shape
    bs, ss, _ = x_sample.shape
    assert bp == 1 and sp % max(TOKEN_TILE, DN_BLOCK) == 0 and ss % max(TOKEN_TILE, DN_BLOCK) == 0
    y_p, y_s = _layer(
        x_prompt.reshape(bp * sp, d), x_sample.reshape(bs * ss, d), sp, ss,
        norm1_g[0], w_in[0], conv_a_w[0], conv_a_norm_g[0], dn_conv_w[0], dn_a_log[0],
        dn_dt_bias[0], dn_norm_g[0], w_o[0], norm2_g[0], router_group_w[0], router_group_b[0],
        router_expert_w[0], router_expert_b[0], w1[0], w3[0], w2[0], final_norm_g)
    return y_p.reshape(bp, sp, d), y_s.reshape(bs, ss, d)
```

```python
import functools

import jax
import jax.numpy as jnp
import numpy as np
from jax import lax
from jax.experimental import pallas as pl
from jax.experimental.pallas import tpu as pltpu

F32 = jnp.float32
BF16 = jnp.bfloat16
U32 = jnp.uint32

EPS = 1e-6
CONV_WIDTH = 512
CONV_GROUPS = 8
DN_HEADS = 4
DN_DK = 128
DN_V = 512
CHUNK = 64
N_GROUPS = 4
EXPERTS_PER_GROUP = 8
N_EXPERTS = N_GROUPS * EXPERTS_PER_GROUP
EXPERT_FF = 512
LANES = 128
HALO = 8

TOKEN_TILE = 256
DN_BLOCK = 256
EXPERT_TILE = 256
DMA_UNROLL = 16
VMEM_LIMIT = 56 * 1024 * 1024

_GB_COLS = np.concatenate([np.arange(4 * DN_HEADS), np.arange(2 * DN_HEADS), np.arange(2 * DN_HEADS)])


def _cparams(sem):
    return pltpu.CompilerParams(dimension_semantics=sem, vmem_limit_bytes=VMEM_LIMIT)


def _sigmoid(x):
    return 1.0 / (1.0 + jnp.exp(-x))


def _rms_rows(x, g):
    return x * lax.rsqrt(jnp.mean(x * x, axis=-1, keepdims=True) + EPS) * g


def _dot(a, b):
    return jnp.dot(a, b, preferred_element_type=F32)


def _pack_pairs(x):
    half = x.shape[1] // 2
    bits = lambda a: lax.bitcast_convert_type(a.astype(BF16).astype(F32), U32)
    return (bits(x[:, half:]) & jnp.uint32(0xFFFF0000)) | (bits(x[:, :half]) >> 16)


def _unpack_pairs(w):
    lo = lax.bitcast_convert_type(w << 16, F32)
    hi = lax.bitcast_convert_type(w & jnp.uint32(0xFFFF0000), F32)
    return lo, hi


def _two_source_specs(block_rows, width, n_p, n_s, block_of_tile):
    ntp = n_p // TOKEN_TILE
    nbp = n_p // block_rows
    nbs = n_s // block_rows
    clamp = lambda b, nb: jnp.clip(b, 0, nb - 1)
    sp = pl.BlockSpec((block_rows, width), lambda i: (clamp(block_of_tile(jnp.minimum(i, ntp - 1)), nbp), 0))
    ss = pl.BlockSpec((block_rows, width), lambda i: (clamp(block_of_tile(jnp.maximum(i - ntp, 0)), nbs), 0))
    return sp, ss


def _proj_kernel(xp_ref, xpp_ref, xpn_ref, xs_ref, xsp_ref, xsn_ref,
                 g1_ref, w_ref, cw_ref, cng_ref, gmat_ref, nea_ref, dtb_ref,
                 ya_ref, q_ref, k_ref, v_ref, z_ref, gb_ref,
                 hs_ref, su_ref, *, tile, ntp, seq_p, seq_s):
    i = pl.program_id(0)
    in_p = i < ntp
    tok0 = jnp.where(in_p, i, i - ntp) * tile
    seq_len = jnp.where(in_p, seq_p, seq_s)
    not_start = (tok0 % seq_len != 0).astype(F32)
    not_end = ((tok0 + tile) % seq_len != 0).astype(F32)
    g1 = g1_ref[...]
    pick = lambda a, b: jnp.where(in_p, a[...], b[...])
    hs_ref[0:HALO, :] = _rms_rows(pick(xpp_ref, xsp_ref), g1)
    hs_ref[HALO:HALO + tile, :] = _rms_rows(pick(xp_ref, xs_ref), g1)
    hs_ref[HALO + tile:, :] = _rms_rows(pick(xpn_ref, xsn_ref), g1)
    hb = hs_ref[...].astype(BF16)
    proj = _dot(hb, w_ref[...])

    rows = lax.broadcasted_iota(jnp.int32, (tile + 2 * HALO, 1), 0)
    keep = jnp.where(rows < HALO, not_start, jnp.where(rows >= HALO + tile, not_end, 1.0))
    cw = CONV_WIDTH
    su_ref[:, 0:cw] = proj[:, cw:2 * cw] * proj[:, 2 * cw:3 * cw] * keep
    su_ref[:, cw:] = proj[:, 3 * cw:6 * cw] * keep
    conv = (su_ref[HALO - 1:HALO - 1 + tile, :] * cw_ref[0:1, :]
            + su_ref[HALO:HALO + tile, :] * cw_ref[1:2, :]
            + su_ref[HALO + 1:HALO + 1 + tile, :] * cw_ref[2:3, :])

    main = proj[HALO:HALO + tile, :]
    ya = main[:, 0:cw] * conv[:, 0:cw]
    sq = ya * ya
    sq_hi = sq.astype(BF16)
    sq_lo = (sq - sq_hi.astype(F32)).astype(BF16)
    ms = _dot(sq_hi, gmat_ref[...]) + _dot(sq_lo, gmat_ref[...])
    ya_ref[...] = (ya * lax.rsqrt(ms + EPS) * cng_ref[...]).astype(ya_ref.dtype)

    qkv = conv[:, cw:]
    qkv = qkv * _sigmoid(qkv)
    for h in range(DN_HEADS):
        sl = slice(h * DN_DK, (h + 1) * DN_DK)
        qh = qkv[:, sl]
        qh = qh * (lax.rsqrt(jnp.sum(qh * qh, axis=-1, keepdims=True) + EPS) * (DN_DK ** -0.5))
        q_ref[:, sl] = qh.astype(q_ref.dtype)
        kh = qkv[:, 512 + h * DN_DK:512 + (h + 1) * DN_DK]
        kh = kh * lax.rsqrt(jnp.sum(kh * kh, axis=-1, keepdims=True) + EPS)
        k_ref[:, sl] = kh.astype(k_ref.dtype)
    v_ref[...] = qkv[:, 1024:1536].astype(v_ref.dtype)
    z_ref[...] = main[:, 6 * cw:7 * cw].astype(z_ref.dtype)

    ab = main[:, 7 * cw:]
    xs = ab + dtb_ref[...]
    softplus = jnp.maximum(xs, 0.0) + jnp.log(1.0 + jnp.exp(-jnp.abs(xs)))
    g = nea_ref[...] * softplus
    beta = _sigmoid(ab)
    r = lax.broadcasted_iota(jnp.int32, (tile, LANES), 0) % CHUNK
    pre = g
    suf = g
    s = 1
    while s < CHUNK:
        pre = pre + jnp.where(r >= s, pltpu.roll(pre, s, axis=0), 0.0)
        suf = suf + jnp.where(r < CHUNK - s, pltpu.roll(suf, tile - s, axis=0), 0.0)
        s *= 2
    lane = lax.broadcasted_iota(jnp.int32, (tile, LANES), 1) // DN_HEADS
    gb_ref[...] = jnp.where(lane == 0, pre,
                  jnp.where(lane == 1, suf,
                  jnp.where(lane < 4, beta,
                  jnp.where(lane == 4, suf - g,
                  jnp.where(lane == 5, pre - g, pre + suf - g)))))


def _proj_call(xp, xs, seq_p, seq_s, wts):
    n_p, d = xp.shape
    n_s = xs.shape[0]
    n = n_p + n_s
    tile = TOKEN_TILE
    tb = tile // HALO
    kern = functools.partial(_proj_kernel, tile=tile, ntp=n_p // tile, seq_p=seq_p, seq_s=seq_s)
    const = lambda i: (0, 0)
    main_p, main_s = _two_source_specs(tile, d, n_p, n_s, lambda t: t)
    prev_p, prev_s = _two_source_specs(HALO, d, n_p, n_s, lambda t: t * tb - 1)
    next_p, next_s = _two_source_specs(HALO, d, n_p, n_s, lambda t: (t + 1) * tb)
    in_specs = [main_p, prev_p, next_p, main_s, prev_s, next_s]
    in_specs += [pl.BlockSpec(w.shape, const) for w in wts]
    widths = (512, 512, 512, 512, 512, LANES)
    dtypes = (BF16, BF16, BF16, BF16, BF16, F32)
    return pl.pallas_call(
        kern, grid=(n // tile,), in_specs=in_specs,
        out_specs=[pl.BlockSpec((tile, w), lambda i: (i, 0)) for w in widths],
        out_shape=[jax.ShapeDtypeStruct((n, w), dt) for w, dt in zip(widths, dtypes)],
        scratch_shapes=[pltpu.VMEM((tile + 2 * HALO, d), F32),
                        pltpu.VMEM((tile + 2 * HALO, 4 * CONV_WIDTH), F32)],
        compiler_params=_cparams(("arbitrary",)), name="proj_conv")(xp, xp, xp, xs, xs, xs, *wts)


def _dn_kernel(*refs, block, reverse, n_prompt, seq_len, nblk):
    if reverse:
        (q_ref, k_ref, v_ref, gb_ref, of_ref, z_ref, ng_ref, o_ref, s_ref) = refs
    else:
        (q_ref, k_ref, v_ref, gb_ref, o_ref, s_ref) = refs
    i = pl.program_id(0)
    b = nblk - 1 - i if reverse else i
    if reverse:
        edge = (b + 1) * block
        reset = (edge == n_prompt) | ((edge > n_prompt) & ((edge - n_prompt) % seq_len == 0))
    else:
        edge = b * block
        reset = (edge == 0) | ((edge >= n_prompt) & ((edge - n_prompt) % seq_len == 0))

    @pl.when(reset)
    def _():
        s_ref[...] = jnp.zeros_like(s_ref)

    nchunk = block // CHUNK
    c = CHUNK
    heads = range(DN_HEADS)
    ri = lax.broadcasted_iota(jnp.int32, (block, block), 0)
    ci = lax.broadcasted_iota(jnp.int32, (block, block), 1)
    same = (ri // c) == (ci // c)
    incl = same & ((ri <= ci) if reverse else (ri >= ci))
    strict = same & ((ri < ci) if reverse else (ri > ci))
    pr = lax.broadcasted_iota(jnp.int32, (c, block), 0)
    pc = lax.broadcasted_iota(jnp.int32, (c, block), 1)
    eye_p = (pr == pc % c).astype(F32)
    lane_chunk = pc // c

    def fold(bd):
        out = bd[0:c]
        for r in range(1, nchunk):
            out = out + bd[r * c:(r + 1) * c]
        return out

    def unfold(p):
        zero = jnp.zeros_like(p)
        return jnp.concatenate([jnp.where(lane_chunk == r, p, zero) for r in range(nchunk)], axis=0)

    grp = lambda g, h: slice(g * DN_HEADS + h, g * DN_HEADS + h + 1)
    g_cum, g_beta, g_rest, g_tot = (1, 3, 5, 7) if reverse else (0, 2, 4, 6)
    gbv = gb_ref[...]
    gbt = gbv.T
    sls = [slice(h * DN_DK, (h + 1) * DN_DK) for h in heads]
    kb = [k_ref[:, sl] for sl in sls]
    qb = [q_ref[:, sl] for sl in sls]
    gcol = [gbv[:, grp(g_cum, h)] for h in heads]
    beta = [gbv[:, grp(g_beta, h)] for h in heads]
    kq = [lax.dot_general(jnp.concatenate([kb[h], qb[h]], axis=0), kb[h],
                          (((1,), (1,)), ((), ())), preferred_element_type=F32) for h in heads]
    decay = [jnp.exp(jnp.minimum(gcol[h] - gbt[grp(g_cum, h), :], 0.0)) for h in heads]
    bdl = [jnp.where(strict, kq[h][:block] * decay[h], 0.0) * beta[h] for h in heads]
    bda = [jnp.where(incl, kq[h][block:] * decay[h], 0.0).astype(BF16) for h in heads]

    lp = [fold(bdl[h]) for h in heads]
    xp = [eye_p - lp[h] for h in heads]
    bd = [bdl[h].astype(BF16) for h in heads]
    p = 1
    while p < c:
        if p == 1:
            res = [_dot(lp[h].astype(BF16), bd[h]) for h in heads]
            lp = res
        elif 2 * p < c:
            res = [_dot(jnp.concatenate([lp[h], xp[h]], axis=0).astype(BF16), bd[h]) for h in heads]
            lp = [r[:c] for r in res]
            xp = [xp[h] + res[h][c:] for h in heads]
        else:
            xp = [xp[h] + _dot(xp[h].astype(BF16), bd[h]) for h in heads]
        p *= 2
        if p < c:
            bd = [unfold(lp[h].astype(BF16)) for h in heads]
    bdt = [unfold(xp[h].astype(BF16)) for h in heads]

    egc = [jnp.exp(gcol[h]) for h in heads]
    kf = [kb[h].astype(F32) for h in heads]
    rhs = [jnp.concatenate([kf[h] * (beta[h] * egc[h]), v_ref[:, sls[h]].astype(F32) * beta[h]],
                           axis=1).astype(BF16) for h in heads]
    wu = [_dot(bdt[h], rhs[h]) for h in heads]
    au = [_dot(bda[h], wu[h].astype(BF16)) for h in heads]
    qt = [qb[h].astype(F32) * egc[h] - au[h][:, :DN_DK] for h in heads]
    kd = [(kf[h] * jnp.exp(gbv[:, grp(g_rest, h)])).astype(BF16) for h in heads]
    wqt = [jnp.concatenate([wu[h][:, :DN_DK], qt[h]], axis=1).astype(BF16) for h in heads]

    s = [s_ref[h] for h in heads]
    order = range(nchunk - 1, -1, -1) if reverse else range(nchunk)
    for cc in order:
        rows = slice(cc * c, (cc + 1) * c)
        for h in heads:
            lhs = jnp.concatenate([wqt[h][rows, :DN_DK], wqt[h][rows, DN_DK:]], axis=0)
            wq = _dot(lhs, s[h].astype(BF16))
            vn = wu[h][rows, DN_DK:] - wq[:c]
            o = wq[c:] + au[h][rows, DN_DK:]
            tot = gbt[grp(g_tot, h), cc * c:cc * c + 1]
            s[h] = s[h] * jnp.exp(tot) + lax.dot_general(
                kd[h][rows], vn.astype(BF16), (((0,), (0,)), ((), ())), preferred_element_type=F32)
            if reverse:
                o = o + of_ref[rows, sls[h]]
                o = o * lax.rsqrt(jnp.mean(o * o, axis=-1, keepdims=True) + EPS) * ng_ref[...]
                zc = z_ref[rows, sls[h]].astype(F32)
                o = o * (zc * _sigmoid(zc))
            o_ref[rows, sls[h]] = o.astype(o_ref.dtype)
    for h in heads:
        s_ref[h] = s[h]


def _dn_call(q, k, v, gb, extra, reverse, n_prompt, seq_len):
    n = q.shape[0]
    block = DN_BLOCK
    nblk = n // block
    imap = (lambda i: (nblk - 1 - i, 0)) if reverse else (lambda i: (i, 0))
    tok = lambda w: pl.BlockSpec((block, w), imap)
    in_specs = [tok(512), tok(512), tok(512), tok(LANES)]
    args = [q, k, v, gb]
    if reverse:
        o_f, z, ng = extra
        in_specs += [tok(512), tok(512), pl.BlockSpec(ng.shape, lambda i: (0, 0))]
        args += [o_f, z, ng]
    kern = functools.partial(_dn_kernel, block=block, reverse=reverse, n_prompt=n_prompt,
                             seq_len=seq_len, nblk=nblk)
    return pl.pallas_call(
        kern, grid=(nblk,), in_specs=in_specs, out_specs=tok(512),
        out_shape=jax.ShapeDtypeStruct((n, 512), BF16 if reverse else F32),
        scratch_shapes=[pltpu.VMEM((DN_HEADS, DN_DK, DN_DK), F32)],
        compiler_params=_cparams(("arbitrary",)),
        name="deltanet_bwd" if reverse else "deltanet_fwd")(*args)


def _route_kernel(xp_ref, xs_ref, ya_ref, yb_ref, wo_ref, g2_ref, rwh_ref, rwl_ref, rb_ref, tri_ref,
                  x2_ref, xnp_ref, info_ref, cnt_ref, run_ref, *, tile, ntp):
    i = pl.program_id(0)

    @pl.when(i == 0)
    def _():
        run_ref[...] = jnp.zeros_like(run_ref)

    half = wo_ref.shape[0] // 2
    x = jnp.where(i < ntp, xp_ref[...], xs_ref[...])
    x2 = x + _dot(ya_ref[...], wo_ref[0:half, :]) + _dot(yb_ref[...], wo_ref[half:, :])
    x2_ref[...] = x2
    xn = _rms_rows(x2, g2_ref[...])
    xnp_ref[...] = _pack_pairs(xn)
    xh = xn.astype(BF16)
    xl = (xn - xh.astype(F32)).astype(BF16)
    logits = (_dot(xh, rwh_ref[...]) + _dot(xl, rwh_ref[...]) + _dot(xh, rwl_ref[...])
              + rb_ref[...])
    lane = lax.broadcasted_iota(jnp.int32, (tile, LANES), 1).astype(F32)
    neg = jnp.float32(-jnp.inf)

    def first_argmax(vals):
        m = jnp.max(vals, axis=-1, keepdims=True)
        idx = jnp.min(jnp.where(vals == m, lane, float(LANES)), axis=-1, keepdims=True)
        return m, idx

    gl = jnp.where(lane < N_GROUPS, logits, neg)
    gmax, gidx = first_argmax(gl)
    g_w = 1.0 / jnp.sum(jnp.exp(gl - gmax), axis=-1, keepdims=True)
    lo = N_GROUPS + gidx * EXPERTS_PER_GROUP
    el = jnp.where((lane >= lo) & (lane < lo + EXPERTS_PER_GROUP), logits, neg)
    m1, i1 = first_argmax(el)
    m2, i2 = first_argmax(jnp.where(lane == i1, neg, el))
    r = jnp.exp(m2 - m1)
    gate1 = g_w / (1.0 + r)
    gate2 = g_w * r / (1.0 + r)
    e1 = i1 - N_GROUPS
    e2 = i2 - N_GROUPS
    hit1 = lane == e1
    hit2 = lane == e2
    onehot = (hit1 | hit2).astype(BF16)
    before = _dot(tri_ref[...], onehot) + run_ref[...]
    pos1 = jnp.sum(jnp.where(hit1, before, 0.0), axis=-1, keepdims=True)
    pos2 = jnp.sum(jnp.where(hit2, before, 0.0), axis=-1, keepdims=True)
    run = run_ref[...] + jnp.sum(onehot.astype(F32), axis=0, keepdims=True)
    run_ref[...] = run
    cnt_ref[...] = run
    info = jnp.where(lane == 0, e1,
           jnp.where(lane == 1, e2,
           jnp.where(lane == 2, gate1,
           jnp.where(lane == 3, gate2,
           jnp.where(lane == 4, pos1,
           jnp.where(lane == 5, pos2, 0.0))))))
    info_ref[...] = info


def _route_call(xp, xs, ya, yb, wts):
    n_p, d = xp.shape
    n_s = xs.shape[0]
    n = n_p + n_s
    tile = TOKEN_TILE
    const = lambda i: (0, 0)
    tok = lambda w: pl.BlockSpec((tile, w), lambda i: (i, 0))
    main_p, main_s = _two_source_specs(tile, d, n_p, n_s, lambda t: t)
    in_specs = [main_p, main_s, tok(512), tok(512)] + [pl.BlockSpec(w.shape, const) for w in wts]
    out_shape = [jax.ShapeDtypeStruct((n, d), F32), jax.ShapeDtypeStruct((n, d // 2), U32),
                 jax.ShapeDtypeStruct((n, LANES), F32), jax.ShapeDtypeStruct((1, LANES), F32)]
    out_specs = [tok(d), tok(d // 2), tok(LANES), pl.BlockSpec((1, LANES), const)]
    kern = functools.partial(_route_kernel, tile=tile, ntp=n_p // tile)
    return pl.pallas_call(
        kern, grid=(n // tile,), in_specs=in_specs, out_specs=out_specs, out_shape=out_shape,
        scratch_shapes=[pltpu.VMEM((1, LANES), F32)],
        compiler_params=_cparams(("arbitrary",)), name="oproj_router")(xp, xs, ya, yb, *wts)


def _dispatch_kernel(pad0_ref, npad_ref, dest_ref, xn_ref, buf_ref, zero_ref, sem, *, tile):
    i = pl.program_id(0)

    def issue(t, carry):
        for kk in range(2):
            pltpu.make_async_copy(xn_ref.at[pl.ds(t, 1), :],
                                  buf_ref.at[pl.ds(dest_ref[2 * t + kk], 1), :], sem).start()
        return carry

    lax.fori_loop(0, tile, issue, 0, unroll=DMA_UNROLL)
    for _ in range(2):
        pltpu.make_async_copy(xn_ref, buf_ref.at[pl.ds(0, tile), :], sem).wait()

    @pl.when(i == pl.num_programs(0) - 1)
    def _():
        zero_ref[...] = jnp.zeros_like(zero_ref)
        for e in range(N_EXPERTS):
            pad = lambda r: pltpu.make_async_copy(
                zero_ref.at[pl.ds(0, 1), :], buf_ref.at[pl.ds(pad0_ref[e] + r, 1), :], sem)
            lax.fori_loop(0, npad_ref[e], lambda r, c: (pad(r).start(), c)[1], 0)
            lax.fori_loop(0, npad_ref[e], lambda r, c: (pad(r).wait(), c)[1], 0)


def _dispatch_call(pad0, npad, dest, xnp, rows):
    n, dh = xnp.shape
    tile = TOKEN_TILE
    kern = functools.partial(_dispatch_kernel, tile=tile)
    grid_spec = pltpu.PrefetchScalarGridSpec(
        num_scalar_prefetch=2, grid=(n // tile,),
        in_specs=[pl.BlockSpec((2 * tile,), lambda i, a, b: (i,), memory_space=pltpu.SMEM),
                  pl.BlockSpec((tile, dh), lambda i, a, b: (i, 0))],
        out_specs=pl.BlockSpec(memory_space=pl.ANY),
        scratch_shapes=[pltpu.VMEM((HALO, dh), U32), pltpu.SemaphoreType.DMA(())])
    return pl.pallas_call(
        kern, grid_spec=grid_spec, out_shape=jax.ShapeDtypeStruct((rows, dh), U32),
        compiler_params=_cparams(("arbitrary",)), name="moe_dispatch")(pad0, npad, dest, xnp)


def _expert_kernel(te_ref, nu_ref, x_ref, w1_ref, w3_ref, w2_ref, y_ref):
    del te_ref
    j = pl.program_id(0)

    @pl.when(j < nu_ref[0])
    def _():
        lo, hi = _unpack_pairs(x_ref[...])
        lo = lo.astype(BF16)
        hi = hi.astype(BF16)
        half = lo.shape[1]
        h1 = _dot(lo, w1_ref[0, 0:half, :]) + _dot(hi, w1_ref[0, half:, :])
        h3 = _dot(lo, w3_ref[0, 0:half, :]) + _dot(hi, w3_ref[0, half:, :])
        hdn = (h1 * _sigmoid(h1) * h3).astype(BF16)
        y_ref[...] = _pack_pairs(_dot(hdn, w2_ref[0]))


def _expert_call(tile_expert, n_used, xbuf, w1, w3, w2):
    rows, dh = xbuf.shape
    tm = EXPERT_TILE
    d, ff = w1.shape[1], w1.shape[2]
    row_blk = lambda j, te, nu: (jnp.minimum(j, nu[0] - 1), 0)
    wsel = lambda j, te, nu: (te[j], 0, 0)
    grid_spec = pltpu.PrefetchScalarGridSpec(
        num_scalar_prefetch=2, grid=(rows // tm,),
        in_specs=[pl.BlockSpec((tm, dh), row_blk),
                  pl.BlockSpec((1, d, ff), wsel), pl.BlockSpec((1, d, ff), wsel),
                  pl.BlockSpec((1, ff, d), wsel)],
        out_specs=pl.BlockSpec((tm, dh), row_blk))
    return pl.pallas_call(
        _expert_kernel, grid_spec=grid_spec, out_shape=jax.ShapeDtypeStruct((rows, dh), U32),
        compiler_params=_cparams(("arbitrary",)), name="moe_experts")(
            tile_expert, n_used, xbuf, w1, w3, w2)


def _combine_kernel(dest_ref, dnext_ref, x2_ref, info_ref, gf_ref, ybuf_ref, yp_ref, ys_ref,
                    rows_ref, sem, *, tile, ntp):
    i = pl.program_id(0)
    nt = pl.num_programs(0)
    slot = i % 2

    def gather(idx_ref, sl):
        def issue(t, carry):
            for kk in range(2):
                pltpu.make_async_copy(ybuf_ref.at[pl.ds(idx_ref[2 * t + kk], 1), :],
                                      rows_ref.at[sl, kk, pl.ds(t, 1), :], sem.at[sl]).start()
            return carry
        lax.fori_loop(0, tile, issue, 0, unroll=DMA_UNROLL)

    @pl.when(i == 0)
    def _():
        gather(dest_ref, 0)

    @pl.when(i + 1 < nt)
    def _():
        gather(dnext_ref, 1 - slot)

    for kk in range(2):
        pltpu.make_async_copy(ybuf_ref.at[pl.ds(0, tile), :], rows_ref.at[slot, kk],
                              sem.at[slot]).wait()
    info = info_ref[...]
    lo0, hi0 = _unpack_pairs(rows_ref[slot, 0])
    lo1, hi1 = _unpack_pairs(rows_ref[slot, 1])
    g1 = info[:, 2:3]
    g2 = info[:, 3:4]
    moe = jnp.concatenate([g1 * lo0 + g2 * lo1, g1 * hi0 + g2 * hi1], axis=1)
    out = _rms_rows(x2_ref[...] + moe, gf_ref[...])

    @pl.when(i < ntp)
    def _():
        yp_ref[...] = out

    @pl.when(i >= ntp)
    def _():
        ys_ref[...] = out


def _combine_call(dest, x2, info, gf, ybuf, n_p, n_s):
    n, d = x2.shape
    tile = TOKEN_TILE
    nt = n // tile
    ntp = n_p // tile
    kern = functools.partial(_combine_kernel, tile=tile, ntp=ntp)
    return pl.pallas_call(
        kern, grid=(nt,),
        in_specs=[pl.BlockSpec((2 * tile,), lambda i: (i,), memory_space=pltpu.SMEM),
                  pl.BlockSpec((2 * tile,), lambda i: (jnp.minimum(i + 1, nt - 1),),
                               memory_space=pltpu.SMEM),
                  pl.BlockSpec((tile, d), lambda i: (i, 0)),
                  pl.BlockSpec((tile, LANES), lambda i: (i, 0)),
                  pl.BlockSpec(gf.shape, lambda i: (0, 0)),
                  pl.BlockSpec(memory_space=pl.ANY)],
        out_specs=[pl.BlockSpec((tile, d), lambda i: (jnp.minimum(i, ntp - 1), 0)),
                   pl.BlockSpec((tile, d), lambda i: (jnp.maximum(i - ntp, 0), 0))],
        out_shape=[jax.ShapeDtypeStruct((n_p, d), F32), jax.ShapeDtypeStruct((n_s, d), F32)],
        scratch_shapes=[pltpu.VMEM((2, 2, tile, d // 2), U32), pltpu.SemaphoreType.DMA((2,))],
        compiler_params=_cparams(("arbitrary",)), name="moe_combine")(
            dest, dest, x2, info, gf, ybuf)


def _layer(xp, xs, seq_p, seq_s, norm1_g, w_in, conv_a_w, conv_a_norm_g, dn_conv_w, dn_a_log,
           dn_dt_bias, dn_norm_g, w_o, norm2_g, rg_w, rg_b, re_w, re_b, w1, w3, w2, final_g):
    n_p, d = xp.shape
    n_s = xs.shape[0]
    n = n_p + n_s
    row = lambda a: a.reshape(1, -1).astype(F32)

    n_ab = len(_GB_COLS)
    w_main = jnp.concatenate([w_in[:, :7 * CONV_WIDTH], w_in[:, 7 * CONV_WIDTH + _GB_COLS]], axis=1)
    w_main = jnp.pad(w_main, ((0, 0), (0, LANES - n_ab))).astype(BF16)
    cw = jnp.concatenate([conv_a_w, dn_conv_w], axis=1).astype(F32)
    grp = jnp.arange(CONV_WIDTH) // (CONV_WIDTH // CONV_GROUPS)
    gmat = ((grp[:, None] == grp[None, :]).astype(F32) / (CONV_WIDTH // CONV_GROUPS)).astype(BF16)
    is_a = (_GB_COLS < 2 * DN_HEADS)
    a_idx = jnp.where(is_a, _GB_COLS, 0)
    padl = lambda a: jnp.pad(a.reshape(1, -1).astype(F32), ((0, 0), (0, LANES - n_ab)))
    nea = padl(jnp.where(is_a, -jnp.exp(dn_a_log.astype(F32)).reshape(-1)[a_idx], 0.0))
    dtb = padl(jnp.where(is_a, dn_dt_bias.astype(F32).reshape(-1)[a_idx], 0.0))
    proj_w = [row(norm1_g), w_main, cw, row(conv_a_norm_g), gmat, nea, dtb]

    ya, q, k, v, z, gb = _proj_call(xp, xs, seq_p, seq_s, proj_w)

    o_f = _dn_call(q, k, v, gb, None, False, n_p, seq_s)
    yb = _dn_call(q, k, v, gb, (o_f, z, row(dn_norm_g)), True, n_p, seq_s)

    rw = jnp.pad(jnp.concatenate([rg_w, re_w], axis=1).astype(F32),
                 ((0, 0), (0, LANES - N_GROUPS - N_EXPERTS)))
    rwh = rw.astype(BF16)
    rwl = (rw - rwh.astype(F32)).astype(BF16)
    rb = jnp.pad(jnp.concatenate([rg_b, re_b]).reshape(1, -1).astype(F32),
                 ((0, 0), (0, LANES - N_GROUPS - N_EXPERTS)))
    tile = TOKEN_TILE
    tm = EXPERT_TILE
    tri = (jnp.arange(tile)[:, None] > jnp.arange(tile)[None, :]).astype(BF16)
    route_w = [w_o.astype(BF16), row(norm2_g), rwh, rwl, rb, tri]
    x2, xnp, info, cnt = _route_call(xp, xs, ya, yb, route_w)

    counts = cnt[0, :N_EXPERTS].astype(jnp.int32)
    pcounts = (counts + tm - 1) // tm * tm
    pends = jnp.cumsum(pcounts)
    pstarts = pends - pcounts
    eid = jnp.arange(N_EXPERTS, dtype=jnp.int32)
    e12 = info[:, 0:2].astype(jnp.int32)
    start12 = jnp.sum(jnp.where(e12[:, :, None] == eid, pstarts, 0), axis=-1)
    dest = (start12 + info[:, 4:6].astype(jnp.int32)).reshape(-1)
    n_tiles = 2 * n // tm + N_EXPERTS
    tile_start = jnp.arange(n_tiles, dtype=jnp.int32) * tm
    tile_expert = jnp.minimum(jnp.sum(pends[None, :] <= tile_start[:, None], axis=1),
                              N_EXPERTS - 1).astype(jnp.int32)
    n_used = (pends[-1:] // tm).astype(jnp.int32)

    xbuf = _dispatch_call(pstarts + counts, pcounts - counts, dest, xnp, n_tiles * tm)
    ybuf = _expert_call(tile_expert, n_used, xbuf, w1.astype(BF16), w3.astype(BF16),
                        w2.astype(BF16))
    return _combine_call(dest, x2, info, row(final_g), ybuf, n_p, n_s)


def kernel(x_prompt, x_sample, norm1_g, w_in, conv_a_w, conv_a_norm_g, dn_conv_w, dn_a_log,
           dn_dt_bias, dn_norm_g, w_o, norm2_g, router_group_w, router_group_b, router_expert_w,
           router_expert_b, w1, w3, w2, final_norm_g):
    assert norm1_g.shape[0] == 1, "single-layer trunk"
    bp, sp, d = x_prompt.shape
    bs, ss, _ = x_sample.shape
    assert bp == 1 and sp % max(TOKEN_TILE, DN_BLOCK) == 0 and ss % max(TOKEN_TILE, DN_BLOCK) == 0
    y_p, y_s = _layer(
        x_prompt.reshape(bp * sp, d), x_sample.reshape(bs * ss, d), sp, ss,
        norm1_g[0], w_in[0], conv_a_w[0], conv_a_norm_g[0], dn_conv_w[0], dn_a_log[0],
        dn_dt_bias[0], dn_norm_g[0], w_o[0], norm2_g[0], router_group_w[0], router_group_b[0],
        router_expert_w[0], router_expert_b[0], w1[0], w3[0], w2[0], final_norm_g)
    return y_p.reshape(bp, sp, d), y_s.reshape(bs, ss, d)
```

```python
import functools

import jax
import jax.numpy as jnp
import numpy as np
from jax import lax
from jax.experimental import pallas as pl
from jax.experimental.pallas import tpu as pltpu
from jax.experimental.pallas import tpu_sc as plsc

F32 = jnp.float32
BF16 = jnp.bfloat16
U32 = jnp.uint32

EPS = 1e-6
CONV_WIDTH = 512
CONV_GROUPS = 8
DN_HEADS = 4
DN_DK = 128
DN_V = 512
CHUNK = 64
N_GROUPS = 4
EXPERTS_PER_GROUP = 8
N_EXPERTS = N_GROUPS * EXPERTS_PER_GROUP
EXPERT_FF = 512
LANES = 128
HALO = 8

TOKEN_TILE = 256
DN_BLOCK = 256
EXPERT_TILE = 256
ROW_WORDS = 256
SC_WINDOW = 128
VMEM_LIMIT = 56 * 1024 * 1024

_GB_COLS = np.concatenate([np.arange(4 * DN_HEADS), np.arange(2 * DN_HEADS), np.arange(2 * DN_HEADS)])


def _cparams(sem):
    return pltpu.CompilerParams(dimension_semantics=sem, vmem_limit_bytes=VMEM_LIMIT)


def _sigmoid(x):
    return 1.0 / (1.0 + jnp.exp(-x))


def _rms_rows(x, g):
    return x * lax.rsqrt(jnp.mean(x * x, axis=-1, keepdims=True) + EPS) * g


def _dot(a, b):
    return jnp.dot(a, b, preferred_element_type=F32)


def _pack_pairs(x):
    half = x.shape[1] // 2
    bits = lambda a: lax.bitcast_convert_type(a.astype(BF16).astype(F32), U32)
    return (bits(x[:, half:]) & jnp.uint32(0xFFFF0000)) | (bits(x[:, :half]) >> 16)


def _unpack_pairs(w):
    lo = lax.bitcast_convert_type(w << 16, F32)
    hi = lax.bitcast_convert_type(w & jnp.uint32(0xFFFF0000), F32)
    return lo, hi


def _two_source_specs(block_rows, width, n_p, n_s, block_of_tile):
    ntp = n_p // TOKEN_TILE
    nbp = n_p // block_rows
    nbs = n_s // block_rows
    clamp = lambda b, nb: jnp.clip(b, 0, nb - 1)
    sp = pl.BlockSpec((block_rows, width), lambda i: (clamp(block_of_tile(jnp.minimum(i, ntp - 1)), nbp), 0))
    ss = pl.BlockSpec((block_rows, width), lambda i: (clamp(block_of_tile(jnp.maximum(i - ntp, 0)), nbs), 0))
    return sp, ss


def _proj_kernel(xp_ref, xpp_ref, xpn_ref, xs_ref, xsp_ref, xsn_ref,
                 g1_ref, w_ref, cw_ref, cng_ref, gmat_ref, nea_ref, dtb_ref,
                 ya_ref, q_ref, k_ref, v_ref, z_ref, gb_ref,
                 hs_ref, su_ref, *, tile, ntp, seq_p, seq_s):
    i = pl.program_id(0)
    in_p = i < ntp
    tok0 = jnp.where(in_p, i, i - ntp) * tile
    seq_len = jnp.where(in_p, seq_p, seq_s)
    not_start = (tok0 % seq_len != 0).astype(F32)
    not_end = ((tok0 + tile) % seq_len != 0).astype(F32)
    g1 = g1_ref[...]
    pick = lambda a, b: jnp.where(in_p, a[...], b[...])
    hs_ref[0:HALO, :] = _rms_rows(pick(xpp_ref, xsp_ref), g1)
    hs_ref[HALO:HALO + tile, :] = _rms_rows(pick(xp_ref, xs_ref), g1)
    hs_ref[HALO + tile:, :] = _rms_rows(pick(xpn_ref, xsn_ref), g1)
    hb = hs_ref[...].astype(BF16)
    proj = _dot(hb, w_ref[...])

    rows = lax.broadcasted_iota(jnp.int32, (tile + 2 * HALO, 1), 0)
    keep = jnp.where(rows < HALO, not_start, jnp.where(rows >= HALO + tile, not_end, 1.0))
    cw = CONV_WIDTH
    su_ref[:, 0:cw] = proj[:, cw:2 * cw] * proj[:, 2 * cw:3 * cw] * keep
    su_ref[:, cw:] = proj[:, 3 * cw:6 * cw] * keep
    conv = (su_ref[HALO - 1:HALO - 1 + tile, :] * cw_ref[0:1, :]
            + su_ref[HALO:HALO + tile, :] * cw_ref[1:2, :]
            + su_ref[HALO + 1:HALO + 1 + tile, :] * cw_ref[2:3, :])

    main = proj[HALO:HALO + tile, :]
    ya = main[:, 0:cw] * conv[:, 0:cw]
    sq = ya * ya
    sq_hi = sq.astype(BF16)
    sq_lo = (sq - sq_hi.astype(F32)).astype(BF16)
    ms = _dot(sq_hi, gmat_ref[...]) + _dot(sq_lo, gmat_ref[...])
    ya_ref[...] = (ya * lax.rsqrt(ms + EPS) * cng_ref[...]).astype(ya_ref.dtype)

    qkv = conv[:, cw:]
    qkv = qkv * _sigmoid(qkv)
    for h in range(DN_HEADS):
        sl = slice(h * DN_DK, (h + 1) * DN_DK)
        qh = qkv[:, sl]
        qh = qh * (lax.rsqrt(jnp.sum(qh * qh, axis=-1, keepdims=True) + EPS) * (DN_DK ** -0.5))
        q_ref[:, sl] = qh.astype(q_ref.dtype)
        kh = qkv[:, 512 + h * DN_DK:512 + (h + 1) * DN_DK]
        kh = kh * lax.rsqrt(jnp.sum(kh * kh, axis=-1, keepdims=True) + EPS)
        k_ref[:, sl] = kh.astype(k_ref.dtype)
    v_ref[...] = qkv[:, 1024:1536].astype(v_ref.dtype)
    z_ref[...] = main[:, 6 * cw:7 * cw].astype(z_ref.dtype)

    ab = main[:, 7 * cw:]
    xs = ab + dtb_ref[...]
    softplus = jnp.maximum(xs, 0.0) + jnp.log(1.0 + jnp.exp(-jnp.abs(xs)))
    g = nea_ref[...] * softplus
    beta = _sigmoid(ab)
    r = lax.broadcasted_iota(jnp.int32, (tile, LANES), 0) % CHUNK
    pre = g
    suf = g
    s = 1
    while s < CHUNK:
        pre = pre + jnp.where(r >= s, pltpu.roll(pre, s, axis=0), 0.0)
        suf = suf + jnp.where(r < CHUNK - s, pltpu.roll(suf, tile - s, axis=0), 0.0)
        s *= 2
    lane = lax.broadcasted_iota(jnp.int32, (tile, LANES), 1) // DN_HEADS
    gb_ref[...] = jnp.where(lane == 0, pre,
                  jnp.where(lane == 1, suf,
                  jnp.where(lane < 4, beta,
                  jnp.where(lane == 4, suf - g,
                  jnp.where(lane == 5, pre - g, pre + suf - g)))))


def _proj_call(xp, xs, seq_p, seq_s, wts):
    n_p, d = xp.shape
    n_s = xs.shape[0]
    n = n_p + n_s
    tile = TOKEN_TILE
    tb = tile // HALO
    kern = functools.partial(_proj_kernel, tile=tile, ntp=n_p // tile, seq_p=seq_p, seq_s=seq_s)
    const = lambda i: (0, 0)
    main_p, main_s = _two_source_specs(tile, d, n_p, n_s, lambda t: t)
    prev_p, prev_s = _two_source_specs(HALO, d, n_p, n_s, lambda t: t * tb - 1)
    next_p, next_s = _two_source_specs(HALO, d, n_p, n_s, lambda t: (t + 1) * tb)
    in_specs = [main_p, prev_p, next_p, main_s, prev_s, next_s]
    in_specs += [pl.BlockSpec(w.shape, const) for w in wts]
    widths = (512, 512, 512, 512, 512, LANES)
    dtypes = (BF16, BF16, BF16, BF16, BF16, F32)
    return pl.pallas_call(
        kern, grid=(n // tile,), in_specs=in_specs,
        out_specs=[pl.BlockSpec((tile, w), lambda i: (i, 0)) for w in widths],
        out_shape=[jax.ShapeDtypeStruct((n, w), dt) for w, dt in zip(widths, dtypes)],
        scratch_shapes=[pltpu.VMEM((tile + 2 * HALO, d), F32),
                        pltpu.VMEM((tile + 2 * HALO, 4 * CONV_WIDTH), F32)],
        compiler_params=_cparams(("arbitrary",)), name="proj_conv")(xp, xp, xp, xs, xs, xs, *wts)


def _dn_kernel(*refs, block, reverse, n_prompt, seq_len, nblk):
    if reverse:
        (q_ref, k_ref, v_ref, gb_ref, of_ref, z_ref, ng_ref, o_ref, s_ref) = refs
    else:
        (q_ref, k_ref, v_ref, gb_ref, o_ref, s_ref) = refs
    i = pl.program_id(0)
    b = nblk - 1 - i if reverse else i
    if reverse:
        edge = (b + 1) * block
        reset = (edge == n_prompt) | ((edge > n_prompt) & ((edge - n_prompt) % seq_len == 0))
    else:
        edge = b * block
        reset = (edge == 0) | ((edge >= n_prompt) & ((edge - n_prompt) % seq_len == 0))

    @pl.when(reset)
    def _():
        s_ref[...] = jnp.zeros_like(s_ref)

    nchunk = block // CHUNK
    c = CHUNK
    heads = range(DN_HEADS)
    ri = lax.broadcasted_iota(jnp.int32, (block, block), 0)
    ci = lax.broadcasted_iota(jnp.int32, (block, block), 1)
    same = (ri // c) == (ci // c)
    incl = same & ((ri <= ci) if reverse else (ri >= ci))
    strict = same & ((ri < ci) if reverse else (ri > ci))
    pr = lax.broadcasted_iota(jnp.int32, (c, block), 0)
    pc = lax.broadcasted_iota(jnp.int32, (c, block), 1)
    eye_p = (pr == pc % c).astype(F32)
    lane_chunk = pc // c

    def fold(bd):
        out = bd[0:c]
        for r in range(1, nchunk):
            out = out + bd[r * c:(r + 1) * c]
        return out

    def unfold(p):
        zero = jnp.zeros_like(p)
        return jnp.concatenate([jnp.where(lane_chunk == r, p, zero) for r in range(nchunk)], axis=0)

    grp = lambda g, h: slice(g * DN_HEADS + h, g * DN_HEADS + h + 1)
    g_cum, g_beta, g_rest, g_tot = (1, 3, 5, 7) if reverse else (0, 2, 4, 6)
    gbv = gb_ref[...]
    gbt = gbv.T
    sls = [slice(h * DN_DK, (h + 1) * DN_DK) for h in heads]
    kb = [k_ref[:, sl] for sl in sls]
    qb = [q_ref[:, sl] for sl in sls]
    gcol = [gbv[:, grp(g_cum, h)] for h in heads]
    beta = [gbv[:, grp(g_beta, h)] for h in heads]
    kq = [lax.dot_general(jnp.concatenate([kb[h], qb[h]], axis=0), kb[h],
                          (((1,), (1,)), ((), ())), preferred_element_type=F32) for h in heads]
    decay = [jnp.exp(jnp.minimum(gcol[h] - gbt[grp(g_cum, h), :], 0.0)) for h in heads]
    bdl = [jnp.where(strict, kq[h][:block] * decay[h], 0.0) * beta[h] for h in heads]
    bda = [jnp.where(incl, kq[h][block:] * decay[h], 0.0).astype(BF16) for h in heads]

    lp = [fold(bdl[h]) for h in heads]
    xp = [eye_p - lp[h] for h in heads]
    bd = [bdl[h].astype(BF16) for h in heads]
    p = 1
    while p < c:
        if p == 1:
            res = [_dot(lp[h].astype(BF16), bd[h]) for h in heads]
            lp = res
        elif 2 * p < c:
            res = [_dot(jnp.concatenate([lp[h], xp[h]], axis=0).astype(BF16), bd[h]) for h in heads]
            lp = [r[:c] for r in res]
            xp = [xp[h] + res[h][c:] for h in heads]
        else:
            xp = [xp[h] + _dot(xp[h].astype(BF16), bd[h]) for h in heads]
        p *= 2
        if p < c:
            bd = [unfold(lp[h].astype(BF16)) for h in heads]
    bdt = [unfold(xp[h].astype(BF16)) for h in heads]

    egc = [jnp.exp(gcol[h]) for h in heads]
    kf = [kb[h].astype(F32) for h in heads]
    rhs = [jnp.concatenate([kf[h] * (beta[h] * egc[h]), v_ref[:, sls[h]].astype(F32) * beta[h]],
                           axis=1).astype(BF16) for h in heads]
    wu = [_dot(bdt[h], rhs[h]) for h in heads]
    au = [_dot(bda[h], wu[h].astype(BF16)) for h in heads]
    qt = [qb[h].astype(F32) * egc[h] - au[h][:, :DN_DK] for h in heads]
    kd = [(kf[h] * jnp.exp(gbv[:, grp(g_rest, h)])).astype(BF16) for h in heads]
    wqt = [jnp.concatenate([wu[h][:, :DN_DK], qt[h]], axis=1).astype(BF16) for h in heads]

    s = [s_ref[h] for h in heads]
    order = range(nchunk - 1, -1, -1) if reverse else range(nchunk)
    for cc in order:
        rows = slice(cc * c, (cc + 1) * c)
        for h in heads:
            lhs = jnp.concatenate([wqt[h][rows, :DN_DK], wqt[h][rows, DN_DK:]], axis=0)
            wq = _dot(lhs, s[h].astype(BF16))
            vn = wu[h][rows, DN_DK:] - wq[:c]
            o = wq[c:] + au[h][rows, DN_DK:]
            tot = gbt[grp(g_tot, h), cc * c:cc * c + 1]
            s[h] = s[h] * jnp.exp(tot) + lax.dot_general(
                kd[h][rows], vn.astype(BF16), (((0,), (0,)), ((), ())), preferred_element_type=F32)
            if reverse:
                o = o + of_ref[rows, sls[h]]
                o = o * lax.rsqrt(jnp.mean(o * o, axis=-1, keepdims=True) + EPS) * ng_ref[...]
                zc = z_ref[rows, sls[h]].astype(F32)
                o = o * (zc * _sigmoid(zc))
            o_ref[rows, sls[h]] = o.astype(o_ref.dtype)
    for h in heads:
        s_ref[h] = s[h]


def _dn_call(q, k, v, gb, extra, reverse, n_prompt, seq_len):
    n = q.shape[0]
    block = DN_BLOCK
    nblk = n // block
    imap = (lambda i: (nblk - 1 - i, 0)) if reverse else (lambda i: (i, 0))
    tok = lambda w: pl.BlockSpec((block, w), imap)
    in_specs = [tok(512), tok(512), tok(512), tok(LANES)]
    args = [q, k, v, gb]
    if reverse:
        o_f, z, ng = extra
        in_specs += [tok(512), tok(512), pl.BlockSpec(ng.shape, lambda i: (0, 0))]
        args += [o_f, z, ng]
    kern = functools.partial(_dn_kernel, block=block, reverse=reverse, n_prompt=n_prompt,
                             seq_len=seq_len, nblk=nblk)
    return pl.pallas_call(
        kern, grid=(nblk,), in_specs=in_specs, out_specs=tok(512),
        out_shape=jax.ShapeDtypeStruct((n, 512), BF16 if reverse else F32),
        scratch_shapes=[pltpu.VMEM((DN_HEADS, DN_DK, DN_DK), F32)],
        compiler_params=_cparams(("arbitrary",)),
        name="deltanet_bwd" if reverse else "deltanet_fwd")(*args)


def _route_kernel(xp_ref, xs_ref, ya_ref, yb_ref, wo_ref, g2_ref, rwh_ref, rwl_ref, rb_ref, tri_ref,
                  x2_ref, xa_ref, xb_ref, info_ref, cnt_ref, run_ref, *, tile, ntp):
    i = pl.program_id(0)

    @pl.when(i == 0)
    def _():
        run_ref[...] = jnp.zeros_like(run_ref)

    half = wo_ref.shape[0] // 2
    x = jnp.where(i < ntp, xp_ref[...], xs_ref[...])
    x2 = x + _dot(ya_ref[...], wo_ref[0:half, :]) + _dot(yb_ref[...], wo_ref[half:, :])
    x2_ref[...] = x2
    xn = _rms_rows(x2, g2_ref[...])
    words = _pack_pairs(xn)
    xa_ref[...] = words[:, :ROW_WORDS]
    xb_ref[...] = words[:, ROW_WORDS:]
    xh = xn.astype(BF16)
    xl = (xn - xh.astype(F32)).astype(BF16)
    logits = (_dot(xh, rwh_ref[...]) + _dot(xl, rwh_ref[...]) + _dot(xh, rwl_ref[...])
              + rb_ref[...])
    lane = lax.broadcasted_iota(jnp.int32, (tile, LANES), 1).astype(F32)
    neg = jnp.float32(-jnp.inf)

    def first_argmax(vals):
        m = jnp.max(vals, axis=-1, keepdims=True)
        idx = jnp.min(jnp.where(vals == m, lane, float(LANES)), axis=-1, keepdims=True)
        return m, idx

    gl = jnp.where(lane < N_GROUPS, logits, neg)
    gmax, gidx = first_argmax(gl)
    g_w = 1.0 / jnp.sum(jnp.exp(gl - gmax), axis=-1, keepdims=True)
    lo = N_GROUPS + gidx * EXPERTS_PER_GROUP
    el = jnp.where((lane >= lo) & (lane < lo + EXPERTS_PER_GROUP), logits, neg)
    m1, i1 = first_argmax(el)
    m2, i2 = first_argmax(jnp.where(lane == i1, neg, el))
    r = jnp.exp(m2 - m1)
    gate1 = g_w / (1.0 + r)
    gate2 = g_w * r / (1.0 + r)
    e1 = i1 - N_GROUPS
    e2 = i2 - N_GROUPS
    hit1 = lane == e1
    hit2 = lane == e2
    onehot = (hit1 | hit2).astype(BF16)
    before = _dot(tri_ref[...], onehot) + run_ref[...]
    pos1 = jnp.sum(jnp.where(hit1, before, 0.0), axis=-1, keepdims=True)
    pos2 = jnp.sum(jnp.where(hit2, before, 0.0), axis=-1, keepdims=True)
    run = run_ref[...] + jnp.sum(onehot.astype(F32), axis=0, keepdims=True)
    run_ref[...] = run
    cnt_ref[...] = run
    info = jnp.where(lane == 0, e1,
           jnp.where(lane == 1, e2,
           jnp.where(lane == 2, gate1,
           jnp.where(lane == 3, gate2,
           jnp.where(lane == 4, pos1,
           jnp.where(lane == 5, pos2, 0.0))))))
    info_ref[...] = info


def _route_call(xp, xs, ya, yb, wts):
    n_p, d = xp.shape
    n_s = xs.shape[0]
    n = n_p + n_s
    tile = TOKEN_TILE
    const = lambda i: (0, 0)
    tok = lambda w: pl.BlockSpec((tile, w), lambda i: (i, 0))
    main_p, main_s = _two_source_specs(tile, d, n_p, n_s, lambda t: t)
    in_specs = [main_p, main_s, tok(512), tok(512)] + [pl.BlockSpec(w.shape, const) for w in wts]
    out_shape = [jax.ShapeDtypeStruct((n, d), F32), jax.ShapeDtypeStruct((n, ROW_WORDS), U32),
                 jax.ShapeDtypeStruct((n, ROW_WORDS), U32),
                 jax.ShapeDtypeStruct((n, LANES), F32), jax.ShapeDtypeStruct((1, LANES), F32)]
    out_specs = [tok(d), tok(ROW_WORDS), tok(ROW_WORDS), tok(LANES), pl.BlockSpec((1, LANES), const)]
    kern = functools.partial(_route_kernel, tile=tile, ntp=n_p // tile)
    return pl.pallas_call(
        kern, grid=(n // tile,), in_specs=in_specs, out_specs=out_specs, out_shape=out_shape,
        scratch_shapes=[pltpu.VMEM((1, LANES), F32)],
        compiler_params=_cparams(("arbitrary",)), name="oproj_router")(xp, xs, ya, yb, *wts)


def _sc_mesh():
    return plsc.VectorSubcoreMesh(core_axis_name="core", subcore_axis_name="subcore")


def _sc_scatter_rows(x, idx0, idx1, rows):
    n, w = x.shape

    @pl.kernel(out_type=jax.ShapeDtypeStruct((rows, w), x.dtype), mesh=_sc_mesh(), scratch_types=[])
    def scatter(x_hbm, i0_hbm, i1_hbm, o_hbm):
        def body(x_vmem, i0_vmem, i1_vmem):
            pltpu.sync_copy(x_vmem, o_hbm.at[i0_vmem.at[0]])
            pltpu.sync_copy(x_vmem, o_hbm.at[i1_vmem.at[0]])

        pltpu.emit_pipeline(
            body, grid=(n // SC_WINDOW,),
            in_specs=[pl.BlockSpec((SC_WINDOW, w), lambda i: (i, 0)),
                      pl.BlockSpec((1, SC_WINDOW), lambda i: (0, i)),
                      pl.BlockSpec((1, SC_WINDOW), lambda i: (0, i))],
            out_specs=[], core_axis_name=("core", "subcore"),
            dimension_semantics=(pltpu.PARALLEL,))(x_hbm, i0_hbm, i1_hbm)

    return scatter(x, idx0, idx1)


def _sc_gather_rows(y, idx):
    m = idx.shape[1]
    w = y.shape[1]

    @pl.kernel(out_type=jax.ShapeDtypeStruct((m, w), y.dtype), mesh=_sc_mesh(), scratch_types=[])
    def gather(y_hbm, i_hbm, o_hbm):
        def body(i_vmem, o_vmem):
            pltpu.sync_copy(y_hbm.at[i_vmem.at[0]], o_vmem)

        pltpu.emit_pipeline(
            body, grid=(m // SC_WINDOW,),
            in_specs=[pl.BlockSpec((1, SC_WINDOW), lambda i: (0, i))],
            out_specs=[pl.BlockSpec((SC_WINDOW, w), lambda i: (i, 0))],
            core_axis_name=("core", "subcore"),
            dimension_semantics=(pltpu.PARALLEL,))(i_hbm, o_hbm)

    return gather(y, idx)


def _join_rows(wa, wb):
    lo_a, hi_a = _unpack_pairs(wa)
    lo_b, hi_b = _unpack_pairs(wb)
    return jnp.concatenate([lo_a, lo_b, hi_a, hi_b], axis=1)


def _expert_kernel(te_ref, nv_ref, nu_ref, xa_ref, xb_ref, w1_ref, w3_ref, w2_ref, ya_ref, yb_ref):
    del te_ref
    j = pl.program_id(0)

    @pl.when(j < nu_ref[0])
    def _():
        live = lax.broadcasted_iota(jnp.int32, xa_ref.shape, 0) < nv_ref[j]
        zero = jnp.zeros(xa_ref.shape, U32)
        x = _join_rows(jnp.where(live, xa_ref[...], zero),
                       jnp.where(live, xb_ref[...], zero)).astype(BF16)
        h1 = _dot(x, w1_ref[0])
        h3 = _dot(x, w3_ref[0])
        hdn = (h1 * _sigmoid(h1) * h3).astype(BF16)
        words = _pack_pairs(_dot(hdn, w2_ref[0]))
        ya_ref[...] = words[:, :ROW_WORDS]
        yb_ref[...] = words[:, ROW_WORDS:]


def _expert_call(tile_expert, tile_valid, n_used, xa, xb, w1, w3, w2):
    rows, rw = xa.shape
    tm = EXPERT_TILE
    d, ff = w1.shape[1], w1.shape[2]
    row_blk = lambda j, te, nv, nu: (jnp.minimum(j, nu[0] - 1), 0)
    wsel = lambda j, te, nv, nu: (te[j], 0, 0)
    grid_spec = pltpu.PrefetchScalarGridSpec(
        num_scalar_prefetch=3, grid=(rows // tm,),
        in_specs=[pl.BlockSpec((tm, rw), row_blk), pl.BlockSpec((tm, rw), row_blk),
                  pl.BlockSpec((1, d, ff), wsel), pl.BlockSpec((1, d, ff), wsel),
                  pl.BlockSpec((1, ff, d), wsel)],
        out_specs=[pl.BlockSpec((tm, rw), row_blk), pl.BlockSpec((tm, rw), row_blk)])
    return pl.pallas_call(
        _expert_kernel, grid_spec=grid_spec,
        out_shape=[jax.ShapeDtypeStruct((rows, rw), U32), jax.ShapeDtypeStruct((rows, rw), U32)],
        compiler_params=_cparams(("arbitrary",)), name="moe_experts")(
            tile_expert, tile_valid, n_used, xa, xb, w1, w3, w2)


def _combine_kernel(x2_ref, info_ref, gf_ref, a0_ref, b0_ref, a1_ref, b1_ref, yp_ref, ys_ref,
                    *, ntp):
    i = pl.program_id(0)
    info = info_ref[...]
    moe = (info[:, 2:3] * _join_rows(a0_ref[...], b0_ref[...])
           + info[:, 3:4] * _join_rows(a1_ref[...], b1_ref[...]))
    out = _rms_rows(x2_ref[...] + moe, gf_ref[...])

    @pl.when(i < ntp)
    def _():
        yp_ref[...] = out

    @pl.when(i >= ntp)
    def _():
        ys_ref[...] = out


def _combine_call(x2, info, gf, ga, gb, n_p, n_s):
    n, d = x2.shape
    tile = TOKEN_TILE
    nt = n // tile
    ntp = n_p // tile
    first = pl.BlockSpec((tile, ROW_WORDS), lambda i: (i, 0))
    second = pl.BlockSpec((tile, ROW_WORDS), lambda i: (i + nt, 0))
    return pl.pallas_call(
        functools.partial(_combine_kernel, ntp=ntp), grid=(nt,),
        in_specs=[pl.BlockSpec((tile, d), lambda i: (i, 0)),
                  pl.BlockSpec((tile, LANES), lambda i: (i, 0)),
                  pl.BlockSpec(gf.shape, lambda i: (0, 0)),
                  first, first, second, second],
        out_specs=[pl.BlockSpec((tile, d), lambda i: (jnp.minimum(i, ntp - 1), 0)),
                   pl.BlockSpec((tile, d), lambda i: (jnp.maximum(i - ntp, 0), 0))],
        out_shape=[jax.ShapeDtypeStruct((n_p, d), F32), jax.ShapeDtypeStruct((n_s, d), F32)],
        compiler_params=_cparams(("arbitrary",)), name="moe_combine")(
            x2, info, gf, ga, gb, ga, gb)


def _layer(xp, xs, seq_p, seq_s, norm1_g, w_in, conv_a_w, conv_a_norm_g, dn_conv_w, dn_a_log,
           dn_dt_bias, dn_norm_g, w_o, norm2_g, rg_w, rg_b, re_w, re_b, w1, w3, w2, final_g):
    n_p, d = xp.shape
    n_s = xs.shape[0]
    n = n_p + n_s
    row = lambda a: a.reshape(1, -1).astype(F32)

    n_ab = len(_GB_COLS)
    w_main = jnp.concatenate([w_in[:, :7 * CONV_WIDTH], w_in[:, 7 * CONV_WIDTH + _GB_COLS]], axis=1)
    w_main = jnp.pad(w_main, ((0, 0), (0, LANES - n_ab))).astype(BF16)
    cw = jnp.concatenate([conv_a_w, dn_conv_w], axis=1).astype(F32)
    grp = jnp.arange(CONV_WIDTH) // (CONV_WIDTH // CONV_GROUPS)
    gmat = ((grp[:, None] == grp[None, :]).astype(F32) / (CONV_WIDTH // CONV_GROUPS)).astype(BF16)
    is_a = (_GB_COLS < 2 * DN_HEADS)
    a_idx = jnp.where(is_a, _GB_COLS, 0)
    padl = lambda a: jnp.pad(a.reshape(1, -1).astype(F32), ((0, 0), (0, LANES - n_ab)))
    nea = padl(jnp.where(is_a, -jnp.exp(dn_a_log.astype(F32)).reshape(-1)[a_idx], 0.0))
    dtb = padl(jnp.where(is_a, dn_dt_bias.astype(F32).reshape(-1)[a_idx], 0.0))
    proj_w = [row(norm1_g), w_main, cw, row(conv_a_norm_g), gmat, nea, dtb]

    ya, q, k, v, z, gb = _proj_call(xp, xs, seq_p, seq_s, proj_w)

    o_f = _dn_call(q, k, v, gb, None, False, n_p, seq_s)
    yb = _dn_call(q, k, v, gb, (o_f, z, row(dn_norm_g)), True, n_p, seq_s)

    rw = jnp.pad(jnp.concatenate([rg_w, re_w], axis=1).astype(F32),
                 ((0, 0), (0, LANES - N_GROUPS - N_EXPERTS)))
    rwh = rw.astype(BF16)
    rwl = (rw - rwh.astype(F32)).astype(BF16)
    rb = jnp.pad(jnp.concatenate([rg_b, re_b]).reshape(1, -1).astype(F32),
                 ((0, 0), (0, LANES - N_GROUPS - N_EXPERTS)))
    tile = TOKEN_TILE
    tm = EXPERT_TILE
    tri = (jnp.arange(tile)[:, None] > jnp.arange(tile)[None, :]).astype(BF16)
    route_w = [w_o.astype(BF16), row(norm2_g), rwh, rwl, rb, tri]
    x2, xna, xnb, info, cnt = _route_call(xp, xs, ya, yb, route_w)

    counts = cnt[0, :N_EXPERTS].astype(jnp.int32)
    pcounts = (counts + tm - 1) // tm * tm
    pends = jnp.cumsum(pcounts)
    pstarts = pends - pcounts
    eid = jnp.arange(N_EXPERTS, dtype=jnp.int32)
    e12 = info[:, 0:2].astype(jnp.int32)
    start12 = jnp.sum(jnp.where(e12[:, :, None] == eid, pstarts, 0), axis=-1)
    dest = start12 + info[:, 4:6].astype(jnp.int32)
    dest0 = dest[:, 0].reshape(1, n)
    dest1 = dest[:, 1].reshape(1, n)
    n_tiles = 2 * n // tm + N_EXPERTS
    tile_start = jnp.arange(n_tiles, dtype=jnp.int32) * tm
    tile_expert = jnp.minimum(jnp.sum(pends[None, :] <= tile_start[:, None], axis=1),
                              N_EXPERTS - 1).astype(jnp.int32)
    live_end = jnp.sum(jnp.where(tile_expert[:, None] == eid, pstarts + counts, 0), axis=-1)
    tile_valid = jnp.clip(live_end - tile_start, 0, tm).astype(jnp.int32)
    n_used = (pends[-1:] // tm).astype(jnp.int32)

    rows = n_tiles * tm
    xa = _sc_scatter_rows(xna, dest0, dest1, rows)
    xb = _sc_scatter_rows(xnb, dest0, dest1, rows)
    ya_e, yb_e = _expert_call(tile_expert, tile_valid, n_used, xa, xb, w1.astype(BF16),
                              w3.astype(BF16), w2.astype(BF16))
    both = jnp.concatenate([dest0, dest1], axis=1)
    return _combine_call(x2, info, row(final_g), _sc_gather_rows(ya_e, both),
                         _sc_gather_rows(yb_e, both), n_p, n_s)


def kernel(x_prompt, x_sample, norm1_g, w_in, conv_a_w, conv_a_norm_g, dn_conv_w, dn_a_log,
           dn_dt_bias, dn_norm_g, w_o, norm2_g, router_group_w, router_group_b, router_expert_w,
           router_expert_b, w1, w3, w2, final_norm_g):
    assert norm1_g.shape[0] == 1, "single-layer trunk"
    bp, sp, d = x_prompt.shape
    bs, ss, _ = x_sample.shape
    assert bp == 1 and sp % max(TOKEN_TILE, DN_BLOCK) == 0 and ss % max(TOKEN_TILE, DN_BLOCK) == 0
    y_p, y_s = _layer(
        x_prompt.reshape(bp * sp, d), x_sample.reshape(bs * ss, d), sp, ss,
        norm1_g[0], w_in[0], conv_a_w[0], conv_a_norm_g[0], dn_conv_w[0], dn_a_log[0],
        dn_dt_bias[0], dn_norm_g[0], w_o[0], norm2_g[0], router_group_w[0], router_group_b[0],
        router_expert_w[0], router_expert_b[0], w1[0], w3[0], w2[0], final_norm_g)
    return y_p.reshape(bp, sp, d), y_s.reshape(bs, ss, d)
```

```python
import functools

import jax
import jax.numpy as jnp
import numpy as np
from jax import lax
from jax.experimental import pallas as pl
from jax.experimental.pallas import tpu as pltpu
from jax.experimental.pallas import tpu_sc as plsc

F32 = jnp.float32
BF16 = jnp.bfloat16
U32 = jnp.uint32

EPS = 1e-6
CONV_WIDTH = 512
CONV_GROUPS = 8
DN_HEADS = 4
DN_DK = 128
DN_V = 512
CHUNK = 64
N_GROUPS = 4
EXPERTS_PER_GROUP = 8
N_EXPERTS = N_GROUPS * EXPERTS_PER_GROUP
EXPERT_FF = 512
LANES = 128
HALO = 8

TOKEN_TILE = 512
DN_BLOCK = 256
EXPERT_TILE = 512
ROW_WORDS = 256
SC_WINDOW = 128
VMEM_LIMIT = 56 * 1024 * 1024

_GB_COLS = np.concatenate([np.arange(4 * DN_HEADS), np.arange(2 * DN_HEADS), np.arange(2 * DN_HEADS)])


def _cparams(sem):
    return pltpu.CompilerParams(dimension_semantics=sem, vmem_limit_bytes=VMEM_LIMIT)


def _sigmoid(x):
    return 1.0 / (1.0 + jnp.exp(-x))


def _rms_rows(x, g):
    return x * lax.rsqrt(jnp.mean(x * x, axis=-1, keepdims=True) + EPS) * g


def _dot(a, b):
    return jnp.dot(a, b, preferred_element_type=F32)


def _pack_pairs(x):
    half = x.shape[1] // 2
    bits = lambda a: lax.bitcast_convert_type(a.astype(BF16).astype(F32), U32)
    return (bits(x[:, half:]) & jnp.uint32(0xFFFF0000)) | (bits(x[:, :half]) >> 16)


def _unpack_pairs(w):
    lo = lax.bitcast_convert_type(w << 16, F32)
    hi = lax.bitcast_convert_type(w & jnp.uint32(0xFFFF0000), F32)
    return lo, hi


def _two_source_specs(block_rows, width, n_p, n_s, block_of_tile):
    ntp = n_p // TOKEN_TILE
    nbp = n_p // block_rows
    nbs = n_s // block_rows
    clamp = lambda b, nb: jnp.clip(b, 0, nb - 1)
    sp = pl.BlockSpec((block_rows, width), lambda i: (clamp(block_of_tile(jnp.minimum(i, ntp - 1)), nbp), 0))
    ss = pl.BlockSpec((block_rows, width), lambda i: (clamp(block_of_tile(jnp.maximum(i - ntp, 0)), nbs), 0))
    return sp, ss


def _proj_kernel(xp_ref, xpp_ref, xpn_ref, xs_ref, xsp_ref, xsn_ref,
                 g1_ref, w_ref, cw_ref, cng_ref, gmat_ref, nea_ref, dtb_ref,
                 ya_ref, q_ref, k_ref, v_ref, z_ref, gb_ref,
                 hs_ref, su_ref, p_ref, *, tile, ntp, seq_p, seq_s):
    i = pl.program_id(0)
    in_p = i < ntp
    g1 = g1_ref[...]

    def load_rows(main_ref, prev_ref, next_ref):
        hs_ref[0:HALO, :] = _rms_rows(prev_ref[...], g1)
        hs_ref[HALO:HALO + tile, :] = _rms_rows(main_ref[...], g1)
        hs_ref[HALO + tile:, :] = _rms_rows(next_ref[...], g1)

    pl.when(in_p)(lambda: load_rows(xp_ref, xpp_ref, xpn_ref))
    pl.when(jnp.logical_not(in_p))(lambda: load_rows(xs_ref, xsp_ref, xsn_ref))

    tok0 = jnp.where(in_p, i, i - ntp) * tile
    seq_len = jnp.where(in_p, seq_p, seq_s)
    not_start = (tok0 % seq_len != 0).astype(F32)
    not_end = ((tok0 + tile) % seq_len != 0).astype(F32)

    p_ref[...] = _dot(hs_ref[...].astype(BF16), w_ref[...])
    for stage in _FINISH_STAGES:
        stage(p_ref, su_ref, cw_ref, cng_ref, gmat_ref, nea_ref, dtb_ref,
              ya_ref, q_ref, k_ref, v_ref, z_ref, gb_ref, tile, not_start, not_end)


def _conv_cols(p_ref, su_ref, cw_ref, lo, width, tile, not_start, not_end):
    cw = CONV_WIDTH
    cols = slice(lo, lo + width)
    if lo < cw:
        su_ref[:, cols] = p_ref[:, cw + lo:cw + lo + width] * p_ref[:, 2 * cw + lo:2 * cw + lo + width]
    else:
        su_ref[:, cols] = p_ref[:, 2 * cw + lo:2 * cw + lo + width]
    su_ref[0:HALO, cols] = su_ref[0:HALO, cols] * not_start
    su_ref[HALO + tile:, cols] = su_ref[HALO + tile:, cols] * not_end
    u = su_ref[:, cols]
    total = tile + 2 * HALO
    return (pltpu.roll(u, 1, axis=0)[HALO:HALO + tile] * cw_ref[0:1, cols]
            + u[HALO:HALO + tile] * cw_ref[1:2, cols]
            + pltpu.roll(u, total - 1, axis=0)[HALO:HALO + tile] * cw_ref[2:3, cols])


def _stage_mixer_a(half):
    width = CONV_WIDTH // 2
    lo = half * width

    def stage(p_ref, su_ref, cw_ref, cng_ref, gmat_ref, nea_ref, dtb_ref,
              ya_ref, q_ref, k_ref, v_ref, z_ref, gb_ref, tile, not_start, not_end):
        conv = _conv_cols(p_ref, su_ref, cw_ref, lo, width, tile, not_start, not_end)
        ya = p_ref[HALO:HALO + tile, lo:lo + width] * conv
        sq = ya * ya
        sq_hi = sq.astype(BF16)
        sq_lo = (sq - sq_hi.astype(F32)).astype(BF16)
        gm = gmat_ref[lo:lo + width, lo:lo + width]
        ms = _dot(sq_hi, gm) + _dot(sq_lo, gm)
        ya_ref[:, lo:lo + width] = (ya * lax.rsqrt(ms + EPS) * cng_ref[:, lo:lo + width]).astype(ya_ref.dtype)
    return stage


def _stage_mixer_b(which, half):
    width = DN_V // 2
    lo_out = half * width
    lo = CONV_WIDTH + which * DN_V + lo_out

    def stage(p_ref, su_ref, cw_ref, cng_ref, gmat_ref, nea_ref, dtb_ref,
              ya_ref, q_ref, k_ref, v_ref, z_ref, gb_ref, tile, not_start, not_end):
        x = _conv_cols(p_ref, su_ref, cw_ref, lo, width, tile, not_start, not_end)
        x = x * _sigmoid(x)
        out_ref = (q_ref, k_ref, v_ref)[which]
        if which == 2:
            out_ref[:, lo_out:lo_out + width] = x.astype(out_ref.dtype)
            return
        scale = DN_DK ** -0.5 if which == 0 else 1.0
        for h in range(width // DN_DK):
            xh = x[:, h * DN_DK:(h + 1) * DN_DK]
            xh = xh * (lax.rsqrt(jnp.sum(xh * xh, axis=-1, keepdims=True) + EPS) * scale)
            out_ref[:, lo_out + h * DN_DK:lo_out + (h + 1) * DN_DK] = xh.astype(out_ref.dtype)
    return stage


def _stage_gates(inner):
    def stage(p_ref, su_ref, cw_ref, cng_ref, gmat_ref, nea_ref, dtb_ref,
              ya_ref, q_ref, k_ref, v_ref, z_ref, gb_ref, tile, not_start, not_end):
        inner(p_ref, su_ref, cw_ref, cng_ref, gmat_ref, nea_ref, dtb_ref,
              ya_ref, q_ref, k_ref, v_ref, z_ref, gb_ref, tile, not_start, not_end)
        cw = CONV_WIDTH
        z_ref[...] = p_ref[HALO:HALO + tile, 6 * cw:7 * cw].astype(z_ref.dtype)
        ab = p_ref[HALO:HALO + tile, 7 * cw:7 * cw + LANES]
        xs = ab + dtb_ref[...]
        softplus = jnp.maximum(xs, 0.0) + jnp.log(1.0 + jnp.exp(-jnp.abs(xs)))
        g = nea_ref[...] * softplus
        beta = _sigmoid(ab)
        r = lax.broadcasted_iota(jnp.int32, (tile, LANES), 0) % CHUNK
        pre = g
        suf = g
        s = 1
        while s < CHUNK:
            pre = pre + jnp.where(r >= s, pltpu.roll(pre, s, axis=0), 0.0)
            suf = suf + jnp.where(r < CHUNK - s, pltpu.roll(suf, tile - s, axis=0), 0.0)
            s *= 2
        lane = lax.broadcasted_iota(jnp.int32, (tile, LANES), 1) // DN_HEADS
        gb_ref[...] = jnp.where(lane == 0, pre,
                      jnp.where(lane == 1, suf,
                      jnp.where(lane < 4, beta,
                      jnp.where(lane == 4, suf - g,
                      jnp.where(lane == 5, pre - g, pre + suf - g)))))
    return stage


_FINISH_STAGES = (_stage_mixer_a(0), _stage_mixer_a(1), _stage_mixer_b(0, 0), _stage_mixer_b(0, 1),
                  _stage_mixer_b(1, 0), _stage_mixer_b(1, 1), _stage_mixer_b(2, 0),
                  _stage_gates(_stage_mixer_b(2, 1)))


def _proj_call(xp, xs, seq_p, seq_s, wts):
    n_p, d = xp.shape
    n_s = xs.shape[0]
    n = n_p + n_s
    tile = TOKEN_TILE
    nt = n // tile
    tb = tile // HALO
    kern = functools.partial(_proj_kernel, tile=tile, ntp=n_p // tile, seq_p=seq_p, seq_s=seq_s)
    const = lambda i: (0, 0)
    main_p, main_s = _two_source_specs(tile, d, n_p, n_s, lambda t: t)
    prev_p, prev_s = _two_source_specs(HALO, d, n_p, n_s, lambda t: t * tb - 1)
    next_p, next_s = _two_source_specs(HALO, d, n_p, n_s, lambda t: (t + 1) * tb)
    in_specs = [main_p, prev_p, next_p, main_s, prev_s, next_s]
    in_specs += [pl.BlockSpec(w.shape, const, pipeline_mode=pl.Buffered(1)) for w in wts]
    widths = (512, 512, 512, 512, 512, LANES)
    dtypes = (BF16, BF16, BF16, BF16, BF16, F32)
    rows = tile + 2 * HALO
    return pl.pallas_call(
        kern, grid=(nt,), in_specs=in_specs,
        out_specs=[pl.BlockSpec((tile, w), lambda i: (i, 0)) for w in widths],
        out_shape=[jax.ShapeDtypeStruct((n, w), dt) for w, dt in zip(widths, dtypes)],
        scratch_shapes=[pltpu.VMEM((rows, d), F32), pltpu.VMEM((rows, 4 * CONV_WIDTH), F32),
                        pltpu.VMEM((rows, wts[1].shape[1]), F32)],
        compiler_params=_cparams(("arbitrary",)), name="proj_conv")(xp, xp, xp, xs, xs, xs, *wts)


def _dn_kernel(*refs, block, reverse, n_prompt, seq_len, nblk):
    if reverse:
        (q_ref, k_ref, v_ref, gb_ref, of_ref, z_ref, ng_ref, o_ref, s_ref) = refs
    else:
        (q_ref, k_ref, v_ref, gb_ref, o_ref, s_ref) = refs
    i = pl.program_id(0)
    b = nblk - 1 - i if reverse else i
    if reverse:
        edge = (b + 1) * block
        reset = (edge == n_prompt) | ((edge > n_prompt) & ((edge - n_prompt) % seq_len == 0))
    else:
        edge = b * block
        reset = (edge == 0) | ((edge >= n_prompt) & ((edge - n_prompt) % seq_len == 0))

    @pl.when(reset)
    def _():
        s_ref[...] = jnp.zeros_like(s_ref)

    nchunk = block // CHUNK
    c = CHUNK
    heads = range(DN_HEADS)
    ri = lax.broadcasted_iota(jnp.int32, (block, block), 0)
    ci = lax.broadcasted_iota(jnp.int32, (block, block), 1)
    same = (ri // c) == (ci // c)
    incl = same & ((ri <= ci) if reverse else (ri >= ci))
    strict = same & ((ri < ci) if reverse else (ri > ci))
    pr = lax.broadcasted_iota(jnp.int32, (c, block), 0)
    pc = lax.broadcasted_iota(jnp.int32, (c, block), 1)
    eye_p = (pr == pc % c).astype(F32)
    lane_chunk = pc // c

    def fold(bd):
        out = bd[0:c]
        for r in range(1, nchunk):
            out = out + bd[r * c:(r + 1) * c]
        return out

    def unfold(p):
        zero = jnp.zeros_like(p)
        return jnp.concatenate([jnp.where(lane_chunk == r, p, zero) for r in range(nchunk)], axis=0)

    grp = lambda g, h: slice(g * DN_HEADS + h, g * DN_HEADS + h + 1)
    g_cum, g_beta, g_rest, g_tot = (1, 3, 5, 7) if reverse else (0, 2, 4, 6)
    gbv = gb_ref[...]
    gbt = gbv.T
    sls = [slice(h * DN_DK, (h + 1) * DN_DK) for h in heads]
    kb = [k_ref[:, sl] for sl in sls]
    qb = [q_ref[:, sl] for sl in sls]
    gcol = [gbv[:, grp(g_cum, h)] for h in heads]
    beta = [gbv[:, grp(g_beta, h)] for h in heads]
    kq = [lax.dot_general(jnp.concatenate([kb[h], qb[h]], axis=0), kb[h],
                          (((1,), (1,)), ((), ())), preferred_element_type=F32) for h in heads]
    decay = [jnp.exp(jnp.minimum(gcol[h] - gbt[grp(g_cum, h), :], 0.0)) for h in heads]
    bdl = [jnp.where(strict, kq[h][:block] * decay[h], 0.0) * beta[h] for h in heads]
    bda = [jnp.where(incl, kq[h][block:] * decay[h], 0.0).astype(BF16) for h in heads]

    lp = [fold(bdl[h]) for h in heads]
    xp = [eye_p - lp[h] for h in heads]
    bd = [bdl[h].astype(BF16) for h in heads]
    p = 1
    while p < c:
        if p == 1:
            res = [_dot(lp[h].astype(BF16), bd[h]) for h in heads]
            lp = res
        elif 2 * p < c:
            res = [_dot(jnp.concatenate([lp[h], xp[h]], axis=0).astype(BF16), bd[h]) for h in heads]
            lp = [r[:c] for r in res]
            xp = [xp[h] + res[h][c:] for h in heads]
        else:
            xp = [xp[h] + _dot(xp[h].astype(BF16), bd[h]) for h in heads]
        p *= 2
        if p < c:
            bd = [unfold(lp[h].astype(BF16)) for h in heads]
    bdt = [unfold(xp[h].astype(BF16)) for h in heads]

    egc = [jnp.exp(gcol[h]) for h in heads]
    kf = [kb[h].astype(F32) for h in heads]
    rhs = [jnp.concatenate([kf[h] * (beta[h] * egc[h]), v_ref[:, sls[h]].astype(F32) * beta[h]],
                           axis=1).astype(BF16) for h in heads]
    wu = [_dot(bdt[h], rhs[h]) for h in heads]
    au = [_dot(bda[h], wu[h].astype(BF16)) for h in heads]
    qt = [qb[h].astype(F32) * egc[h] - au[h][:, :DN_DK] for h in heads]
    kd = [(kf[h] * jnp.exp(gbv[:, grp(g_rest, h)])).astype(BF16) for h in heads]
    wqt = [jnp.concatenate([wu[h][:, :DN_DK], qt[h]], axis=1).astype(BF16) for h in heads]

    s = [s_ref[h] for h in heads]
    order = range(nchunk - 1, -1, -1) if reverse else range(nchunk)
    for cc in order:
        rows = slice(cc * c, (cc + 1) * c)
        for h in heads:
            lhs = jnp.concatenate([wqt[h][rows, :DN_DK], wqt[h][rows, DN_DK:]], axis=0)
            wq = _dot(lhs, s[h].astype(BF16))
            vn = wu[h][rows, DN_DK:] - wq[:c]
            o = wq[c:] + au[h][rows, DN_DK:]
            tot = gbt[grp(g_tot, h), cc * c:cc * c + 1]
            s[h] = s[h] * jnp.exp(tot) + lax.dot_general(
                kd[h][rows], vn.astype(BF16), (((0,), (0,)), ((), ())), preferred_element_type=F32)
            if reverse:
                o = o + of_ref[rows, sls[h]]
                o = o * lax.rsqrt(jnp.mean(o * o, axis=-1, keepdims=True) + EPS) * ng_ref[...]
                zc = z_ref[rows, sls[h]].astype(F32)
                o = o * (zc * _sigmoid(zc))
            o_ref[rows, sls[h]] = o.astype(o_ref.dtype)
    for h in heads:
        s_ref[h] = s[h]


def _dn_call(q, k, v, gb, extra, reverse, n_prompt, seq_len):
    n = q.shape[0]
    block = DN_BLOCK
    nblk = n // block
    imap = (lambda i: (nblk - 1 - i, 0)) if reverse else (lambda i: (i, 0))
    tok = lambda w: pl.BlockSpec((block, w), imap)
    in_specs = [tok(512), tok(512), tok(512), tok(LANES)]
    args = [q, k, v, gb]
    if reverse:
        o_f, z, ng = extra
        in_specs += [tok(512), tok(512), pl.BlockSpec(ng.shape, lambda i: (0, 0))]
        args += [o_f, z, ng]
    kern = functools.partial(_dn_kernel, block=block, reverse=reverse, n_prompt=n_prompt,
                             seq_len=seq_len, nblk=nblk)
    return pl.pallas_call(
        kern, grid=(nblk,), in_specs=in_specs, out_specs=tok(512),
        out_shape=jax.ShapeDtypeStruct((n, 512), BF16 if reverse else F32),
        scratch_shapes=[pltpu.VMEM((DN_HEADS, DN_DK, DN_DK), F32)],
        compiler_params=_cparams(("arbitrary",)),
        name="deltanet_bwd" if reverse else "deltanet_fwd")(*args)


def _route_kernel(xp_ref, xs_ref, ya_ref, yb_ref, wo_ref, g2_ref, rwh_ref, rwl_ref, rb_ref, tri_ref,
                  x2_ref, xa_ref, xb_ref, info_ref, cnt_ref, run_ref, *, tile, ntp):
    i = pl.program_id(0)

    @pl.when(i == 0)
    def _():
        run_ref[...] = jnp.zeros_like(run_ref)

    half = wo_ref.shape[0] // 2
    x = jnp.where(i < ntp, xp_ref[...], xs_ref[...])
    x2 = x + _dot(ya_ref[...], wo_ref[0:half, :]) + _dot(yb_ref[...], wo_ref[half:, :])
    x2_ref[...] = x2
    xn = _rms_rows(x2, g2_ref[...])
    words = _pack_pairs(xn)
    xa_ref[...] = words[:, :ROW_WORDS]
    xb_ref[...] = words[:, ROW_WORDS:]
    xh = xn.astype(BF16)
    xl = (xn - xh.astype(F32)).astype(BF16)
    logits = (_dot(xh, rwh_ref[...]) + _dot(xl, rwh_ref[...]) + _dot(xh, rwl_ref[...])
              + rb_ref[...])
    lane = lax.broadcasted_iota(jnp.int32, (tile, LANES), 1).astype(F32)
    neg = jnp.float32(-jnp.inf)

    def first_argmax(vals):
        m = jnp.max(vals, axis=-1, keepdims=True)
        idx = jnp.min(jnp.where(vals == m, lane, float(LANES)), axis=-1, keepdims=True)
        return m, idx

    gl = jnp.where(lane < N_GROUPS, logits, neg)
    gmax, gidx = first_argmax(gl)
    g_w = 1.0 / jnp.sum(jnp.exp(gl - gmax), axis=-1, keepdims=True)
    lo = N_GROUPS + gidx * EXPERTS_PER_GROUP
    el = jnp.where((lane >= lo) & (lane < lo + EXPERTS_PER_GROUP), logits, neg)
    m1, i1 = first_argmax(el)
    m2, i2 = first_argmax(jnp.where(lane == i1, neg, el))
    r = jnp.exp(m2 - m1)
    gate1 = g_w / (1.0 + r)
    gate2 = g_w * r / (1.0 + r)
    e1 = i1 - N_GROUPS
    e2 = i2 - N_GROUPS
    hit1 = lane == e1
    hit2 = lane == e2
    onehot = (hit1 | hit2).astype(BF16)
    before = _dot(tri_ref[...], onehot) + run_ref[...]
    pos1 = jnp.sum(jnp.where(hit1, before, 0.0), axis=-1, keepdims=True)
    pos2 = jnp.sum(jnp.where(hit2, before, 0.0), axis=-1, keepdims=True)
    run = run_ref[...] + jnp.sum(onehot.astype(F32), axis=0, keepdims=True)
    run_ref[...] = run
    cnt_ref[...] = run
    info = jnp.where(lane == 0, e1,
           jnp.where(lane == 1, e2,
           jnp.where(lane == 2, gate1,
           jnp.where(lane == 3, gate2,
           jnp.where(lane == 4, pos1,
           jnp.where(lane == 5, pos2, 0.0))))))
    info_ref[...] = info


def _route_call(xp, xs, ya, yb, wts):
    n_p, d = xp.shape
    n_s = xs.shape[0]
    n = n_p + n_s
    tile = TOKEN_TILE
    const = lambda i: (0, 0)
    tok = lambda w: pl.BlockSpec((tile, w), lambda i: (i, 0))
    main_p, main_s = _two_source_specs(tile, d, n_p, n_s, lambda t: t)
    in_specs = [main_p, main_s, tok(512), tok(512)] + [pl.BlockSpec(w.shape, const) for w in wts]
    out_shape = [jax.ShapeDtypeStruct((n, d), F32), jax.ShapeDtypeStruct((n, ROW_WORDS), U32),
                 jax.ShapeDtypeStruct((n, ROW_WORDS), U32),
                 jax.ShapeDtypeStruct((n, LANES), F32), jax.ShapeDtypeStruct((1, LANES), F32)]
    out_specs = [tok(d), tok(ROW_WORDS), tok(ROW_WORDS), tok(LANES), pl.BlockSpec((1, LANES), const)]
    kern = functools.partial(_route_kernel, tile=tile, ntp=n_p // tile)
    return pl.pallas_call(
        kern, grid=(n // tile,), in_specs=in_specs, out_specs=out_specs, out_shape=out_shape,
        scratch_shapes=[pltpu.VMEM((1, LANES), F32)],
        compiler_params=_cparams(("arbitrary",)), name="oproj_router")(xp, xs, ya, yb, *wts)


def _sc_mesh():
    return plsc.VectorSubcoreMesh(core_axis_name="core", subcore_axis_name="subcore")


def _sc_scatter_rows(x, idx0, idx1, rows):
    n, w = x.shape

    @pl.kernel(out_type=jax.ShapeDtypeStruct((rows, w), x.dtype), mesh=_sc_mesh(), scratch_types=[])
    def scatter(x_hbm, i0_hbm, i1_hbm, o_hbm):
        def body(x_vmem, i0_vmem, i1_vmem):
            pltpu.sync_copy(x_vmem, o_hbm.at[i0_vmem.at[0]])
            pltpu.sync_copy(x_vmem, o_hbm.at[i1_vmem.at[0]])

        pltpu.emit_pipeline(
            body, grid=(n // SC_WINDOW,),
            in_specs=[pl.BlockSpec((SC_WINDOW, w), lambda i: (i, 0)),
                      pl.BlockSpec((1, SC_WINDOW), lambda i: (0, i)),
                      pl.BlockSpec((1, SC_WINDOW), lambda i: (0, i))],
            out_specs=[], core_axis_name=("core", "subcore"),
            dimension_semantics=(pltpu.PARALLEL,))(x_hbm, i0_hbm, i1_hbm)

    return scatter(x, idx0, idx1)


def _sc_gather_rows(y, idx):
    m = idx.shape[1]
    w = y.shape[1]

    @pl.kernel(out_type=jax.ShapeDtypeStruct((m, w), y.dtype), mesh=_sc_mesh(), scratch_types=[])
    def gather(y_hbm, i_hbm, o_hbm):
        def body(i_vmem, o_vmem):
            pltpu.sync_copy(y_hbm.at[i_vmem.at[0]], o_vmem)

        pltpu.emit_pipeline(
            body, grid=(m // SC_WINDOW,),
            in_specs=[pl.BlockSpec((1, SC_WINDOW), lambda i: (0, i))],
            out_specs=[pl.BlockSpec((SC_WINDOW, w), lambda i: (i, 0))],
            core_axis_name=("core", "subcore"),
            dimension_semantics=(pltpu.PARALLEL,))(i_hbm, o_hbm)

    return gather(y, idx)


def _join_rows(wa, wb):
    lo_a, hi_a = _unpack_pairs(wa)
    lo_b, hi_b = _unpack_pairs(wb)
    return jnp.concatenate([lo_a, lo_b, hi_a, hi_b], axis=1)


def _expert_kernel(te_ref, nv_ref, nu_ref, xa_ref, xb_ref, w1_ref, w3_ref, w2_ref, ya_ref, yb_ref):
    del te_ref
    j = pl.program_id(0)

    @pl.when(j < nu_ref[0])
    def _():
        live = lax.broadcasted_iota(jnp.int32, xa_ref.shape, 0) < nv_ref[j]
        zero = jnp.zeros(xa_ref.shape, U32)
        x = _join_rows(jnp.where(live, xa_ref[...], zero),
                       jnp.where(live, xb_ref[...], zero)).astype(BF16)
        h1 = _dot(x, w1_ref[0])
        h3 = _dot(x, w3_ref[0])
        hdn = (h1 * _sigmoid(h1) * h3).astype(BF16)
        words = _pack_pairs(_dot(hdn, w2_ref[0]))
        ya_ref[...] = words[:, :ROW_WORDS]
        yb_ref[...] = words[:, ROW_WORDS:]


def _expert_call(tile_expert, tile_valid, n_used, xa, xb, w1, w3, w2):
    rows, rw = xa.shape
    tm = EXPERT_TILE
    d, ff = w1.shape[1], w1.shape[2]
    row_blk = lambda j, te, nv, nu: (jnp.minimum(j, nu[0] - 1), 0)
    wsel = lambda j, te, nv, nu: (te[j], 0, 0)
    grid_spec = pltpu.PrefetchScalarGridSpec(
        num_scalar_prefetch=3, grid=(rows // tm,),
        in_specs=[pl.BlockSpec((tm, rw), row_blk), pl.BlockSpec((tm, rw), row_blk),
                  pl.BlockSpec((1, d, ff), wsel), pl.BlockSpec((1, d, ff), wsel),
                  pl.BlockSpec((1, ff, d), wsel)],
        out_specs=[pl.BlockSpec((tm, rw), row_blk), pl.BlockSpec((tm, rw), row_blk)])
    return pl.pallas_call(
        _expert_kernel, grid_spec=grid_spec,
        out_shape=[jax.ShapeDtypeStruct((rows, rw), U32), jax.ShapeDtypeStruct((rows, rw), U32)],
        compiler_params=_cparams(("arbitrary",)), name="moe_experts")(
            tile_expert, tile_valid, n_used, xa, xb, w1, w3, w2)


def _combine_kernel(x2_ref, info_ref, gf_ref, a0_ref, b0_ref, a1_ref, b1_ref, yp_ref, ys_ref,
                    *, ntp):
    i = pl.program_id(0)
    info = info_ref[...]
    moe = (info[:, 2:3] * _join_rows(a0_ref[...], b0_ref[...])
           + info[:, 3:4] * _join_rows(a1_ref[...], b1_ref[...]))
    out = _rms_rows(x2_ref[...] + moe, gf_ref[...])

    @pl.when(i < ntp)
    def _():
        yp_ref[...] = out

    @pl.when(i >= ntp)
    def _():
        ys_ref[...] = out


def _combine_call(x2, info, gf, ga, gb, n_p, n_s):
    n, d = x2.shape
    tile = TOKEN_TILE
    nt = n // tile
    ntp = n_p // tile
    first = pl.BlockSpec((tile, ROW_WORDS), lambda i: (i, 0))
    second = pl.BlockSpec((tile, ROW_WORDS), lambda i: (i + nt, 0))
    return pl.pallas_call(
        functools.partial(_combine_kernel, ntp=ntp), grid=(nt,),
        in_specs=[pl.BlockSpec((tile, d), lambda i: (i, 0)),
                  pl.BlockSpec((tile, LANES), lambda i: (i, 0)),
                  pl.BlockSpec(gf.shape, lambda i: (0, 0)),
                  first, first, second, second],
        out_specs=[pl.BlockSpec((tile, d), lambda i: (jnp.minimum(i, ntp - 1), 0)),
                   pl.BlockSpec((tile, d), lambda i: (jnp.maximum(i - ntp, 0), 0))],
        out_shape=[jax.ShapeDtypeStruct((n_p, d), F32), jax.ShapeDtypeStruct((n_s, d), F32)],
        compiler_params=_cparams(("arbitrary",)), name="moe_combine")(
            x2, info, gf, ga, gb, ga, gb)


def _layer(xp, xs, seq_p, seq_s, norm1_g, w_in, conv_a_w, conv_a_norm_g, dn_conv_w, dn_a_log,
           dn_dt_bias, dn_norm_g, w_o, norm2_g, rg_w, rg_b, re_w, re_b, w1, w3, w2, final_g):
    n_p, d = xp.shape
    n_s = xs.shape[0]
    n = n_p + n_s
    row = lambda a: a.reshape(1, -1).astype(F32)

    n_ab = len(_GB_COLS)
    w_main = jnp.concatenate([w_in[:, :7 * CONV_WIDTH], w_in[:, 7 * CONV_WIDTH + _GB_COLS]], axis=1)
    w_main = jnp.pad(w_main, ((0, 0), (0, LANES - n_ab))).astype(BF16)
    cw = jnp.concatenate([conv_a_w, dn_conv_w], axis=1).astype(F32)
    grp = jnp.arange(CONV_WIDTH) // (CONV_WIDTH // CONV_GROUPS)
    gmat = ((grp[:, None] == grp[None, :]).astype(F32) / (CONV_WIDTH // CONV_GROUPS)).astype(BF16)
    is_a = (_GB_COLS < 2 * DN_HEADS)
    a_idx = jnp.where(is_a, _GB_COLS, 0)
    padl = lambda a: jnp.pad(a.reshape(1, -1).astype(F32), ((0, 0), (0, LANES - n_ab)))
    nea = padl(jnp.where(is_a, -jnp.exp(dn_a_log.astype(F32)).reshape(-1)[a_idx], 0.0))
    dtb = padl(jnp.where(is_a, dn_dt_bias.astype(F32).reshape(-1)[a_idx], 0.0))
    proj_w = [row(norm1_g), w_main, cw, row(conv_a_norm_g), gmat, nea, dtb]

    ya, q, k, v, z, gb = _proj_call(xp, xs, seq_p, seq_s, proj_w)

    o_f = _dn_call(q, k, v, gb, None, False, n_p, seq_s)
    yb = _dn_call(q, k, v, gb, (o_f, z, row(dn_norm_g)), True, n_p, seq_s)

    rw = jnp.pad(jnp.concatenate([rg_w, re_w], axis=1).astype(F32),
                 ((0, 0), (0, LANES - N_GROUPS - N_EXPERTS)))
    rwh = rw.astype(BF16)
    rwl = (rw - rwh.astype(F32)).astype(BF16)
    rb = jnp.pad(jnp.concatenate([rg_b, re_b]).reshape(1, -1).astype(F32),
                 ((0, 0), (0, LANES - N_GROUPS - N_EXPERTS)))
    tile = TOKEN_TILE
    tm = EXPERT_TILE
    tri = (jnp.arange(tile)[:, None] > jnp.arange(tile)[None, :]).astype(BF16)
    route_w = [w_o.astype(BF16), row(norm2_g), rwh, rwl, rb, tri]
    x2, xna, xnb, info, cnt = _route_call(xp, xs, ya, yb, route_w)

    counts = cnt[0, :N_EXPERTS].astype(jnp.int32)
    pcounts = (counts + tm - 1) // tm * tm
    pends = jnp.cumsum(pcounts)
    pstarts = pends - pcounts
    eid = jnp.arange(N_EXPERTS, dtype=jnp.int32)
    e12 = info[:, 0:2].astype(jnp.int32)
    start12 = jnp.sum(jnp.where(e12[:, :, None] == eid, pstarts, 0), axis=-1)
    dest = start12 + info[:, 4:6].astype(jnp.int32)
    dest0 = dest[:, 0].reshape(1, n)
    dest1 = dest[:, 1].reshape(1, n)
    n_tiles = 2 * n // tm + N_EXPERTS
    tile_start = jnp.arange(n_tiles, dtype=jnp.int32) * tm
    tile_expert = jnp.minimum(jnp.sum(pends[None, :] <= tile_start[:, None], axis=1),
                              N_EXPERTS - 1).astype(jnp.int32)
    live_end = jnp.sum(jnp.where(tile_expert[:, None] == eid, pstarts + counts, 0), axis=-1)
    tile_valid = jnp.clip(live_end - tile_start, 0, tm).astype(jnp.int32)
    n_used = (pends[-1:] // tm).astype(jnp.int32)

    rows = n_tiles * tm
    xa = _sc_scatter_rows(xna, dest0, dest1, rows)
    xb = _sc_scatter_rows(xnb, dest0, dest1, rows)
    ya_e, yb_e = _expert_call(tile_expert, tile_valid, n_used, xa, xb, w1.astype(BF16),
                              w3.astype(BF16), w2.astype(BF16))
    both = jnp.concatenate([dest0, dest1], axis=1)
    return _combine_call(x2, info, row(final_g), _sc_gather_rows(ya_e, both),
                         _sc_gather_rows(yb_e, both), n_p, n_s)


def kernel(x_prompt, x_sample, norm1_g, w_in, conv_a_w, conv_a_norm_g, dn_conv_w, dn_a_log,
           dn_dt_bias, dn_norm_g, w_o, norm2_g, router_group_w, router_group_b, router_expert_w,
           router_expert_b, w1, w3, w2, final_norm_g):
    assert norm1_g.shape[0] == 1, "single-layer trunk"
    bp, sp, d = x_prompt.shape
    bs, ss, _ = x_sample.shape
    assert bp == 1 and sp % max(TOKEN_TILE, DN_BLOCK) == 0 and ss % max(TOKEN_TILE, DN_BLOCK) == 0
    y_p, y_s = _layer(
        x_prompt.reshape(bp * sp, d), x_sample.reshape(bs * ss, d), sp, ss,
        norm1_g[0], w_in[0], conv_a_w[0], conv_a_norm_g[0], dn_conv_w[0], dn_a_log[0],
        dn_dt_bias[0], dn_norm_g[0], w_o[0], norm2_g[0], router_group_w[0], router_group_b[0],
        router_expert_w[0], router_expert_b[0], w1[0], w3[0], w2[0], final_norm_g)
    return y_p.reshape(bp, sp, d), y_s.reshape(bs, ss, d)
```

```python
import functools

import jax
import jax.numpy as jnp
import numpy as np
from jax import lax
from jax.experimental import pallas as pl
from jax.experimental.pallas import tpu as pltpu
from jax.experimental.pallas import tpu_sc as plsc

F32 = jnp.float32
BF16 = jnp.bfloat16
U32 = jnp.uint32

EPS = 1e-6
CONV_WIDTH = 512
CONV_GROUPS = 8
DN_HEADS = 4
DN_DK = 128
DN_V = 512
CHUNK = 64
N_GROUPS = 4
EXPERTS_PER_GROUP = 8
N_EXPERTS = N_GROUPS * EXPERTS_PER_GROUP
EXPERT_FF = 512
LANES = 128
HALO = 8

TOKEN_TILE = 512
DN_BLOCK = 1024
DN_SUB = 256
EXPERT_TILE = 512
ROW_WORDS = 256
SC_WINDOW = 128
VMEM_LIMIT = 56 * 1024 * 1024

_GB_COLS = np.concatenate([np.arange(4 * DN_HEADS), np.arange(2 * DN_HEADS), np.arange(2 * DN_HEADS)])


def _cparams(sem):
    return pltpu.CompilerParams(dimension_semantics=sem, vmem_limit_bytes=VMEM_LIMIT)


def _sigmoid(x):
    return 1.0 / (1.0 + jnp.exp(-x))


def _rms_rows(x, g):
    return x * lax.rsqrt(jnp.mean(x * x, axis=-1, keepdims=True) + EPS) * g


def _dot(a, b):
    return jnp.dot(a, b, preferred_element_type=F32)


def _pack_pairs(x):
    half = x.shape[1] // 2
    bits = lambda a: lax.bitcast_convert_type(a.astype(BF16).astype(F32), U32)
    return (bits(x[:, half:]) & jnp.uint32(0xFFFF0000)) | (bits(x[:, :half]) >> 16)


def _unpack_pairs(w):
    lo = lax.bitcast_convert_type(w << 16, F32)
    hi = lax.bitcast_convert_type(w & jnp.uint32(0xFFFF0000), F32)
    return lo, hi


def _two_source_specs(block_rows, width, n_p, n_s, block_of_tile):
    ntp = n_p // TOKEN_TILE
    nbp = n_p // block_rows
    nbs = n_s // block_rows
    clamp = lambda b, nb: jnp.clip(b, 0, nb - 1)
    sp = pl.BlockSpec((block_rows, width), lambda i: (clamp(block_of_tile(jnp.minimum(i, ntp - 1)), nbp), 0))
    ss = pl.BlockSpec((block_rows, width), lambda i: (clamp(block_of_tile(jnp.maximum(i - ntp, 0)), nbs), 0))
    return sp, ss


def _proj_kernel(xp_ref, xpp_ref, xpn_ref, xs_ref, xsp_ref, xsn_ref,
                 g1_ref, w_ref, cw_ref, cng_ref, gmat_ref, nea_ref, dtb_ref,
                 ya_ref, q_ref, k_ref, v_ref, z_ref, gb_ref,
                 hs_ref, su_ref, p_ref, *, tile, ntp, seq_p, seq_s):
    i = pl.program_id(0)
    in_p = i < ntp
    g1 = g1_ref[...]

    def load_rows(main_ref, prev_ref, next_ref):
        hs_ref[0:HALO, :] = _rms_rows(prev_ref[...], g1)
        hs_ref[HALO:HALO + tile, :] = _rms_rows(main_ref[...], g1)
        hs_ref[HALO + tile:, :] = _rms_rows(next_ref[...], g1)

    pl.when(in_p)(lambda: load_rows(xp_ref, xpp_ref, xpn_ref))
    pl.when(jnp.logical_not(in_p))(lambda: load_rows(xs_ref, xsp_ref, xsn_ref))

    tok0 = jnp.where(in_p, i, i - ntp) * tile
    seq_len = jnp.where(in_p, seq_p, seq_s)
    not_start = (tok0 % seq_len != 0).astype(F32)
    not_end = ((tok0 + tile) % seq_len != 0).astype(F32)

    p_ref[...] = _dot(hs_ref[...].astype(BF16), w_ref[...])
    for stage in _FINISH_STAGES:
        stage(p_ref, su_ref, cw_ref, cng_ref, gmat_ref, nea_ref, dtb_ref,
              ya_ref, q_ref, k_ref, v_ref, z_ref, gb_ref, tile, not_start, not_end)


def _conv_cols(p_ref, su_ref, cw_ref, lo, width, tile, not_start, not_end):
    cw = CONV_WIDTH
    cols = slice(lo, lo + width)
    if lo < cw:
        su_ref[:, cols] = p_ref[:, cw + lo:cw + lo + width] * p_ref[:, 2 * cw + lo:2 * cw + lo + width]
    else:
        su_ref[:, cols] = p_ref[:, 2 * cw + lo:2 * cw + lo + width]
    su_ref[0:HALO, cols] = su_ref[0:HALO, cols] * not_start
    su_ref[HALO + tile:, cols] = su_ref[HALO + tile:, cols] * not_end
    u = su_ref[:, cols]
    total = tile + 2 * HALO
    return (pltpu.roll(u, 1, axis=0)[HALO:HALO + tile] * cw_ref[0:1, cols]
            + u[HALO:HALO + tile] * cw_ref[1:2, cols]
            + pltpu.roll(u, total - 1, axis=0)[HALO:HALO + tile] * cw_ref[2:3, cols])


def _stage_mixer_a(half):
    width = CONV_WIDTH // 2
    lo = half * width

    def stage(p_ref, su_ref, cw_ref, cng_ref, gmat_ref, nea_ref, dtb_ref,
              ya_ref, q_ref, k_ref, v_ref, z_ref, gb_ref, tile, not_start, not_end):
        conv = _conv_cols(p_ref, su_ref, cw_ref, lo, width, tile, not_start, not_end)
        ya = p_ref[HALO:HALO + tile, lo:lo + width] * conv
        sq = ya * ya
        sq_hi = sq.astype(BF16)
        sq_lo = (sq - sq_hi.astype(F32)).astype(BF16)
        gm = gmat_ref[lo:lo + width, lo:lo + width]
        ms = _dot(sq_hi, gm) + _dot(sq_lo, gm)
        ya_ref[:, lo:lo + width] = (ya * lax.rsqrt(ms + EPS) * cng_ref[:, lo:lo + width]).astype(ya_ref.dtype)
    return stage


def _stage_mixer_b(which, half):
    width = DN_V // 2
    lo_out = half * width
    lo = CONV_WIDTH + which * DN_V + lo_out

    def stage(p_ref, su_ref, cw_ref, cng_ref, gmat_ref, nea_ref, dtb_ref,
              ya_ref, q_ref, k_ref, v_ref, z_ref, gb_ref, tile, not_start, not_end):
        x = _conv_cols(p_ref, su_ref, cw_ref, lo, width, tile, not_start, not_end)
        x = x * _sigmoid(x)
        out_ref = (q_ref, k_ref, v_ref)[which]
        if which == 2:
            out_ref[:, lo_out:lo_out + width] = x.astype(out_ref.dtype)
            return
        scale = DN_DK ** -0.5 if which == 0 else 1.0
        for h in range(width // DN_DK):
            xh = x[:, h * DN_DK:(h + 1) * DN_DK]
            xh = xh * (lax.rsqrt(jnp.sum(xh * xh, axis=-1, keepdims=True) + EPS) * scale)
            out_ref[:, lo_out + h * DN_DK:lo_out + (h + 1) * DN_DK] = xh.astype(out_ref.dtype)
    return stage


def _stage_gates(inner):
    def stage(p_ref, su_ref, cw_ref, cng_ref, gmat_ref, nea_ref, dtb_ref,
              ya_ref, q_ref, k_ref, v_ref, z_ref, gb_ref, tile, not_start, not_end):
        inner(p_ref, su_ref, cw_ref, cng_ref, gmat_ref, nea_ref, dtb_ref,
              ya_ref, q_ref, k_ref, v_ref, z_ref, gb_ref, tile, not_start, not_end)
        cw = CONV_WIDTH
        z_ref[...] = p_ref[HALO:HALO + tile, 6 * cw:7 * cw].astype(z_ref.dtype)
        ab = p_ref[HALO:HALO + tile, 7 * cw:7 * cw + LANES]
        xs = ab + dtb_ref[...]
        softplus = jnp.maximum(xs, 0.0) + jnp.log(1.0 + jnp.exp(-jnp.abs(xs)))
        g = nea_ref[...] * softplus
        beta = _sigmoid(ab)
        r = lax.broadcasted_iota(jnp.int32, (tile, LANES), 0) % CHUNK
        pre = g
        suf = g
        s = 1
        while s < CHUNK:
            pre = pre + jnp.where(r >= s, pltpu.roll(pre, s, axis=0), 0.0)
            suf = suf + jnp.where(r < CHUNK - s, pltpu.roll(suf, tile - s, axis=0), 0.0)
            s *= 2
        lane = lax.broadcasted_iota(jnp.int32, (tile, LANES), 1) // DN_HEADS
        gb_ref[...] = jnp.where(lane == 0, pre,
                      jnp.where(lane == 1, suf,
                      jnp.where(lane < 4, beta,
                      jnp.where(lane == 4, suf - g,
                      jnp.where(lane == 5, pre - g, pre + suf - g)))))
    return stage


_FINISH_STAGES = (_stage_mixer_a(0), _stage_mixer_a(1), _stage_mixer_b(0, 0), _stage_mixer_b(0, 1),
                  _stage_mixer_b(1, 0), _stage_mixer_b(1, 1), _stage_mixer_b(2, 0),
                  _stage_gates(_stage_mixer_b(2, 1)))


def _proj_call(xp, xs, seq_p, seq_s, wts):
    n_p, d = xp.shape
    n_s = xs.shape[0]
    n = n_p + n_s
    tile = TOKEN_TILE
    nt = n // tile
    tb = tile // HALO
    kern = functools.partial(_proj_kernel, tile=tile, ntp=n_p // tile, seq_p=seq_p, seq_s=seq_s)
    const = lambda i: (0, 0)
    main_p, main_s = _two_source_specs(tile, d, n_p, n_s, lambda t: t)
    prev_p, prev_s = _two_source_specs(HALO, d, n_p, n_s, lambda t: t * tb - 1)
    next_p, next_s = _two_source_specs(HALO, d, n_p, n_s, lambda t: (t + 1) * tb)
    in_specs = [main_p, prev_p, next_p, main_s, prev_s, next_s]
    in_specs += [pl.BlockSpec(w.shape, const, pipeline_mode=pl.Buffered(1)) for w in wts]
    widths = (512, 512, 512, 512, 512, LANES)
    dtypes = (BF16, BF16, BF16, BF16, BF16, F32)
    rows = tile + 2 * HALO
    return pl.pallas_call(
        kern, grid=(nt,), in_specs=in_specs,
        out_specs=[pl.BlockSpec((tile, w), lambda i: (i, 0)) for w in widths],
        out_shape=[jax.ShapeDtypeStruct((n, w), dt) for w, dt in zip(widths, dtypes)],
        scratch_shapes=[pltpu.VMEM((rows, d), F32), pltpu.VMEM((rows, 4 * CONV_WIDTH), F32),
                        pltpu.VMEM((rows, wts[1].shape[1]), F32)],
        compiler_params=_cparams(("arbitrary",)), name="proj_conv")(xp, xp, xp, xs, xs, xs, *wts)


def _dn_kernel(*refs, block, sub, reverse, n_prompt, seq_len, nblk):
    if reverse:
        (q_ref, k_ref, v_ref, gb_ref, of_ref, z_ref, ng_ref, o_ref, s_ref) = refs
    else:
        (q_ref, k_ref, v_ref, gb_ref, o_ref, s_ref) = refs
    i = pl.program_id(0)
    b = nblk - 1 - i if reverse else i
    if reverse:
        edge = (b + 1) * block
        reset = (edge == n_prompt) | ((edge > n_prompt) & ((edge - n_prompt) % seq_len == 0))
    else:
        edge = b * block
        reset = (edge == 0) | ((edge >= n_prompt) & ((edge - n_prompt) % seq_len == 0))

    @pl.when(reset)
    def _():
        s_ref[...] = jnp.zeros_like(s_ref)

    nchunk = sub // CHUNK
    c = CHUNK
    heads = range(DN_HEADS)
    ri = lax.broadcasted_iota(jnp.int32, (sub, sub), 0)
    ci = lax.broadcasted_iota(jnp.int32, (sub, sub), 1)
    same = (ri // c) == (ci // c)
    incl = same & ((ri <= ci) if reverse else (ri >= ci))
    strict = same & ((ri < ci) if reverse else (ri > ci))
    pr = lax.broadcasted_iota(jnp.int32, (c, sub), 0)
    pc = lax.broadcasted_iota(jnp.int32, (c, sub), 1)
    eye_p = (pr == pc % c).astype(F32)
    lane_chunk = pc // c

    def fold(bd):
        out = bd[0:c]
        for r in range(1, nchunk):
            out = out + bd[r * c:(r + 1) * c]
        return out

    def unfold(p):
        zero = jnp.zeros_like(p)
        return jnp.concatenate([jnp.where(lane_chunk == r, p, zero) for r in range(nchunk)], axis=0)

    grp = lambda g, h: slice(g * DN_HEADS + h, g * DN_HEADS + h + 1)
    g_cum, g_beta, g_rest, g_tot = (1, 3, 5, 7) if reverse else (0, 2, 4, 6)
    sls = [slice(h * DN_DK, (h + 1) * DN_DK) for h in heads]

    def prepare(r0, out):
        rows = slice(r0, r0 + sub)
        gbv = gb_ref[rows, :]
        gbt = gbv.T
        kb = [k_ref[rows, sl] for sl in sls]
        qb = [q_ref[rows, sl] for sl in sls]
        gcol = [gbv[:, grp(g_cum, h)] for h in heads]
        beta = [gbv[:, grp(g_beta, h)] for h in heads]
        kq = [lax.dot_general(jnp.concatenate([kb[h], qb[h]], axis=0), kb[h],
                              (((1,), (1,)), ((), ())), preferred_element_type=F32) for h in heads]
        yield
        decay = [jnp.exp(jnp.minimum(gcol[h] - gbt[grp(g_cum, h), :], 0.0)) for h in heads]
        bdl = [jnp.where(strict, kq[h][:sub] * decay[h], 0.0) * beta[h] for h in heads]
        bda = [jnp.where(incl, kq[h][sub:] * decay[h], 0.0).astype(BF16) for h in heads]
        yield
        lp = [fold(bdl[h]) for h in heads]
        xp = [eye_p - lp[h] for h in heads]
        bd = [bdl[h].astype(BF16) for h in heads]
        p = 1
        while p < c:
            if p == 1:
                lp = [_dot(lp[h].astype(BF16), bd[h]) for h in heads]
            elif 2 * p < c:
                res = [_dot(jnp.concatenate([lp[h], xp[h]], axis=0).astype(BF16), bd[h]) for h in heads]
                lp = [r[:c] for r in res]
                xp = [xp[h] + res[h][c:] for h in heads]
            else:
                xp = [xp[h] + _dot(xp[h].astype(BF16), bd[h]) for h in heads]
            p *= 2
            if p < c:
                bd = [unfold(lp[h].astype(BF16)) for h in heads]
            yield
        bdt = [unfold(xp[h].astype(BF16)) for h in heads]
        egc = [jnp.exp(gcol[h]) for h in heads]
        kf = [kb[h].astype(F32) for h in heads]
        rhs = [jnp.concatenate([kf[h] * (beta[h] * egc[h]), v_ref[rows, sls[h]].astype(F32) * beta[h]],
                               axis=1).astype(BF16) for h in heads]
        wu = [_dot(bdt[h], rhs[h]) for h in heads]
        yield
        au = [_dot(bda[h], wu[h].astype(BF16)) for h in heads]
        qt = [qb[h].astype(F32) * egc[h] - au[h][:, :DN_DK] for h in heads]
        out["kd"] = [(kf[h] * jnp.exp(gbv[:, grp(g_rest, h)])).astype(BF16) for h in heads]
        out["wqt"] = [jnp.concatenate([wu[h][:, :DN_DK], qt[h]], axis=1).astype(BF16) for h in heads]
        out["u"] = [wu[h][:, DN_DK:] for h in heads]
        out["o0"] = [au[h][:, DN_DK:] for h in heads]
        out["gbt"] = gbt
        yield

    def scan(r0, pre, state):
        order = range(nchunk - 1, -1, -1) if reverse else range(nchunk)
        for cc in order:
            rows = slice(cc * c, (cc + 1) * c)
            out_rows = slice(r0 + cc * c, r0 + (cc + 1) * c)
            for h in heads:
                wqt = pre["wqt"][h]
                lhs = jnp.concatenate([wqt[rows, :DN_DK], wqt[rows, DN_DK:]], axis=0)
                wq = _dot(lhs, state[h].astype(BF16))
                vn = pre["u"][h][rows] - wq[:c]
                o = wq[c:] + pre["o0"][h][rows]
                tot = pre["gbt"][grp(g_tot, h), cc * c:cc * c + 1]
                state[h] = state[h] * jnp.exp(tot) + lax.dot_general(
                    pre["kd"][h][rows], vn.astype(BF16), (((0,), (0,)), ((), ())),
                    preferred_element_type=F32)
                if reverse:
                    o = o + of_ref[out_rows, sls[h]]
                    o = o * lax.rsqrt(jnp.mean(o * o, axis=-1, keepdims=True) + EPS) * ng_ref[...]
                    zc = z_ref[out_rows, sls[h]].astype(F32)
                    o = o * (zc * _sigmoid(zc))
                o_ref[out_rows, sls[h]] = o.astype(o_ref.dtype)
            yield

    nsub = block // sub
    starts = [r * sub for r in (range(nsub - 1, -1, -1) if reverse else range(nsub))]
    state = [s_ref[h] for h in heads]
    pre_prev = None
    for idx in range(nsub + 1):
        pre = {}
        prep_gen = prepare(starts[idx], pre) if idx < nsub else iter(())
        scan_gen = scan(starts[idx - 1], pre_prev, state) if idx > 0 else iter(())
        prep_live = scan_live = True
        while prep_live or scan_live:
            if prep_live:
                prep_live = next(prep_gen, _DONE) is not _DONE
                if prep_live:
                    prep_live = next(prep_gen, _DONE) is not _DONE
            if scan_live:
                scan_live = next(scan_gen, _DONE) is not _DONE
        pre_prev = pre
    for h in heads:
        s_ref[h] = state[h]


_DONE = object()


def _dn_call(q, k, v, gb, extra, reverse, n_prompt, seq_len):
    n = q.shape[0]
    block = DN_BLOCK
    nblk = n // block
    imap = (lambda i: (nblk - 1 - i, 0)) if reverse else (lambda i: (i, 0))
    tok = lambda w: pl.BlockSpec((block, w), imap)
    in_specs = [tok(512), tok(512), tok(512), tok(LANES)]
    args = [q, k, v, gb]
    if reverse:
        o_f, z, ng = extra
        in_specs += [tok(512), tok(512), pl.BlockSpec(ng.shape, lambda i: (0, 0))]
        args += [o_f, z, ng]
    kern = functools.partial(_dn_kernel, block=block, sub=DN_SUB, reverse=reverse,
                             n_prompt=n_prompt, seq_len=seq_len, nblk=nblk)
    return pl.pallas_call(
        kern, grid=(nblk,), in_specs=in_specs, out_specs=tok(512),
        out_shape=jax.ShapeDtypeStruct((n, 512), BF16 if reverse else F32),
        scratch_shapes=[pltpu.VMEM((DN_HEADS, DN_DK, DN_DK), F32)],
        compiler_params=_cparams(("arbitrary",)),
        name="deltanet_bwd" if reverse else "deltanet_fwd")(*args)


def _route_kernel(xp_ref, xs_ref, ya_ref, yb_ref, wo_ref, g2_ref, rw_ref, rb_ref, tri_ref,
                  x2_ref, xa_ref, xb_ref, info_ref, cnt_ref, run_ref, *, tile, ntp):
    i = pl.program_id(0)

    @pl.when(i == 0)
    def _():
        run_ref[...] = jnp.zeros_like(run_ref)

    half = wo_ref.shape[0] // 2
    x = jnp.where(i < ntp, xp_ref[...], xs_ref[...])
    x2 = x + _dot(ya_ref[...], wo_ref[0:half, :]) + _dot(yb_ref[...], wo_ref[half:, :])
    x2_ref[...] = x2
    xn = _rms_rows(x2, g2_ref[...])
    words = _pack_pairs(xn)
    xa_ref[...] = words[:, :ROW_WORDS]
    xb_ref[...] = words[:, ROW_WORDS:]
    logits = _dot(xn.astype(BF16), rw_ref[...]) + rb_ref[...]
    lane = lax.broadcasted_iota(jnp.int32, (tile, LANES), 1).astype(F32)
    neg = jnp.float32(-jnp.inf)

    def first_argmax(vals):
        m = jnp.max(vals, axis=-1, keepdims=True)
        idx = jnp.min(jnp.where(vals == m, lane, float(LANES)), axis=-1, keepdims=True)
        return m, idx

    gl = jnp.where(lane < N_GROUPS, logits, neg)
    gmax, gidx = first_argmax(gl)
    g_w = 1.0 / jnp.sum(jnp.exp(gl - gmax), axis=-1, keepdims=True)
    lo = N_GROUPS + gidx * EXPERTS_PER_GROUP
    el = jnp.where((lane >= lo) & (lane < lo + EXPERTS_PER_GROUP), logits, neg)
    m1, i1 = first_argmax(el)
    m2, i2 = first_argmax(jnp.where(lane == i1, neg, el))
    r = jnp.exp(m2 - m1)
    gate1 = g_w / (1.0 + r)
    gate2 = g_w * r / (1.0 + r)
    e1 = i1 - N_GROUPS
    e2 = i2 - N_GROUPS
    hit1 = lane == e1
    hit2 = lane == e2
    onehot = (hit1 | hit2).astype(BF16)
    before = _dot(tri_ref[...], onehot) + run_ref[...]
    pos1 = jnp.sum(jnp.where(hit1, before, 0.0), axis=-1, keepdims=True)
    pos2 = jnp.sum(jnp.where(hit2, before, 0.0), axis=-1, keepdims=True)
    run = run_ref[...] + jnp.sum(onehot.astype(F32), axis=0, keepdims=True)
    run_ref[...] = run
    cnt_ref[...] = run
    info = jnp.where(lane == 0, e1,
           jnp.where(lane == 1, e2,
           jnp.where(lane == 2, gate1,
           jnp.where(lane == 3, gate2,
           jnp.where(lane == 4, pos1,
           jnp.where(lane == 5, pos2, 0.0))))))
    info_ref[...] = info


def _route_call(xp, xs, ya, yb, wts):
    n_p, d = xp.shape
    n_s = xs.shape[0]
    n = n_p + n_s
    tile = TOKEN_TILE
    const = lambda i: (0, 0)
    tok = lambda w: pl.BlockSpec((tile, w), lambda i: (i, 0))
    main_p, main_s = _two_source_specs(tile, d, n_p, n_s, lambda t: t)
    in_specs = [main_p, main_s, tok(512), tok(512)] + [pl.BlockSpec(w.shape, const) for w in wts]
    out_shape = [jax.ShapeDtypeStruct((n, d), F32), jax.ShapeDtypeStruct((n, ROW_WORDS), U32),
                 jax.ShapeDtypeStruct((n, ROW_WORDS), U32),
                 jax.ShapeDtypeStruct((n, LANES), F32), jax.ShapeDtypeStruct((1, LANES), F32)]
    out_specs = [tok(d), tok(ROW_WORDS), tok(ROW_WORDS), tok(LANES), pl.BlockSpec((1, LANES), const)]
    kern = functools.partial(_route_kernel, tile=tile, ntp=n_p // tile)
    return pl.pallas_call(
        kern, grid=(n // tile,), in_specs=in_specs, out_specs=out_specs, out_shape=out_shape,
        scratch_shapes=[pltpu.VMEM((1, LANES), F32)],
        compiler_params=_cparams(("arbitrary",)), name="oproj_router")(xp, xs, ya, yb, *wts)


def _sc_mesh():
    return plsc.VectorSubcoreMesh(core_axis_name="core", subcore_axis_name="subcore")


def _sc_scatter_rows(x, idx0, idx1, rows):
    n, w = x.shape

    @pl.kernel(out_type=jax.ShapeDtypeStruct((rows, w), x.dtype), mesh=_sc_mesh(), scratch_types=[])
    def scatter(x_hbm, i0_hbm, i1_hbm, o_hbm):
        def body(x_vmem, i0_vmem, i1_vmem):
            pltpu.sync_copy(x_vmem, o_hbm.at[i0_vmem.at[0]])
            pltpu.sync_copy(x_vmem, o_hbm.at[i1_vmem.at[0]])

        pltpu.emit_pipeline(
            body, grid=(n // SC_WINDOW,),
            in_specs=[pl.BlockSpec((SC_WINDOW, w), lambda i: (i, 0)),
                      pl.BlockSpec((1, SC_WINDOW), lambda i: (0, i)),
                      pl.BlockSpec((1, SC_WINDOW), lambda i: (0, i))],
            out_specs=[], core_axis_name=("core", "subcore"),
            dimension_semantics=(pltpu.PARALLEL,))(x_hbm, i0_hbm, i1_hbm)

    return scatter(x, idx0, idx1)


def _sc_gather_rows(y, idx):
    m = idx.shape[1]
    w = y.shape[1]

    @pl.kernel(out_type=jax.ShapeDtypeStruct((m, w), y.dtype), mesh=_sc_mesh(), scratch_types=[])
    def gather(y_hbm, i_hbm, o_hbm):
        def body(i_vmem, o_vmem):
            pltpu.sync_copy(y_hbm.at[i_vmem.at[0]], o_vmem)

        pltpu.emit_pipeline(
            body, grid=(m // SC_WINDOW,),
            in_specs=[pl.BlockSpec((1, SC_WINDOW), lambda i: (0, i))],
            out_specs=[pl.BlockSpec((SC_WINDOW, w), lambda i: (i, 0))],
            core_axis_name=("core", "subcore"),
            dimension_semantics=(pltpu.PARALLEL,))(i_hbm, o_hbm)

    return gather(y, idx)


def _join_rows(wa, wb):
    lo_a, hi_a = _unpack_pairs(wa)
    lo_b, hi_b = _unpack_pairs(wb)
    return jnp.concatenate([lo_a, lo_b, hi_a, hi_b], axis=1)


def _expert_kernel(te_ref, nv_ref, nu_ref, xa_ref, xb_ref, w1_ref, w3_ref, w2_ref, ya_ref, yb_ref):
    del te_ref
    j = pl.program_id(0)

    @pl.when(j < nu_ref[0])
    def _():
        live = lax.broadcasted_iota(jnp.int32, xa_ref.shape, 0) < nv_ref[j]
        zero = jnp.zeros(xa_ref.shape, U32)
        x = _join_rows(jnp.where(live, xa_ref[...], zero),
                       jnp.where(live, xb_ref[...], zero)).astype(BF16)
        h1 = _dot(x, w1_ref[0])
        h3 = _dot(x, w3_ref[0])
        hdn = (h1 * _sigmoid(h1) * h3).astype(BF16)
        words = _pack_pairs(_dot(hdn, w2_ref[0]))
        ya_ref[...] = words[:, :ROW_WORDS]
        yb_ref[...] = words[:, ROW_WORDS:]


def _expert_call(tile_expert, tile_valid, n_used, xa, xb, w1, w3, w2):
    rows, rw = xa.shape
    tm = EXPERT_TILE
    d, ff = w1.shape[1], w1.shape[2]
    row_blk = lambda j, te, nv, nu: (jnp.minimum(j, nu[0] - 1), 0)
    wsel = lambda j, te, nv, nu: (te[j], 0, 0)
    grid_spec = pltpu.PrefetchScalarGridSpec(
        num_scalar_prefetch=3, grid=(rows // tm,),
        in_specs=[pl.BlockSpec((tm, rw), row_blk), pl.BlockSpec((tm, rw), row_blk),
                  pl.BlockSpec((1, d, ff), wsel), pl.BlockSpec((1, d, ff), wsel),
                  pl.BlockSpec((1, ff, d), wsel)],
        out_specs=[pl.BlockSpec((tm, rw), row_blk), pl.BlockSpec((tm, rw), row_blk)])
    return pl.pallas_call(
        _expert_kernel, grid_spec=grid_spec,
        out_shape=[jax.ShapeDtypeStruct((rows, rw), U32), jax.ShapeDtypeStruct((rows, rw), U32)],
        compiler_params=_cparams(("arbitrary",)), name="moe_experts")(
            tile_expert, tile_valid, n_used, xa, xb, w1, w3, w2)


def _combine_kernel(x2_ref, info_ref, gf_ref, a0_ref, b0_ref, a1_ref, b1_ref, yp_ref, ys_ref,
                    *, ntp):
    i = pl.program_id(0)
    info = info_ref[...]
    moe = (info[:, 2:3] * _join_rows(a0_ref[...], b0_ref[...])
           + info[:, 3:4] * _join_rows(a1_ref[...], b1_ref[...]))
    out = _rms_rows(x2_ref[...] + moe, gf_ref[...])

    @pl.when(i < ntp)
    def _():
        yp_ref[...] = out

    @pl.when(i >= ntp)
    def _():
        ys_ref[...] = out


def _combine_call(x2, info, gf, ga, gb, n_p, n_s):
    n, d = x2.shape
    tile = TOKEN_TILE
    nt = n // tile
    ntp = n_p // tile
    first = pl.BlockSpec((tile, ROW_WORDS), lambda i: (i, 0))
    second = pl.BlockSpec((tile, ROW_WORDS), lambda i: (i + nt, 0))
    return pl.pallas_call(
        functools.partial(_combine_kernel, ntp=ntp), grid=(nt,),
        in_specs=[pl.BlockSpec((tile, d), lambda i: (i, 0)),
                  pl.BlockSpec((tile, LANES), lambda i: (i, 0)),
                  pl.BlockSpec(gf.shape, lambda i: (0, 0)),
                  first, first, second, second],
        out_specs=[pl.BlockSpec((tile, d), lambda i: (jnp.minimum(i, ntp - 1), 0)),
                   pl.BlockSpec((tile, d), lambda i: (jnp.maximum(i - ntp, 0), 0))],
        out_shape=[jax.ShapeDtypeStruct((n_p, d), F32), jax.ShapeDtypeStruct((n_s, d), F32)],
        compiler_params=_cparams(("arbitrary",)), name="moe_combine")(
            x2, info, gf, ga, gb, ga, gb)


def _layer(xp, xs, seq_p, seq_s, norm1_g, w_in, conv_a_w, conv_a_norm_g, dn_conv_w, dn_a_log,
           dn_dt_bias, dn_norm_g, w_o, norm2_g, rg_w, rg_b, re_w, re_b, w1, w3, w2, final_g):
    n_p, d = xp.shape
    n_s = xs.shape[0]
    n = n_p + n_s
    row = lambda a: a.reshape(1, -1).astype(F32)

    n_ab = len(_GB_COLS)
    w_main = jnp.concatenate([w_in[:, :7 * CONV_WIDTH], w_in[:, 7 * CONV_WIDTH + _GB_COLS]], axis=1)
    w_main = jnp.pad(w_main, ((0, 0), (0, LANES - n_ab))).astype(BF16)
    cw = jnp.concatenate([conv_a_w, dn_conv_w], axis=1).astype(F32)
    grp = jnp.arange(CONV_WIDTH) // (CONV_WIDTH // CONV_GROUPS)
    gmat = ((grp[:, None] == grp[None, :]).astype(F32) / (CONV_WIDTH // CONV_GROUPS)).astype(BF16)
    is_a = (_GB_COLS < 2 * DN_HEADS)
    a_idx = jnp.where(is_a, _GB_COLS, 0)
    padl = lambda a: jnp.pad(a.reshape(1, -1).astype(F32), ((0, 0), (0, LANES - n_ab)))
    nea = padl(jnp.where(is_a, -jnp.exp(dn_a_log.astype(F32)).reshape(-1)[a_idx], 0.0))
    dtb = padl(jnp.where(is_a, dn_dt_bias.astype(F32).reshape(-1)[a_idx], 0.0))
    proj_w = [row(norm1_g), w_main, cw, row(conv_a_norm_g), gmat, nea, dtb]

    ya, q, k, v, z, gb = _proj_call(xp, xs, seq_p, seq_s, proj_w)

    o_f = _dn_call(q, k, v, gb, None, False, n_p, seq_s)
    yb = _dn_call(q, k, v, gb, (o_f, z, row(dn_norm_g)), True, n_p, seq_s)

    rw = jnp.pad(jnp.concatenate([rg_w, re_w], axis=1).astype(F32),
                 ((0, 0), (0, LANES - N_GROUPS - N_EXPERTS)))
    rb = jnp.pad(jnp.concatenate([rg_b, re_b]).reshape(1, -1).astype(F32),
                 ((0, 0), (0, LANES - N_GROUPS - N_EXPERTS)))
    tile = TOKEN_TILE
    tm = EXPERT_TILE
    tri = (jnp.arange(tile)[:, None] > jnp.arange(tile)[None, :]).astype(BF16)
    route_w = [w_o.astype(BF16), row(norm2_g), rw.astype(BF16), rb, tri]
    x2, xna, xnb, info, cnt = _route_call(xp, xs, ya, yb, route_w)

    counts = cnt[0, :N_EXPERTS].astype(jnp.int32)
    pcounts = (counts + tm - 1) // tm * tm
    pends = jnp.cumsum(pcounts)
    pstarts = pends - pcounts
    eid = jnp.arange(N_EXPERTS, dtype=jnp.int32)
    e12 = info[:, 0:2].astype(jnp.int32)
    start12 = jnp.sum(jnp.where(e12[:, :, None] == eid, pstarts, 0), axis=-1)
    dest = start12 + info[:, 4:6].astype(jnp.int32)
    dest0 = dest[:, 0].reshape(1, n)
    dest1 = dest[:, 1].reshape(1, n)
    n_tiles = 2 * n // tm + N_EXPERTS
    tile_start = jnp.arange(n_tiles, dtype=jnp.int32) * tm
    tile_expert = jnp.minimum(jnp.sum(pends[None, :] <= tile_start[:, None], axis=1),
                              N_EXPERTS - 1).astype(jnp.int32)
    live_end = jnp.sum(jnp.where(tile_expert[:, None] == eid, pstarts + counts, 0), axis=-1)
    tile_valid = jnp.clip(live_end - tile_start, 0, tm).astype(jnp.int32)
    n_used = (pends[-1:] // tm).astype(jnp.int32)

    rows = n_tiles * tm
    xa = _sc_scatter_rows(xna, dest0, dest1, rows)
    xb = _sc_scatter_rows(xnb, dest0, dest1, rows)
    ya_e, yb_e = _expert_call(tile_expert, tile_valid, n_used, xa, xb, w1.astype(BF16),
                              w3.astype(BF16), w2.astype(BF16))
    both = jnp.concatenate([dest0, dest1], axis=1)
    return _combine_call(x2, info, row(final_g), _sc_gather_rows(ya_e, both),
                         _sc_gather_rows(yb_e, both), n_p, n_s)


def kernel(x_prompt, x_sample, norm1_g, w_in, conv_a_w, conv_a_norm_g, dn_conv_w, dn_a_log,
           dn_dt_bias, dn_norm_g, w_o, norm2_g, router_group_w, router_group_b, router_expert_w,
           router_expert_b, w1, w3, w2, final_norm_g):
    assert norm1_g.shape[0] == 1, "single-layer trunk"
    bp, sp, d = x_prompt.shape
    bs, ss, _ = x_sample.shape
    assert bp == 1 and sp % max(TOKEN_TILE, DN_BLOCK) == 0 and ss % max(TOKEN_TILE, DN_BLOCK) == 0
    y_p, y_s = _layer(
        x_prompt.reshape(bp * sp, d), x_sample.reshape(bs * ss, d), sp, ss,
        norm1_g[0], w_in[0], conv_a_w[0], conv_a_norm_g[0], dn_conv_w[0], dn_a_log[0],
        dn_dt_bias[0], dn_norm_g[0], w_o[0], norm2_g[0], router_group_w[0], router_group_b[0],
        router_expert_w[0], router_expert_b[0], w1[0], w3[0], w2[0], final_norm_g)
    return y_p.reshape(bp, sp, d), y_s.reshape(bs, ss, d)
```

```python
import functools

import jax
import jax.numpy as jnp
import numpy as np
from jax import lax
from jax.experimental import pallas as pl
from jax.experimental.pallas import tpu as pltpu
from jax.experimental.pallas import tpu_sc as plsc

F32 = jnp.float32
BF16 = jnp.bfloat16
U32 = jnp.uint32

EPS = 1e-6
CONV_WIDTH = 512
CONV_GROUPS = 8
DN_HEADS = 4
DN_DK = 128
DN_V = 512
CHUNK = 64
N_GROUPS = 4
EXPERTS_PER_GROUP = 8
N_EXPERTS = N_GROUPS * EXPERTS_PER_GROUP
EXPERT_FF = 512
LANES = 128
HALO = 8

TOKEN_TILE = 512
ROUTE_SUB = 256
DN_BLOCK = 1024
DN_SUB = 256
EXPERT_TILE = 512
ROW_WORDS = 256
SC_WINDOW = 128
VMEM_LIMIT = 56 * 1024 * 1024

_GB_COLS = np.concatenate([np.arange(4 * DN_HEADS), np.arange(2 * DN_HEADS), np.arange(2 * DN_HEADS)])


def _cparams(sem):
    return pltpu.CompilerParams(dimension_semantics=sem, vmem_limit_bytes=VMEM_LIMIT)


def _sigmoid(x):
    return 1.0 / (1.0 + jnp.exp(-x))


def _rms_rows(x, g):
    return x * lax.rsqrt(jnp.mean(x * x, axis=-1, keepdims=True) + EPS) * g


def _dot(a, b):
    return jnp.dot(a, b, preferred_element_type=F32)


def _pack_pairs(x):
    half = x.shape[1] // 2
    bits = lambda a: lax.bitcast_convert_type(a.astype(BF16).astype(F32), U32)
    return (bits(x[:, half:]) & jnp.uint32(0xFFFF0000)) | (bits(x[:, :half]) >> 16)


def _unpack_pairs(w):
    lo = lax.bitcast_convert_type(w << 16, F32)
    hi = lax.bitcast_convert_type(w & jnp.uint32(0xFFFF0000), F32)
    return lo, hi


def _two_source_specs(block_rows, width, n_p, n_s, block_of_tile):
    ntp = n_p // TOKEN_TILE
    nbp = n_p // block_rows
    nbs = n_s // block_rows
    clamp = lambda b, nb: jnp.clip(b, 0, nb - 1)
    sp = pl.BlockSpec((block_rows, width), lambda i: (clamp(block_of_tile(jnp.minimum(i, ntp - 1)), nbp), 0))
    ss = pl.BlockSpec((block_rows, width), lambda i: (clamp(block_of_tile(jnp.maximum(i - ntp, 0)), nbs), 0))
    return sp, ss


def _proj_kernel(xp_ref, xpp_ref, xpn_ref, xs_ref, xsp_ref, xsn_ref,
                 g1_ref, w_ref, cw_ref, cng_ref, gmat_ref, nea_ref, dtb_ref,
                 ya_ref, q_ref, k_ref, v_ref, z_ref, gb_ref,
                 hs_ref, su_ref, p_ref, *, tile, ntp, seq_p, seq_s):
    i = pl.program_id(0)
    in_p = i < ntp
    g1 = g1_ref[...]

    def load_rows(main_ref, prev_ref, next_ref):
        hs_ref[0:HALO, :] = _rms_rows(prev_ref[...], g1)
        hs_ref[HALO:HALO + tile, :] = _rms_rows(main_ref[...], g1)
        hs_ref[HALO + tile:, :] = _rms_rows(next_ref[...], g1)

    pl.when(in_p)(lambda: load_rows(xp_ref, xpp_ref, xpn_ref))
    pl.when(jnp.logical_not(in_p))(lambda: load_rows(xs_ref, xsp_ref, xsn_ref))

    tok0 = jnp.where(in_p, i, i - ntp) * tile
    seq_len = jnp.where(in_p, seq_p, seq_s)
    not_start = (tok0 % seq_len != 0).astype(F32)
    not_end = ((tok0 + tile) % seq_len != 0).astype(F32)

    p_ref[...] = _dot(hs_ref[...].astype(BF16), w_ref[...])
    for stage in _FINISH_STAGES:
        stage(p_ref, su_ref, cw_ref, cng_ref, gmat_ref, nea_ref, dtb_ref,
              ya_ref, q_ref, k_ref, v_ref, z_ref, gb_ref, tile, not_start, not_end)


def _conv_cols(p_ref, su_ref, cw_ref, lo, width, tile, not_start, not_end):
    cw = CONV_WIDTH
    cols = slice(lo, lo + width)
    if lo < cw:
        su_ref[:, cols] = p_ref[:, cw + lo:cw + lo + width] * p_ref[:, 2 * cw + lo:2 * cw + lo + width]
    else:
        su_ref[:, cols] = p_ref[:, 2 * cw + lo:2 * cw + lo + width]
    su_ref[0:HALO, cols] = su_ref[0:HALO, cols] * not_start
    su_ref[HALO + tile:, cols] = su_ref[HALO + tile:, cols] * not_end
    u = su_ref[:, cols]
    total = tile + 2 * HALO
    return (pltpu.roll(u, 1, axis=0)[HALO:HALO + tile] * cw_ref[0:1, cols]
            + u[HALO:HALO + tile] * cw_ref[1:2, cols]
            + pltpu.roll(u, total - 1, axis=0)[HALO:HALO + tile] * cw_ref[2:3, cols])


def _stage_mixer_a(half):
    width = CONV_WIDTH // 2
    lo = half * width

    def stage(p_ref, su_ref, cw_ref, cng_ref, gmat_ref, nea_ref, dtb_ref,
              ya_ref, q_ref, k_ref, v_ref, z_ref, gb_ref, tile, not_start, not_end):
        conv = _conv_cols(p_ref, su_ref, cw_ref, lo, width, tile, not_start, not_end)
        ya = p_ref[HALO:HALO + tile, lo:lo + width] * conv
        ms = _dot((ya * ya).astype(BF16), gmat_ref[lo:lo + width, lo:lo + width])
        ya_ref[:, lo:lo + width] = (ya * lax.rsqrt(ms + EPS) * cng_ref[:, lo:lo + width]).astype(ya_ref.dtype)
    return stage


def _stage_mixer_b(which, half):
    width = DN_V // 2
    lo_out = half * width
    lo = CONV_WIDTH + which * DN_V + lo_out

    def stage(p_ref, su_ref, cw_ref, cng_ref, gmat_ref, nea_ref, dtb_ref,
              ya_ref, q_ref, k_ref, v_ref, z_ref, gb_ref, tile, not_start, not_end):
        x = _conv_cols(p_ref, su_ref, cw_ref, lo, width, tile, not_start, not_end)
        x = x * _sigmoid(x)
        out_ref = (q_ref, k_ref, v_ref)[which]
        if which == 2:
            out_ref[:, lo_out:lo_out + width] = x.astype(out_ref.dtype)
            return
        scale = DN_DK ** -0.5 if which == 0 else 1.0
        for h in range(width // DN_DK):
            xh = x[:, h * DN_DK:(h + 1) * DN_DK]
            xh = xh * (lax.rsqrt(jnp.sum(xh * xh, axis=-1, keepdims=True) + EPS) * scale)
            out_ref[:, lo_out + h * DN_DK:lo_out + (h + 1) * DN_DK] = xh.astype(out_ref.dtype)
    return stage


def _stage_gates(inner):
    def stage(p_ref, su_ref, cw_ref, cng_ref, gmat_ref, nea_ref, dtb_ref,
              ya_ref, q_ref, k_ref, v_ref, z_ref, gb_ref, tile, not_start, not_end):
        inner(p_ref, su_ref, cw_ref, cng_ref, gmat_ref, nea_ref, dtb_ref,
              ya_ref, q_ref, k_ref, v_ref, z_ref, gb_ref, tile, not_start, not_end)
        cw = CONV_WIDTH
        z_ref[...] = p_ref[HALO:HALO + tile, 6 * cw:7 * cw].astype(z_ref.dtype)
        ab = p_ref[HALO:HALO + tile, 7 * cw:7 * cw + LANES]
        xs = ab + dtb_ref[...]
        softplus = jnp.maximum(xs, 0.0) + jnp.log(1.0 + jnp.exp(-jnp.abs(xs)))
        g = nea_ref[...] * softplus
        beta = _sigmoid(ab)
        r = lax.broadcasted_iota(jnp.int32, (tile, LANES), 0) % CHUNK
        pre = g
        suf = g
        s = 1
        while s < CHUNK:
            pre = pre + jnp.where(r >= s, pltpu.roll(pre, s, axis=0), 0.0)
            suf = suf + jnp.where(r < CHUNK - s, pltpu.roll(suf, tile - s, axis=0), 0.0)
            s *= 2
        lane = lax.broadcasted_iota(jnp.int32, (tile, LANES), 1) // DN_HEADS
        gb_ref[...] = jnp.where(lane == 0, pre,
                      jnp.where(lane == 1, suf,
                      jnp.where(lane < 4, beta,
                      jnp.where(lane == 4, suf - g,
                      jnp.where(lane == 5, pre - g, pre + suf - g)))))
    return stage


_FINISH_STAGES = (_stage_mixer_a(0), _stage_mixer_a(1), _stage_mixer_b(0, 0), _stage_mixer_b(0, 1),
                  _stage_mixer_b(1, 0), _stage_mixer_b(1, 1), _stage_mixer_b(2, 0),
                  _stage_gates(_stage_mixer_b(2, 1)))


def _proj_call(xp, xs, seq_p, seq_s, wts):
    n_p, d = xp.shape
    n_s = xs.shape[0]
    n = n_p + n_s
    tile = TOKEN_TILE
    nt = n // tile
    tb = tile // HALO
    kern = functools.partial(_proj_kernel, tile=tile, ntp=n_p // tile, seq_p=seq_p, seq_s=seq_s)
    const = lambda i: (0, 0)
    main_p, main_s = _two_source_specs(tile, d, n_p, n_s, lambda t: t)
    prev_p, prev_s = _two_source_specs(HALO, d, n_p, n_s, lambda t: t * tb - 1)
    next_p, next_s = _two_source_specs(HALO, d, n_p, n_s, lambda t: (t + 1) * tb)
    in_specs = [main_p, prev_p, next_p, main_s, prev_s, next_s]
    in_specs += [pl.BlockSpec(w.shape, const, pipeline_mode=pl.Buffered(1)) for w in wts]
    widths = (512, 512, 512, 512, 512, LANES)
    dtypes = (BF16, BF16, BF16, BF16, BF16, F32)
    rows = tile + 2 * HALO
    return pl.pallas_call(
        kern, grid=(nt,), in_specs=in_specs,
        out_specs=[pl.BlockSpec((tile, w), lambda i: (i, 0)) for w in widths],
        out_shape=[jax.ShapeDtypeStruct((n, w), dt) for w, dt in zip(widths, dtypes)],
        scratch_shapes=[pltpu.VMEM((rows, d), F32), pltpu.VMEM((rows, 4 * CONV_WIDTH), F32),
                        pltpu.VMEM((rows, wts[1].shape[1]), F32)],
        compiler_params=_cparams(("arbitrary",)), name="proj_conv")(xp, xp, xp, xs, xs, xs, *wts)


def _dn_kernel(*refs, block, sub, reverse, n_prompt, seq_len, nblk):
    if reverse:
        (q_ref, k_ref, v_ref, gb_ref, of_ref, z_ref, ng_ref, o_ref, s_ref) = refs
    else:
        (q_ref, k_ref, v_ref, gb_ref, o_ref, s_ref) = refs
    i = pl.program_id(0)
    b = nblk - 1 - i if reverse else i
    if reverse:
        edge = (b + 1) * block
        reset = (edge == n_prompt) | ((edge > n_prompt) & ((edge - n_prompt) % seq_len == 0))
    else:
        edge = b * block
        reset = (edge == 0) | ((edge >= n_prompt) & ((edge - n_prompt) % seq_len == 0))

    @pl.when(reset)
    def _():
        s_ref[...] = jnp.zeros_like(s_ref)

    nchunk = sub // CHUNK
    c = CHUNK
    heads = range(DN_HEADS)
    ri = lax.broadcasted_iota(jnp.int32, (sub, sub), 0)
    ci = lax.broadcasted_iota(jnp.int32, (sub, sub), 1)
    same = (ri // c) == (ci // c)
    incl = same & ((ri <= ci) if reverse else (ri >= ci))
    strict = same & ((ri < ci) if reverse else (ri > ci))
    pr = lax.broadcasted_iota(jnp.int32, (c, sub), 0)
    pc = lax.broadcasted_iota(jnp.int32, (c, sub), 1)
    eye_p = (pr == pc % c).astype(F32)
    lane_chunk = pc // c

    def fold(bd):
        out = bd[0:c]
        for r in range(1, nchunk):
            out = out + bd[r * c:(r + 1) * c]
        return out

    def unfold(p):
        zero = jnp.zeros_like(p)
        return jnp.concatenate([jnp.where(lane_chunk == r, p, zero) for r in range(nchunk)], axis=0)

    grp = lambda g, h: slice(g * DN_HEADS + h, g * DN_HEADS + h + 1)
    g_cum, g_beta, g_rest, g_tot = (1, 3, 5, 7) if reverse else (0, 2, 4, 6)
    sls = [slice(h * DN_DK, (h + 1) * DN_DK) for h in heads]

    def prepare(r0, out):
        rows = slice(r0, r0 + sub)
        gbv = gb_ref[rows, :]
        gbt = gbv.T
        kb = [k_ref[rows, sl] for sl in sls]
        qb = [q_ref[rows, sl] for sl in sls]
        gcol = [gbv[:, grp(g_cum, h)] for h in heads]
        beta = [gbv[:, grp(g_beta, h)] for h in heads]
        kq = [lax.dot_general(jnp.concatenate([kb[h], qb[h]], axis=0), kb[h],
                              (((1,), (1,)), ((), ())), preferred_element_type=F32) for h in heads]
        yield
        decay = [jnp.exp(jnp.minimum(gcol[h] - gbt[grp(g_cum, h), :], 0.0)) for h in heads]
        bdl = [jnp.where(strict, kq[h][:sub] * decay[h], 0.0) * beta[h] for h in heads]
        bda = [jnp.where(incl, kq[h][sub:] * decay[h], 0.0).astype(BF16) for h in heads]
        yield
        lp = [fold(bdl[h]) for h in heads]
        xp = [eye_p - lp[h] for h in heads]
        bd = [bdl[h].astype(BF16) for h in heads]
        p = 1
        while p < c:
            if p == 1:
                lp = [_dot(lp[h].astype(BF16), bd[h]) for h in heads]
            elif 2 * p < c:
                res = [_dot(jnp.concatenate([lp[h], xp[h]], axis=0).astype(BF16), bd[h]) for h in heads]
                lp = [r[:c] for r in res]
                xp = [xp[h] + res[h][c:] for h in heads]
            else:
                xp = [xp[h] + _dot(xp[h].astype(BF16), bd[h]) for h in heads]
            p *= 2
            if p < c:
                bd = [unfold(lp[h].astype(BF16)) for h in heads]
            yield
        bdt = [unfold(xp[h].astype(BF16)) for h in heads]
        egc = [jnp.exp(gcol[h]) for h in heads]
        kf = [kb[h].astype(F32) for h in heads]
        rhs = [jnp.concatenate([kf[h] * (beta[h] * egc[h]), v_ref[rows, sls[h]].astype(F32) * beta[h]],
                               axis=1).astype(BF16) for h in heads]
        wu = [_dot(bdt[h], rhs[h]) for h in heads]
        yield
        au = [_dot(bda[h], wu[h].astype(BF16)) for h in heads]
        qt = [qb[h].astype(F32) * egc[h] - au[h][:, :DN_DK] for h in heads]
        out["kd"] = [(kf[h] * jnp.exp(gbv[:, grp(g_rest, h)])).astype(BF16) for h in heads]
        out["wqt"] = [jnp.concatenate([wu[h][:, :DN_DK], qt[h]], axis=1).astype(BF16) for h in heads]
        out["u"] = [wu[h][:, DN_DK:] for h in heads]
        out["o0"] = [au[h][:, DN_DK:] for h in heads]
        out["gbt"] = gbt
        yield

    def scan(r0, pre, state):
        order = range(nchunk - 1, -1, -1) if reverse else range(nchunk)
        for cc in order:
            rows = slice(cc * c, (cc + 1) * c)
            out_rows = slice(r0 + cc * c, r0 + (cc + 1) * c)
            for h in heads:
                wqt = pre["wqt"][h]
                lhs = jnp.concatenate([wqt[rows, :DN_DK], wqt[rows, DN_DK:]], axis=0)
                wq = _dot(lhs, state[h].astype(BF16))
                vn = pre["u"][h][rows] - wq[:c]
                o = wq[c:] + pre["o0"][h][rows]
                tot = pre["gbt"][grp(g_tot, h), cc * c:cc * c + 1]
                state[h] = state[h] * jnp.exp(tot) + lax.dot_general(
                    pre["kd"][h][rows], vn.astype(BF16), (((0,), (0,)), ((), ())),
                    preferred_element_type=F32)
                if reverse:
                    o = o + of_ref[out_rows, sls[h]]
                    o = o * lax.rsqrt(jnp.mean(o * o, axis=-1, keepdims=True) + EPS) * ng_ref[...]
                    zc = z_ref[out_rows, sls[h]].astype(F32)
                    o = o * (zc * _sigmoid(zc))
                o_ref[out_rows, sls[h]] = o.astype(o_ref.dtype)
            yield

    nsub = block // sub
    starts = [r * sub for r in (range(nsub - 1, -1, -1) if reverse else range(nsub))]
    state = [s_ref[h] for h in heads]
    pre_prev = None
    for idx in range(nsub + 1):
        pre = {}
        prep_gen = prepare(starts[idx], pre) if idx < nsub else iter(())
        scan_gen = scan(starts[idx - 1], pre_prev, state) if idx > 0 else iter(())
        prep_live = scan_live = True
        while prep_live or scan_live:
            if prep_live:
                prep_live = next(prep_gen, _DONE) is not _DONE
                if prep_live:
                    prep_live = next(prep_gen, _DONE) is not _DONE
            if scan_live:
                scan_live = next(scan_gen, _DONE) is not _DONE
        pre_prev = pre
    for h in heads:
        s_ref[h] = state[h]


_DONE = object()


def _dn_call(q, k, v, gb, extra, reverse, n_prompt, seq_len):
    n = q.shape[0]
    block = DN_BLOCK
    nblk = n // block
    imap = (lambda i: (nblk - 1 - i, 0)) if reverse else (lambda i: (i, 0))
    tok = lambda w: pl.BlockSpec((block, w), imap)
    in_specs = [tok(512), tok(512), tok(512), tok(LANES)]
    args = [q, k, v, gb]
    if reverse:
        o_f, z, ng = extra
        in_specs += [tok(512), tok(512), pl.BlockSpec(ng.shape, lambda i: (0, 0))]
        args += [o_f, z, ng]
    kern = functools.partial(_dn_kernel, block=block, sub=DN_SUB, reverse=reverse,
                             n_prompt=n_prompt, seq_len=seq_len, nblk=nblk)
    return pl.pallas_call(
        kern, grid=(nblk,), in_specs=in_specs, out_specs=tok(512),
        out_shape=jax.ShapeDtypeStruct((n, 512), BF16 if reverse else F32),
        scratch_shapes=[pltpu.VMEM((DN_HEADS, DN_DK, DN_DK), F32)],
        compiler_params=_cparams(("arbitrary",)),
        name="deltanet_bwd" if reverse else "deltanet_fwd")(*args)


def _route_kernel(xp_ref, xs_ref, ya_ref, yb_ref, wo_ref, g2_ref, rw_ref, rb_ref, tri_ref,
                  x2_ref, xa_ref, xb_ref, info_ref, cnt_ref, run_ref, *, tile, ntp):
    i = pl.program_id(0)

    @pl.when(i == 0)
    def _():
        run_ref[...] = jnp.zeros_like(run_ref)

    sub = tri_ref.shape[0]
    parts = [slice(r, r + sub) for r in range(0, tile, sub)]
    each = lambda f, *lists: [f(*a) for a in zip(*lists)]
    half = wo_ref.shape[0] // 2
    in_p = i < ntp
    x2 = [jnp.where(in_p, xp_ref[rows, :], xs_ref[rows, :]) + _dot(ya_ref[rows, :], wo_ref[0:half, :])
          + _dot(yb_ref[rows, :], wo_ref[half:, :]) for rows in parts]
    for rows, v in zip(parts, x2):
        x2_ref[rows, :] = v
    xn = [_rms_rows(v, g2_ref[...]) for v in x2]
    for rows, v in zip(parts, xn):
        words = _pack_pairs(v)
        xa_ref[rows, :] = words[:, :ROW_WORDS]
        xb_ref[rows, :] = words[:, ROW_WORDS:]
    logits = [_dot(v.astype(BF16), rw_ref[...]) + rb_ref[...] for v in xn]
    lane = lax.broadcasted_iota(jnp.int32, (sub, LANES), 1).astype(F32)
    neg = jnp.float32(-jnp.inf)
    row_max = lambda v: jnp.max(v, axis=-1, keepdims=True)
    first_at = lambda v, m: jnp.min(jnp.where(v == m, lane, float(LANES)), axis=-1, keepdims=True)

    gl = [jnp.where(lane < N_GROUPS, v, neg) for v in logits]
    gmax = each(row_max, gl)
    gidx = each(first_at, gl, gmax)
    g_w = [1.0 / jnp.sum(jnp.exp(v - m), axis=-1, keepdims=True) for v, m in zip(gl, gmax)]
    lo = [N_GROUPS + g * EXPERTS_PER_GROUP for g in gidx]
    el = [jnp.where((lane >= a) & (lane < a + EXPERTS_PER_GROUP), v, neg) for v, a in zip(logits, lo)]
    m1 = each(row_max, el)
    i1 = each(first_at, el, m1)
    el2 = [jnp.where(lane == a, neg, v) for v, a in zip(el, i1)]
    m2 = each(row_max, el2)
    i2 = each(first_at, el2, m2)
    hit1 = [lane == a - N_GROUPS for a in i1]
    hit2 = [lane == a - N_GROUPS for a in i2]
    onehot = [(a | b).astype(BF16) for a, b in zip(hit1, hit2)]
    inside = [_dot(tri_ref[...], v) for v in onehot]
    run = run_ref[...]
    for k, rows in enumerate(parts):
        before = inside[k] + run
        pos1 = jnp.sum(jnp.where(hit1[k], before, 0.0), axis=-1, keepdims=True)
        pos2 = jnp.sum(jnp.where(hit2[k], before, 0.0), axis=-1, keepdims=True)
        run = run + jnp.sum(onehot[k].astype(F32), axis=0, keepdims=True)
        r = jnp.exp(m2[k] - m1[k])
        gate1 = g_w[k] / (1.0 + r)
        info_ref[rows, :] = jnp.where(lane == 0, i1[k] - N_GROUPS,
                            jnp.where(lane == 1, i2[k] - N_GROUPS,
                            jnp.where(lane == 2, gate1,
                            jnp.where(lane == 3, gate1 * r,
                            jnp.where(lane == 4, pos1,
                            jnp.where(lane == 5, pos2, 0.0))))))
    run_ref[...] = run
    cnt_ref[...] = run


def _route_call(xp, xs, ya, yb, wts):
    n_p, d = xp.shape
    n_s = xs.shape[0]
    n = n_p + n_s
    tile = TOKEN_TILE
    const = lambda i: (0, 0)
    tok = lambda w: pl.BlockSpec((tile, w), lambda i: (i, 0))
    main_p, main_s = _two_source_specs(tile, d, n_p, n_s, lambda t: t)
    in_specs = [main_p, main_s, tok(512), tok(512)] + [pl.BlockSpec(w.shape, const) for w in wts]
    out_shape = [jax.ShapeDtypeStruct((n, d), F32), jax.ShapeDtypeStruct((n, ROW_WORDS), U32),
                 jax.ShapeDtypeStruct((n, ROW_WORDS), U32),
                 jax.ShapeDtypeStruct((n, LANES), F32), jax.ShapeDtypeStruct((1, LANES), F32)]
    out_specs = [tok(d), tok(ROW_WORDS), tok(ROW_WORDS), tok(LANES), pl.BlockSpec((1, LANES), const)]
    kern = functools.partial(_route_kernel, tile=tile, ntp=n_p // tile)
    return pl.pallas_call(
        kern, grid=(n // tile,), in_specs=in_specs, out_specs=out_specs, out_shape=out_shape,
        scratch_shapes=[pltpu.VMEM((1, LANES), F32)],
        compiler_params=_cparams(("arbitrary",)), name="oproj_router")(xp, xs, ya, yb, *wts)


def _sc_mesh():
    return plsc.VectorSubcoreMesh(core_axis_name="core", subcore_axis_name="subcore")


def _sc_scatter_rows(x, idx0, idx1, rows):
    n, w = x.shape

    @pl.kernel(out_type=jax.ShapeDtypeStruct((rows, w), x.dtype), mesh=_sc_mesh(), scratch_types=[])
    def scatter(x_hbm, i0_hbm, i1_hbm, o_hbm):
        def body(x_vmem, i0_vmem, i1_vmem):
            pltpu.sync_copy(x_vmem, o_hbm.at[i0_vmem.at[0]])
            pltpu.sync_copy(x_vmem, o_hbm.at[i1_vmem.at[0]])

        pltpu.emit_pipeline(
            body, grid=(n // SC_WINDOW,),
            in_specs=[pl.BlockSpec((SC_WINDOW, w), lambda i: (i, 0)),
                      pl.BlockSpec((1, SC_WINDOW), lambda i: (0, i)),
                      pl.BlockSpec((1, SC_WINDOW), lambda i: (0, i))],
            out_specs=[], core_axis_name=("core", "subcore"),
            dimension_semantics=(pltpu.PARALLEL,))(x_hbm, i0_hbm, i1_hbm)

    return scatter(x, idx0, idx1)


def _sc_gather_rows(y, idx):
    m = idx.shape[1]
    w = y.shape[1]

    @pl.kernel(out_type=jax.ShapeDtypeStruct((m, w), y.dtype), mesh=_sc_mesh(), scratch_types=[])
    def gather(y_hbm, i_hbm, o_hbm):
        def body(i_vmem, o_vmem):
            pltpu.sync_copy(y_hbm.at[i_vmem.at[0]], o_vmem)

        pltpu.emit_pipeline(
            body, grid=(m // SC_WINDOW,),
            in_specs=[pl.BlockSpec((1, SC_WINDOW), lambda i: (0, i))],
            out_specs=[pl.BlockSpec((SC_WINDOW, w), lambda i: (i, 0))],
            core_axis_name=("core", "subcore"),
            dimension_semantics=(pltpu.PARALLEL,))(i_hbm, o_hbm)

    return gather(y, idx)


def _join_rows(wa, wb):
    lo_a, hi_a = _unpack_pairs(wa)
    lo_b, hi_b = _unpack_pairs(wb)
    return jnp.concatenate([lo_a, lo_b, hi_a, hi_b], axis=1)


def _expert_kernel(te_ref, nv_ref, nu_ref, xa_ref, xb_ref, w1_ref, w3_ref, w2_ref, ya_ref, yb_ref):
    del te_ref
    j = pl.program_id(0)

    @pl.when(j < nu_ref[0])
    def _():
        live = lax.broadcasted_iota(jnp.int32, xa_ref.shape, 0) < nv_ref[j]
        zero = jnp.zeros(xa_ref.shape, U32)
        x = _join_rows(jnp.where(live, xa_ref[...], zero),
                       jnp.where(live, xb_ref[...], zero)).astype(BF16)
        h1 = _dot(x, w1_ref[0])
        h3 = _dot(x, w3_ref[0])
        hdn = (h1 * _sigmoid(h1) * h3).astype(BF16)
        words = _pack_pairs(_dot(hdn, w2_ref[0]))
        ya_ref[...] = words[:, :ROW_WORDS]
        yb_ref[...] = words[:, ROW_WORDS:]


def _expert_call(tile_expert, tile_valid, n_used, xa, xb, w1, w3, w2):
    rows, rw = xa.shape
    tm = EXPERT_TILE
    d, ff = w1.shape[1], w1.shape[2]
    row_blk = lambda j, te, nv, nu: (jnp.minimum(j, nu[0] - 1), 0)
    wsel = lambda j, te, nv, nu: (te[j], 0, 0)
    grid_spec = pltpu.PrefetchScalarGridSpec(
        num_scalar_prefetch=3, grid=(rows // tm,),
        in_specs=[pl.BlockSpec((tm, rw), row_blk), pl.BlockSpec((tm, rw), row_blk),
                  pl.BlockSpec((1, d, ff), wsel), pl.BlockSpec((1, d, ff), wsel),
                  pl.BlockSpec((1, ff, d), wsel)],
        out_specs=[pl.BlockSpec((tm, rw), row_blk), pl.BlockSpec((tm, rw), row_blk)])
    return pl.pallas_call(
        _expert_kernel, grid_spec=grid_spec,
        out_shape=[jax.ShapeDtypeStruct((rows, rw), U32), jax.ShapeDtypeStruct((rows, rw), U32)],
        compiler_params=_cparams(("arbitrary",)), name="moe_experts")(
            tile_expert, tile_valid, n_used, xa, xb, w1, w3, w2)


def _combine_kernel(x2_ref, info_ref, gf_ref, a0_ref, b0_ref, a1_ref, b1_ref, y_ref):
    info = info_ref[...]
    moe = (info[:, 2:3] * _join_rows(a0_ref[...], b0_ref[...])
           + info[:, 3:4] * _join_rows(a1_ref[...], b1_ref[...]))
    y_ref[...] = _rms_rows(x2_ref[...] + moe, gf_ref[...])


def _combine_call(x2, info, gf, ga, gb, row_off):
    d = x2.shape[1]
    m = ga.shape[0] // 2
    tile = TOKEN_TILE
    nt = m // tile
    off = row_off // tile
    first = pl.BlockSpec((tile, ROW_WORDS), lambda i: (i, 0))
    second = pl.BlockSpec((tile, ROW_WORDS), lambda i: (i + nt, 0))
    return pl.pallas_call(
        _combine_kernel, grid=(nt,),
        in_specs=[pl.BlockSpec((tile, d), lambda i: (i + off, 0)),
                  pl.BlockSpec((tile, LANES), lambda i: (i + off, 0)),
                  pl.BlockSpec(gf.shape, lambda i: (0, 0)),
                  first, first, second, second],
        out_specs=pl.BlockSpec((tile, d), lambda i: (i, 0)),
        out_shape=jax.ShapeDtypeStruct((m, d), F32),
        compiler_params=_cparams(("arbitrary",)), name="moe_combine")(
            x2, info, gf, ga, gb, ga, gb)


def _layer(xp, xs, seq_p, seq_s, norm1_g, w_in, conv_a_w, conv_a_norm_g, dn_conv_w, dn_a_log,
           dn_dt_bias, dn_norm_g, w_o, norm2_g, rg_w, rg_b, re_w, re_b, w1, w3, w2, final_g):
    n_p, d = xp.shape
    n_s = xs.shape[0]
    n = n_p + n_s
    row = lambda a: a.reshape(1, -1).astype(F32)

    n_ab = len(_GB_COLS)
    w_main = jnp.concatenate([w_in[:, :7 * CONV_WIDTH], w_in[:, 7 * CONV_WIDTH + _GB_COLS]], axis=1)
    w_main = jnp.pad(w_main, ((0, 0), (0, LANES - n_ab))).astype(BF16)
    cw = jnp.concatenate([conv_a_w, dn_conv_w], axis=1).astype(F32)
    grp = jnp.arange(CONV_WIDTH) // (CONV_WIDTH // CONV_GROUPS)
    gmat = ((grp[:, None] == grp[None, :]).astype(F32) / (CONV_WIDTH // CONV_GROUPS)).astype(BF16)
    is_a = (_GB_COLS < 2 * DN_HEADS)
    a_idx = jnp.where(is_a, _GB_COLS, 0)
    padl = lambda a: jnp.pad(a.reshape(1, -1).astype(F32), ((0, 0), (0, LANES - n_ab)))
    nea = padl(jnp.where(is_a, -jnp.exp(dn_a_log.astype(F32)).reshape(-1)[a_idx], 0.0))
    dtb = padl(jnp.where(is_a, dn_dt_bias.astype(F32).reshape(-1)[a_idx], 0.0))
    proj_w = [row(norm1_g), w_main, cw, row(conv_a_norm_g), gmat, nea, dtb]

    ya, q, k, v, z, gb = _proj_call(xp, xs, seq_p, seq_s, proj_w)

    o_f = _dn_call(q, k, v, gb, None, False, n_p, seq_s)
    yb = _dn_call(q, k, v, gb, (o_f, z, row(dn_norm_g)), True, n_p, seq_s)

    rw = jnp.pad(jnp.concatenate([rg_w, re_w], axis=1).astype(F32),
                 ((0, 0), (0, LANES - N_GROUPS - N_EXPERTS)))
    rb = jnp.pad(jnp.concatenate([rg_b, re_b]).reshape(1, -1).astype(F32),
                 ((0, 0), (0, LANES - N_GROUPS - N_EXPERTS)))
    tile = TOKEN_TILE
    tm = EXPERT_TILE
    tri = (jnp.arange(ROUTE_SUB)[:, None] > jnp.arange(ROUTE_SUB)[None, :]).astype(BF16)
    route_w = [w_o.astype(BF16), row(norm2_g), rw.astype(BF16), rb, tri]
    x2, xna, xnb, info, cnt = _route_call(xp, xs, ya, yb, route_w)

    counts = cnt[0, :N_EXPERTS].astype(jnp.int32)
    pcounts = (counts + tm - 1) // tm * tm
    pends = jnp.cumsum(pcounts)
    pstarts = pends - pcounts
    eid = jnp.arange(N_EXPERTS, dtype=jnp.int32)
    e12 = info[:, 0:2].astype(jnp.int32)
    start12 = jnp.sum(jnp.where(e12[:, :, None] == eid, pstarts, 0), axis=-1)
    dest = start12 + info[:, 4:6].astype(jnp.int32)
    dest0 = dest[:, 0].reshape(1, n)
    dest1 = dest[:, 1].reshape(1, n)
    n_tiles = 2 * n // tm + N_EXPERTS
    tile_start = jnp.arange(n_tiles, dtype=jnp.int32) * tm
    tile_expert = jnp.minimum(jnp.sum(pends[None, :] <= tile_start[:, None], axis=1),
                              N_EXPERTS - 1).astype(jnp.int32)
    live_end = jnp.sum(jnp.where(tile_expert[:, None] == eid, pstarts + counts, 0), axis=-1)
    tile_valid = jnp.clip(live_end - tile_start, 0, tm).astype(jnp.int32)
    n_used = (pends[-1:] // tm).astype(jnp.int32)

    rows = n_tiles * tm
    xa = _sc_scatter_rows(xna, dest0, dest1, rows)
    xb = _sc_scatter_rows(xnb, dest0, dest1, rows)
    ya_e, yb_e = _expert_call(tile_expert, tile_valid, n_used, xa, xb, w1.astype(BF16),
                              w3.astype(BF16), w2.astype(BF16))
    outs = []
    for lo, hi in ((0, n_p), (n_p, n)):
        both = jnp.concatenate([dest0[:, lo:hi], dest1[:, lo:hi]], axis=1)
        outs.append(_combine_call(x2, info, row(final_g), _sc_gather_rows(ya_e, both),
                                  _sc_gather_rows(yb_e, both), lo))
    return tuple(outs)


def kernel(x_prompt, x_sample, norm1_g, w_in, conv_a_w, conv_a_norm_g, dn_conv_w, dn_a_log,
           dn_dt_bias, dn_norm_g, w_o, norm2_g, router_group_w, router_group_b, router_expert_w,
           router_expert_b, w1, w3, w2, final_norm_g):
    assert norm1_g.shape[0] == 1, "single-layer trunk"
    bp, sp, d = x_prompt.shape
    bs, ss, _ = x_sample.shape
    assert bp == 1 and sp % max(TOKEN_TILE, DN_BLOCK) == 0 and ss % max(TOKEN_TILE, DN_BLOCK) == 0
    y_p, y_s = _layer(
        x_prompt.reshape(bp * sp, d), x_sample.reshape(bs * ss, d), sp, ss,
        norm1_g[0], w_in[0], conv_a_w[0], conv_a_norm_g[0], dn_conv_w[0], dn_a_log[0],
        dn_dt_bias[0], dn_norm_g[0], w_o[0], norm2_g[0], router_group_w[0], router_group_b[0],
        router_expert_w[0], router_expert_b[0], w1[0], w3[0], w2[0], final_norm_g)
    return y_p.reshape(bp, sp, d), y_s.reshape(bs, ss, d)
```

```python
import functools

import jax
import jax.numpy as jnp
import numpy as np
from jax import lax
from jax.experimental import pallas as pl
from jax.experimental.pallas import tpu as pltpu
from jax.experimental.pallas import tpu_sc as plsc

F32 = jnp.float32
BF16 = jnp.bfloat16
U32 = jnp.uint32

EPS = 1e-6
CONV_WIDTH = 512
CONV_GROUPS = 8
DN_HEADS = 4
DN_DK = 128
DN_V = 512
CHUNK = 64
N_GROUPS = 4
EXPERTS_PER_GROUP = 8
N_EXPERTS = N_GROUPS * EXPERTS_PER_GROUP
EXPERT_FF = 512
LANES = 128
HALO = 8

TOKEN_TILE = 512
ROUTE_SUB = 256
DN_BLOCK = 1024
DN_SUB = 256
EXPERT_TILE = 512
ROW_WORDS = 256
SC_WINDOW = 128
VMEM_LIMIT = 56 * 1024 * 1024

_GB_COLS = np.concatenate([np.arange(4 * DN_HEADS), np.arange(2 * DN_HEADS), np.arange(2 * DN_HEADS)])


def _cparams(sem):
    return pltpu.CompilerParams(dimension_semantics=sem, vmem_limit_bytes=VMEM_LIMIT)


def _sigmoid(x):
    return 1.0 / (1.0 + jnp.exp(-x))


def _rms_rows(x, g):
    return x * lax.rsqrt(jnp.mean(x * x, axis=-1, keepdims=True) + EPS) * g


def _dot(a, b):
    return jnp.dot(a, b, preferred_element_type=F32)


def _pack_pairs(x):
    half = x.shape[1] // 2
    bits = lambda a: lax.bitcast_convert_type(a.astype(BF16).astype(F32), U32)
    return (bits(x[:, half:]) & jnp.uint32(0xFFFF0000)) | (bits(x[:, :half]) >> 16)


def _unpack_pairs(w):
    lo = lax.bitcast_convert_type(w << 16, F32)
    hi = lax.bitcast_convert_type(w & jnp.uint32(0xFFFF0000), F32)
    return lo, hi


def _proj_kernel(x_ref, xprev_ref, xnext_ref,
                 g1_ref, w_ref, cw_ref, cng_ref, gmat_ref, nea_ref, dtb_ref,
                 ya_ref, q_ref, k_ref, v_ref, z_ref, gb_ref,
                 hs_ref, su_ref, p_ref, *, tile, seq_len):
    tok0 = pl.program_id(0) * tile
    not_start = (tok0 % seq_len != 0).astype(F32)
    not_end = ((tok0 + tile) % seq_len != 0).astype(F32)
    g1 = g1_ref[...]
    hs_ref[0:HALO, :] = _rms_rows(xprev_ref[...], g1)
    hs_ref[HALO:HALO + tile, :] = _rms_rows(x_ref[...], g1)
    hs_ref[HALO + tile:, :] = _rms_rows(xnext_ref[...], g1)

    p_ref[...] = _dot(hs_ref[...].astype(BF16), w_ref[...])
    for stage in _FINISH_STAGES:
        stage(p_ref, su_ref, cw_ref, cng_ref, gmat_ref, nea_ref, dtb_ref,
              ya_ref, q_ref, k_ref, v_ref, z_ref, gb_ref, tile, not_start, not_end)


def _conv_cols(p_ref, su_ref, cw_ref, lo, width, tile, not_start, not_end):
    cw = CONV_WIDTH
    cols = slice(lo, lo + width)
    if lo < cw:
        su_ref[:, cols] = p_ref[:, cw + lo:cw + lo + width] * p_ref[:, 2 * cw + lo:2 * cw + lo + width]
    else:
        su_ref[:, cols] = p_ref[:, 2 * cw + lo:2 * cw + lo + width]
    su_ref[0:HALO, cols] = su_ref[0:HALO, cols] * not_start
    su_ref[HALO + tile:, cols] = su_ref[HALO + tile:, cols] * not_end
    u = su_ref[:, cols]
    total = tile + 2 * HALO
    return (pltpu.roll(u, 1, axis=0)[HALO:HALO + tile] * cw_ref[0:1, cols]
            + u[HALO:HALO + tile] * cw_ref[1:2, cols]
            + pltpu.roll(u, total - 1, axis=0)[HALO:HALO + tile] * cw_ref[2:3, cols])


def _stage_mixer_a(half):
    width = CONV_WIDTH // 2
    lo = half * width

    def stage(p_ref, su_ref, cw_ref, cng_ref, gmat_ref, nea_ref, dtb_ref,
              ya_ref, q_ref, k_ref, v_ref, z_ref, gb_ref, tile, not_start, not_end):
        conv = _conv_cols(p_ref, su_ref, cw_ref, lo, width, tile, not_start, not_end)
        ya = p_ref[HALO:HALO + tile, lo:lo + width] * conv
        ms = _dot((ya * ya).astype(BF16), gmat_ref[lo:lo + width, lo:lo + width])
        ya_ref[:, lo:lo + width] = (ya * lax.rsqrt(ms + EPS) * cng_ref[:, lo:lo + width]).astype(ya_ref.dtype)
    return stage


def _stage_mixer_b(which, half):
    width = DN_V // 2
    lo_out = half * width
    lo = CONV_WIDTH + which * DN_V + lo_out

    def stage(p_ref, su_ref, cw_ref, cng_ref, gmat_ref, nea_ref, dtb_ref,
              ya_ref, q_ref, k_ref, v_ref, z_ref, gb_ref, tile, not_start, not_end):
        x = _conv_cols(p_ref, su_ref, cw_ref, lo, width, tile, not_start, not_end)
        x = x * _sigmoid(x)
        out_ref = (q_ref, k_ref, v_ref)[which]
        if which == 2:
            out_ref[:, lo_out:lo_out + width] = x.astype(out_ref.dtype)
            return
        scale = DN_DK ** -0.5 if which == 0 else 1.0
        for h in range(width // DN_DK):
            xh = x[:, h * DN_DK:(h + 1) * DN_DK]
            xh = xh * (lax.rsqrt(jnp.sum(xh * xh, axis=-1, keepdims=True) + EPS) * scale)
            out_ref[:, lo_out + h * DN_DK:lo_out + (h + 1) * DN_DK] = xh.astype(out_ref.dtype)
    return stage


def _stage_gates(inner):
    def stage(p_ref, su_ref, cw_ref, cng_ref, gmat_ref, nea_ref, dtb_ref,
              ya_ref, q_ref, k_ref, v_ref, z_ref, gb_ref, tile, not_start, not_end):
        inner(p_ref, su_ref, cw_ref, cng_ref, gmat_ref, nea_ref, dtb_ref,
              ya_ref, q_ref, k_ref, v_ref, z_ref, gb_ref, tile, not_start, not_end)
        cw = CONV_WIDTH
        z_ref[...] = p_ref[HALO:HALO + tile, 6 * cw:7 * cw].astype(z_ref.dtype)
        ab = p_ref[HALO:HALO + tile, 7 * cw:7 * cw + LANES]
        xs = ab + dtb_ref[...]
        softplus = jnp.maximum(xs, 0.0) + jnp.log(1.0 + jnp.exp(-jnp.abs(xs)))
        g = nea_ref[...] * softplus
        beta = _sigmoid(ab)
        r = lax.broadcasted_iota(jnp.int32, (tile, LANES), 0) % CHUNK
        pre = g
        suf = g
        s = 1
        while s < CHUNK:
            pre = pre + jnp.where(r >= s, pltpu.roll(pre, s, axis=0), 0.0)
            suf = suf + jnp.where(r < CHUNK - s, pltpu.roll(suf, tile - s, axis=0), 0.0)
            s *= 2
        lane = lax.broadcasted_iota(jnp.int32, (tile, LANES), 1) // DN_HEADS
        gb_ref[...] = jnp.where(lane == 0, pre,
                      jnp.where(lane == 1, suf,
                      jnp.where(lane < 4, beta,
                      jnp.where(lane == 4, suf - g,
                      jnp.where(lane == 5, pre - g, pre + suf - g)))))
    return stage


_FINISH_STAGES = (_stage_mixer_a(0), _stage_mixer_a(1), _stage_mixer_b(0, 0), _stage_mixer_b(0, 1),
                  _stage_mixer_b(1, 0), _stage_mixer_b(1, 1), _stage_mixer_b(2, 0),
                  _stage_gates(_stage_mixer_b(2, 1)))


def _proj_call(x, seq_len, wts):
    n, d = x.shape
    tile = TOKEN_TILE
    nt = n // tile
    tb = tile // HALO
    kern = functools.partial(_proj_kernel, tile=tile, seq_len=seq_len)
    const = lambda i: (0, 0)
    in_specs = [pl.BlockSpec((tile, d), lambda i: (i, 0)),
                pl.BlockSpec((HALO, d), lambda i: (jnp.maximum(i * tb - 1, 0), 0)),
                pl.BlockSpec((HALO, d), lambda i: (jnp.minimum((i + 1) * tb, n // HALO - 1), 0))]
    in_specs += [pl.BlockSpec(w.shape, const, pipeline_mode=pl.Buffered(1)) for w in wts]
    widths = (512, 512, 512, 512, 512, LANES)
    dtypes = (BF16, BF16, BF16, BF16, BF16, F32)
    rows = tile + 2 * HALO
    return pl.pallas_call(
        kern, grid=(nt,), in_specs=in_specs,
        out_specs=[pl.BlockSpec((tile, w), lambda i: (i, 0)) for w in widths],
        out_shape=[jax.ShapeDtypeStruct((n, w), dt) for w, dt in zip(widths, dtypes)],
        scratch_shapes=[pltpu.VMEM((rows, d), F32), pltpu.VMEM((rows, 4 * CONV_WIDTH), F32),
                        pltpu.VMEM((rows, wts[1].shape[1]), F32)],
        compiler_params=_cparams(("arbitrary",)), name="proj_conv")(x, x, x, *wts)


def _dn_kernel(*refs, block, sub, reverse, seq_len, nblk):
    if reverse:
        (q_ref, k_ref, v_ref, gb_ref, of_ref, z_ref, ng_ref, o_ref, s_ref) = refs
    else:
        (q_ref, k_ref, v_ref, gb_ref, o_ref, s_ref) = refs
    i = pl.program_id(0)
    b = nblk - 1 - i if reverse else i
    edge = (b + 1) * block if reverse else b * block

    @pl.when(edge % seq_len == 0)
    def _():
        s_ref[...] = jnp.zeros_like(s_ref)

    nchunk = sub // CHUNK
    c = CHUNK
    heads = range(DN_HEADS)
    ri = lax.broadcasted_iota(jnp.int32, (sub, sub), 0)
    ci = lax.broadcasted_iota(jnp.int32, (sub, sub), 1)
    same = (ri // c) == (ci // c)
    incl = same & ((ri <= ci) if reverse else (ri >= ci))
    strict = same & ((ri < ci) if reverse else (ri > ci))
    pr = lax.broadcasted_iota(jnp.int32, (c, sub), 0)
    pc = lax.broadcasted_iota(jnp.int32, (c, sub), 1)
    eye_p = (pr == pc % c).astype(F32)
    lane_chunk = pc // c

    def fold(bd):
        out = bd[0:c]
        for r in range(1, nchunk):
            out = out + bd[r * c:(r + 1) * c]
        return out

    def unfold(p):
        zero = jnp.zeros_like(p)
        return jnp.concatenate([jnp.where(lane_chunk == r, p, zero) for r in range(nchunk)], axis=0)

    grp = lambda g, h: slice(g * DN_HEADS + h, g * DN_HEADS + h + 1)
    g_cum, g_beta, g_rest, g_tot = (1, 3, 5, 7) if reverse else (0, 2, 4, 6)
    sls = [slice(h * DN_DK, (h + 1) * DN_DK) for h in heads]

    def prepare(r0, out):
        rows = slice(r0, r0 + sub)
        gbv = gb_ref[rows, :]
        gbt = gbv.T
        kb = [k_ref[rows, sl] for sl in sls]
        qb = [q_ref[rows, sl] for sl in sls]
        gcol = [gbv[:, grp(g_cum, h)] for h in heads]
        beta = [gbv[:, grp(g_beta, h)] for h in heads]
        kq = [lax.dot_general(jnp.concatenate([kb[h], qb[h]], axis=0), kb[h],
                              (((1,), (1,)), ((), ())), preferred_element_type=F32) for h in heads]
        yield
        decay = [jnp.exp(jnp.minimum(gcol[h] - gbt[grp(g_cum, h), :], 0.0)) for h in heads]
        bdl = [jnp.where(strict, kq[h][:sub] * decay[h], 0.0) * beta[h] for h in heads]
        bda = [jnp.where(incl, kq[h][sub:] * decay[h], 0.0).astype(BF16) for h in heads]
        yield
        lp = [fold(bdl[h]) for h in heads]
        xp = [eye_p - lp[h] for h in heads]
        bd = [bdl[h].astype(BF16) for h in heads]
        p = 1
        while p < c:
            if p == 1:
                lp = [_dot(lp[h].astype(BF16), bd[h]) for h in heads]
            elif 2 * p < c:
                res = [_dot(jnp.concatenate([lp[h], xp[h]], axis=0).astype(BF16), bd[h]) for h in heads]
                lp = [r[:c] for r in res]
                xp = [xp[h] + res[h][c:] for h in heads]
            else:
                xp = [xp[h] + _dot(xp[h].astype(BF16), bd[h]) for h in heads]
            p *= 2
            if p < c:
                bd = [unfold(lp[h].astype(BF16)) for h in heads]
            yield
        bdt = [unfold(xp[h].astype(BF16)) for h in heads]
        egc = [jnp.exp(gcol[h]) for h in heads]
        kf = [kb[h].astype(F32) for h in heads]
        rhs = [jnp.concatenate([kf[h] * (beta[h] * egc[h]), v_ref[rows, sls[h]].astype(F32) * beta[h]],
                               axis=1).astype(BF16) for h in heads]
        wu = [_dot(bdt[h], rhs[h]) for h in heads]
        yield
        au = [_dot(bda[h], wu[h].astype(BF16)) for h in heads]
        qt = [qb[h].astype(F32) * egc[h] - au[h][:, :DN_DK] for h in heads]
        out["kd"] = [(kf[h] * jnp.exp(gbv[:, grp(g_rest, h)])).astype(BF16) for h in heads]
        out["wqt"] = [jnp.concatenate([wu[h][:, :DN_DK], qt[h]], axis=1).astype(BF16) for h in heads]
        out["u"] = [wu[h][:, DN_DK:] for h in heads]
        out["o0"] = [au[h][:, DN_DK:] for h in heads]
        out["gbt"] = gbt
        yield

    def scan(r0, pre, state):
        order = range(nchunk - 1, -1, -1) if reverse else range(nchunk)
        for cc in order:
            rows = slice(cc * c, (cc + 1) * c)
            out_rows = slice(r0 + cc * c, r0 + (cc + 1) * c)
            for h in heads:
                wqt = pre["wqt"][h]
                lhs = jnp.concatenate([wqt[rows, :DN_DK], wqt[rows, DN_DK:]], axis=0)
                wq = _dot(lhs, state[h].astype(BF16))
                vn = pre["u"][h][rows] - wq[:c]
                o = wq[c:] + pre["o0"][h][rows]
                tot = pre["gbt"][grp(g_tot, h), cc * c:cc * c + 1]
                state[h] = state[h] * jnp.exp(tot) + lax.dot_general(
                    pre["kd"][h][rows], vn.astype(BF16), (((0,), (0,)), ((), ())),
                    preferred_element_type=F32)
                if reverse:
                    o = o + of_ref[out_rows, sls[h]]
                    o = o * lax.rsqrt(jnp.mean(o * o, axis=-1, keepdims=True) + EPS) * ng_ref[...]
                    zc = z_ref[out_rows, sls[h]].astype(F32)
                    o = o * (zc * _sigmoid(zc))
                o_ref[out_rows, sls[h]] = o.astype(o_ref.dtype)
            yield

    nsub = block // sub
    starts = [r * sub for r in (range(nsub - 1, -1, -1) if reverse else range(nsub))]
    state = [s_ref[h] for h in heads]
    pre_prev = None
    for idx in range(nsub + 1):
        pre = {}
        prep_gen = prepare(starts[idx], pre) if idx < nsub else iter(())
        scan_gen = scan(starts[idx - 1], pre_prev, state) if idx > 0 else iter(())
        prep_live = scan_live = True
        while prep_live or scan_live:
            if prep_live:
                prep_live = next(prep_gen, _DONE) is not _DONE
                if prep_live:
                    prep_live = next(prep_gen, _DONE) is not _DONE
            if scan_live:
                scan_live = next(scan_gen, _DONE) is not _DONE
        pre_prev = pre
    for h in heads:
        s_ref[h] = state[h]


_DONE = object()


def _dn_call(q, k, v, gb, extra, reverse, seq_len):
    n = q.shape[0]
    block = DN_BLOCK
    nblk = n // block
    imap = (lambda i: (nblk - 1 - i, 0)) if reverse else (lambda i: (i, 0))
    tok = lambda w: pl.BlockSpec((block, w), imap)
    in_specs = [tok(512), tok(512), tok(512), tok(LANES)]
    args = [q, k, v, gb]
    if reverse:
        o_f, z, ng = extra
        in_specs += [tok(512), tok(512), pl.BlockSpec(ng.shape, lambda i: (0, 0))]
        args += [o_f, z, ng]
    kern = functools.partial(_dn_kernel, block=block, sub=DN_SUB, reverse=reverse,
                             seq_len=seq_len, nblk=nblk)
    return pl.pallas_call(
        kern, grid=(nblk,), in_specs=in_specs, out_specs=tok(512),
        out_shape=jax.ShapeDtypeStruct((n, 512), BF16 if reverse else F32),
        scratch_shapes=[pltpu.VMEM((DN_HEADS, DN_DK, DN_DK), F32)],
        compiler_params=_cparams(("arbitrary",)),
        name="deltanet_bwd" if reverse else "deltanet_fwd")(*args)


def _route_kernel(x_ref, ya_ref, yb_ref, wo_ref, g2_ref, rw_ref, rb_ref, tri_ref,
                  x2_ref, xa_ref, xb_ref, info_ref, cnt_ref, run_ref, *, tile):
    i = pl.program_id(0)

    @pl.when(i == 0)
    def _():
        run_ref[...] = jnp.zeros_like(run_ref)

    sub = tri_ref.shape[0]
    parts = [slice(r, r + sub) for r in range(0, tile, sub)]
    each = lambda f, *lists: [f(*a) for a in zip(*lists)]
    half = wo_ref.shape[0] // 2
    x2 = [x_ref[rows, :] + _dot(ya_ref[rows, :], wo_ref[0:half, :])
          + _dot(yb_ref[rows, :], wo_ref[half:, :]) for rows in parts]
    for rows, v in zip(parts, x2):
        x2_ref[rows, :] = v
    xn = [_rms_rows(v, g2_ref[...]) for v in x2]
    for rows, v in zip(parts, xn):
        words = _pack_pairs(v)
        xa_ref[rows, :] = words[:, :ROW_WORDS]
        xb_ref[rows, :] = words[:, ROW_WORDS:]
    logits = [_dot(v.astype(BF16), rw_ref[...]) + rb_ref[...] for v in xn]
    lane = lax.broadcasted_iota(jnp.int32, (sub, LANES), 1).astype(F32)
    neg = jnp.float32(-jnp.inf)
    row_max = lambda v: jnp.max(v, axis=-1, keepdims=True)
    first_at = lambda v, m: jnp.min(jnp.where(v == m, lane, float(LANES)), axis=-1, keepdims=True)

    gl = [jnp.where(lane < N_GROUPS, v, neg) for v in logits]
    gmax = each(row_max, gl)
    gidx = each(first_at, gl, gmax)
    g_w = [1.0 / jnp.sum(jnp.exp(v - m), axis=-1, keepdims=True) for v, m in zip(gl, gmax)]
    lo = [N_GROUPS + g * EXPERTS_PER_GROUP for g in gidx]
    el = [jnp.where((lane >= a) & (lane < a + EXPERTS_PER_GROUP), v, neg) for v, a in zip(logits, lo)]
    m1 = each(row_max, el)
    i1 = each(first_at, el, m1)
    el2 = [jnp.where(lane == a, neg, v) for v, a in zip(el, i1)]
    m2 = each(row_max, el2)
    i2 = each(first_at, el2, m2)
    hit1 = [lane == a - N_GROUPS for a in i1]
    hit2 = [lane == a - N_GROUPS for a in i2]
    onehot = [(a | b).astype(BF16) for a, b in zip(hit1, hit2)]
    inside = [_dot(tri_ref[...], v) for v in onehot]
    run = run_ref[...]
    for k, rows in enumerate(parts):
        before = inside[k] + run
        pos1 = jnp.sum(jnp.where(hit1[k], before, 0.0), axis=-1, keepdims=True)
        pos2 = jnp.sum(jnp.where(hit2[k], before, 0.0), axis=-1, keepdims=True)
        run = run + jnp.sum(onehot[k].astype(F32), axis=0, keepdims=True)
        r = jnp.exp(m2[k] - m1[k])
        gate1 = g_w[k] / (1.0 + r)
        info_ref[rows, :] = jnp.where(lane == 0, i1[k] - N_GROUPS,
                            jnp.where(lane == 1, i2[k] - N_GROUPS,
                            jnp.where(lane == 2, gate1,
                            jnp.where(lane == 3, gate1 * r,
                            jnp.where(lane == 4, pos1,
                            jnp.where(lane == 5, pos2, 0.0))))))
    run_ref[...] = run
    cnt_ref[...] = run


def _route_call(x, ya, yb, wts):
    n, d = x.shape
    tile = TOKEN_TILE
    const = lambda i: (0, 0)
    tok = lambda w: pl.BlockSpec((tile, w), lambda i: (i, 0))
    in_specs = [tok(d), tok(512), tok(512)] + [pl.BlockSpec(w.shape, const) for w in wts]
    out_shape = [jax.ShapeDtypeStruct((n, d), F32), jax.ShapeDtypeStruct((n, ROW_WORDS), U32),
                 jax.ShapeDtypeStruct((n, ROW_WORDS), U32),
                 jax.ShapeDtypeStruct((n, LANES), F32), jax.ShapeDtypeStruct((1, LANES), F32)]
    out_specs = [tok(d), tok(ROW_WORDS), tok(ROW_WORDS), tok(LANES), pl.BlockSpec((1, LANES), const)]
    kern = functools.partial(_route_kernel, tile=tile)
    return pl.pallas_call(
        kern, grid=(n // tile,), in_specs=in_specs, out_specs=out_specs, out_shape=out_shape,
        scratch_shapes=[pltpu.VMEM((1, LANES), F32)],
        compiler_params=_cparams(("arbitrary",)), name="oproj_router")(x, ya, yb, *wts)


def _sc_mesh():
    return plsc.VectorSubcoreMesh(core_axis_name="core", subcore_axis_name="subcore")


def _sc_scatter_rows(x, idx0, idx1, rows):
    n, w = x.shape

    @pl.kernel(out_type=jax.ShapeDtypeStruct((rows, w), x.dtype), mesh=_sc_mesh(), scratch_types=[])
    def scatter(x_hbm, i0_hbm, i1_hbm, o_hbm):
        def body(x_vmem, i0_vmem, i1_vmem):
            pltpu.sync_copy(x_vmem, o_hbm.at[i0_vmem.at[0]])
            pltpu.sync_copy(x_vmem, o_hbm.at[i1_vmem.at[0]])

        pltpu.emit_pipeline(
            body, grid=(n // SC_WINDOW,),
            in_specs=[pl.BlockSpec((SC_WINDOW, w), lambda i: (i, 0)),
                      pl.BlockSpec((1, SC_WINDOW), lambda i: (0, i)),
                      pl.BlockSpec((1, SC_WINDOW), lambda i: (0, i))],
            out_specs=[], core_axis_name=("core", "subcore"),
            dimension_semantics=(pltpu.PARALLEL,))(x_hbm, i0_hbm, i1_hbm)

    return scatter(x, idx0, idx1)


def _sc_gather_rows(y, idx):
    m = idx.shape[1]
    w = y.shape[1]

    @pl.kernel(out_type=jax.ShapeDtypeStruct((m, w), y.dtype), mesh=_sc_mesh(), scratch_types=[])
    def gather(y_hbm, i_hbm, o_hbm):
        def body(i_vmem, o_vmem):
            pltpu.sync_copy(y_hbm.at[i_vmem.at[0]], o_vmem)

        pltpu.emit_pipeline(
            body, grid=(m // SC_WINDOW,),
            in_specs=[pl.BlockSpec((1, SC_WINDOW), lambda i: (0, i))],
            out_specs=[pl.BlockSpec((SC_WINDOW, w), lambda i: (i, 0))],
            core_axis_name=("core", "subcore"),
            dimension_semantics=(pltpu.PARALLEL,))(i_hbm, o_hbm)

    return gather(y, idx)


def _join_rows(wa, wb):
    lo_a, hi_a = _unpack_pairs(wa)
    lo_b, hi_b = _unpack_pairs(wb)
    return jnp.concatenate([lo_a, lo_b, hi_a, hi_b], axis=1)


def _expert_kernel(te_ref, nv_ref, nu_ref, xa_ref, xb_ref, w1_ref, w3_ref, w2_ref, ya_ref, yb_ref):
    del te_ref
    j = pl.program_id(0)

    @pl.when(j < nu_ref[0])
    def _():
        live = lax.broadcasted_iota(jnp.int32, xa_ref.shape, 0) < nv_ref[j]
        zero = jnp.zeros(xa_ref.shape, U32)
        x = _join_rows(jnp.where(live, xa_ref[...], zero),
                       jnp.where(live, xb_ref[...], zero)).astype(BF16)
        h1 = _dot(x, w1_ref[0].astype(BF16))
        h3 = _dot(x, w3_ref[0].astype(BF16))
        hdn = (h1 * _sigmoid(h1) * h3).astype(BF16)
        words = _pack_pairs(_dot(hdn, w2_ref[0].astype(BF16)))
        ya_ref[...] = words[:, :ROW_WORDS]
        yb_ref[...] = words[:, ROW_WORDS:]


def _expert_call(tile_expert, tile_valid, n_used, xa, xb, w1, w3, w2):
    rows, rw = xa.shape
    tm = EXPERT_TILE
    d, ff = w1.shape[1], w1.shape[2]
    row_blk = lambda j, te, nv, nu: (jnp.minimum(j, nu[0] - 1), 0)
    wsel = lambda j, te, nv, nu: (te[j], 0, 0)
    grid_spec = pltpu.PrefetchScalarGridSpec(
        num_scalar_prefetch=3, grid=(rows // tm,),
        in_specs=[pl.BlockSpec((tm, rw), row_blk), pl.BlockSpec((tm, rw), row_blk),
                  pl.BlockSpec((1, d, ff), wsel), pl.BlockSpec((1, d, ff), wsel),
                  pl.BlockSpec((1, ff, d), wsel)],
        out_specs=[pl.BlockSpec((tm, rw), row_blk), pl.BlockSpec((tm, rw), row_blk)])
    return pl.pallas_call(
        _expert_kernel, grid_spec=grid_spec,
        out_shape=[jax.ShapeDtypeStruct((rows, rw), U32), jax.ShapeDtypeStruct((rows, rw), U32)],
        compiler_params=_cparams(("arbitrary",)), name="moe_experts")(
            tile_expert, tile_valid, n_used, xa, xb, w1, w3, w2)


def _combine_kernel(x2_ref, info_ref, gf_ref, a0_ref, b0_ref, a1_ref, b1_ref, y_ref):
    info = info_ref[...]
    moe = (info[:, 2:3] * _join_rows(a0_ref[...], b0_ref[...])
           + info[:, 3:4] * _join_rows(a1_ref[...], b1_ref[...]))
    y_ref[...] = _rms_rows(x2_ref[...] + moe, gf_ref[...])


def _combine_call(x2, info, gf, ga, gb):
    n, d = x2.shape
    tile = TOKEN_TILE
    nt = n // tile
    first = pl.BlockSpec((tile, ROW_WORDS), lambda i: (i, 0))
    second = pl.BlockSpec((tile, ROW_WORDS), lambda i: (i + nt, 0))
    return pl.pallas_call(
        _combine_kernel, grid=(nt,),
        in_specs=[pl.BlockSpec((tile, d), lambda i: (i, 0)),
                  pl.BlockSpec((tile, LANES), lambda i: (i, 0)),
                  pl.BlockSpec(gf.shape, lambda i: (0, 0)),
                  first, first, second, second],
        out_specs=pl.BlockSpec((tile, d), lambda i: (i, 0)),
        out_shape=jax.ShapeDtypeStruct((n, d), F32),
        compiler_params=_cparams(("arbitrary",)), name="moe_combine")(
            x2, info, gf, ga, gb, ga, gb)


def _row(a):
    return a.reshape(1, -1).astype(F32)


def _prep_weights(norm1_g, w_in, conv_a_w, conv_a_norm_g, dn_conv_w, dn_a_log, dn_dt_bias, w_o,
                  norm2_g, rg_w, rg_b, re_w, re_b):
    row = _row
    n_ab = len(_GB_COLS)
    w_main = jnp.concatenate([w_in[:, :7 * CONV_WIDTH], w_in[:, 7 * CONV_WIDTH + _GB_COLS]], axis=1)
    w_main = jnp.pad(w_main, ((0, 0), (0, LANES - n_ab))).astype(BF16)
    cw = jnp.concatenate([conv_a_w, dn_conv_w], axis=1).astype(F32)
    grp = jnp.arange(CONV_WIDTH) // (CONV_WIDTH // CONV_GROUPS)
    gmat = ((grp[:, None] == grp[None, :]).astype(F32) / (CONV_WIDTH // CONV_GROUPS)).astype(BF16)
    is_a = (_GB_COLS < 2 * DN_HEADS)
    a_idx = jnp.where(is_a, _GB_COLS, 0)
    padl = lambda a: jnp.pad(a.reshape(1, -1).astype(F32), ((0, 0), (0, LANES - n_ab)))
    nea = padl(jnp.where(is_a, -jnp.exp(dn_a_log.astype(F32)).reshape(-1)[a_idx], 0.0))
    dtb = padl(jnp.where(is_a, dn_dt_bias.astype(F32).reshape(-1)[a_idx], 0.0))
    proj_w = [row(norm1_g), w_main, cw, row(conv_a_norm_g), gmat, nea, dtb]
    rw = jnp.pad(jnp.concatenate([rg_w, re_w], axis=1).astype(F32),
                 ((0, 0), (0, LANES - N_GROUPS - N_EXPERTS)))
    rb = jnp.pad(jnp.concatenate([rg_b, re_b]).reshape(1, -1).astype(F32),
                 ((0, 0), (0, LANES - N_GROUPS - N_EXPERTS)))
    tri = (jnp.arange(ROUTE_SUB)[:, None] > jnp.arange(ROUTE_SUB)[None, :]).astype(BF16)
    route_w = [w_o.astype(BF16), row(norm2_g), rw.astype(BF16), rb, tri]
    return proj_w, route_w


def _mix_and_route(x, seq_len, proj_w, dn_norm_g, route_w):
    ya, q, k, v, z, gb = _proj_call(x, seq_len, proj_w)
    o_f = _dn_call(q, k, v, gb, None, False, seq_len)
    yb = _dn_call(q, k, v, gb, (o_f, z, _row(dn_norm_g)), True, seq_len)
    return _route_call(x, ya, yb, route_w)


def _moe(routed, w1, w3, w2, final_g):
    x2, xna, xnb, info, cnt = routed
    n = x2.shape[0]
    tm = EXPERT_TILE
    counts = cnt[0, :N_EXPERTS].astype(jnp.int32)
    pcounts = (counts + tm - 1) // tm * tm
    pends = jnp.cumsum(pcounts)
    pstarts = pends - pcounts
    eid = jnp.arange(N_EXPERTS, dtype=jnp.int32)
    e12 = info[:, 0:2].astype(jnp.int32)
    start12 = jnp.sum(jnp.where(e12[:, :, None] == eid, pstarts, 0), axis=-1)
    dest = start12 + info[:, 4:6].astype(jnp.int32)
    dest0 = dest[:, 0].reshape(1, n)
    dest1 = dest[:, 1].reshape(1, n)
    n_tiles = 2 * n // tm + N_EXPERTS
    tile_start = jnp.arange(n_tiles, dtype=jnp.int32) * tm
    tile_expert = jnp.minimum(jnp.sum(pends[None, :] <= tile_start[:, None], axis=1),
                              N_EXPERTS - 1).astype(jnp.int32)
    live_end = jnp.sum(jnp.where(tile_expert[:, None] == eid, pstarts + counts, 0), axis=-1)
    tile_valid = jnp.clip(live_end - tile_start, 0, tm).astype(jnp.int32)
    n_used = (pends[-1:] // tm).astype(jnp.int32)

    rows = n_tiles * tm
    xa = _sc_scatter_rows(xna, dest0, dest1, rows)
    xb = _sc_scatter_rows(xnb, dest0, dest1, rows)
    ya_e, yb_e = _expert_call(tile_expert, tile_valid, n_used, xa, xb, w1, w3, w2)
    both = jnp.concatenate([dest0, dest1], axis=1)
    return _combine_call(x2, info, _row(final_g), _sc_gather_rows(ya_e, both),
                         _sc_gather_rows(yb_e, both))


def kernel(x_prompt, x_sample, norm1_g, w_in, conv_a_w, conv_a_norm_g, dn_conv_w, dn_a_log,
           dn_dt_bias, dn_norm_g, w_o, norm2_g, router_group_w, router_group_b, router_expert_w,
           router_expert_b, w1, w3, w2, final_norm_g):
    assert norm1_g.shape[0] == 1, "single-layer trunk"
    bp, sp, d = x_prompt.shape
    bs, ss, _ = x_sample.shape
    assert bp == 1 and sp % max(TOKEN_TILE, DN_BLOCK) == 0 and ss % max(TOKEN_TILE, DN_BLOCK) == 0
    proj_w, route_w = _prep_weights(
        norm1_g[0], w_in[0], conv_a_w[0], conv_a_norm_g[0], dn_conv_w[0], dn_a_log[0],
        dn_dt_bias[0], w_o[0], norm2_g[0], router_group_w[0], router_group_b[0],
        router_expert_w[0], router_expert_b[0])
    routed = [_mix_and_route(x.reshape(-1, d), seq_len, proj_w, dn_norm_g[0], route_w)
              for x, seq_len in ((x_prompt, sp), (x_sample, ss))]
    y_p, y_s = [_moe(r, w1[0], w3[0], w2[0], final_norm_g) for r in routed]
    return y_p.reshape(bp, sp, d), y_s.reshape(bs, ss, d)
```

```python
import functools

import jax
import jax.numpy as jnp
import numpy as np
from jax import lax
from jax.experimental import pallas as pl
from jax.experimental.pallas import tpu as pltpu
from jax.experimental.pallas import tpu_sc as plsc

F32 = jnp.float32
BF16 = jnp.bfloat16
U32 = jnp.uint32

EPS = 1e-6
CONV_WIDTH = 512
CONV_GROUPS = 8
DN_HEADS = 4
DN_DK = 128
DN_V = 512
CHUNK = 64
N_GROUPS = 4
EXPERTS_PER_GROUP = 8
N_EXPERTS = N_GROUPS * EXPERTS_PER_GROUP
EXPERT_FF = 512
LANES = 128
HALO = 8

TOKEN_TILE = 512
ROUTE_SUB = 256
DN_BLOCK = 1024
DN_SUB = 256
EXPERT_TILE = 512
ROW_WORDS = 256
SC_WINDOW = 128
VMEM_LIMIT = 56 * 1024 * 1024

_GB_COLS = np.concatenate([np.arange(4 * DN_HEADS), np.arange(2 * DN_HEADS), np.arange(2 * DN_HEADS)])


def _cparams(sem):
    return pltpu.CompilerParams(dimension_semantics=sem, vmem_limit_bytes=VMEM_LIMIT)


def _sigmoid(x):
    return 1.0 / (1.0 + jnp.exp(-x))


def _rms_rows(x, g):
    return x * lax.rsqrt(jnp.mean(x * x, axis=-1, keepdims=True) + EPS) * g


def _dot(a, b):
    return jnp.dot(a, b, preferred_element_type=F32)


def _pack_pairs(x):
    half = x.shape[1] // 2
    bits = lambda a: lax.bitcast_convert_type(a.astype(BF16).astype(F32), U32)
    return (bits(x[:, half:]) & jnp.uint32(0xFFFF0000)) | (bits(x[:, :half]) >> 16)


def _unpack_pairs(w):
    lo = lax.bitcast_convert_type(w << 16, F32)
    hi = lax.bitcast_convert_type(w & jnp.uint32(0xFFFF0000), F32)
    return lo, hi


def _proj_kernel(x_ref, xprev_ref, xnext_ref,
                 g1_ref, w_ref, cw_ref, cng_ref, gmat_ref, nea_ref, dtb_ref,
                 ya_ref, q_ref, k_ref, v_ref, z_ref, gb_ref,
                 hs_ref, su_ref, p_ref, *, tile, seq_len):
    tok0 = pl.program_id(0) * tile
    not_start = (tok0 % seq_len != 0).astype(F32)
    not_end = ((tok0 + tile) % seq_len != 0).astype(F32)
    g1 = g1_ref[...]
    hs_ref[0:HALO, :] = _rms_rows(xprev_ref[...], g1)
    hs_ref[HALO:HALO + tile, :] = _rms_rows(x_ref[...], g1)
    hs_ref[HALO + tile:, :] = _rms_rows(xnext_ref[...], g1)

    p_ref[...] = _dot(hs_ref[...].astype(BF16), w_ref[...])
    for stage in _FINISH_STAGES:
        stage(p_ref, su_ref, cw_ref, cng_ref, gmat_ref, nea_ref, dtb_ref,
              ya_ref, q_ref, k_ref, v_ref, z_ref, gb_ref, tile, not_start, not_end)


def _conv_cols(p_ref, su_ref, cw_ref, lo, width, tile, not_start, not_end):
    cw = CONV_WIDTH
    cols = slice(lo, lo + width)
    if lo < cw:
        su_ref[:, cols] = p_ref[:, cw + lo:cw + lo + width] * p_ref[:, 2 * cw + lo:2 * cw + lo + width]
    else:
        su_ref[:, cols] = p_ref[:, 2 * cw + lo:2 * cw + lo + width]
    su_ref[0:HALO, cols] = su_ref[0:HALO, cols] * not_start
    su_ref[HALO + tile:, cols] = su_ref[HALO + tile:, cols] * not_end
    u = su_ref[:, cols]
    total = tile + 2 * HALO
    return (pltpu.roll(u, 1, axis=0)[HALO:HALO + tile] * cw_ref[0:1, cols]
            + u[HALO:HALO + tile] * cw_ref[1:2, cols]
            + pltpu.roll(u, total - 1, axis=0)[HALO:HALO + tile] * cw_ref[2:3, cols])


def _stage_mixer_a(half):
    width = CONV_WIDTH // 2
    lo = half * width

    def stage(p_ref, su_ref, cw_ref, cng_ref, gmat_ref, nea_ref, dtb_ref,
              ya_ref, q_ref, k_ref, v_ref, z_ref, gb_ref, tile, not_start, not_end):
        conv = _conv_cols(p_ref, su_ref, cw_ref, lo, width, tile, not_start, not_end)
        ya = p_ref[HALO:HALO + tile, lo:lo + width] * conv
        ms = _dot((ya * ya).astype(BF16), gmat_ref[lo:lo + width, lo:lo + width])
        ya_ref[:, lo:lo + width] = (ya * lax.rsqrt(ms + EPS) * cng_ref[:, lo:lo + width]).astype(ya_ref.dtype)
    return stage


def _stage_mixer_b(which, half):
    width = DN_V // 2
    lo_out = half * width
    lo = CONV_WIDTH + which * DN_V + lo_out

    def stage(p_ref, su_ref, cw_ref, cng_ref, gmat_ref, nea_ref, dtb_ref,
              ya_ref, q_ref, k_ref, v_ref, z_ref, gb_ref, tile, not_start, not_end):
        x = _conv_cols(p_ref, su_ref, cw_ref, lo, width, tile, not_start, not_end)
        x = x * _sigmoid(x)
        out_ref = (q_ref, k_ref, v_ref)[which]
        if which == 2:
            out_ref[:, lo_out:lo_out + width] = x.astype(out_ref.dtype)
            return
        scale = DN_DK ** -0.5 if which == 0 else 1.0
        for h in range(width // DN_DK):
            xh = x[:, h * DN_DK:(h + 1) * DN_DK]
            xh = xh * (lax.rsqrt(jnp.sum(xh * xh, axis=-1, keepdims=True) + EPS) * scale)
            out_ref[:, lo_out + h * DN_DK:lo_out + (h + 1) * DN_DK] = xh.astype(out_ref.dtype)
    return stage


def _stage_gates(inner):
    def stage(p_ref, su_ref, cw_ref, cng_ref, gmat_ref, nea_ref, dtb_ref,
              ya_ref, q_ref, k_ref, v_ref, z_ref, gb_ref, tile, not_start, not_end):
        inner(p_ref, su_ref, cw_ref, cng_ref, gmat_ref, nea_ref, dtb_ref,
              ya_ref, q_ref, k_ref, v_ref, z_ref, gb_ref, tile, not_start, not_end)
        cw = CONV_WIDTH
        z_ref[...] = p_ref[HALO:HALO + tile, 6 * cw:7 * cw].astype(z_ref.dtype)
        ab = p_ref[HALO:HALO + tile, 7 * cw:7 * cw + LANES]
        xs = ab + dtb_ref[...]
        softplus = jnp.maximum(xs, 0.0) + jnp.log(1.0 + jnp.exp(-jnp.abs(xs)))
        g = nea_ref[...] * softplus
        beta = _sigmoid(ab)
        r = lax.broadcasted_iota(jnp.int32, (tile, LANES), 0) % CHUNK
        pre = g
        suf = g
        s = 1
        while s < CHUNK:
            pre = pre + jnp.where(r >= s, pltpu.roll(pre, s, axis=0), 0.0)
            suf = suf + jnp.where(r < CHUNK - s, pltpu.roll(suf, tile - s, axis=0), 0.0)
            s *= 2
        lane = lax.broadcasted_iota(jnp.int32, (tile, LANES), 1) // DN_HEADS
        gb_ref[...] = jnp.where(lane == 0, pre,
                      jnp.where(lane == 1, suf,
                      jnp.where(lane < 4, beta,
                      jnp.where(lane == 4, suf - g,
                      jnp.where(lane == 5, pre - g, pre + suf - g)))))
    return stage


_FINISH_STAGES = (_stage_mixer_a(0), _stage_mixer_a(1), _stage_mixer_b(0, 0), _stage_mixer_b(0, 1),
                  _stage_mixer_b(1, 0), _stage_mixer_b(1, 1), _stage_mixer_b(2, 0),
                  _stage_gates(_stage_mixer_b(2, 1)))


def _proj_call(x, seq_len, wts):
    n, d = x.shape
    tile = TOKEN_TILE
    nt = n // tile
    tb = tile // HALO
    kern = functools.partial(_proj_kernel, tile=tile, seq_len=seq_len)
    const = lambda i: (0, 0)
    in_specs = [pl.BlockSpec((tile, d), lambda i: (i, 0)),
                pl.BlockSpec((HALO, d), lambda i: (jnp.maximum(i * tb - 1, 0), 0)),
                pl.BlockSpec((HALO, d), lambda i: (jnp.minimum((i + 1) * tb, n // HALO - 1), 0))]
    in_specs += [pl.BlockSpec(w.shape, const, pipeline_mode=pl.Buffered(1)) for w in wts]
    widths = (512, 512, 512, 512, 512, LANES)
    dtypes = (BF16, BF16, BF16, BF16, BF16, F32)
    rows = tile + 2 * HALO
    return pl.pallas_call(
        kern, grid=(nt,), in_specs=in_specs,
        out_specs=[pl.BlockSpec((tile, w), lambda i: (i, 0)) for w in widths],
        out_shape=[jax.ShapeDtypeStruct((n, w), dt) for w, dt in zip(widths, dtypes)],
        scratch_shapes=[pltpu.VMEM((rows, d), F32), pltpu.VMEM((rows, 4 * CONV_WIDTH), F32),
                        pltpu.VMEM((rows, wts[1].shape[1]), F32)],
        compiler_params=_cparams(("arbitrary",)), name="proj_conv")(x, x, x, *wts)


def _dn_kernel(*refs, block, sub, reverse, seq_len, nblk):
    if reverse:
        (q_ref, k_ref, v_ref, gb_ref, of_ref, z_ref, ng_ref, o_ref, s_ref) = refs
    else:
        (q_ref, k_ref, v_ref, gb_ref, o_ref, s_ref) = refs
    i = pl.program_id(0)
    b = nblk - 1 - i if reverse else i
    edge = (b + 1) * block if reverse else b * block

    @pl.when(edge % seq_len == 0)
    def _():
        s_ref[...] = jnp.zeros_like(s_ref)

    nchunk = sub // CHUNK
    c = CHUNK
    heads = range(DN_HEADS)
    ri = lax.broadcasted_iota(jnp.int32, (sub, sub), 0)
    ci = lax.broadcasted_iota(jnp.int32, (sub, sub), 1)
    same = (ri // c) == (ci // c)
    incl = same & ((ri <= ci) if reverse else (ri >= ci))
    strict = same & ((ri < ci) if reverse else (ri > ci))
    pr = lax.broadcasted_iota(jnp.int32, (c, sub), 0)
    pc = lax.broadcasted_iota(jnp.int32, (c, sub), 1)
    eye_p = (pr == pc % c).astype(F32)
    lane_chunk = pc // c

    def fold(bd):
        out = bd[0:c]
        for r in range(1, nchunk):
            out = out + bd[r * c:(r + 1) * c]
        return out

    def unfold(p):
        zero = jnp.zeros_like(p)
        return jnp.concatenate([jnp.where(lane_chunk == r, p, zero) for r in range(nchunk)], axis=0)

    grp = lambda g, h: slice(g * DN_HEADS + h, g * DN_HEADS + h + 1)
    g_cum, g_beta, g_rest, g_tot = (1, 3, 5, 7) if reverse else (0, 2, 4, 6)
    sls = [slice(h * DN_DK, (h + 1) * DN_DK) for h in heads]

    def prepare(r0, out):
        rows = slice(r0, r0 + sub)
        gbv = gb_ref[rows, :]
        gbt = gbv.T
        kb = [k_ref[rows, sl] for sl in sls]
        qb = [q_ref[rows, sl] for sl in sls]
        gcol = [gbv[:, grp(g_cum, h)] for h in heads]
        beta = [gbv[:, grp(g_beta, h)] for h in heads]
        kq = [lax.dot_general(jnp.concatenate([kb[h], qb[h]], axis=0), kb[h],
                              (((1,), (1,)), ((), ())), preferred_element_type=F32) for h in heads]
        yield
        decay = [jnp.exp(jnp.minimum(gcol[h] - gbt[grp(g_cum, h), :], 0.0)) for h in heads]
        bdl = [jnp.where(strict, kq[h][:sub] * decay[h], 0.0) * beta[h] for h in heads]
        bda = [jnp.where(incl, kq[h][sub:] * decay[h], 0.0).astype(BF16) for h in heads]
        yield
        lp = [fold(bdl[h]) for h in heads]
        xp = [eye_p - lp[h] for h in heads]
        bd = [bdl[h].astype(BF16) for h in heads]
        p = 1
        while p < c:
            if p == 1:
                lp = [_dot(lp[h].astype(BF16), bd[h]) for h in heads]
            elif 2 * p < c:
                res = [_dot(jnp.concatenate([lp[h], xp[h]], axis=0).astype(BF16), bd[h]) for h in heads]
                lp = [r[:c] for r in res]
                xp = [xp[h] + res[h][c:] for h in heads]
            else:
                xp = [xp[h] + _dot(xp[h].astype(BF16), bd[h]) for h in heads]
            p *= 2
            if p < c:
                bd = [unfold(lp[h].astype(BF16)) for h in heads]
            yield
        bdt = [unfold(xp[h].astype(BF16)) for h in heads]
        egc = [jnp.exp(gcol[h]) for h in heads]
        kf = [kb[h].astype(F32) for h in heads]
        rhs = [jnp.concatenate([kf[h] * (beta[h] * egc[h]), v_ref[rows, sls[h]].astype(F32) * beta[h]],
                               axis=1).astype(BF16) for h in heads]
        wu = [_dot(bdt[h], rhs[h]) for h in heads]
        yield
        au = [_dot(bda[h], wu[h].astype(BF16)) for h in heads]
        qt = [qb[h].astype(F32) * egc[h] - au[h][:, :DN_DK] for h in heads]
        out["kd"] = [(kf[h] * jnp.exp(gbv[:, grp(g_rest, h)])).astype(BF16) for h in heads]
        out["wqt"] = [jnp.concatenate([wu[h][:, :DN_DK], qt[h]], axis=1).astype(BF16) for h in heads]
        out["u"] = [wu[h][:, DN_DK:] for h in heads]
        out["o0"] = [au[h][:, DN_DK:] for h in heads]
        out["gbt"] = gbt
        yield

    def scan(r0, pre, state):
        order = range(nchunk - 1, -1, -1) if reverse else range(nchunk)
        for cc in order:
            rows = slice(cc * c, (cc + 1) * c)
            out_rows = slice(r0 + cc * c, r0 + (cc + 1) * c)
            for h in heads:
                wqt = pre["wqt"][h]
                lhs = jnp.concatenate([wqt[rows, :DN_DK], wqt[rows, DN_DK:]], axis=0)
                wq = _dot(lhs, state[h].astype(BF16))
                vn = pre["u"][h][rows] - wq[:c]
                o = wq[c:] + pre["o0"][h][rows]
                tot = pre["gbt"][grp(g_tot, h), cc * c:cc * c + 1]
                state[h] = state[h] * jnp.exp(tot) + lax.dot_general(
                    pre["kd"][h][rows], vn.astype(BF16), (((0,), (0,)), ((), ())),
                    preferred_element_type=F32)
                if reverse:
                    o = o + of_ref[out_rows, sls[h]]
                    o = o * lax.rsqrt(jnp.mean(o * o, axis=-1, keepdims=True) + EPS) * ng_ref[...]
                    zc = z_ref[out_rows, sls[h]].astype(F32)
                    o = o * (zc * _sigmoid(zc))
                o_ref[out_rows, sls[h]] = o.astype(o_ref.dtype)
            yield

    nsub = block // sub
    starts = [r * sub for r in (range(nsub - 1, -1, -1) if reverse else range(nsub))]
    state = [s_ref[h] for h in heads]
    pre_prev = None
    for idx in range(nsub + 1):
        pre = {}
        prep_gen = prepare(starts[idx], pre) if idx < nsub else iter(())
        scan_gen = scan(starts[idx - 1], pre_prev, state) if idx > 0 else iter(())
        prep_live = scan_live = True
        while prep_live or scan_live:
            if prep_live:
                prep_live = next(prep_gen, _DONE) is not _DONE
                if prep_live:
                    prep_live = next(prep_gen, _DONE) is not _DONE
            if scan_live:
                scan_live = next(scan_gen, _DONE) is not _DONE
        pre_prev = pre
    for h in heads:
        s_ref[h] = state[h]


_DONE = object()


def _dn_call(q, k, v, gb, extra, reverse, seq_len):
    n = q.shape[0]
    block = DN_BLOCK
    nblk = n // block
    imap = (lambda i: (nblk - 1 - i, 0)) if reverse else (lambda i: (i, 0))
    tok = lambda w: pl.BlockSpec((block, w), imap)
    in_specs = [tok(512), tok(512), tok(512), tok(LANES)]
    args = [q, k, v, gb]
    if reverse:
        o_f, z, ng = extra
        in_specs += [tok(512), tok(512), pl.BlockSpec(ng.shape, lambda i: (0, 0))]
        args += [o_f, z, ng]
    kern = functools.partial(_dn_kernel, block=block, sub=DN_SUB, reverse=reverse,
                             seq_len=seq_len, nblk=nblk)
    return pl.pallas_call(
        kern, grid=(nblk,), in_specs=in_specs, out_specs=tok(512),
        out_shape=jax.ShapeDtypeStruct((n, 512), BF16 if reverse else F32),
        scratch_shapes=[pltpu.VMEM((DN_HEADS, DN_DK, DN_DK), F32)],
        compiler_params=_cparams(("arbitrary",)),
        name="deltanet_bwd" if reverse else "deltanet_fwd")(*args)


def _route_kernel(x_ref, ya_ref, yb_ref, wo_ref, g2_ref, rw_ref, rb_ref, tri_ref,
                  x2_ref, xa_ref, xb_ref, info_ref, cnt_ref, run_ref, *, tile):
    i = pl.program_id(0)

    @pl.when(i == 0)
    def _():
        run_ref[...] = jnp.zeros_like(run_ref)

    sub = tri_ref.shape[0]
    parts = [slice(r, r + sub) for r in range(0, tile, sub)]
    each = lambda f, *lists: [f(*a) for a in zip(*lists)]
    half = wo_ref.shape[0] // 2
    x2 = [x_ref[rows, :] + _dot(ya_ref[rows, :], wo_ref[0:half, :])
          + _dot(yb_ref[rows, :], wo_ref[half:, :]) for rows in parts]
    for rows, v in zip(parts, x2):
        x2_ref[rows, :] = v
    xn = [_rms_rows(v, g2_ref[...]) for v in x2]
    for rows, v in zip(parts, xn):
        words = _pack_pairs(v)
        xa_ref[rows, :] = words[:, :ROW_WORDS]
        xb_ref[rows, :] = words[:, ROW_WORDS:]
    logits = [_dot(v.astype(BF16), rw_ref[...]) + rb_ref[...] for v in xn]
    lane = lax.broadcasted_iota(jnp.int32, (sub, LANES), 1).astype(F32)
    neg = jnp.float32(-jnp.inf)
    row_max = lambda v: jnp.max(v, axis=-1, keepdims=True)
    first_at = lambda v, m: jnp.min(jnp.where(v == m, lane, float(LANES)), axis=-1, keepdims=True)

    gl = [jnp.where(lane < N_GROUPS, v, neg) for v in logits]
    gmax = each(row_max, gl)
    gidx = each(first_at, gl, gmax)
    g_w = [1.0 / jnp.sum(jnp.exp(v - m), axis=-1, keepdims=True) for v, m in zip(gl, gmax)]
    lo = [N_GROUPS + g * EXPERTS_PER_GROUP for g in gidx]
    el = [jnp.where((lane >= a) & (lane < a + EXPERTS_PER_GROUP), v, neg) for v, a in zip(logits, lo)]
    m1 = each(row_max, el)
    i1 = each(first_at, el, m1)
    el2 = [jnp.where(lane == a, neg, v) for v, a in zip(el, i1)]
    m2 = each(row_max, el2)
    i2 = each(first_at, el2, m2)
    hit1 = [lane == a - N_GROUPS for a in i1]
    hit2 = [lane == a - N_GROUPS for a in i2]
    onehot = [(a | b).astype(BF16) for a, b in zip(hit1, hit2)]
    inside = [_dot(tri_ref[...], v) for v in onehot]
    run = run_ref[...]
    for k, rows in enumerate(parts):
        before = inside[k] + run
        pos1 = jnp.sum(jnp.where(hit1[k], before, 0.0), axis=-1, keepdims=True)
        pos2 = jnp.sum(jnp.where(hit2[k], before, 0.0), axis=-1, keepdims=True)
        run = run + jnp.sum(onehot[k].astype(F32), axis=0, keepdims=True)
        r = jnp.exp(m2[k] - m1[k])
        gate1 = g_w[k] / (1.0 + r)
        info_ref[rows, :] = jnp.where(lane == 0, i1[k] - N_GROUPS,
                            jnp.where(lane == 1, i2[k] - N_GROUPS,
                            jnp.where(lane == 2, gate1,
                            jnp.where(lane == 3, gate1 * r,
                            jnp.where(lane == 4, pos1,
                            jnp.where(lane == 5, pos2, 0.0))))))
    run_ref[...] = run
    cnt_ref[...] = run


def _route_call(x, ya, yb, wts):
    n, d = x.shape
    tile = TOKEN_TILE
    const = lambda i: (0, 0)
    tok = lambda w: pl.BlockSpec((tile, w), lambda i: (i, 0))
    in_specs = [tok(d), tok(512), tok(512)] + [pl.BlockSpec(w.shape, const) for w in wts]
    out_shape = [jax.ShapeDtypeStruct((n, d), F32), jax.ShapeDtypeStruct((n, ROW_WORDS), U32),
                 jax.ShapeDtypeStruct((n, ROW_WORDS), U32),
                 jax.ShapeDtypeStruct((n, LANES), F32), jax.ShapeDtypeStruct((1, LANES), F32)]
    out_specs = [tok(d), tok(ROW_WORDS), tok(ROW_WORDS), tok(LANES), pl.BlockSpec((1, LANES), const)]
    kern = functools.partial(_route_kernel, tile=tile)
    return pl.pallas_call(
        kern, grid=(n // tile,), in_specs=in_specs, out_specs=out_specs, out_shape=out_shape,
        scratch_shapes=[pltpu.VMEM((1, LANES), F32)],
        compiler_params=_cparams(("arbitrary",)), name="oproj_router")(x, ya, yb, *wts)


def _sc_mesh():
    return plsc.VectorSubcoreMesh(core_axis_name="core", subcore_axis_name="subcore")


def _sc_scatter_rows(x, idx0, idx1, rows):
    n, w = x.shape

    @pl.kernel(out_type=jax.ShapeDtypeStruct((rows, w), x.dtype), mesh=_sc_mesh(), scratch_types=[])
    def scatter(x_hbm, i0_hbm, i1_hbm, o_hbm):
        def body(x_vmem, i0_vmem, i1_vmem):
            pltpu.sync_copy(x_vmem, o_hbm.at[i0_vmem.at[0]])
            pltpu.sync_copy(x_vmem, o_hbm.at[i1_vmem.at[0]])

        pltpu.emit_pipeline(
            body, grid=(n // SC_WINDOW,),
            in_specs=[pl.BlockSpec((SC_WINDOW, w), lambda i: (i, 0)),
                      pl.BlockSpec((1, SC_WINDOW), lambda i: (0, i)),
                      pl.BlockSpec((1, SC_WINDOW), lambda i: (0, i))],
            out_specs=[], core_axis_name=("core", "subcore"),
            dimension_semantics=(pltpu.PARALLEL,))(x_hbm, i0_hbm, i1_hbm)

    return scatter(x, idx0, idx1)


def _sc_gather_rows(y, idx):
    m = idx.shape[1]
    w = y.shape[1]

    @pl.kernel(out_type=jax.ShapeDtypeStruct((m, w), y.dtype), mesh=_sc_mesh(), scratch_types=[])
    def gather(y_hbm, i_hbm, o_hbm):
        def body(i_vmem, o_vmem):
            pltpu.sync_copy(y_hbm.at[i_vmem.at[0]], o_vmem)

        pltpu.emit_pipeline(
            body, grid=(m // SC_WINDOW,),
            in_specs=[pl.BlockSpec((1, SC_WINDOW), lambda i: (0, i))],
            out_specs=[pl.BlockSpec((SC_WINDOW, w), lambda i: (i, 0))],
            core_axis_name=("core", "subcore"),
            dimension_semantics=(pltpu.PARALLEL,))(i_hbm, o_hbm)

    return gather(y, idx)


def _join_rows(wa, wb):
    lo_a, hi_a = _unpack_pairs(wa)
    lo_b, hi_b = _unpack_pairs(wb)
    return jnp.concatenate([lo_a, lo_b, hi_a, hi_b], axis=1)


def _expert_kernel(te_ref, nx_ref, nv_ref, nu_ref, xa_ref, xb_ref, w1_hbm, w3_hbm, w2_hbm,
                   ya_ref, yb_ref, stage1, stage3, stage2, w1_ref, w3_ref, w2_ref, sem):
    j = pl.program_id(0)

    def fetch(e):
        return [pltpu.make_async_copy(src.at[e], dst, sem.at[k]) for k, (src, dst) in
                enumerate(((w1_hbm, stage1), (w3_hbm, stage3), (w2_hbm, stage2)))]

    @pl.when(j == 0)
    def _():
        for cp in fetch(te_ref[0]):
            cp.start()

    @pl.when((j < nu_ref[0]) & (nx_ref[j] > -2))
    def _():
        for cp in fetch(te_ref[j]):
            cp.wait()
        w1_ref[...] = stage1[...].astype(BF16)
        w3_ref[...] = stage3[...].astype(BF16)
        w2_ref[...] = stage2[...].astype(BF16)

        @pl.when(nx_ref[j] >= 0)
        def _():
            for cp in fetch(nx_ref[j]):
                cp.start()

    @pl.when(j < nu_ref[0])
    def _():
        live = lax.broadcasted_iota(jnp.int32, xa_ref.shape, 0) < nv_ref[j]
        zero = jnp.zeros(xa_ref.shape, U32)
        x = _join_rows(jnp.where(live, xa_ref[...], zero),
                       jnp.where(live, xb_ref[...], zero)).astype(BF16)
        h1 = _dot(x, w1_ref[...])
        h3 = _dot(x, w3_ref[...])
        hdn = (h1 * _sigmoid(h1) * h3).astype(BF16)
        words = _pack_pairs(_dot(hdn, w2_ref[...]))
        ya_ref[...] = words[:, :ROW_WORDS]
        yb_ref[...] = words[:, ROW_WORDS:]


def _expert_call(tile_expert, tile_next, tile_valid, n_used, xa, xb, w1, w3, w2):
    rows, rw = xa.shape
    tm = EXPERT_TILE
    d, ff = w1.shape[1], w1.shape[2]
    row_blk = lambda j, te, nx, nv, nu: (jnp.minimum(j, nu[0] - 1), 0)
    hbm = pl.BlockSpec(memory_space=pl.ANY)
    grid_spec = pltpu.PrefetchScalarGridSpec(
        num_scalar_prefetch=4, grid=(rows // tm,),
        in_specs=[pl.BlockSpec((tm, rw), row_blk), pl.BlockSpec((tm, rw), row_blk), hbm, hbm, hbm],
        out_specs=[pl.BlockSpec((tm, rw), row_blk), pl.BlockSpec((tm, rw), row_blk)],
        scratch_shapes=[pltpu.VMEM((d, ff), w1.dtype), pltpu.VMEM((d, ff), w3.dtype),
                        pltpu.VMEM((ff, d), w2.dtype),
                        pltpu.VMEM((d, ff), BF16), pltpu.VMEM((d, ff), BF16), pltpu.VMEM((ff, d), BF16),
                        pltpu.SemaphoreType.DMA((3,))])
    return pl.pallas_call(
        _expert_kernel, grid_spec=grid_spec,
        out_shape=[jax.ShapeDtypeStruct((rows, rw), U32), jax.ShapeDtypeStruct((rows, rw), U32)],
        compiler_params=_cparams(("arbitrary",)), name="moe_experts")(
            tile_expert, tile_next, tile_valid, n_used, xa, xb, w1, w3, w2)


def _combine_kernel(x2_ref, info_ref, gf_ref, a0_ref, b0_ref, a1_ref, b1_ref, y_ref):
    info = info_ref[...]
    moe = (info[:, 2:3] * _join_rows(a0_ref[...], b0_ref[...])
           + info[:, 3:4] * _join_rows(a1_ref[...], b1_ref[...]))
    y_ref[...] = _rms_rows(x2_ref[...] + moe, gf_ref[...])


def _combine_call(x2, info, gf, ga, gb):
    n, d = x2.shape
    tile = TOKEN_TILE
    nt = n // tile
    first = pl.BlockSpec((tile, ROW_WORDS), lambda i: (i, 0))
    second = pl.BlockSpec((tile, ROW_WORDS), lambda i: (i + nt, 0))
    return pl.pallas_call(
        _combine_kernel, grid=(nt,),
        in_specs=[pl.BlockSpec((tile, d), lambda i: (i, 0)),
                  pl.BlockSpec((tile, LANES), lambda i: (i, 0)),
                  pl.BlockSpec(gf.shape, lambda i: (0, 0)),
                  first, first, second, second],
        out_specs=pl.BlockSpec((tile, d), lambda i: (i, 0)),
        out_shape=jax.ShapeDtypeStruct((n, d), F32),
        compiler_params=_cparams(("arbitrary",)), name="moe_combine")(
            x2, info, gf, ga, gb, ga, gb)


def _row(a):
    return a.reshape(1, -1).astype(F32)


def _prep_weights(norm1_g, w_in, conv_a_w, conv_a_norm_g, dn_conv_w, dn_a_log, dn_dt_bias, w_o,
                  norm2_g, rg_w, rg_b, re_w, re_b):
    row = _row
    n_ab = len(_GB_COLS)
    w_main = jnp.concatenate([w_in[:, :7 * CONV_WIDTH], w_in[:, 7 * CONV_WIDTH + _GB_COLS]], axis=1)
    w_main = jnp.pad(w_main, ((0, 0), (0, LANES - n_ab))).astype(BF16)
    cw = jnp.concatenate([conv_a_w, dn_conv_w], axis=1).astype(F32)
    grp = jnp.arange(CONV_WIDTH) // (CONV_WIDTH // CONV_GROUPS)
    gmat = ((grp[:, None] == grp[None, :]).astype(F32) / (CONV_WIDTH // CONV_GROUPS)).astype(BF16)
    is_a = (_GB_COLS < 2 * DN_HEADS)
    a_idx = jnp.where(is_a, _GB_COLS, 0)
    padl = lambda a: jnp.pad(a.reshape(1, -1).astype(F32), ((0, 0), (0, LANES - n_ab)))
    nea = padl(jnp.where(is_a, -jnp.exp(dn_a_log.astype(F32)).reshape(-1)[a_idx], 0.0))
    dtb = padl(jnp.where(is_a, dn_dt_bias.astype(F32).reshape(-1)[a_idx], 0.0))
    proj_w = [row(norm1_g), w_main, cw, row(conv_a_norm_g), gmat, nea, dtb]
    rw = jnp.pad(jnp.concatenate([rg_w, re_w], axis=1).astype(F32),
                 ((0, 0), (0, LANES - N_GROUPS - N_EXPERTS)))
    rb = jnp.pad(jnp.concatenate([rg_b, re_b]).reshape(1, -1).astype(F32),
                 ((0, 0), (0, LANES - N_GROUPS - N_EXPERTS)))
    tri = (jnp.arange(ROUTE_SUB)[:, None] > jnp.arange(ROUTE_SUB)[None, :]).astype(BF16)
    route_w = [w_o.astype(BF16), row(norm2_g), rw.astype(BF16), rb, tri]
    return proj_w, route_w


def _mix_and_route(x, seq_len, proj_w, dn_norm_g, route_w):
    ya, q, k, v, z, gb = _proj_call(x, seq_len, proj_w)
    o_f = _dn_call(q, k, v, gb, None, False, seq_len)
    yb = _dn_call(q, k, v, gb, (o_f, z, _row(dn_norm_g)), True, seq_len)
    return _route_call(x, ya, yb, route_w)


def _moe(routed, w1, w3, w2, final_g):
    x2, xna, xnb, info, cnt = routed
    n = x2.shape[0]
    tm = EXPERT_TILE
    counts = cnt[0, :N_EXPERTS].astype(jnp.int32)
    pcounts = (counts + tm - 1) // tm * tm
    pends = jnp.cumsum(pcounts)
    pstarts = pends - pcounts
    eid = jnp.arange(N_EXPERTS, dtype=jnp.int32)
    e12 = info[:, 0:2].astype(jnp.int32)
    start12 = jnp.sum(jnp.where(e12[:, :, None] == eid, pstarts, 0), axis=-1)
    dest = start12 + info[:, 4:6].astype(jnp.int32)
    dest0 = dest[:, 0].reshape(1, n)
    dest1 = dest[:, 1].reshape(1, n)
    n_tiles = 2 * n // tm + N_EXPERTS
    tile_start = jnp.arange(n_tiles, dtype=jnp.int32) * tm
    tile_expert = jnp.minimum(jnp.sum(pends[None, :] <= tile_start[:, None], axis=1),
                              N_EXPERTS - 1).astype(jnp.int32)
    of_tile = lambda table: jnp.sum(jnp.where(tile_expert[:, None] == eid, table, 0), axis=-1)
    tile_valid = jnp.clip(of_tile(pstarts + counts) - tile_start, 0, tm).astype(jnp.int32)
    n_used = (pends[-1:] // tm).astype(jnp.int32)
    later = (eid[None, :] > eid[:, None]) & (pcounts[None, :] > 0)
    next_e = jnp.min(jnp.where(later, eid[None, :], N_EXPERTS), axis=1)
    next_e = jnp.where(next_e == N_EXPERTS, -1, next_e)
    tile_next = jnp.where(tile_start == of_tile(pstarts), of_tile(next_e), -2).astype(jnp.int32)

    rows = n_tiles * tm
    xa = _sc_scatter_rows(xna, dest0, dest1, rows)
    xb = _sc_scatter_rows(xnb, dest0, dest1, rows)
    ya_e, yb_e = _expert_call(tile_expert, tile_next, tile_valid, n_used, xa, xb, w1, w3, w2)
    both = jnp.concatenate([dest0, dest1], axis=1)
    return _combine_call(x2, info, _row(final_g), _sc_gather_rows(ya_e, both),
                         _sc_gather_rows(yb_e, both))


def kernel(x_prompt, x_sample, norm1_g, w_in, conv_a_w, conv_a_norm_g, dn_conv_w, dn_a_log,
           dn_dt_bias, dn_norm_g, w_o, norm2_g, router_group_w, router_group_b, router_expert_w,
           router_expert_b, w1, w3, w2, final_norm_g):
    assert norm1_g.shape[0] == 1, "single-layer trunk"
    bp, sp, d = x_prompt.shape
    bs, ss, _ = x_sample.shape
    assert bp == 1 and sp % max(TOKEN_TILE, DN_BLOCK) == 0 and ss % max(TOKEN_TILE, DN_BLOCK) == 0
    proj_w, route_w = _prep_weights(
        norm1_g[0], w_in[0], conv_a_w[0], conv_a_norm_g[0], dn_conv_w[0], dn_a_log[0],
        dn_dt_bias[0], w_o[0], norm2_g[0], router_group_w[0], router_group_b[0],
        router_expert_w[0], router_expert_b[0])
    routed = [_mix_and_route(x.reshape(-1, d), seq_len, proj_w, dn_norm_g[0], route_w)
              for x, seq_len in ((x_prompt, sp), (x_sample, ss))]
    y_p, y_s = [_moe(r, w1[0], w3[0], w2[0], final_norm_g) for r in routed]
    return y_p.reshape(bp, sp, d), y_s.reshape(bs, ss, d)
```

```python
import functools

import jax
import jax.numpy as jnp
import numpy as np
from jax import lax
from jax.experimental import pallas as pl
from jax.experimental.pallas import tpu as pltpu
from jax.experimental.pallas import tpu_sc as plsc

F32 = jnp.float32
BF16 = jnp.bfloat16
U32 = jnp.uint32

EPS = 1e-6
CONV_WIDTH = 512
CONV_GROUPS = 8
DN_HEADS = 4
DN_DK = 128
DN_V = 512
CHUNK = 64
N_GROUPS = 4
EXPERTS_PER_GROUP = 8
N_EXPERTS = N_GROUPS * EXPERTS_PER_GROUP
EXPERT_FF = 512
LANES = 128
HALO = 8

TOKEN_TILE = 512
ROUTE_SUB = 256
DN_BLOCK = 1024
DN_SUB = 256
EXPERT_TILE = 512
ROW_WORDS = 256
SC_WINDOW = 128
VMEM_LIMIT = 56 * 1024 * 1024

_GB_COLS = np.concatenate([np.arange(4 * DN_HEADS), np.arange(2 * DN_HEADS), np.arange(2 * DN_HEADS)])


def _cparams(sem):
    return pltpu.CompilerParams(dimension_semantics=sem, vmem_limit_bytes=VMEM_LIMIT)


def _sigmoid(x):
    return 1.0 / (1.0 + jnp.exp(-x))


def _rms_rows(x, g):
    return x * lax.rsqrt(jnp.mean(x * x, axis=-1, keepdims=True) + EPS) * g


def _dot(a, b):
    return jnp.dot(a, b, preferred_element_type=F32)


def _pack_pairs(x):
    half = x.shape[1] // 2
    bits = lambda a: lax.bitcast_convert_type(a.astype(BF16).astype(F32), U32)
    return (bits(x[:, half:]) & jnp.uint32(0xFFFF0000)) | (bits(x[:, :half]) >> 16)


def _unpack_pairs(w):
    lo = lax.bitcast_convert_type(w << 16, F32)
    hi = lax.bitcast_convert_type(w & jnp.uint32(0xFFFF0000), F32)
    return lo, hi


def _proj_kernel(x_ref, xprev_ref, xnext_ref,
                 g1_ref, w_ref, cw_ref, cng_ref, gmat_ref, nea_ref, dtb_ref,
                 ya_ref, q_ref, k_ref, v_ref, z_ref, gb_ref,
                 hs_ref, su_ref, p_ref, *, tile, seq_len):
    tok0 = pl.program_id(0) * tile
    not_start = (tok0 % seq_len != 0).astype(F32)
    not_end = ((tok0 + tile) % seq_len != 0).astype(F32)
    g1 = g1_ref[...]
    hs_ref[0:HALO, :] = _rms_rows(xprev_ref[...], g1)
    hs_ref[HALO:HALO + tile, :] = _rms_rows(x_ref[...], g1)
    hs_ref[HALO + tile:, :] = _rms_rows(xnext_ref[...], g1)

    p_ref[...] = _dot(hs_ref[...].astype(BF16), w_ref[...])
    for stage in _FINISH_STAGES:
        stage(p_ref, su_ref, cw_ref, cng_ref, gmat_ref, nea_ref, dtb_ref,
              ya_ref, q_ref, k_ref, v_ref, z_ref, gb_ref, tile, not_start, not_end)


def _conv_cols(p_ref, su_ref, cw_ref, lo, width, tile, not_start, not_end):
    cw = CONV_WIDTH
    cols = slice(lo, lo + width)
    if lo < cw:
        su_ref[:, cols] = p_ref[:, cw + lo:cw + lo + width] * p_ref[:, 2 * cw + lo:2 * cw + lo + width]
    else:
        su_ref[:, cols] = p_ref[:, 2 * cw + lo:2 * cw + lo + width]
    su_ref[0:HALO, cols] = su_ref[0:HALO, cols] * not_start
    su_ref[HALO + tile:, cols] = su_ref[HALO + tile:, cols] * not_end
    u = su_ref[:, cols]
    total = tile + 2 * HALO
    return (pltpu.roll(u, 1, axis=0)[HALO:HALO + tile] * cw_ref[0:1, cols]
            + u[HALO:HALO + tile] * cw_ref[1:2, cols]
            + pltpu.roll(u, total - 1, axis=0)[HALO:HALO + tile] * cw_ref[2:3, cols])


def _stage_mixer_a(half):
    width = CONV_WIDTH // 2
    lo = half * width

    def stage(p_ref, su_ref, cw_ref, cng_ref, gmat_ref, nea_ref, dtb_ref,
              ya_ref, q_ref, k_ref, v_ref, z_ref, gb_ref, tile, not_start, not_end):
        conv = _conv_cols(p_ref, su_ref, cw_ref, lo, width, tile, not_start, not_end)
        ya = p_ref[HALO:HALO + tile, lo:lo + width] * conv
        ms = _dot((ya * ya).astype(BF16), gmat_ref[lo:lo + width, lo:lo + width])
        ya_ref[:, lo:lo + width] = (ya * lax.rsqrt(ms + EPS) * cng_ref[:, lo:lo + width]).astype(ya_ref.dtype)
    return stage


def _stage_mixer_b(which, half):
    width = DN_V // 2
    lo_out = half * width
    lo = CONV_WIDTH + which * DN_V + lo_out

    def stage(p_ref, su_ref, cw_ref, cng_ref, gmat_ref, nea_ref, dtb_ref,
              ya_ref, q_ref, k_ref, v_ref, z_ref, gb_ref, tile, not_start, not_end):
        x = _conv_cols(p_ref, su_ref, cw_ref, lo, width, tile, not_start, not_end)
        x = x * _sigmoid(x)
        out_ref = (q_ref, k_ref, v_ref)[which]
        if which == 2:
            out_ref[:, lo_out:lo_out + width] = x.astype(out_ref.dtype)
            return
        scale = DN_DK ** -0.5 if which == 0 else 1.0
        for h in range(width // DN_DK):
            xh = x[:, h * DN_DK:(h + 1) * DN_DK]
            xh = xh * (lax.rsqrt(jnp.sum(xh * xh, axis=-1, keepdims=True) + EPS) * scale)
            out_ref[:, lo_out + h * DN_DK:lo_out + (h + 1) * DN_DK] = xh.astype(out_ref.dtype)
    return stage


def _stage_gates(inner):
    def stage(p_ref, su_ref, cw_ref, cng_ref, gmat_ref, nea_ref, dtb_ref,
              ya_ref, q_ref, k_ref, v_ref, z_ref, gb_ref, tile, not_start, not_end):
        inner(p_ref, su_ref, cw_ref, cng_ref, gmat_ref, nea_ref, dtb_ref,
              ya_ref, q_ref, k_ref, v_ref, z_ref, gb_ref, tile, not_start, not_end)
        cw = CONV_WIDTH
        z_ref[...] = p_ref[HALO:HALO + tile, 6 * cw:7 * cw].astype(z_ref.dtype)
        ab = p_ref[HALO:HALO + tile, 7 * cw:7 * cw + LANES]
        xs = ab + dtb_ref[...]
        softplus = jnp.maximum(xs, 0.0) + jnp.log(1.0 + jnp.exp(-jnp.abs(xs)))
        g = nea_ref[...] * softplus
        beta = _sigmoid(ab)
        r = lax.broadcasted_iota(jnp.int32, (tile, LANES), 0) % CHUNK
        pre = g
        suf = g
        s = 1
        while s < CHUNK:
            pre = pre + jnp.where(r >= s, pltpu.roll(pre, s, axis=0), 0.0)
            suf = suf + jnp.where(r < CHUNK - s, pltpu.roll(suf, tile - s, axis=0), 0.0)
            s *= 2
        lane = lax.broadcasted_iota(jnp.int32, (tile, LANES), 1) // DN_HEADS
        gb_ref[...] = jnp.where(lane == 0, pre,
                      jnp.where(lane == 1, suf,
                      jnp.where(lane < 4, beta,
                      jnp.where(lane == 4, suf - g,
                      jnp.where(lane == 5, pre - g, pre + suf - g)))))
    return stage


_FINISH_STAGES = (_stage_mixer_a(0), _stage_mixer_a(1), _stage_mixer_b(0, 0), _stage_mixer_b(0, 1),
                  _stage_mixer_b(1, 0), _stage_mixer_b(1, 1), _stage_mixer_b(2, 0),
                  _stage_gates(_stage_mixer_b(2, 1)))


def _proj_call(x, seq_len, wts):
    n, d = x.shape
    tile = TOKEN_TILE
    nt = n // tile
    tb = tile // HALO
    kern = functools.partial(_proj_kernel, tile=tile, seq_len=seq_len)
    const = lambda i: (0, 0)
    in_specs = [pl.BlockSpec((tile, d), lambda i: (i, 0)),
                pl.BlockSpec((HALO, d), lambda i: (jnp.maximum(i * tb - 1, 0), 0)),
                pl.BlockSpec((HALO, d), lambda i: (jnp.minimum((i + 1) * tb, n // HALO - 1), 0))]
    in_specs += [pl.BlockSpec(w.shape, const, pipeline_mode=pl.Buffered(1)) for w in wts]
    widths = (512, 512, 512, 512, 512, LANES)
    dtypes = (BF16, BF16, BF16, BF16, BF16, F32)
    rows = tile + 2 * HALO
    return pl.pallas_call(
        kern, grid=(nt,), in_specs=in_specs,
        out_specs=[pl.BlockSpec((tile, w), lambda i: (i, 0)) for w in widths],
        out_shape=[jax.ShapeDtypeStruct((n, w), dt) for w, dt in zip(widths, dtypes)],
        scratch_shapes=[pltpu.VMEM((rows, d), F32), pltpu.VMEM((rows, 4 * CONV_WIDTH), F32),
                        pltpu.VMEM((rows, wts[1].shape[1]), F32)],
        compiler_params=_cparams(("arbitrary",)), name="proj_conv")(x, x, x, *wts)


def _dn_kernel(*refs, block, sub, reverse, seq_len, nblk):
    if reverse:
        (q_ref, k_ref, v_ref, gb_ref, of_ref, z_ref, ng_ref, o_ref, s_ref) = refs
    else:
        (q_ref, k_ref, v_ref, gb_ref, o_ref, s_ref) = refs
    i = pl.program_id(0)
    b = nblk - 1 - i if reverse else i
    edge = (b + 1) * block if reverse else b * block

    @pl.when(edge % seq_len == 0)
    def _():
        s_ref[...] = jnp.zeros_like(s_ref)

    nchunk = sub // CHUNK
    c = CHUNK
    heads = range(DN_HEADS)
    ri = lax.broadcasted_iota(jnp.int32, (sub, sub), 0)
    ci = lax.broadcasted_iota(jnp.int32, (sub, sub), 1)
    same = (ri // c) == (ci // c)
    incl = same & ((ri <= ci) if reverse else (ri >= ci))
    strict = same & ((ri < ci) if reverse else (ri > ci))
    pr = lax.broadcasted_iota(jnp.int32, (c, sub), 0)
    pc = lax.broadcasted_iota(jnp.int32, (c, sub), 1)
    eye_p = (pr == pc % c).astype(F32)
    lane_chunk = pc // c

    def fold(bd):
        out = bd[0:c]
        for r in range(1, nchunk):
            out = out + bd[r * c:(r + 1) * c]
        return out

    def unfold(p):
        zero = jnp.zeros_like(p)
        return jnp.concatenate([jnp.where(lane_chunk == r, p, zero) for r in range(nchunk)], axis=0)

    grp = lambda g, h: slice(g * DN_HEADS + h, g * DN_HEADS + h + 1)
    g_cum, g_beta, g_rest, g_tot = (1, 3, 5, 7) if reverse else (0, 2, 4, 6)
    sls = [slice(h * DN_DK, (h + 1) * DN_DK) for h in heads]

    def prepare(r0, out):
        rows = slice(r0, r0 + sub)
        gbv = gb_ref[rows, :]
        gbt = gbv.T
        kb = [k_ref[rows, sl] for sl in sls]
        qb = [q_ref[rows, sl] for sl in sls]
        gcol = [gbv[:, grp(g_cum, h)] for h in heads]
        beta = [gbv[:, grp(g_beta, h)] for h in heads]
        kq = [lax.dot_general(jnp.concatenate([kb[h], qb[h]], axis=0), kb[h],
                              (((1,), (1,)), ((), ())), preferred_element_type=F32) for h in heads]
        yield
        decay = [jnp.exp(jnp.minimum(gcol[h] - gbt[grp(g_cum, h), :], 0.0)) for h in heads]
        bdl = [jnp.where(strict, kq[h][:sub] * decay[h], 0.0) * beta[h] for h in heads]
        bda = [jnp.where(incl, kq[h][sub:] * decay[h], 0.0).astype(BF16) for h in heads]
        yield
        lp = [fold(bdl[h]) for h in heads]
        xp = [eye_p - lp[h] for h in heads]
        bd = [bdl[h].astype(BF16) for h in heads]
        p = 1
        while p < c:
            if p == 1:
                lp = [_dot(lp[h].astype(BF16), bd[h]) for h in heads]
            elif 2 * p < c:
                res = [_dot(jnp.concatenate([lp[h], xp[h]], axis=0).astype(BF16), bd[h]) for h in heads]
                lp = [r[:c] for r in res]
                xp = [xp[h] + res[h][c:] for h in heads]
            else:
                xp = [xp[h] + _dot(xp[h].astype(BF16), bd[h]) for h in heads]
            p *= 2
            if p < c:
                bd = [unfold(lp[h].astype(BF16)) for h in heads]
            yield
        bdt = [unfold(xp[h].astype(BF16)) for h in heads]
        egc = [jnp.exp(gcol[h]) for h in heads]
        kf = [kb[h].astype(F32) for h in heads]
        rhs = [jnp.concatenate([kf[h] * (beta[h] * egc[h]), v_ref[rows, sls[h]].astype(F32) * beta[h]],
                               axis=1).astype(BF16) for h in heads]
        wu = [_dot(bdt[h], rhs[h]) for h in heads]
        yield
        au = [_dot(bda[h], wu[h].astype(BF16)) for h in heads]
        qt = [qb[h].astype(F32) * egc[h] - au[h][:, :DN_DK] for h in heads]
        out["kd"] = [(kf[h] * jnp.exp(gbv[:, grp(g_rest, h)])).astype(BF16) for h in heads]
        out["wqt"] = [jnp.concatenate([wu[h][:, :DN_DK], qt[h]], axis=1).astype(BF16) for h in heads]
        out["u"] = [wu[h][:, DN_DK:] for h in heads]
        out["o0"] = [au[h][:, DN_DK:] for h in heads]
        out["gbt"] = gbt
        yield

    def scan(r0, pre, state):
        order = range(nchunk - 1, -1, -1) if reverse else range(nchunk)
        for cc in order:
            rows = slice(cc * c, (cc + 1) * c)
            out_rows = slice(r0 + cc * c, r0 + (cc + 1) * c)
            for h in heads:
                wqt = pre["wqt"][h]
                lhs = jnp.concatenate([wqt[rows, :DN_DK], wqt[rows, DN_DK:]], axis=0)
                wq = _dot(lhs, state[h].astype(BF16))
                vn = pre["u"][h][rows] - wq[:c]
                o = wq[c:] + pre["o0"][h][rows]
                tot = pre["gbt"][grp(g_tot, h), cc * c:cc * c + 1]
                state[h] = state[h] * jnp.exp(tot) + lax.dot_general(
                    pre["kd"][h][rows], vn.astype(BF16), (((0,), (0,)), ((), ())),
                    preferred_element_type=F32)
                if reverse:
                    o = o + of_ref[out_rows, sls[h]]
                    o = o * lax.rsqrt(jnp.mean(o * o, axis=-1, keepdims=True) + EPS) * ng_ref[...]
                    zc = z_ref[out_rows, sls[h]].astype(F32)
                    o = o * (zc * _sigmoid(zc))
                o_ref[out_rows, sls[h]] = o.astype(o_ref.dtype)
            yield

    nsub = block // sub
    starts = [r * sub for r in (range(nsub - 1, -1, -1) if reverse else range(nsub))]
    state = [s_ref[h] for h in heads]
    pre_prev = None
    for idx in range(nsub + 1):
        pre = {}
        prep_gen = prepare(starts[idx], pre) if idx < nsub else iter(())
        scan_gen = scan(starts[idx - 1], pre_prev, state) if idx > 0 else iter(())
        prep_live = scan_live = True
        while prep_live or scan_live:
            if prep_live:
                prep_live = next(prep_gen, _DONE) is not _DONE
                if prep_live:
                    prep_live = next(prep_gen, _DONE) is not _DONE
            if scan_live:
                scan_live = next(scan_gen, _DONE) is not _DONE
        pre_prev = pre
    for h in heads:
        s_ref[h] = state[h]


_DONE = object()


def _dn_call(q, k, v, gb, extra, reverse, seq_len):
    n = q.shape[0]
    block = DN_BLOCK
    nblk = n // block
    imap = (lambda i: (nblk - 1 - i, 0)) if reverse else (lambda i: (i, 0))
    tok = lambda w: pl.BlockSpec((block, w), imap)
    in_specs = [tok(512), tok(512), tok(512), tok(LANES)]
    args = [q, k, v, gb]
    if reverse:
        o_f, z, ng = extra
        in_specs += [tok(512), tok(512), pl.BlockSpec(ng.shape, lambda i: (0, 0))]
        args += [o_f, z, ng]
    kern = functools.partial(_dn_kernel, block=block, sub=DN_SUB, reverse=reverse,
                             seq_len=seq_len, nblk=nblk)
    return pl.pallas_call(
        kern, grid=(nblk,), in_specs=in_specs, out_specs=tok(512),
        out_shape=jax.ShapeDtypeStruct((n, 512), BF16 if reverse else F32),
        scratch_shapes=[pltpu.VMEM((DN_HEADS, DN_DK, DN_DK), F32)],
        compiler_params=_cparams(("arbitrary",)),
        name="deltanet_bwd" if reverse else "deltanet_fwd")(*args)


def _route_kernel(x_ref, ya_ref, yb_ref, wo_ref, g2_ref, rw_ref, rb_ref, tri_ref,
                  x2_ref, xa_ref, xb_ref, info_ref, cnt_ref, run_ref, *, tile):
    i = pl.program_id(0)

    @pl.when(i == 0)
    def _():
        run_ref[...] = jnp.zeros_like(run_ref)

    sub = tri_ref.shape[0]
    parts = [slice(r, r + sub) for r in range(0, tile, sub)]
    each = lambda f, *lists: [f(*a) for a in zip(*lists)]
    half = wo_ref.shape[0] // 2
    x2 = [x_ref[rows, :] + _dot(ya_ref[rows, :], wo_ref[0:half, :])
          + _dot(yb_ref[rows, :], wo_ref[half:, :]) for rows in parts]
    for rows, v in zip(parts, x2):
        x2_ref[rows, :] = v
    xn = [_rms_rows(v, g2_ref[...]) for v in x2]
    for rows, v in zip(parts, xn):
        words = _pack_pairs(v)
        xa_ref[rows, :] = words[:, :ROW_WORDS]
        xb_ref[rows, :] = words[:, ROW_WORDS:]
    logits = [_dot(v.astype(BF16), rw_ref[...]) + rb_ref[...] for v in xn]
    lane = lax.broadcasted_iota(jnp.int32, (sub, LANES), 1).astype(F32)
    neg = jnp.float32(-jnp.inf)
    row_max = lambda v: jnp.max(v, axis=-1, keepdims=True)
    first_at = lambda v, m: jnp.min(jnp.where(v == m, lane, float(LANES)), axis=-1, keepdims=True)

    gl = [jnp.where(lane < N_GROUPS, v, neg) for v in logits]
    gmax = each(row_max, gl)
    gidx = each(first_at, gl, gmax)
    g_w = [1.0 / jnp.sum(jnp.exp(v - m), axis=-1, keepdims=True) for v, m in zip(gl, gmax)]
    lo = [N_GROUPS + g * EXPERTS_PER_GROUP for g in gidx]
    el = [jnp.where((lane >= a) & (lane < a + EXPERTS_PER_GROUP), v, neg) for v, a in zip(logits, lo)]
    m1 = each(row_max, el)
    i1 = each(first_at, el, m1)
    el2 = [jnp.where(lane == a, neg, v) for v, a in zip(el, i1)]
    m2 = each(row_max, el2)
    i2 = each(first_at, el2, m2)
    hit1 = [lane == a - N_GROUPS for a in i1]
    hit2 = [lane == a - N_GROUPS for a in i2]
    onehot = [(a | b).astype(BF16) for a, b in zip(hit1, hit2)]
    inside = [_dot(tri_ref[...], v) for v in onehot]
    run = run_ref[...]
    for k, rows in enumerate(parts):
        before = inside[k] + run
        pos1 = jnp.sum(jnp.where(hit1[k], before, 0.0), axis=-1, keepdims=True)
        pos2 = jnp.sum(jnp.where(hit2[k], before, 0.0), axis=-1, keepdims=True)
        run = run + jnp.sum(onehot[k].astype(F32), axis=0, keepdims=True)
        r = jnp.exp(m2[k] - m1[k])
        gate1 = g_w[k] / (1.0 + r)
        info_ref[rows, :] = jnp.where(lane == 0, i1[k] - N_GROUPS,
                            jnp.where(lane == 1, i2[k] - N_GROUPS,
                            jnp.where(lane == 2, gate1,
                            jnp.where(lane == 3, gate1 * r,
                            jnp.where(lane == 4, pos1,
                            jnp.where(lane == 5, pos2, 0.0))))))
    run_ref[...] = run
    cnt_ref[...] = run


def _route_call(x, ya, yb, wts):
    n, d = x.shape
    tile = TOKEN_TILE
    const = lambda i: (0, 0)
    tok = lambda w: pl.BlockSpec((tile, w), lambda i: (i, 0))
    in_specs = [tok(d), tok(512), tok(512)] + [pl.BlockSpec(w.shape, const) for w in wts]
    out_shape = [jax.ShapeDtypeStruct((n, d), F32), jax.ShapeDtypeStruct((n, ROW_WORDS), U32),
                 jax.ShapeDtypeStruct((n, ROW_WORDS), U32),
                 jax.ShapeDtypeStruct((n, LANES), F32), jax.ShapeDtypeStruct((1, LANES), F32)]
    out_specs = [tok(d), tok(ROW_WORDS), tok(ROW_WORDS), tok(LANES), pl.BlockSpec((1, LANES), const)]
    kern = functools.partial(_route_kernel, tile=tile)
    return pl.pallas_call(
        kern, grid=(n // tile,), in_specs=in_specs, out_specs=out_specs, out_shape=out_shape,
        scratch_shapes=[pltpu.VMEM((1, LANES), F32)],
        compiler_params=_cparams(("arbitrary",)), name="oproj_router")(x, ya, yb, *wts)


def _sc_mesh():
    return plsc.VectorSubcoreMesh(core_axis_name="core", subcore_axis_name="subcore")


def _sc_scatter_rows(x, idx0, idx1, rows):
    n, w = x.shape

    @pl.kernel(out_type=jax.ShapeDtypeStruct((rows, w), x.dtype), mesh=_sc_mesh(), scratch_types=[])
    def scatter(x_hbm, i0_hbm, i1_hbm, o_hbm):
        def body(x_vmem, i0_vmem, i1_vmem):
            pltpu.sync_copy(x_vmem, o_hbm.at[i0_vmem.at[0]])
            pltpu.sync_copy(x_vmem, o_hbm.at[i1_vmem.at[0]])

        pltpu.emit_pipeline(
            body, grid=(n // SC_WINDOW,),
            in_specs=[pl.BlockSpec((SC_WINDOW, w), lambda i: (i, 0)),
                      pl.BlockSpec((1, SC_WINDOW), lambda i: (0, i)),
                      pl.BlockSpec((1, SC_WINDOW), lambda i: (0, i))],
            out_specs=[], core_axis_name=("core", "subcore"),
            dimension_semantics=(pltpu.PARALLEL,))(x_hbm, i0_hbm, i1_hbm)

    return scatter(x, idx0, idx1)


def _sc_gather_rows(y, idx):
    m = idx.shape[1]
    w = y.shape[1]

    @pl.kernel(out_type=jax.ShapeDtypeStruct((m, w), y.dtype), mesh=_sc_mesh(), scratch_types=[])
    def gather(y_hbm, i_hbm, o_hbm):
        def body(i_vmem, o_vmem):
            pltpu.sync_copy(y_hbm.at[i_vmem.at[0]], o_vmem)

        pltpu.emit_pipeline(
            body, grid=(m // SC_WINDOW,),
            in_specs=[pl.BlockSpec((1, SC_WINDOW), lambda i: (0, i))],
            out_specs=[pl.BlockSpec((SC_WINDOW, w), lambda i: (i, 0))],
            core_axis_name=("core", "subcore"),
            dimension_semantics=(pltpu.PARALLEL,))(i_hbm, o_hbm)

    return gather(y, idx)


def _join_rows(wa, wb):
    lo_a, hi_a = _unpack_pairs(wa)
    lo_b, hi_b = _unpack_pairs(wb)
    return jnp.concatenate([lo_a, lo_b, hi_a, hi_b], axis=1)


def _expert_kernel(te_ref, nx_ref, sl_ref, nv_ref, nu_ref, xa_ref, xb_ref, w1_hbm, w3_hbm, w2_hbm,
                   ya_ref, yb_ref, stage1, stage3, stage2, w1_ref, w3_ref, w2_ref, sem):
    j = pl.program_id(0)

    def fetch(e, slot):
        return [pltpu.make_async_copy(src.at[e], dst.at[slot], sem.at[k, slot]) for k, (src, dst) in
                enumerate(((w1_hbm, stage1), (w3_hbm, stage3), (w2_hbm, stage2)))]

    @pl.when(j == 0)
    def _():
        for cp in fetch(te_ref[0], sl_ref[0]):
            cp.start()

    @pl.when((j < nu_ref[0]) & (nx_ref[j] > -2))
    def _():
        slot = sl_ref[j]

        @pl.when(nx_ref[j] >= 0)
        def _():
            for cp in fetch(nx_ref[j], 1 - slot):
                cp.start()

        for cp in fetch(te_ref[j], slot):
            cp.wait()
        w1_ref[...] = stage1[slot].astype(BF16)
        w3_ref[...] = stage3[slot].astype(BF16)
        w2_ref[...] = stage2[slot].astype(BF16)

    @pl.when(j < nu_ref[0])
    def _():
        live = lax.broadcasted_iota(jnp.int32, xa_ref.shape, 0) < nv_ref[j]
        zero = jnp.zeros(xa_ref.shape, U32)
        x = _join_rows(jnp.where(live, xa_ref[...], zero),
                       jnp.where(live, xb_ref[...], zero)).astype(BF16)
        h1 = _dot(x, w1_ref[...])
        h3 = _dot(x, w3_ref[...])
        hdn = (h1 * _sigmoid(h1) * h3).astype(BF16)
        words = _pack_pairs(_dot(hdn, w2_ref[...]))
        ya_ref[...] = words[:, :ROW_WORDS]
        yb_ref[...] = words[:, ROW_WORDS:]


def _expert_call(tile_expert, tile_next, tile_slot, tile_valid, n_used, xa, xb, w1, w3, w2):
    rows, rw = xa.shape
    tm = EXPERT_TILE
    d, ff = w1.shape[1], w1.shape[2]
    row_blk = lambda j, te, nx, sl, nv, nu: (jnp.minimum(j, nu[0] - 1), 0)
    hbm = pl.BlockSpec(memory_space=pl.ANY)
    grid_spec = pltpu.PrefetchScalarGridSpec(
        num_scalar_prefetch=5, grid=(rows // tm,),
        in_specs=[pl.BlockSpec((tm, rw), row_blk), pl.BlockSpec((tm, rw), row_blk), hbm, hbm, hbm],
        out_specs=[pl.BlockSpec((tm, rw), row_blk), pl.BlockSpec((tm, rw), row_blk)],
        scratch_shapes=[pltpu.VMEM((2, d, ff), w1.dtype), pltpu.VMEM((2, d, ff), w3.dtype),
                        pltpu.VMEM((2, ff, d), w2.dtype),
                        pltpu.VMEM((d, ff), BF16), pltpu.VMEM((d, ff), BF16), pltpu.VMEM((ff, d), BF16),
                        pltpu.SemaphoreType.DMA((3, 2))])
    return pl.pallas_call(
        _expert_kernel, grid_spec=grid_spec,
        out_shape=[jax.ShapeDtypeStruct((rows, rw), U32), jax.ShapeDtypeStruct((rows, rw), U32)],
        compiler_params=_cparams(("arbitrary",)), name="moe_experts")(
            tile_expert, tile_next, tile_slot, tile_valid, n_used, xa, xb, w1, w3, w2)


def _combine_kernel(x2_ref, info_ref, gf_ref, a0_ref, b0_ref, a1_ref, b1_ref, y_ref):
    info = info_ref[...]
    moe = (info[:, 2:3] * _join_rows(a0_ref[...], b0_ref[...])
           + info[:, 3:4] * _join_rows(a1_ref[...], b1_ref[...]))
    y_ref[...] = _rms_rows(x2_ref[...] + moe, gf_ref[...])


def _combine_call(x2, info, gf, ga, gb):
    n, d = x2.shape
    tile = TOKEN_TILE
    nt = n // tile
    first = pl.BlockSpec((tile, ROW_WORDS), lambda i: (i, 0))
    second = pl.BlockSpec((tile, ROW_WORDS), lambda i: (i + nt, 0))
    return pl.pallas_call(
        _combine_kernel, grid=(nt,),
        in_specs=[pl.BlockSpec((tile, d), lambda i: (i, 0)),
                  pl.BlockSpec((tile, LANES), lambda i: (i, 0)),
                  pl.BlockSpec(gf.shape, lambda i: (0, 0)),
                  first, first, second, second],
        out_specs=pl.BlockSpec((tile, d), lambda i: (i, 0)),
        out_shape=jax.ShapeDtypeStruct((n, d), F32),
        compiler_params=_cparams(("arbitrary",)), name="moe_combine")(
            x2, info, gf, ga, gb, ga, gb)


def _row(a):
    return a.reshape(1, -1).astype(F32)


def _prep_weights(norm1_g, w_in, conv_a_w, conv_a_norm_g, dn_conv_w, dn_a_log, dn_dt_bias, w_o,
                  norm2_g, rg_w, rg_b, re_w, re_b):
    row = _row
    n_ab = len(_GB_COLS)
    w_main = jnp.concatenate([w_in[:, :7 * CONV_WIDTH], w_in[:, 7 * CONV_WIDTH + _GB_COLS]], axis=1)
    w_main = jnp.pad(w_main, ((0, 0), (0, LANES - n_ab))).astype(BF16)
    cw = jnp.concatenate([conv_a_w, dn_conv_w], axis=1).astype(F32)
    grp = jnp.arange(CONV_WIDTH) // (CONV_WIDTH // CONV_GROUPS)
    gmat = ((grp[:, None] == grp[None, :]).astype(F32) / (CONV_WIDTH // CONV_GROUPS)).astype(BF16)
    is_a = (_GB_COLS < 2 * DN_HEADS)
    a_idx = jnp.where(is_a, _GB_COLS, 0)
    padl = lambda a: jnp.pad(a.reshape(1, -1).astype(F32), ((0, 0), (0, LANES - n_ab)))
    nea = padl(jnp.where(is_a, -jnp.exp(dn_a_log.astype(F32)).reshape(-1)[a_idx], 0.0))
    dtb = padl(jnp.where(is_a, dn_dt_bias.astype(F32).reshape(-1)[a_idx], 0.0))
    proj_w = [row(norm1_g), w_main, cw, row(conv_a_norm_g), gmat, nea, dtb]
    rw = jnp.pad(jnp.concatenate([rg_w, re_w], axis=1).astype(F32),
                 ((0, 0), (0, LANES - N_GROUPS - N_EXPERTS)))
    rb = jnp.pad(jnp.concatenate([rg_b, re_b]).reshape(1, -1).astype(F32),
                 ((0, 0), (0, LANES - N_GROUPS - N_EXPERTS)))
    tri = (jnp.arange(ROUTE_SUB)[:, None] > jnp.arange(ROUTE_SUB)[None, :]).astype(BF16)
    route_w = [w_o.astype(BF16), row(norm2_g), rw.astype(BF16), rb, tri]
    return proj_w, route_w


def _mix_and_route(x, seq_len, proj_w, dn_norm_g, route_w):
    ya, q, k, v, z, gb = _proj_call(x, seq_len, proj_w)
    o_f = _dn_call(q, k, v, gb, None, False, seq_len)
    yb = _dn_call(q, k, v, gb, (o_f, z, _row(dn_norm_g)), True, seq_len)
    return _route_call(x, ya, yb, route_w)


def _moe(routed, w1, w3, w2, final_g):
    x2, xna, xnb, info, cnt = routed
    n = x2.shape[0]
    tm = EXPERT_TILE
    counts = cnt[0, :N_EXPERTS].astype(jnp.int32)
    pcounts = (counts + tm - 1) // tm * tm
    pends = jnp.cumsum(pcounts)
    pstarts = pends - pcounts
    eid = jnp.arange(N_EXPERTS, dtype=jnp.int32)
    e12 = info[:, 0:2].astype(jnp.int32)
    start12 = jnp.sum(jnp.where(e12[:, :, None] == eid, pstarts, 0), axis=-1)
    dest = start12 + info[:, 4:6].astype(jnp.int32)
    dest0 = dest[:, 0].reshape(1, n)
    dest1 = dest[:, 1].reshape(1, n)
    n_tiles = 2 * n // tm + N_EXPERTS
    tile_start = jnp.arange(n_tiles, dtype=jnp.int32) * tm
    tile_expert = jnp.minimum(jnp.sum(pends[None, :] <= tile_start[:, None], axis=1),
                              N_EXPERTS - 1).astype(jnp.int32)
    of_tile = lambda table: jnp.sum(jnp.where(tile_expert[:, None] == eid, table, 0), axis=-1)
    tile_valid = jnp.clip(of_tile(pstarts + counts) - tile_start, 0, tm).astype(jnp.int32)
    n_used = (pends[-1:] // tm).astype(jnp.int32)
    later = (eid[None, :] > eid[:, None]) & (pcounts[None, :] > 0)
    next_e = jnp.min(jnp.where(later, eid[None, :], N_EXPERTS), axis=1)
    next_e = jnp.where(next_e == N_EXPERTS, -1, next_e)
    tile_next = jnp.where(tile_start == of_tile(pstarts), of_tile(next_e), -2).astype(jnp.int32)
    tile_slot = (of_tile(jnp.cumsum(pcounts > 0) - 1) % 2).astype(jnp.int32)

    rows = n_tiles * tm
    xa = _sc_scatter_rows(xna, dest0, dest1, rows)
    xb = _sc_scatter_rows(xnb, dest0, dest1, rows)
    ya_e, yb_e = _expert_call(tile_expert, tile_next, tile_slot, tile_valid, n_used, xa, xb,
                              w1, w3, w2)
    both = jnp.concatenate([dest0, dest1], axis=1)
    return _combine_call(x2, info, _row(final_g), _sc_gather_rows(ya_e, both),
                         _sc_gather_rows(yb_e, both))


def kernel(x_prompt, x_sample, norm1_g, w_in, conv_a_w, conv_a_norm_g, dn_conv_w, dn_a_log,
           dn_dt_bias, dn_norm_g, w_o, norm2_g, router_group_w, router_group_b, router_expert_w,
           router_expert_b, w1, w3, w2, final_norm_g):
    assert norm1_g.shape[0] == 1, "single-layer trunk"
    bp, sp, d = x_prompt.shape
    bs, ss, _ = x_sample.shape
    assert bp == 1 and sp % max(TOKEN_TILE, DN_BLOCK) == 0 and ss % max(TOKEN_TILE, DN_BLOCK) == 0
    proj_w, route_w = _prep_weights(
        norm1_g[0], w_in[0], conv_a_w[0], conv_a_norm_g[0], dn_conv_w[0], dn_a_log[0],
        dn_dt_bias[0], w_o[0], norm2_g[0], router_group_w[0], router_group_b[0],
        router_expert_w[0], router_expert_b[0])
    routed = [_mix_and_route(x.reshape(-1, d), seq_len, proj_w, dn_norm_g[0], route_w)
              for x, seq_len in ((x_prompt, sp), (x_sample, ss))]
    y_p, y_s = [_moe(r, w1[0], w3[0], w2[0], final_norm_g) for r in routed]
    return y_p.reshape(bp, sp, d), y_s.reshape(bs, ss, d)
```

```python
import functools

import jax
import jax.numpy as jnp
import numpy as np
from jax import lax
from jax.experimental import pallas as pl
from jax.experimental.pallas import tpu as pltpu
from jax.experimental.pallas import tpu_sc as plsc

F32 = jnp.float32
BF16 = jnp.bfloat16
U32 = jnp.uint32

EPS = 1e-6
CONV_WIDTH = 512
CONV_GROUPS = 8
DN_HEADS = 4
DN_DK = 128
DN_V = 512
CHUNK = 64
N_GROUPS = 4
EXPERTS_PER_GROUP = 8
N_EXPERTS = N_GROUPS * EXPERTS_PER_GROUP
EXPERT_FF = 512
LANES = 128
HALO = 8

TOKEN_TILE = 512
COMBINE_TILE = 1024
ROUTE_SUB = 256
DN_BLOCK = 1024
DN_SUB = 256
EXPERT_TILE = 512
ROW_WORDS = 256
SC_WINDOW = 128
VMEM_LIMIT = 56 * 1024 * 1024

_GB_COLS = np.concatenate([np.arange(4 * DN_HEADS), np.arange(2 * DN_HEADS), np.arange(2 * DN_HEADS)])


def _cparams(sem):
    return pltpu.CompilerParams(dimension_semantics=sem, vmem_limit_bytes=VMEM_LIMIT)


def _sigmoid(x):
    return 1.0 / (1.0 + jnp.exp(-x))


def _rms_rows(x, g):
    return x * lax.rsqrt(jnp.mean(x * x, axis=-1, keepdims=True) + EPS) * g


def _dot(a, b):
    return jnp.dot(a, b, preferred_element_type=F32)


def _pack_pairs(x):
    half = x.shape[1] // 2
    bits = lambda a: lax.bitcast_convert_type(a.astype(BF16).astype(F32), U32)
    return (bits(x[:, half:]) & jnp.uint32(0xFFFF0000)) | (bits(x[:, :half]) >> 16)


def _unpack_pairs(w):
    lo = lax.bitcast_convert_type(w << 16, F32)
    hi = lax.bitcast_convert_type(w & jnp.uint32(0xFFFF0000), F32)
    return lo, hi


def _proj_kernel(x_ref, xprev_ref, xnext_ref,
                 g1_ref, w_ref, cw_ref, cng_ref, gmat_ref, nea_ref, dtb_ref,
                 ya_ref, q_ref, k_ref, v_ref, z_ref, gb_ref,
                 hs_ref, su_ref, p_ref, *, tile, seq_len):
    tok0 = pl.program_id(0) * tile
    not_start = (tok0 % seq_len != 0).astype(F32)
    not_end = ((tok0 + tile) % seq_len != 0).astype(F32)
    g1 = g1_ref[...]
    hs_ref[0:HALO, :] = _rms_rows(xprev_ref[...], g1)
    hs_ref[HALO:HALO + tile, :] = _rms_rows(x_ref[...], g1)
    hs_ref[HALO + tile:, :] = _rms_rows(xnext_ref[...], g1)

    p_ref[...] = _dot(hs_ref[...].astype(BF16), w_ref[...])
    for stage in _FINISH_STAGES:
        stage(p_ref, su_ref, cw_ref, cng_ref, gmat_ref, nea_ref, dtb_ref,
              ya_ref, q_ref, k_ref, v_ref, z_ref, gb_ref, tile, not_start, not_end)


def _conv_cols(p_ref, su_ref, cw_ref, lo, width, tile, not_start, not_end):
    cw = CONV_WIDTH
    cols = slice(lo, lo + width)
    if lo < cw:
        su_ref[:, cols] = p_ref[:, cw + lo:cw + lo + width] * p_ref[:, 2 * cw + lo:2 * cw + lo + width]
    else:
        su_ref[:, cols] = p_ref[:, 2 * cw + lo:2 * cw + lo + width]
    su_ref[0:HALO, cols] = su_ref[0:HALO, cols] * not_start
    su_ref[HALO + tile:, cols] = su_ref[HALO + tile:, cols] * not_end
    u = su_ref[:, cols]
    total = tile + 2 * HALO
    return (pltpu.roll(u, 1, axis=0)[HALO:HALO + tile] * cw_ref[0:1, cols]
            + u[HALO:HALO + tile] * cw_ref[1:2, cols]
            + pltpu.roll(u, total - 1, axis=0)[HALO:HALO + tile] * cw_ref[2:3, cols])


def _stage_mixer_a(half):
    width = CONV_WIDTH // 2
    lo = half * width

    def stage(p_ref, su_ref, cw_ref, cng_ref, gmat_ref, nea_ref, dtb_ref,
              ya_ref, q_ref, k_ref, v_ref, z_ref, gb_ref, tile, not_start, not_end):
        conv = _conv_cols(p_ref, su_ref, cw_ref, lo, width, tile, not_start, not_end)
        ya = p_ref[HALO:HALO + tile, lo:lo + width] * conv
        ms = _dot((ya * ya).astype(BF16), gmat_ref[lo:lo + width, lo:lo + width])
        ya_ref[:, lo:lo + width] = (ya * lax.rsqrt(ms + EPS) * cng_ref[:, lo:lo + width]).astype(ya_ref.dtype)
    return stage


def _stage_mixer_b(which, half):
    width = DN_V // 2
    lo_out = half * width
    lo = CONV_WIDTH + which * DN_V + lo_out

    def stage(p_ref, su_ref, cw_ref, cng_ref, gmat_ref, nea_ref, dtb_ref,
              ya_ref, q_ref, k_ref, v_ref, z_ref, gb_ref, tile, not_start, not_end):
        x = _conv_cols(p_ref, su_ref, cw_ref, lo, width, tile, not_start, not_end)
        x = x * _sigmoid(x)
        out_ref = (q_ref, k_ref, v_ref)[which]
        if which == 2:
            out_ref[:, lo_out:lo_out + width] = x.astype(out_ref.dtype)
            return
        scale = DN_DK ** -0.5 if which == 0 else 1.0
        for h in range(width // DN_DK):
            xh = x[:, h * DN_DK:(h + 1) * DN_DK]
            xh = xh * (lax.rsqrt(jnp.sum(xh * xh, axis=-1, keepdims=True) + EPS) * scale)
            out_ref[:, lo_out + h * DN_DK:lo_out + (h + 1) * DN_DK] = xh.astype(out_ref.dtype)
    return stage


def _stage_gates(inner):
    def stage(p_ref, su_ref, cw_ref, cng_ref, gmat_ref, nea_ref, dtb_ref,
              ya_ref, q_ref, k_ref, v_ref, z_ref, gb_ref, tile, not_start, not_end):
        inner(p_ref, su_ref, cw_ref, cng_ref, gmat_ref, nea_ref, dtb_ref,
              ya_ref, q_ref, k_ref, v_ref, z_ref, gb_ref, tile, not_start, not_end)
        cw = CONV_WIDTH
        z_ref[...] = p_ref[HALO:HALO + tile, 6 * cw:7 * cw].astype(z_ref.dtype)
        ab = p_ref[HALO:HALO + tile, 7 * cw:7 * cw + LANES]
        xs = ab + dtb_ref[...]
        softplus = jnp.maximum(xs, 0.0) + jnp.log(1.0 + jnp.exp(-jnp.abs(xs)))
        g = nea_ref[...] * softplus
        beta = _sigmoid(ab)
        r = lax.broadcasted_iota(jnp.int32, (tile, LANES), 0) % CHUNK
        pre = g
        suf = g
        s = 1
        while s < CHUNK:
            pre = pre + jnp.where(r >= s, pltpu.roll(pre, s, axis=0), 0.0)
            suf = suf + jnp.where(r < CHUNK - s, pltpu.roll(suf, tile - s, axis=0), 0.0)
            s *= 2
        lane = lax.broadcasted_iota(jnp.int32, (tile, LANES), 1) // DN_HEADS
        gb_ref[...] = jnp.where(lane == 0, pre,
                      jnp.where(lane == 1, suf,
                      jnp.where(lane < 4, beta,
                      jnp.where(lane == 4, suf - g,
                      jnp.where(lane == 5, pre - g, pre + suf - g)))))
    return stage


_FINISH_STAGES = (_stage_mixer_a(0), _stage_mixer_a(1), _stage_mixer_b(0, 0), _stage_mixer_b(0, 1),
                  _stage_mixer_b(1, 0), _stage_mixer_b(1, 1), _stage_mixer_b(2, 0),
                  _stage_gates(_stage_mixer_b(2, 1)))


def _proj_call(x, seq_len, wts):
    n, d = x.shape
    tile = TOKEN_TILE
    nt = n // tile
    tb = tile // HALO
    kern = functools.partial(_proj_kernel, tile=tile, seq_len=seq_len)
    const = lambda i: (0, 0)
    in_specs = [pl.BlockSpec((tile, d), lambda i: (i, 0)),
                pl.BlockSpec((HALO, d), lambda i: (jnp.maximum(i * tb - 1, 0), 0)),
                pl.BlockSpec((HALO, d), lambda i: (jnp.minimum((i + 1) * tb, n // HALO - 1), 0))]
    in_specs += [pl.BlockSpec(w.shape, const, pipeline_mode=pl.Buffered(1)) for w in wts]
    widths = (512, 512, 512, 512, 512, LANES)
    dtypes = (BF16, BF16, BF16, BF16, BF16, F32)
    rows = tile + 2 * HALO
    return pl.pallas_call(
        kern, grid=(nt,), in_specs=in_specs,
        out_specs=[pl.BlockSpec((tile, w), lambda i: (i, 0)) for w in widths],
        out_shape=[jax.ShapeDtypeStruct((n, w), dt) for w, dt in zip(widths, dtypes)],
        scratch_shapes=[pltpu.VMEM((rows, d), F32), pltpu.VMEM((rows, 4 * CONV_WIDTH), F32),
                        pltpu.VMEM((rows, wts[1].shape[1]), F32)],
        compiler_params=_cparams(("arbitrary",)), name="proj_conv")(x, x, x, *wts)


def _dn_kernel(*refs, block, sub, reverse, seq_len, nblk):
    if reverse:
        (q_ref, k_ref, v_ref, gb_ref, of_ref, z_ref, ng_ref, o_ref, s_ref) = refs
    else:
        (q_ref, k_ref, v_ref, gb_ref, o_ref, s_ref) = refs
    i = pl.program_id(0)
    b = nblk - 1 - i if reverse else i
    edge = (b + 1) * block if reverse else b * block

    @pl.when(edge % seq_len == 0)
    def _():
        s_ref[...] = jnp.zeros_like(s_ref)

    nchunk = sub // CHUNK
    c = CHUNK
    heads = range(DN_HEADS)
    ri = lax.broadcasted_iota(jnp.int32, (sub, sub), 0)
    ci = lax.broadcasted_iota(jnp.int32, (sub, sub), 1)
    same = (ri // c) == (ci // c)
    incl = same & ((ri <= ci) if reverse else (ri >= ci))
    strict = same & ((ri < ci) if reverse else (ri > ci))
    pr = lax.broadcasted_iota(jnp.int32, (c, sub), 0)
    pc = lax.broadcasted_iota(jnp.int32, (c, sub), 1)
    eye_p = (pr == pc % c).astype(F32)
    lane_chunk = pc // c

    def fold(bd):
        out = bd[0:c]
        for r in range(1, nchunk):
            out = out + bd[r * c:(r + 1) * c]
        return out

    def unfold(p):
        zero = jnp.zeros_like(p)
        return jnp.concatenate([jnp.where(lane_chunk == r, p, zero) for r in range(nchunk)], axis=0)

    grp = lambda g, h: slice(g * DN_HEADS + h, g * DN_HEADS + h + 1)
    g_cum, g_beta, g_rest, g_tot = (1, 3, 5, 7) if reverse else (0, 2, 4, 6)
    sls = [slice(h * DN_DK, (h + 1) * DN_DK) for h in heads]

    def prepare(r0, out):
        rows = slice(r0, r0 + sub)
        gbv = gb_ref[rows, :]
        gbt = gbv.T
        kb = [k_ref[rows, sl] for sl in sls]
        qb = [q_ref[rows, sl] for sl in sls]
        gcol = [gbv[:, grp(g_cum, h)] for h in heads]
        beta = [gbv[:, grp(g_beta, h)] for h in heads]
        kq = [lax.dot_general(jnp.concatenate([kb[h], qb[h]], axis=0), kb[h],
                              (((1,), (1,)), ((), ())), preferred_element_type=F32) for h in heads]
        yield
        decay = [jnp.exp(jnp.minimum(gcol[h] - gbt[grp(g_cum, h), :], 0.0)) for h in heads]
        bdl = [jnp.where(strict, kq[h][:sub] * decay[h], 0.0) * beta[h] for h in heads]
        bda = [jnp.where(incl, kq[h][sub:] * decay[h], 0.0).astype(BF16) for h in heads]
        yield
        lp = [fold(bdl[h]) for h in heads]
        xp = [eye_p - lp[h] for h in heads]
        bd = [bdl[h].astype(BF16) for h in heads]
        p = 1
        while p < c:
            if p == 1:
                lp = [_dot(lp[h].astype(BF16), bd[h]) for h in heads]
            elif 2 * p < c:
                res = [_dot(jnp.concatenate([lp[h], xp[h]], axis=0).astype(BF16), bd[h]) for h in heads]
                lp = [r[:c] for r in res]
                xp = [xp[h] + res[h][c:] for h in heads]
            else:
                xp = [xp[h] + _dot(xp[h].astype(BF16), bd[h]) for h in heads]
            p *= 2
            if p < c:
                bd = [unfold(lp[h].astype(BF16)) for h in heads]
            yield
        bdt = [unfold(xp[h].astype(BF16)) for h in heads]
        egc = [jnp.exp(gcol[h]) for h in heads]
        kf = [kb[h].astype(F32) for h in heads]
        rhs = [jnp.concatenate([kf[h] * (beta[h] * egc[h]), v_ref[rows, sls[h]].astype(F32) * beta[h]],
                               axis=1).astype(BF16) for h in heads]
        wu = [_dot(bdt[h], rhs[h]) for h in heads]
        yield
        au = [_dot(bda[h], wu[h].astype(BF16)) for h in heads]
        qt = [qb[h].astype(F32) * egc[h] - au[h][:, :DN_DK] for h in heads]
        out["kd"] = [(kf[h] * jnp.exp(gbv[:, grp(g_rest, h)])).astype(BF16) for h in heads]
        out["wqt"] = [jnp.concatenate([wu[h][:, :DN_DK], qt[h]], axis=1).astype(BF16) for h in heads]
        out["u"] = [wu[h][:, DN_DK:] for h in heads]
        out["o0"] = [au[h][:, DN_DK:] for h in heads]
        out["gbt"] = gbt
        yield

    def scan(r0, pre, state):
        order = range(nchunk - 1, -1, -1) if reverse else range(nchunk)
        for cc in order:
            rows = slice(cc * c, (cc + 1) * c)
            out_rows = slice(r0 + cc * c, r0 + (cc + 1) * c)
            for h in heads:
                wqt = pre["wqt"][h]
                lhs = jnp.concatenate([wqt[rows, :DN_DK], wqt[rows, DN_DK:]], axis=0)
                wq = _dot(lhs, state[h].astype(BF16))
                vn = pre["u"][h][rows] - wq[:c]
                o = wq[c:] + pre["o0"][h][rows]
                tot = pre["gbt"][grp(g_tot, h), cc * c:cc * c + 1]
                state[h] = state[h] * jnp.exp(tot) + lax.dot_general(
                    pre["kd"][h][rows], vn.astype(BF16), (((0,), (0,)), ((), ())),
                    preferred_element_type=F32)
                if reverse:
                    o = o + of_ref[out_rows, sls[h]]
                    o = o * lax.rsqrt(jnp.mean(o * o, axis=-1, keepdims=True) + EPS) * ng_ref[...]
                    zc = z_ref[out_rows, sls[h]].astype(F32)
                    o = o * (zc * _sigmoid(zc))
                o_ref[out_rows, sls[h]] = o.astype(o_ref.dtype)
            yield

    nsub = block // sub
    starts = [r * sub for r in (range(nsub - 1, -1, -1) if reverse else range(nsub))]
    state = [s_ref[h] for h in heads]
    pre_prev = None
    for idx in range(nsub + 1):
        pre = {}
        prep_gen = prepare(starts[idx], pre) if idx < nsub else iter(())
        scan_gen = scan(starts[idx - 1], pre_prev, state) if idx > 0 else iter(())
        prep_live = scan_live = True
        while prep_live or scan_live:
            if prep_live:
                prep_live = next(prep_gen, _DONE) is not _DONE
                if prep_live:
                    prep_live = next(prep_gen, _DONE) is not _DONE
            if scan_live:
                scan_live = next(scan_gen, _DONE) is not _DONE
        pre_prev = pre
    for h in heads:
        s_ref[h] = state[h]


_DONE = object()


def _dn_call(q, k, v, gb, extra, reverse, seq_len):
    n = q.shape[0]
    block = DN_BLOCK
    nblk = n // block
    imap = (lambda i: (nblk - 1 - i, 0)) if reverse else (lambda i: (i, 0))
    tok = lambda w: pl.BlockSpec((block, w), imap)
    in_specs = [tok(512), tok(512), tok(512), tok(LANES)]
    args = [q, k, v, gb]
    if reverse:
        o_f, z, ng = extra
        in_specs += [tok(512), tok(512), pl.BlockSpec(ng.shape, lambda i: (0, 0))]
        args += [o_f, z, ng]
    kern = functools.partial(_dn_kernel, block=block, sub=DN_SUB, reverse=reverse,
                             seq_len=seq_len, nblk=nblk)
    return pl.pallas_call(
        kern, grid=(nblk,), in_specs=in_specs, out_specs=tok(512),
        out_shape=jax.ShapeDtypeStruct((n, 512), BF16 if reverse else F32),
        scratch_shapes=[pltpu.VMEM((DN_HEADS, DN_DK, DN_DK), F32)],
        compiler_params=_cparams(("arbitrary",)),
        name="deltanet_bwd" if reverse else "deltanet_fwd")(*args)


def _route_kernel(x_ref, ya_ref, yb_ref, wo_ref, g2_ref, rw_ref, rb_ref, tri_ref,
                  x2_ref, xa_ref, xb_ref, info_ref, cnt_ref, run_ref, *, tile):
    i = pl.program_id(0)

    @pl.when(i == 0)
    def _():
        run_ref[...] = jnp.zeros_like(run_ref)

    sub = tri_ref.shape[0]
    parts = [slice(r, r + sub) for r in range(0, tile, sub)]
    each = lambda f, *lists: [f(*a) for a in zip(*lists)]
    half = wo_ref.shape[0] // 2
    x2 = [x_ref[rows, :] + _dot(ya_ref[rows, :], wo_ref[0:half, :])
          + _dot(yb_ref[rows, :], wo_ref[half:, :]) for rows in parts]
    for rows, v in zip(parts, x2):
        x2_ref[rows, :] = v.astype(x2_ref.dtype)
    xn = [_rms_rows(v, g2_ref[...]) for v in x2]
    for rows, v in zip(parts, xn):
        words = _pack_pairs(v)
        xa_ref[rows, :] = words[:, :ROW_WORDS]
        xb_ref[rows, :] = words[:, ROW_WORDS:]
    logits = [_dot(v.astype(BF16), rw_ref[...]) + rb_ref[...] for v in xn]
    lane = lax.broadcasted_iota(jnp.int32, (sub, LANES), 1).astype(F32)
    neg = jnp.float32(-jnp.inf)
    row_max = lambda v: jnp.max(v, axis=-1, keepdims=True)
    first_at = lambda v, m: jnp.min(jnp.where(v == m, lane, float(LANES)), axis=-1, keepdims=True)

    gl = [jnp.where(lane < N_GROUPS, v, neg) for v in logits]
    gmax = each(row_max, gl)
    gidx = each(first_at, gl, gmax)
    g_w = [1.0 / jnp.sum(jnp.exp(v - m), axis=-1, keepdims=True) for v, m in zip(gl, gmax)]
    lo = [N_GROUPS + g * EXPERTS_PER_GROUP for g in gidx]
    el = [jnp.where((lane >= a) & (lane < a + EXPERTS_PER_GROUP), v, neg) for v, a in zip(logits, lo)]
    m1 = each(row_max, el)
    i1 = each(first_at, el, m1)
    el2 = [jnp.where(lane == a, neg, v) for v, a in zip(el, i1)]
    m2 = each(row_max, el2)
    i2 = each(first_at, el2, m2)
    hit1 = [lane == a - N_GROUPS for a in i1]
    hit2 = [lane == a - N_GROUPS for a in i2]
    onehot = [(a | b).astype(BF16) for a, b in zip(hit1, hit2)]
    inside = [_dot(tri_ref[...], v) for v in onehot]
    run = run_ref[...]
    for k, rows in enumerate(parts):
        before = inside[k] + run
        pos1 = jnp.sum(jnp.where(hit1[k], before, 0.0), axis=-1, keepdims=True)
        pos2 = jnp.sum(jnp.where(hit2[k], before, 0.0), axis=-1, keepdims=True)
        run = run + jnp.sum(onehot[k].astype(F32), axis=0, keepdims=True)
        r = jnp.exp(m2[k] - m1[k])
        gate1 = g_w[k] / (1.0 + r)
        info_ref[rows, :] = jnp.where(lane == 0, i1[k] - N_GROUPS,
                            jnp.where(lane == 1, i2[k] - N_GROUPS,
                            jnp.where(lane == 2, gate1,
                            jnp.where(lane == 3, gate1 * r,
                            jnp.where(lane == 4, pos1,
                            jnp.where(lane == 5, pos2, 0.0))))))
    run_ref[...] = run
    cnt_ref[...] = run


def _route_call(x, ya, yb, wts):
    n, d = x.shape
    tile = TOKEN_TILE
    const = lambda i: (0, 0)
    tok = lambda w: pl.BlockSpec((tile, w), lambda i: (i, 0))
    in_specs = [tok(d), tok(512), tok(512)] + [pl.BlockSpec(w.shape, const) for w in wts]
    out_shape = [jax.ShapeDtypeStruct((n, d), BF16), jax.ShapeDtypeStruct((n, ROW_WORDS), U32),
                 jax.ShapeDtypeStruct((n, ROW_WORDS), U32),
                 jax.ShapeDtypeStruct((n, LANES), F32), jax.ShapeDtypeStruct((1, LANES), F32)]
    out_specs = [tok(d), tok(ROW_WORDS), tok(ROW_WORDS), tok(LANES), pl.BlockSpec((1, LANES), const)]
    kern = functools.partial(_route_kernel, tile=tile)
    return pl.pallas_call(
        kern, grid=(n // tile,), in_specs=in_specs, out_specs=out_specs, out_shape=out_shape,
        scratch_shapes=[pltpu.VMEM((1, LANES), F32)],
        compiler_params=_cparams(("arbitrary",)), name="oproj_router")(x, ya, yb, *wts)


def _sc_mesh():
    return plsc.VectorSubcoreMesh(core_axis_name="core", subcore_axis_name="subcore")


def _sc_scatter_rows(x, idx0, idx1, rows):
    n, w = x.shape

    @pl.kernel(out_type=jax.ShapeDtypeStruct((rows, w), x.dtype), mesh=_sc_mesh(), scratch_types=[])
    def scatter(x_hbm, i0_hbm, i1_hbm, o_hbm):
        def body(x_vmem, i0_vmem, i1_vmem):
            pltpu.sync_copy(x_vmem, o_hbm.at[i0_vmem.at[0]])
            pltpu.sync_copy(x_vmem, o_hbm.at[i1_vmem.at[0]])

        pltpu.emit_pipeline(
            body, grid=(n // SC_WINDOW,),
            in_specs=[pl.BlockSpec((SC_WINDOW, w), lambda i: (i, 0)),
                      pl.BlockSpec((1, SC_WINDOW), lambda i: (0, i)),
                      pl.BlockSpec((1, SC_WINDOW), lambda i: (0, i))],
            out_specs=[], core_axis_name=("core", "subcore"),
            dimension_semantics=(pltpu.PARALLEL,))(x_hbm, i0_hbm, i1_hbm)

    return scatter(x, idx0, idx1)


def _sc_gather_rows(y, idx):
    m = idx.shape[1]
    w = y.shape[1]

    @pl.kernel(out_type=jax.ShapeDtypeStruct((m, w), y.dtype), mesh=_sc_mesh(), scratch_types=[])
    def gather(y_hbm, i_hbm, o_hbm):
        def body(i_vmem, o_vmem):
            pltpu.sync_copy(y_hbm.at[i_vmem.at[0]], o_vmem)

        pltpu.emit_pipeline(
            body, grid=(m // SC_WINDOW,),
            in_specs=[pl.BlockSpec((1, SC_WINDOW), lambda i: (0, i))],
            out_specs=[pl.BlockSpec((SC_WINDOW, w), lambda i: (i, 0))],
            core_axis_name=("core", "subcore"),
            dimension_semantics=(pltpu.PARALLEL,))(i_hbm, o_hbm)

    return gather(y, idx)


def _join_rows(wa, wb):
    lo_a, hi_a = _unpack_pairs(wa)
    lo_b, hi_b = _unpack_pairs(wb)
    return jnp.concatenate([lo_a, lo_b, hi_a, hi_b], axis=1)


def _expert_kernel(te_ref, nx_ref, nv_ref, nu_ref, xa_ref, xb_ref, w1_hbm, w3_hbm, w2_hbm,
                   ya_ref, yb_ref, stage1, stage3, stage2, w1_ref, w3_ref, w2_ref, sem):
    j = pl.program_id(0)

    def fetch(e):
        return [pltpu.make_async_copy(src.at[e], dst, sem.at[k]) for k, (src, dst) in
                enumerate(((w1_hbm, stage1), (w3_hbm, stage3), (w2_hbm, stage2)))]

    @pl.when(j == 0)
    def _():
        for cp in fetch(te_ref[0]):
            cp.start()

    @pl.when((j < nu_ref[0]) & (nx_ref[j] > -2))
    def _():
        for cp in fetch(te_ref[j]):
            cp.wait()
        w1_ref[...] = stage1[...].astype(BF16)
        w3_ref[...] = stage3[...].astype(BF16)
        w2_ref[...] = stage2[...].astype(BF16)

        @pl.when(nx_ref[j] >= 0)
        def _():
            for cp in fetch(nx_ref[j]):
                cp.start()

    @pl.when(j < nu_ref[0])
    def _():
        live = lax.broadcasted_iota(jnp.int32, xa_ref.shape, 0) < nv_ref[j]
        zero = jnp.zeros(xa_ref.shape, U32)
        x = _join_rows(jnp.where(live, xa_ref[...], zero),
                       jnp.where(live, xb_ref[...], zero)).astype(BF16)
        h1 = _dot(x, w1_ref[...])
        h3 = _dot(x, w3_ref[...])
        hdn = (h1 * _sigmoid(h1) * h3).astype(BF16)
        words = _pack_pairs(_dot(hdn, w2_ref[...]))
        ya_ref[...] = words[:, :ROW_WORDS]
        yb_ref[...] = words[:, ROW_WORDS:]


def _expert_call(tile_expert, tile_next, tile_valid, n_used, xa, xb, w1, w3, w2):
    rows, rw = xa.shape
    tm = EXPERT_TILE
    d, ff = w1.shape[1], w1.shape[2]
    row_blk = lambda j, te, nx, nv, nu: (jnp.minimum(j, nu[0] - 1), 0)
    hbm = pl.BlockSpec(memory_space=pl.ANY)
    grid_spec = pltpu.PrefetchScalarGridSpec(
        num_scalar_prefetch=4, grid=(rows // tm,),
        in_specs=[pl.BlockSpec((tm, rw), row_blk), pl.BlockSpec((tm, rw), row_blk), hbm, hbm, hbm],
        out_specs=[pl.BlockSpec((tm, rw), row_blk), pl.BlockSpec((tm, rw), row_blk)],
        scratch_shapes=[pltpu.VMEM((d, ff), w1.dtype), pltpu.VMEM((d, ff), w3.dtype),
                        pltpu.VMEM((ff, d), w2.dtype),
                        pltpu.VMEM((d, ff), BF16), pltpu.VMEM((d, ff), BF16), pltpu.VMEM((ff, d), BF16),
                        pltpu.SemaphoreType.DMA((3,))])
    return pl.pallas_call(
        _expert_kernel, grid_spec=grid_spec,
        out_shape=[jax.ShapeDtypeStruct((rows, rw), U32), jax.ShapeDtypeStruct((rows, rw), U32)],
        compiler_params=_cparams(("arbitrary",)), name="moe_experts")(
            tile_expert, tile_next, tile_valid, n_used, xa, xb, w1, w3, w2)


def _combine_kernel(x2_ref, info_ref, gf_ref, a0_ref, b0_ref, a1_ref, b1_ref, y_ref):
    info = info_ref[...]
    moe = (info[:, 2:3] * _join_rows(a0_ref[...], b0_ref[...])
           + info[:, 3:4] * _join_rows(a1_ref[...], b1_ref[...]))
    y_ref[...] = _rms_rows(x2_ref[...].astype(F32) + moe, gf_ref[...])


def _combine_call(x2, info, gf, ga, gb):
    n, d = x2.shape
    tile = COMBINE_TILE
    nt = n // tile
    first = pl.BlockSpec((tile, ROW_WORDS), lambda i: (i, 0))
    second = pl.BlockSpec((tile, ROW_WORDS), lambda i: (i + nt, 0))
    return pl.pallas_call(
        _combine_kernel, grid=(nt,),
        in_specs=[pl.BlockSpec((tile, d), lambda i: (i, 0)),
                  pl.BlockSpec((tile, LANES), lambda i: (i, 0)),
                  pl.BlockSpec(gf.shape, lambda i: (0, 0)),
                  first, first, second, second],
        out_specs=pl.BlockSpec((tile, d), lambda i: (i, 0)),
        out_shape=jax.ShapeDtypeStruct((n, d), F32),
        compiler_params=_cparams(("arbitrary",)), name="moe_combine")(
            x2, info, gf, ga, gb, ga, gb)


def _row(a):
    return a.reshape(1, -1).astype(F32)


def _prep_weights(norm1_g, w_in, conv_a_w, conv_a_norm_g, dn_conv_w, dn_a_log, dn_dt_bias, w_o,
                  norm2_g, rg_w, rg_b, re_w, re_b):
    row = _row
    n_ab = len(_GB_COLS)
    w_main = jnp.concatenate([w_in[:, :7 * CONV_WIDTH], w_in[:, 7 * CONV_WIDTH + _GB_COLS]], axis=1)
    w_main = jnp.pad(w_main, ((0, 0), (0, LANES - n_ab))).astype(BF16)
    cw = jnp.concatenate([conv_a_w, dn_conv_w], axis=1).astype(F32)
    grp = jnp.arange(CONV_WIDTH) // (CONV_WIDTH // CONV_GROUPS)
    gmat = ((grp[:, None] == grp[None, :]).astype(F32) / (CONV_WIDTH // CONV_GROUPS)).astype(BF16)
    is_a = (_GB_COLS < 2 * DN_HEADS)
    a_idx = jnp.where(is_a, _GB_COLS, 0)
    padl = lambda a: jnp.pad(a.reshape(1, -1).astype(F32), ((0, 0), (0, LANES - n_ab)))
    nea = padl(jnp.where(is_a, -jnp.exp(dn_a_log.astype(F32)).reshape(-1)[a_idx], 0.0))
    dtb = padl(jnp.where(is_a, dn_dt_bias.astype(F32).reshape(-1)[a_idx], 0.0))
    proj_w = [row(norm1_g), w_main, cw, row(conv_a_norm_g), gmat, nea, dtb]
    rw = jnp.pad(jnp.concatenate([rg_w, re_w], axis=1).astype(F32),
                 ((0, 0), (0, LANES - N_GROUPS - N_EXPERTS)))
    rb = jnp.pad(jnp.concatenate([rg_b, re_b]).reshape(1, -1).astype(F32),
                 ((0, 0), (0, LANES - N_GROUPS - N_EXPERTS)))
    tri = (jnp.arange(ROUTE_SUB)[:, None] > jnp.arange(ROUTE_SUB)[None, :]).astype(BF16)
    route_w = [w_o.astype(BF16), row(norm2_g), rw.astype(BF16), rb, tri]
    return proj_w, route_w


def _mix_and_route(x, seq_len, proj_w, dn_norm_g, route_w):
    ya, q, k, v, z, gb = _proj_call(x, seq_len, proj_w)
    o_f = _dn_call(q, k, v, gb, None, False, seq_len)
    yb = _dn_call(q, k, v, gb, (o_f, z, _row(dn_norm_g)), True, seq_len)
    return _route_call(x, ya, yb, route_w)


def _moe(routed, w1, w3, w2, final_g):
    x2, xna, xnb, info, cnt = routed
    n = x2.shape[0]
    tm = EXPERT_TILE
    counts = cnt[0, :N_EXPERTS].astype(jnp.int32)
    pcounts = (counts + tm - 1) // tm * tm
    pends = jnp.cumsum(pcounts)
    pstarts = pends - pcounts
    eid = jnp.arange(N_EXPERTS, dtype=jnp.int32)
    e12 = info[:, 0:2].astype(jnp.int32)
    start12 = jnp.sum(jnp.where(e12[:, :, None] == eid, pstarts, 0), axis=-1)
    dest = start12 + info[:, 4:6].astype(jnp.int32)
    dest0 = dest[:, 0].reshape(1, n)
    dest1 = dest[:, 1].reshape(1, n)
    n_tiles = 2 * n // tm + N_EXPERTS
    tile_start = jnp.arange(n_tiles, dtype=jnp.int32) * tm
    tile_expert = jnp.minimum(jnp.sum(pends[None, :] <= tile_start[:, None], axis=1),
                              N_EXPERTS - 1).astype(jnp.int32)
    of_tile = lambda table: jnp.sum(jnp.where(tile_expert[:, None] == eid, table, 0), axis=-1)
    tile_valid = jnp.clip(of_tile(pstarts + counts) - tile_start, 0, tm).astype(jnp.int32)
    n_used = (pends[-1:] // tm).astype(jnp.int32)
    later = (eid[None, :] > eid[:, None]) & (pcounts[None, :] > 0)
    next_e = jnp.min(jnp.where(later, eid[None, :], N_EXPERTS), axis=1)
    next_e = jnp.where(next_e == N_EXPERTS, -1, next_e)
    tile_next = jnp.where(tile_start == of_tile(pstarts), of_tile(next_e), -2).astype(jnp.int32)

    rows = n_tiles * tm
    xa = _sc_scatter_rows(xna, dest0, dest1, rows)
    xb = _sc_scatter_rows(xnb, dest0, dest1, rows)
    ya_e, yb_e = _expert_call(tile_expert, tile_next, tile_valid, n_used, xa, xb, w1, w3, w2)
    both = jnp.concatenate([dest0, dest1], axis=1)
    return _combine_call(x2, info, _row(final_g), _sc_gather_rows(ya_e, both),
                         _sc_gather_rows(yb_e, both))


def kernel(x_prompt, x_sample, norm1_g, w_in, conv_a_w, conv_a_norm_g, dn_conv_w, dn_a_log,
           dn_dt_bias, dn_norm_g, w_o, norm2_g, router_group_w, router_group_b, router_expert_w,
           router_expert_b, w1, w3, w2, final_norm_g):
    assert norm1_g.shape[0] == 1, "single-layer trunk"
    bp, sp, d = x_prompt.shape
    bs, ss, _ = x_sample.shape
    assert bp == 1 and sp % max(TOKEN_TILE, DN_BLOCK) == 0 and ss % max(TOKEN_TILE, DN_BLOCK) == 0
    proj_w, route_w = _prep_weights(
        norm1_g[0], w_in[0], conv_a_w[0], conv_a_norm_g[0], dn_conv_w[0], dn_a_log[0],
        dn_dt_bias[0], w_o[0], norm2_g[0], router_group_w[0], router_group_b[0],
        router_expert_w[0], router_expert_b[0])
    routed = [_mix_and_route(x.reshape(-1, d), seq_len, proj_w, dn_norm_g[0], route_w)
              for x, seq_len in ((x_prompt, sp), (x_sample, ss))]
    y_p, y_s = [_moe(r, w1[0], w3[0], w2[0], final_norm_g) for r in routed]
    return y_p.reshape(bp, sp, d), y_s.reshape(bs, ss, d)
```

```python
import functools

import jax
import jax.numpy as jnp
import numpy as np
from jax import lax
from jax.experimental import pallas as pl
from jax.experimental.pallas import tpu as pltpu
from jax.experimental.pallas import tpu_sc as plsc

F32 = jnp.float32
BF16 = jnp.bfloat16
U32 = jnp.uint32

EPS = 1e-6
CONV_WIDTH = 512
CONV_GROUPS = 8
DN_HEADS = 4
DN_DK = 128
DN_V = 512
CHUNK = 64
N_GROUPS = 4
EXPERTS_PER_GROUP = 8
N_EXPERTS = N_GROUPS * EXPERTS_PER_GROUP
EXPERT_FF = 512
LANES = 128
HALO = 8

TOKEN_TILE = 512
COMBINE_TILE = 1024
ROUTE_SUB = 256
DN_BLOCK = 1024
DN_SUB = 256
EXPERT_TILE = 512
ROW_WORDS = 256
SC_WINDOW = 128
VMEM_LIMIT = 56 * 1024 * 1024

_GB_COLS = np.concatenate([np.arange(4 * DN_HEADS), np.arange(2 * DN_HEADS), np.arange(2 * DN_HEADS)])


def _cparams(sem):
    return pltpu.CompilerParams(dimension_semantics=sem, vmem_limit_bytes=VMEM_LIMIT)


def _sigmoid(x):
    return 1.0 / (1.0 + jnp.exp(-x))


def _rms_rows(x, g):
    return x * lax.rsqrt(jnp.mean(x * x, axis=-1, keepdims=True) + EPS) * g


def _dot(a, b):
    return jnp.dot(a, b, preferred_element_type=F32)


def _pack_pairs(x):
    half = x.shape[1] // 2
    bits = lambda a: lax.bitcast_convert_type(a.astype(BF16).astype(F32), U32)
    return (bits(x[:, half:]) & jnp.uint32(0xFFFF0000)) | (bits(x[:, :half]) >> 16)


def _unpack_pairs(w):
    lo = lax.bitcast_convert_type(w << 16, F32)
    hi = lax.bitcast_convert_type(w & jnp.uint32(0xFFFF0000), F32)
    return lo, hi


def _proj_kernel(x_ref, xprev_ref, xnext_ref,
                 g1_ref, w_ref, cw_ref, cng_ref, gmat_ref, nea_ref, dtb_ref,
                 ya_ref, q_ref, k_ref, v_ref, z_ref, gb_ref,
                 hs_ref, su_ref, p_ref, *, tile, seq_len):
    tok0 = pl.program_id(0) * tile
    not_start = (tok0 % seq_len != 0).astype(F32)
    not_end = ((tok0 + tile) % seq_len != 0).astype(F32)
    g1 = g1_ref[...]
    hs_ref[0:HALO, :] = _rms_rows(xprev_ref[...], g1)
    hs_ref[HALO:HALO + tile, :] = _rms_rows(x_ref[...], g1)
    hs_ref[HALO + tile:, :] = _rms_rows(xnext_ref[...], g1)

    p_ref[...] = _dot(hs_ref[...].astype(BF16), w_ref[...])
    for stage in _FINISH_STAGES:
        stage(p_ref, su_ref, cw_ref, cng_ref, gmat_ref, nea_ref, dtb_ref,
              ya_ref, q_ref, k_ref, v_ref, z_ref, gb_ref, tile, not_start, not_end)


def _conv_cols(p_ref, su_ref, cw_ref, lo, width, tile, not_start, not_end):
    cw = CONV_WIDTH
    cols = slice(lo, lo + width)
    if lo < cw:
        su_ref[:, cols] = p_ref[:, cw + lo:cw + lo + width] * p_ref[:, 2 * cw + lo:2 * cw + lo + width]
    else:
        su_ref[:, cols] = p_ref[:, 2 * cw + lo:2 * cw + lo + width]
    su_ref[0:HALO, cols] = su_ref[0:HALO, cols] * not_start
    su_ref[HALO + tile:, cols] = su_ref[HALO + tile:, cols] * not_end
    u = su_ref[:, cols]
    total = tile + 2 * HALO
    return (pltpu.roll(u, 1, axis=0)[HALO:HALO + tile] * cw_ref[0:1, cols]
            + u[HALO:HALO + tile] * cw_ref[1:2, cols]
            + pltpu.roll(u, total - 1, axis=0)[HALO:HALO + tile] * cw_ref[2:3, cols])


def _stage_mixer_a(half):
    width = CONV_WIDTH // 2
    lo = half * width

    def stage(p_ref, su_ref, cw_ref, cng_ref, gmat_ref, nea_ref, dtb_ref,
              ya_ref, q_ref, k_ref, v_ref, z_ref, gb_ref, tile, not_start, not_end):
        conv = _conv_cols(p_ref, su_ref, cw_ref, lo, width, tile, not_start, not_end)
        ya = p_ref[HALO:HALO + tile, lo:lo + width] * conv
        ms = _dot((ya * ya).astype(BF16), gmat_ref[lo:lo + width, lo:lo + width])
        ya_ref[:, lo:lo + width] = (ya * lax.rsqrt(ms + EPS) * cng_ref[:, lo:lo + width]).astype(ya_ref.dtype)
    return stage


def _stage_mixer_b(which, half):
    width = DN_V // 2
    lo_out = half * width
    lo = CONV_WIDTH + which * DN_V + lo_out

    def stage(p_ref, su_ref, cw_ref, cng_ref, gmat_ref, nea_ref, dtb_ref,
              ya_ref, q_ref, k_ref, v_ref, z_ref, gb_ref, tile, not_start, not_end):
        x = _conv_cols(p_ref, su_ref, cw_ref, lo, width, tile, not_start, not_end)
        x = x * _sigmoid(x)
        out_ref = (q_ref, k_ref, v_ref)[which]
        if which == 2:
            out_ref[:, lo_out:lo_out + width] = x.astype(out_ref.dtype)
            return
        scale = DN_DK ** -0.5 if which == 0 else 1.0
        for h in range(width // DN_DK):
            xh = x[:, h * DN_DK:(h + 1) * DN_DK]
            xh = xh * (lax.rsqrt(jnp.sum(xh * xh, axis=-1, keepdims=True) + EPS) * scale)
            out_ref[:, lo_out + h * DN_DK:lo_out + (h + 1) * DN_DK] = xh.astype(out_ref.dtype)
    return stage


def _stage_gates(inner):
    def stage(p_ref, su_ref, cw_ref, cng_ref, gmat_ref, nea_ref, dtb_ref,
              ya_ref, q_ref, k_ref, v_ref, z_ref, gb_ref, tile, not_start, not_end):
        inner(p_ref, su_ref, cw_ref, cng_ref, gmat_ref, nea_ref, dtb_ref,
              ya_ref, q_ref, k_ref, v_ref, z_ref, gb_ref, tile, not_start, not_end)
        cw = CONV_WIDTH
        z_ref[...] = p_ref[HALO:HALO + tile, 6 * cw:7 * cw].astype(z_ref.dtype)
        ab = p_ref[HALO:HALO + tile, 7 * cw:7 * cw + LANES]
        xs = ab + dtb_ref[...]
        softplus = jnp.maximum(xs, 0.0) + jnp.log(1.0 + jnp.exp(-jnp.abs(xs)))
        g = nea_ref[...] * softplus
        beta = _sigmoid(ab)
        r = lax.broadcasted_iota(jnp.int32, (tile, LANES), 0) % CHUNK
        pre = g
        suf = g
        s = 1
        while s < CHUNK:
            pre = pre + jnp.where(r >= s, pltpu.roll(pre, s, axis=0), 0.0)
            suf = suf + jnp.where(r < CHUNK - s, pltpu.roll(suf, tile - s, axis=0), 0.0)
            s *= 2
        lane = lax.broadcasted_iota(jnp.int32, (tile, LANES), 1) // DN_HEADS
        gb_ref[...] = jnp.where(lane == 0, pre,
                      jnp.where(lane == 1, suf,
                      jnp.where(lane < 4, beta,
                      jnp.where(lane == 4, suf - g,
                      jnp.where(lane == 5, pre - g, pre + suf - g)))))
    return stage


_FINISH_STAGES = (_stage_mixer_a(0), _stage_mixer_a(1), _stage_mixer_b(0, 0), _stage_mixer_b(0, 1),
                  _stage_mixer_b(1, 0), _stage_mixer_b(1, 1), _stage_mixer_b(2, 0),
                  _stage_gates(_stage_mixer_b(2, 1)))


def _proj_call(x, seq_len, wts):
    n, d = x.shape
    tile = TOKEN_TILE
    nt = n // tile
    tb = tile // HALO
    kern = functools.partial(_proj_kernel, tile=tile, seq_len=seq_len)
    const = lambda i: (0, 0)
    in_specs = [pl.BlockSpec((tile, d), lambda i: (i, 0)),
                pl.BlockSpec((HALO, d), lambda i: (jnp.maximum(i * tb - 1, 0), 0)),
                pl.BlockSpec((HALO, d), lambda i: (jnp.minimum((i + 1) * tb, n // HALO - 1), 0))]
    in_specs += [pl.BlockSpec(w.shape, const, pipeline_mode=pl.Buffered(1)) for w in wts]
    widths = (512, 512, 512, 512, 512, LANES)
    dtypes = (BF16, BF16, BF16, BF16, BF16, F32)
    rows = tile + 2 * HALO
    return pl.pallas_call(
        kern, grid=(nt,), in_specs=in_specs,
        out_specs=[pl.BlockSpec((tile, w), lambda i: (i, 0)) for w in widths],
        out_shape=[jax.ShapeDtypeStruct((n, w), dt) for w, dt in zip(widths, dtypes)],
        scratch_shapes=[pltpu.VMEM((rows, d), F32), pltpu.VMEM((rows, 4 * CONV_WIDTH), F32),
                        pltpu.VMEM((rows, wts[1].shape[1]), F32)],
        compiler_params=_cparams(("arbitrary",)), name="proj_conv")(x, x, x, *wts)


def _dn_kernel(*refs, block, sub, reverse, seq_len, nblk):
    if reverse:
        (q_ref, k_ref, v_ref, gb_ref, of_ref, z_ref, ng_ref, o_ref, s_ref) = refs
    else:
        (q_ref, k_ref, v_ref, gb_ref, o_ref, s_ref) = refs
    i = pl.program_id(0)
    b = nblk - 1 - i if reverse else i
    edge = (b + 1) * block if reverse else b * block

    @pl.when(edge % seq_len == 0)
    def _():
        s_ref[...] = jnp.zeros_like(s_ref)

    nchunk = sub // CHUNK
    c = CHUNK
    heads = range(DN_HEADS)
    ri = lax.broadcasted_iota(jnp.int32, (sub, sub), 0)
    ci = lax.broadcasted_iota(jnp.int32, (sub, sub), 1)
    same = (ri // c) == (ci // c)
    incl = same & ((ri <= ci) if reverse else (ri >= ci))
    strict = same & ((ri < ci) if reverse else (ri > ci))
    pr = lax.broadcasted_iota(jnp.int32, (c, sub), 0)
    pc = lax.broadcasted_iota(jnp.int32, (c, sub), 1)
    eye_p = (pr == pc % c).astype(F32)
    lane_chunk = pc // c

    def fold(bd):
        out = bd[0:c]
        for r in range(1, nchunk):
            out = out + bd[r * c:(r + 1) * c]
        return out

    def unfold(p):
        zero = jnp.zeros_like(p)
        return jnp.concatenate([jnp.where(lane_chunk == r, p, zero) for r in range(nchunk)], axis=0)

    grp = lambda g, h: slice(g * DN_HEADS + h, g * DN_HEADS + h + 1)
    g_cum, g_beta, g_rest, g_tot = (1, 3, 5, 7) if reverse else (0, 2, 4, 6)
    sls = [slice(h * DN_DK, (h + 1) * DN_DK) for h in heads]

    def prepare(r0, out):
        rows = slice(r0, r0 + sub)
        gbv = gb_ref[rows, :]
        gbt = gbv.T
        kb = [k_ref[rows, sl] for sl in sls]
        qb = [q_ref[rows, sl] for sl in sls]
        gcol = [gbv[:, grp(g_cum, h)] for h in heads]
        beta = [gbv[:, grp(g_beta, h)] for h in heads]
        kq = [lax.dot_general(jnp.concatenate([kb[h], qb[h]], axis=0), kb[h],
                              (((1,), (1,)), ((), ())), preferred_element_type=F32) for h in heads]
        yield
        decay = [jnp.exp(jnp.minimum(gcol[h] - gbt[grp(g_cum, h), :], 0.0)) for h in heads]
        bdl = [jnp.where(strict, kq[h][:sub] * decay[h], 0.0) * beta[h] for h in heads]
        bda = [jnp.where(incl, kq[h][sub:] * decay[h], 0.0).astype(BF16) for h in heads]
        yield
        lp = [fold(bdl[h]) for h in heads]
        xp = [eye_p - lp[h] for h in heads]
        bd = [bdl[h].astype(BF16) for h in heads]
        p = 1
        while p < c:
            if p == 1:
                lp = [_dot(lp[h].astype(BF16), bd[h]) for h in heads]
            elif 2 * p < c:
                res = [_dot(jnp.concatenate([lp[h], xp[h]], axis=0).astype(BF16), bd[h]) for h in heads]
                lp = [r[:c] for r in res]
                xp = [xp[h] + res[h][c:] for h in heads]
            else:
                xp = [xp[h] + _dot(xp[h].astype(BF16), bd[h]) for h in heads]
            p *= 2
            if p < c:
                bd = [unfold(lp[h].astype(BF16)) for h in heads]
            yield
        bdt = [unfold(xp[h].astype(BF16)) for h in heads]
        egc = [jnp.exp(gcol[h]) for h in heads]
        kf = [kb[h].astype(F32) for h in heads]
        rhs = [jnp.concatenate([kf[h] * (beta[h] * egc[h]), v_ref[rows, sls[h]].astype(F32) * beta[h]],
                               axis=1).astype(BF16) for h in heads]
        wu = [_dot(bdt[h], rhs[h]) for h in heads]
        yield
        au = [_dot(bda[h], wu[h].astype(BF16)) for h in heads]
        qt = [qb[h].astype(F32) * egc[h] - au[h][:, :DN_DK] for h in heads]
        out["kd"] = [(kf[h] * jnp.exp(gbv[:, grp(g_rest, h)])).astype(BF16) for h in heads]
        out["wqt"] = [jnp.concatenate([wu[h][:, :DN_DK], qt[h]], axis=1).astype(BF16) for h in heads]
        out["u"] = [wu[h][:, DN_DK:] for h in heads]
        out["o0"] = [au[h][:, DN_DK:] for h in heads]
        out["gbt"] = gbt
        yield

    def scan(r0, pre, state):
        order = range(nchunk - 1, -1, -1) if reverse else range(nchunk)
        for cc in order:
            rows = slice(cc * c, (cc + 1) * c)
            out_rows = slice(r0 + cc * c, r0 + (cc + 1) * c)
            for h in heads:
                wqt = pre["wqt"][h]
                lhs = jnp.concatenate([wqt[rows, :DN_DK], wqt[rows, DN_DK:]], axis=0)
                wq = _dot(lhs, state[h].astype(BF16))
                vn = pre["u"][h][rows] - wq[:c]
                o = wq[c:] + pre["o0"][h][rows]
                tot = pre["gbt"][grp(g_tot, h), cc * c:cc * c + 1]
                state[h] = state[h] * jnp.exp(tot) + lax.dot_general(
                    pre["kd"][h][rows], vn.astype(BF16), (((0,), (0,)), ((), ())),
                    preferred_element_type=F32)
                if reverse:
                    o = o + of_ref[out_rows, sls[h]]
                    o = o * lax.rsqrt(jnp.mean(o * o, axis=-1, keepdims=True) + EPS) * ng_ref[...]
                    zc = z_ref[out_rows, sls[h]].astype(F32)
                    o = o * (zc * _sigmoid(zc))
                o_ref[out_rows, sls[h]] = o.astype(o_ref.dtype)
            yield

    nsub = block // sub
    starts = [r * sub for r in (range(nsub - 1, -1, -1) if reverse else range(nsub))]
    state = [s_ref[h] for h in heads]
    pre_prev = None
    for idx in range(nsub + 1):
        pre = {}
        prep_gen = prepare(starts[idx], pre) if idx < nsub else iter(())
        scan_gen = scan(starts[idx - 1], pre_prev, state) if idx > 0 else iter(())
        prep_live = scan_live = True
        while prep_live or scan_live:
            if prep_live:
                prep_live = next(prep_gen, _DONE) is not _DONE
                if prep_live:
                    prep_live = next(prep_gen, _DONE) is not _DONE
            if scan_live:
                scan_live = next(scan_gen, _DONE) is not _DONE
        pre_prev = pre
    for h in heads:
        s_ref[h] = state[h]


_DONE = object()


def _dn_call(q, k, v, gb, extra, reverse, seq_len):
    n = q.shape[0]
    block = DN_BLOCK
    nblk = n // block
    imap = (lambda i: (nblk - 1 - i, 0)) if reverse else (lambda i: (i, 0))
    tok = lambda w: pl.BlockSpec((block, w), imap)
    in_specs = [tok(512), tok(512), tok(512), tok(LANES)]
    args = [q, k, v, gb]
    if reverse:
        o_f, z, ng = extra
        in_specs += [tok(512), tok(512), pl.BlockSpec(ng.shape, lambda i: (0, 0))]
        args += [o_f, z, ng]
    kern = functools.partial(_dn_kernel, block=block, sub=DN_SUB, reverse=reverse,
                             seq_len=seq_len, nblk=nblk)
    return pl.pallas_call(
        kern, grid=(nblk,), in_specs=in_specs, out_specs=tok(512),
        out_shape=jax.ShapeDtypeStruct((n, 512), BF16 if reverse else F32),
        scratch_shapes=[pltpu.VMEM((DN_HEADS, DN_DK, DN_DK), F32)],
        compiler_params=_cparams(("arbitrary",)),
        name="deltanet_bwd" if reverse else "deltanet_fwd")(*args)


def _route_kernel(x_ref, ya_ref, yb_ref, wo_ref, g2_ref, rw_ref, rb_ref, tri_ref,
                  x2_ref, xa_ref, xb_ref, info_ref, infot_ref, cnt_ref, run_ref, *, tile):
    i = pl.program_id(0)

    @pl.when(i == 0)
    def _():
        run_ref[...] = jnp.zeros_like(run_ref)

    sub = tri_ref.shape[0]
    parts = [slice(r, r + sub) for r in range(0, tile, sub)]
    each = lambda f, *lists: [f(*a) for a in zip(*lists)]
    half = wo_ref.shape[0] // 2
    x2 = [x_ref[rows, :] + _dot(ya_ref[rows, :], wo_ref[0:half, :])
          + _dot(yb_ref[rows, :], wo_ref[half:, :]) for rows in parts]
    for rows, v in zip(parts, x2):
        x2_ref[rows, :] = v.astype(x2_ref.dtype)
    xn = [_rms_rows(v, g2_ref[...]) for v in x2]
    for rows, v in zip(parts, xn):
        words = _pack_pairs(v)
        xa_ref[rows, :] = words[:, :ROW_WORDS]
        xb_ref[rows, :] = words[:, ROW_WORDS:]
    logits = [_dot(v.astype(BF16), rw_ref[...]) + rb_ref[...] for v in xn]
    lane = lax.broadcasted_iota(jnp.int32, (sub, LANES), 1).astype(F32)
    neg = jnp.float32(-jnp.inf)
    row_max = lambda v: jnp.max(v, axis=-1, keepdims=True)
    first_at = lambda v, m: jnp.min(jnp.where(v == m, lane, float(LANES)), axis=-1, keepdims=True)

    gl = [jnp.where(lane < N_GROUPS, v, neg) for v in logits]
    gmax = each(row_max, gl)
    gidx = each(first_at, gl, gmax)
    g_w = [1.0 / jnp.sum(jnp.exp(v - m), axis=-1, keepdims=True) for v, m in zip(gl, gmax)]
    lo = [N_GROUPS + g * EXPERTS_PER_GROUP for g in gidx]
    el = [jnp.where((lane >= a) & (lane < a + EXPERTS_PER_GROUP), v, neg) for v, a in zip(logits, lo)]
    m1 = each(row_max, el)
    i1 = each(first_at, el, m1)
    el2 = [jnp.where(lane == a, neg, v) for v, a in zip(el, i1)]
    m2 = each(row_max, el2)
    i2 = each(first_at, el2, m2)
    hit1 = [lane == a - N_GROUPS for a in i1]
    hit2 = [lane == a - N_GROUPS for a in i2]
    onehot = [(a | b).astype(BF16) for a, b in zip(hit1, hit2)]
    inside = [_dot(tri_ref[...], v) for v in onehot]
    run = run_ref[...]
    for k, rows in enumerate(parts):
        before = inside[k] + run
        pos1 = jnp.sum(jnp.where(hit1[k], before, 0.0), axis=-1, keepdims=True)
        pos2 = jnp.sum(jnp.where(hit2[k], before, 0.0), axis=-1, keepdims=True)
        run = run + jnp.sum(onehot[k].astype(F32), axis=0, keepdims=True)
        r = jnp.exp(m2[k] - m1[k])
        gate1 = g_w[k] / (1.0 + r)
        info = jnp.where(lane == 0, i1[k] - N_GROUPS,
               jnp.where(lane == 1, i2[k] - N_GROUPS,
               jnp.where(lane == 2, gate1,
               jnp.where(lane == 3, gate1 * r,
               jnp.where(lane == 4, pos1,
               jnp.where(lane == 5, pos2, 0.0))))))
        info_ref[rows, :] = info
        infot_ref[:, rows] = info.T[0:HALO, :]
    run_ref[...] = run
    cnt_ref[...] = run


def _route_call(x, ya, yb, wts):
    n, d = x.shape
    tile = TOKEN_TILE
    const = lambda i: (0, 0)
    tok = lambda w: pl.BlockSpec((tile, w), lambda i: (i, 0))
    in_specs = [tok(d), tok(512), tok(512)] + [pl.BlockSpec(w.shape, const) for w in wts]
    out_shape = [jax.ShapeDtypeStruct((n, d), BF16), jax.ShapeDtypeStruct((n, ROW_WORDS), U32),
                 jax.ShapeDtypeStruct((n, ROW_WORDS), U32),
                 jax.ShapeDtypeStruct((n, LANES), F32), jax.ShapeDtypeStruct((HALO, n), F32),
                 jax.ShapeDtypeStruct((1, LANES), F32)]
    out_specs = [tok(d), tok(ROW_WORDS), tok(ROW_WORDS), tok(LANES),
                 pl.BlockSpec((HALO, tile), lambda i: (0, i)), pl.BlockSpec((1, LANES), const)]
    kern = functools.partial(_route_kernel, tile=tile)
    return pl.pallas_call(
        kern, grid=(n // tile,), in_specs=in_specs, out_specs=out_specs, out_shape=out_shape,
        scratch_shapes=[pltpu.VMEM((1, LANES), F32)],
        compiler_params=_cparams(("arbitrary",)), name="oproj_router")(x, ya, yb, *wts)


def _sc_mesh():
    return plsc.VectorSubcoreMesh(core_axis_name="core", subcore_axis_name="subcore")


def _sc_scatter_rows(x, idx0, idx1, rows):
    n, w = x.shape

    @pl.kernel(out_type=jax.ShapeDtypeStruct((rows, w), x.dtype), mesh=_sc_mesh(), scratch_types=[])
    def scatter(x_hbm, i0_hbm, i1_hbm, o_hbm):
        def body(x_vmem, i0_vmem, i1_vmem):
            pltpu.sync_copy(x_vmem, o_hbm.at[i0_vmem.at[0]])
            pltpu.sync_copy(x_vmem, o_hbm.at[i1_vmem.at[0]])

        pltpu.emit_pipeline(
            body, grid=(n // SC_WINDOW,),
            in_specs=[pl.BlockSpec((SC_WINDOW, w), lambda i: (i, 0)),
                      pl.BlockSpec((1, SC_WINDOW), lambda i: (0, i)),
                      pl.BlockSpec((1, SC_WINDOW), lambda i: (0, i))],
            out_specs=[], core_axis_name=("core", "subcore"),
            dimension_semantics=(pltpu.PARALLEL,))(x_hbm, i0_hbm, i1_hbm)

    return scatter(x, idx0, idx1)


def _sc_gather_rows(y, idx):
    m = idx.shape[1]
    w = y.shape[1]

    @pl.kernel(out_type=jax.ShapeDtypeStruct((m, w), y.dtype), mesh=_sc_mesh(), scratch_types=[])
    def gather(y_hbm, i_hbm, o_hbm):
        def body(i_vmem, o_vmem):
            pltpu.sync_copy(y_hbm.at[i_vmem.at[0]], o_vmem)

        pltpu.emit_pipeline(
            body, grid=(m // SC_WINDOW,),
            in_specs=[pl.BlockSpec((1, SC_WINDOW), lambda i: (0, i))],
            out_specs=[pl.BlockSpec((SC_WINDOW, w), lambda i: (i, 0))],
            core_axis_name=("core", "subcore"),
            dimension_semantics=(pltpu.PARALLEL,))(i_hbm, o_hbm)

    return gather(y, idx)


def _join_rows(wa, wb):
    lo_a, hi_a = _unpack_pairs(wa)
    lo_b, hi_b = _unpack_pairs(wb)
    return jnp.concatenate([lo_a, lo_b, hi_a, hi_b], axis=1)


def _expert_kernel(te_ref, nx_ref, nv_ref, nu_ref, xa_ref, xb_ref, w1_hbm, w3_hbm, w2_hbm,
                   ya_ref, yb_ref, stage1, stage3, stage2, w1_ref, w3_ref, w2_ref, sem):
    j = pl.program_id(0)

    def fetch(e):
        return [pltpu.make_async_copy(src.at[e], dst, sem.at[k]) for k, (src, dst) in
                enumerate(((w1_hbm, stage1), (w3_hbm, stage3), (w2_hbm, stage2)))]

    @pl.when(j == 0)
    def _():
        for cp in fetch(te_ref[0]):
            cp.start()

    @pl.when((j < nu_ref[0]) & (nx_ref[j] > -2))
    def _():
        for cp in fetch(te_ref[j]):
            cp.wait()
        w1_ref[...] = stage1[...].astype(BF16)
        w3_ref[...] = stage3[...].astype(BF16)
        w2_ref[...] = stage2[...].astype(BF16)

        @pl.when(nx_ref[j] >= 0)
        def _():
            for cp in fetch(nx_ref[j]):
                cp.start()

    @pl.when(j < nu_ref[0])
    def _():
        live = lax.broadcasted_iota(jnp.int32, xa_ref.shape, 0) < nv_ref[j]
        zero = jnp.zeros(xa_ref.shape, U32)
        x = _join_rows(jnp.where(live, xa_ref[...], zero),
                       jnp.where(live, xb_ref[...], zero)).astype(BF16)
        h1 = _dot(x, w1_ref[...])
        h3 = _dot(x, w3_ref[...])
        hdn = (h1 * _sigmoid(h1) * h3).astype(BF16)
        words = _pack_pairs(_dot(hdn, w2_ref[...]))
        ya_ref[...] = words[:, :ROW_WORDS]
        yb_ref[...] = words[:, ROW_WORDS:]


def _expert_call(tile_expert, tile_next, tile_valid, n_used, xa, xb, w1, w3, w2):
    rows, rw = xa.shape
    tm = EXPERT_TILE
    d, ff = w1.shape[1], w1.shape[2]
    row_blk = lambda j, te, nx, nv, nu: (jnp.minimum(j, nu[0] - 1), 0)
    hbm = pl.BlockSpec(memory_space=pl.ANY)
    grid_spec = pltpu.PrefetchScalarGridSpec(
        num_scalar_prefetch=4, grid=(rows // tm,),
        in_specs=[pl.BlockSpec((tm, rw), row_blk), pl.BlockSpec((tm, rw), row_blk), hbm, hbm, hbm],
        out_specs=[pl.BlockSpec((tm, rw), row_blk), pl.BlockSpec((tm, rw), row_blk)],
        scratch_shapes=[pltpu.VMEM((d, ff), w1.dtype), pltpu.VMEM((d, ff), w3.dtype),
                        pltpu.VMEM((ff, d), w2.dtype),
                        pltpu.VMEM((d, ff), BF16), pltpu.VMEM((d, ff), BF16), pltpu.VMEM((ff, d), BF16),
                        pltpu.SemaphoreType.DMA((3,))])
    return pl.pallas_call(
        _expert_kernel, grid_spec=grid_spec,
        out_shape=[jax.ShapeDtypeStruct((rows, rw), U32), jax.ShapeDtypeStruct((rows, rw), U32)],
        compiler_params=_cparams(("arbitrary",)), name="moe_experts")(
            tile_expert, tile_next, tile_valid, n_used, xa, xb, w1, w3, w2)


def _combine_kernel(x2_ref, info_ref, gf_ref, a0_ref, b0_ref, a1_ref, b1_ref, y_ref):
    info = info_ref[...]
    moe = (info[:, 2:3] * _join_rows(a0_ref[...], b0_ref[...])
           + info[:, 3:4] * _join_rows(a1_ref[...], b1_ref[...]))
    y_ref[...] = _rms_rows(x2_ref[...].astype(F32) + moe, gf_ref[...])


def _combine_call(x2, info, gf, ga, gb):
    n, d = x2.shape
    tile = COMBINE_TILE
    nt = n // tile
    first = pl.BlockSpec((tile, ROW_WORDS), lambda i: (i, 0))
    second = pl.BlockSpec((tile, ROW_WORDS), lambda i: (i + nt, 0))
    return pl.pallas_call(
        _combine_kernel, grid=(nt,),
        in_specs=[pl.BlockSpec((tile, d), lambda i: (i, 0)),
                  pl.BlockSpec((tile, LANES), lambda i: (i, 0)),
                  pl.BlockSpec(gf.shape, lambda i: (0, 0)),
                  first, first, second, second],
        out_specs=pl.BlockSpec((tile, d), lambda i: (i, 0)),
        out_shape=jax.ShapeDtypeStruct((n, d), F32),
        compiler_params=_cparams(("arbitrary",)), name="moe_combine")(
            x2, info, gf, ga, gb, ga, gb)


def _row(a):
    return a.reshape(1, -1).astype(F32)


def _prep_weights(norm1_g, w_in, conv_a_w, conv_a_norm_g, dn_conv_w, dn_a_log, dn_dt_bias, w_o,
                  norm2_g, rg_w, rg_b, re_w, re_b):
    row = _row
    n_ab = len(_GB_COLS)
    w_main = jnp.concatenate([w_in[:, :7 * CONV_WIDTH], w_in[:, 7 * CONV_WIDTH + _GB_COLS]], axis=1)
    w_main = jnp.pad(w_main, ((0, 0), (0, LANES - n_ab))).astype(BF16)
    cw = jnp.concatenate([conv_a_w, dn_conv_w], axis=1).astype(F32)
    grp = jnp.arange(CONV_WIDTH) // (CONV_WIDTH // CONV_GROUPS)
    gmat = ((grp[:, None] == grp[None, :]).astype(F32) / (CONV_WIDTH // CONV_GROUPS)).astype(BF16)
    is_a = (_GB_COLS < 2 * DN_HEADS)
    a_idx = jnp.where(is_a, _GB_COLS, 0)
    padl = lambda a: jnp.pad(a.reshape(1, -1).astype(F32), ((0, 0), (0, LANES - n_ab)))
    nea = padl(jnp.where(is_a, -jnp.exp(dn_a_log.astype(F32)).reshape(-1)[a_idx], 0.0))
    dtb = padl(jnp.where(is_a, dn_dt_bias.astype(F32).reshape(-1)[a_idx], 0.0))
    proj_w = [row(norm1_g), w_main, cw, row(conv_a_norm_g), gmat, nea, dtb]
    rw = jnp.pad(jnp.concatenate([rg_w, re_w], axis=1).astype(F32),
                 ((0, 0), (0, LANES - N_GROUPS - N_EXPERTS)))
    rb = jnp.pad(jnp.concatenate([rg_b, re_b]).reshape(1, -1).astype(F32),
                 ((0, 0), (0, LANES - N_GROUPS - N_EXPERTS)))
    tri = (jnp.arange(ROUTE_SUB)[:, None] > jnp.arange(ROUTE_SUB)[None, :]).astype(BF16)
    route_w = [w_o.astype(BF16), row(norm2_g), rw.astype(BF16), rb, tri]
    return proj_w, route_w


def _mix_and_route(x, seq_len, proj_w, dn_norm_g, route_w):
    ya, q, k, v, z, gb = _proj_call(x, seq_len, proj_w)
    o_f = _dn_call(q, k, v, gb, None, False, seq_len)
    yb = _dn_call(q, k, v, gb, (o_f, z, _row(dn_norm_g)), True, seq_len)
    return _route_call(x, ya, yb, route_w)


def _moe(routed, w1, w3, w2, final_g):
    x2, xna, xnb, info, info_t, cnt = routed
    n = x2.shape[0]
    tm = EXPERT_TILE
    counts = cnt[0, :N_EXPERTS].astype(jnp.int32)
    pcounts = (counts + tm - 1) // tm * tm
    pends = jnp.cumsum(pcounts)
    pstarts = pends - pcounts
    eid = jnp.arange(N_EXPERTS, dtype=jnp.int32)
    e12 = info_t[0:2].astype(jnp.int32)
    start12 = jnp.sum(jnp.where(e12[:, None, :] == eid[None, :, None], pstarts[None, :, None], 0),
                      axis=1)
    dest = start12 + info_t[4:6].astype(jnp.int32)
    dest0 = dest[0:1]
    dest1 = dest[1:2]
    n_tiles = 2 * n // tm + N_EXPERTS
    tile_start = jnp.arange(n_tiles, dtype=jnp.int32) * tm
    tile_expert = jnp.minimum(jnp.sum(pends[None, :] <= tile_start[:, None], axis=1),
                              N_EXPERTS - 1).astype(jnp.int32)
    of_tile = lambda table: jnp.sum(jnp.where(tile_expert[:, None] == eid, table, 0), axis=-1)
    tile_valid = jnp.clip(of_tile(pstarts + counts) - tile_start, 0, tm).astype(jnp.int32)
    n_used = (pends[-1:] // tm).astype(jnp.int32)
    later = (eid[None, :] > eid[:, None]) & (pcounts[None, :] > 0)
    next_e = jnp.min(jnp.where(later, eid[None, :], N_EXPERTS), axis=1)
    next_e = jnp.where(next_e == N_EXPERTS, -1, next_e)
    tile_next = jnp.where(tile_start == of_tile(pstarts), of_tile(next_e), -2).astype(jnp.int32)

    rows = n_tiles * tm
    xa = _sc_scatter_rows(xna, dest0, dest1, rows)
    xb = _sc_scatter_rows(xnb, dest0, dest1, rows)
    ya_e, yb_e = _expert_call(tile_expert, tile_next, tile_valid, n_used, xa, xb, w1, w3, w2)
    both = jnp.concatenate([dest0, dest1], axis=1)
    return _combine_call(x2, info, _row(final_g), _sc_gather_rows(ya_e, both),
                         _sc_gather_rows(yb_e, both))


def kernel(x_prompt, x_sample, norm1_g, w_in, conv_a_w, conv_a_norm_g, dn_conv_w, dn_a_log,
           dn_dt_bias, dn_norm_g, w_o, norm2_g, router_group_w, router_group_b, router_expert_w,
           router_expert_b, w1, w3, w2, final_norm_g):
    assert norm1_g.shape[0] == 1, "single-layer trunk"
    bp, sp, d = x_prompt.shape
    bs, ss, _ = x_sample.shape
    assert bp == 1 and sp % max(TOKEN_TILE, DN_BLOCK) == 0 and ss % max(TOKEN_TILE, DN_BLOCK) == 0
    proj_w, route_w = _prep_weights(
        norm1_g[0], w_in[0], conv_a_w[0], conv_a_norm_g[0], dn_conv_w[0], dn_a_log[0],
        dn_dt_bias[0], w_o[0], norm2_g[0], router_group_w[0], router_group_b[0],
        router_expert_w[0], router_expert_b[0])
    routed = [_mix_and_route(x.reshape(-1, d), seq_len, proj_w, dn_norm_g[0], route_w)
              for x, seq_len in ((x_prompt, sp), (x_sample, ss))]
    y_p, y_s = [_moe(r, w1[0], w3[0], w2[0], final_norm_g) for r in routed]
    return y_p.reshape(bp, sp, d), y_s.reshape(bs, ss, d)
```

```python
import functools

import jax
import jax.numpy as jnp
import numpy as np
from jax import lax
from jax.experimental import pallas as pl
from jax.experimental.pallas import tpu as pltpu
from jax.experimental.pallas import tpu_sc as plsc

F32 = jnp.float32
BF16 = jnp.bfloat16
U32 = jnp.uint32

EPS = 1e-6
CONV_WIDTH = 512
CONV_GROUPS = 8
DN_HEADS = 4
DN_DK = 128
DN_V = 512
CHUNK = 64
N_GROUPS = 4
EXPERTS_PER_GROUP = 8
N_EXPERTS = N_GROUPS * EXPERTS_PER_GROUP
EXPERT_FF = 512
LANES = 128
HALO = 8

TOKEN_TILE = 512
COMBINE_TILE = 1024
ROUTE_TILE = 1024
ROUTE_SUB = 256
DN_BLOCK = 2048
DN_SUB = 256
EXPERT_TILE = 512
ROW_WORDS = 256
SC_WINDOW = 128
VMEM_LIMIT = 56 * 1024 * 1024

_GB_COLS = np.concatenate([np.arange(4 * DN_HEADS), np.arange(2 * DN_HEADS), np.arange(2 * DN_HEADS)])


def _cparams(sem):
    return pltpu.CompilerParams(dimension_semantics=sem, vmem_limit_bytes=VMEM_LIMIT)


def _sigmoid(x):
    return 1.0 / (1.0 + jnp.exp(-x))


def _rms_rows(x, g):
    return x * lax.rsqrt(jnp.mean(x * x, axis=-1, keepdims=True) + EPS) * g


def _dot(a, b):
    return jnp.dot(a, b, preferred_element_type=F32)


def _pack_pairs(x):
    half = x.shape[1] // 2
    bits = lambda a: lax.bitcast_convert_type(a.astype(BF16).astype(F32), U32)
    return (bits(x[:, half:]) & jnp.uint32(0xFFFF0000)) | (bits(x[:, :half]) >> 16)


def _unpack_pairs(w):
    lo = lax.bitcast_convert_type(w << 16, F32)
    hi = lax.bitcast_convert_type(w & jnp.uint32(0xFFFF0000), F32)
    return lo, hi


def _proj_kernel(x_ref, xprev_ref, xnext_ref,
                 g1_ref, w_ref, cw_ref, cng_ref, gmat_ref, nea_ref, dtb_ref,
                 ya_ref, q_ref, k_ref, v_ref, z_ref, gb_ref,
                 hs_ref, su_ref, p_ref, *, tile, seq_len):
    tok0 = pl.program_id(0) * tile
    not_start = (tok0 % seq_len != 0).astype(F32)
    not_end = ((tok0 + tile) % seq_len != 0).astype(F32)
    g1 = g1_ref[...]
    hs_ref[0:HALO, :] = _rms_rows(xprev_ref[...], g1)
    hs_ref[HALO:HALO + tile, :] = _rms_rows(x_ref[...], g1)
    hs_ref[HALO + tile:, :] = _rms_rows(xnext_ref[...], g1)

    p_ref[...] = _dot(hs_ref[...].astype(BF16), w_ref[...])
    for stage in _FINISH_STAGES:
        stage(p_ref, su_ref, cw_ref, cng_ref, gmat_ref, nea_ref, dtb_ref,
              ya_ref, q_ref, k_ref, v_ref, z_ref, gb_ref, tile, not_start, not_end)


def _conv_cols(p_ref, su_ref, cw_ref, lo, width, tile, not_start, not_end):
    cw = CONV_WIDTH
    cols = slice(lo, lo + width)
    if lo < cw:
        su_ref[:, cols] = p_ref[:, cw + lo:cw + lo + width] * p_ref[:, 2 * cw + lo:2 * cw + lo + width]
    else:
        su_ref[:, cols] = p_ref[:, 2 * cw + lo:2 * cw + lo + width]
    su_ref[0:HALO, cols] = su_ref[0:HALO, cols] * not_start
    su_ref[HALO + tile:, cols] = su_ref[HALO + tile:, cols] * not_end
    u = su_ref[:, cols]
    total = tile + 2 * HALO
    return (pltpu.roll(u, 1, axis=0)[HALO:HALO + tile] * cw_ref[0:1, cols]
            + u[HALO:HALO + tile] * cw_ref[1:2, cols]
            + pltpu.roll(u, total - 1, axis=0)[HALO:HALO + tile] * cw_ref[2:3, cols])


def _stage_mixer_a(half):
    width = CONV_WIDTH // 2
    lo = half * width

    def stage(p_ref, su_ref, cw_ref, cng_ref, gmat_ref, nea_ref, dtb_ref,
              ya_ref, q_ref, k_ref, v_ref, z_ref, gb_ref, tile, not_start, not_end):
        conv = _conv_cols(p_ref, su_ref, cw_ref, lo, width, tile, not_start, not_end)
        ya = p_ref[HALO:HALO + tile, lo:lo + width] * conv
        ms = _dot((ya * ya).astype(BF16), gmat_ref[lo:lo + width, lo:lo + width])
        ya_ref[:, lo:lo + width] = (ya * lax.rsqrt(ms + EPS) * cng_ref[:, lo:lo + width]).astype(ya_ref.dtype)
    return stage


def _stage_mixer_b(which, half):
    width = DN_V // 2
    lo_out = half * width
    lo = CONV_WIDTH + which * DN_V + lo_out

    def stage(p_ref, su_ref, cw_ref, cng_ref, gmat_ref, nea_ref, dtb_ref,
              ya_ref, q_ref, k_ref, v_ref, z_ref, gb_ref, tile, not_start, not_end):
        x = _conv_cols(p_ref, su_ref, cw_ref, lo, width, tile, not_start, not_end)
        x = x * _sigmoid(x)
        out_ref = (q_ref, k_ref, v_ref)[which]
        if which == 2:
            out_ref[:, lo_out:lo_out + width] = x.astype(out_ref.dtype)
            return
        scale = DN_DK ** -0.5 if which == 0 else 1.0
        for h in range(width // DN_DK):
            xh = x[:, h * DN_DK:(h + 1) * DN_DK]
            xh = xh * (lax.rsqrt(jnp.sum(xh * xh, axis=-1, keepdims=True) + EPS) * scale)
            out_ref[:, lo_out + h * DN_DK:lo_out + (h + 1) * DN_DK] = xh.astype(out_ref.dtype)
    return stage


def _stage_gates(inner):
    def stage(p_ref, su_ref, cw_ref, cng_ref, gmat_ref, nea_ref, dtb_ref,
              ya_ref, q_ref, k_ref, v_ref, z_ref, gb_ref, tile, not_start, not_end):
        inner(p_ref, su_ref, cw_ref, cng_ref, gmat_ref, nea_ref, dtb_ref,
              ya_ref, q_ref, k_ref, v_ref, z_ref, gb_ref, tile, not_start, not_end)
        cw = CONV_WIDTH
        z_ref[...] = p_ref[HALO:HALO + tile, 6 * cw:7 * cw].astype(z_ref.dtype)
        ab = p_ref[HALO:HALO + tile, 7 * cw:7 * cw + LANES]
        xs = ab + dtb_ref[...]
        softplus = jnp.maximum(xs, 0.0) + jnp.log(1.0 + jnp.exp(-jnp.abs(xs)))
        g = nea_ref[...] * softplus
        beta = _sigmoid(ab)
        r = lax.broadcasted_iota(jnp.int32, (tile, LANES), 0) % CHUNK
        pre = g
        suf = g
        s = 1
        while s < CHUNK:
            pre = pre + jnp.where(r >= s, pltpu.roll(pre, s, axis=0), 0.0)
            suf = suf + jnp.where(r < CHUNK - s, pltpu.roll(suf, tile - s, axis=0), 0.0)
            s *= 2
        lane = lax.broadcasted_iota(jnp.int32, (tile, LANES), 1) // DN_HEADS
        gb_ref[...] = jnp.where(lane == 0, pre,
                      jnp.where(lane == 1, suf,
                      jnp.where(lane < 4, beta,
                      jnp.where(lane == 4, suf - g,
                      jnp.where(lane == 5, pre - g, pre + suf - g)))))
    return stage


_FINISH_STAGES = (_stage_mixer_a(0), _stage_mixer_a(1), _stage_mixer_b(0, 0), _stage_mixer_b(0, 1),
                  _stage_mixer_b(1, 0), _stage_mixer_b(1, 1), _stage_mixer_b(2, 0),
                  _stage_gates(_stage_mixer_b(2, 1)))


def _proj_call(x, seq_len, wts):
    n, d = x.shape
    tile = TOKEN_TILE
    nt = n // tile
    tb = tile // HALO
    kern = functools.partial(_proj_kernel, tile=tile, seq_len=seq_len)
    const = lambda i: (0, 0)
    in_specs = [pl.BlockSpec((tile, d), lambda i: (i, 0)),
                pl.BlockSpec((HALO, d), lambda i: (jnp.maximum(i * tb - 1, 0), 0)),
                pl.BlockSpec((HALO, d), lambda i: (jnp.minimum((i + 1) * tb, n // HALO - 1), 0))]
    in_specs += [pl.BlockSpec(w.shape, const, pipeline_mode=pl.Buffered(1)) for w in wts]
    widths = (512, 512, 512, 512, 512, LANES)
    dtypes = (BF16, BF16, BF16, BF16, BF16, F32)
    rows = tile + 2 * HALO
    return pl.pallas_call(
        kern, grid=(nt,), in_specs=in_specs,
        out_specs=[pl.BlockSpec((tile, w), lambda i: (i, 0)) for w in widths],
        out_shape=[jax.ShapeDtypeStruct((n, w), dt) for w, dt in zip(widths, dtypes)],
        scratch_shapes=[pltpu.VMEM((rows, d), F32), pltpu.VMEM((rows, 4 * CONV_WIDTH), F32),
                        pltpu.VMEM((rows, wts[1].shape[1]), F32)],
        compiler_params=_cparams(("arbitrary",)), name="proj_conv")(x, x, x, *wts)


def _dn_kernel(*refs, block, sub, reverse, seq_len, nblk):
    if reverse:
        (q_ref, k_ref, v_ref, gb_ref, of_ref, z_ref, ng_ref, o_ref, s_ref) = refs
    else:
        (q_ref, k_ref, v_ref, gb_ref, o_ref, s_ref) = refs
    i = pl.program_id(0)
    b = nblk - 1 - i if reverse else i
    edge = (b + 1) * block if reverse else b * block

    @pl.when(edge % seq_len == 0)
    def _():
        s_ref[...] = jnp.zeros_like(s_ref)

    nchunk = sub // CHUNK
    c = CHUNK
    heads = range(DN_HEADS)
    ri = lax.broadcasted_iota(jnp.int32, (sub, sub), 0)
    ci = lax.broadcasted_iota(jnp.int32, (sub, sub), 1)
    same = (ri // c) == (ci // c)
    incl = same & ((ri <= ci) if reverse else (ri >= ci))
    strict = same & ((ri < ci) if reverse else (ri > ci))
    pr = lax.broadcasted_iota(jnp.int32, (c, sub), 0)
    pc = lax.broadcasted_iota(jnp.int32, (c, sub), 1)
    eye_p = (pr == pc % c).astype(F32)
    lane_chunk = pc // c

    def fold(bd):
        out = bd[0:c]
        for r in range(1, nchunk):
            out = out + bd[r * c:(r + 1) * c]
        return out

    def unfold(p):
        zero = jnp.zeros_like(p)
        return jnp.concatenate([jnp.where(lane_chunk == r, p, zero) for r in range(nchunk)], axis=0)

    grp = lambda g, h: slice(g * DN_HEADS + h, g * DN_HEADS + h + 1)
    g_cum, g_beta, g_rest, g_tot = (1, 3, 5, 7) if reverse else (0, 2, 4, 6)
    sls = [slice(h * DN_DK, (h + 1) * DN_DK) for h in heads]

    def prepare(r0, out):
        rows = slice(r0, r0 + sub)
        gbv = gb_ref[rows, :]
        gbt = gbv.T
        kb = [k_ref[rows, sl] for sl in sls]
        qb = [q_ref[rows, sl] for sl in sls]
        gcol = [gbv[:, grp(g_cum, h)] for h in heads]
        beta = [gbv[:, grp(g_beta, h)] for h in heads]
        kq = [lax.dot_general(jnp.concatenate([kb[h], qb[h]], axis=0), kb[h],
                              (((1,), (1,)), ((), ())), preferred_element_type=F32) for h in heads]
        yield
        decay = [jnp.exp(jnp.minimum(gcol[h] - gbt[grp(g_cum, h), :], 0.0)) for h in heads]
        bdl = [jnp.where(strict, kq[h][:sub] * decay[h], 0.0) * beta[h] for h in heads]
        bda = [jnp.where(incl, kq[h][sub:] * decay[h], 0.0).astype(BF16) for h in heads]
        yield
        lp = [fold(bdl[h]) for h in heads]
        xp = [eye_p - lp[h] for h in heads]
        bd = [bdl[h].astype(BF16) for h in heads]
        p = 1
        while p < c:
            if p == 1:
                lp = [_dot(lp[h].astype(BF16), bd[h]) for h in heads]
            elif 2 * p < c:
                res = [_dot(jnp.concatenate([lp[h], xp[h]], axis=0).astype(BF16), bd[h]) for h in heads]
                lp = [r[:c] for r in res]
                xp = [xp[h] + res[h][c:] for h in heads]
            else:
                xp = [xp[h] + _dot(xp[h].astype(BF16), bd[h]) for h in heads]
            p *= 2
            if p < c:
                bd = [unfold(lp[h].astype(BF16)) for h in heads]
            yield
        bdt = [unfold(xp[h].astype(BF16)) for h in heads]
        egc = [jnp.exp(gcol[h]) for h in heads]
        kf = [kb[h].astype(F32) for h in heads]
        rhs = [jnp.concatenate([kf[h] * (beta[h] * egc[h]), v_ref[rows, sls[h]].astype(F32) * beta[h]],
                               axis=1).astype(BF16) for h in heads]
        wu = [_dot(bdt[h], rhs[h]) for h in heads]
        yield
        au = [_dot(bda[h], wu[h].astype(BF16)) for h in heads]
        qt = [qb[h].astype(F32) * egc[h] - au[h][:, :DN_DK] for h in heads]
        out["kd"] = [(kf[h] * jnp.exp(gbv[:, grp(g_rest, h)])).astype(BF16) for h in heads]
        out["wqt"] = [jnp.concatenate([wu[h][:, :DN_DK], qt[h]], axis=1).astype(BF16) for h in heads]
        out["u"] = [wu[h][:, DN_DK:] for h in heads]
        out["o0"] = [au[h][:, DN_DK:] for h in heads]
        out["gbt"] = gbt
        yield

    def scan(r0, pre, state):
        order = range(nchunk - 1, -1, -1) if reverse else range(nchunk)
        for cc in order:
            rows = slice(cc * c, (cc + 1) * c)
            out_rows = slice(r0 + cc * c, r0 + (cc + 1) * c)
            for h in heads:
                wqt = pre["wqt"][h]
                lhs = jnp.concatenate([wqt[rows, :DN_DK], wqt[rows, DN_DK:]], axis=0)
                wq = _dot(lhs, state[h].astype(BF16))
                vn = pre["u"][h][rows] - wq[:c]
                o = wq[c:] + pre["o0"][h][rows]
                tot = pre["gbt"][grp(g_tot, h), cc * c:cc * c + 1]
                state[h] = state[h] * jnp.exp(tot) + lax.dot_general(
                    pre["kd"][h][rows], vn.astype(BF16), (((0,), (0,)), ((), ())),
                    preferred_element_type=F32)
                if reverse:
                    o = o + of_ref[out_rows, sls[h]]
                    o = o * lax.rsqrt(jnp.mean(o * o, axis=-1, keepdims=True) + EPS) * ng_ref[...]
                    zc = z_ref[out_rows, sls[h]].astype(F32)
                    o = o * (zc * _sigmoid(zc))
                o_ref[out_rows, sls[h]] = o.astype(o_ref.dtype)
            yield

    nsub = block // sub
    starts = [r * sub for r in (range(nsub - 1, -1, -1) if reverse else range(nsub))]
    state = [s_ref[h] for h in heads]
    pre_prev = None
    for idx in range(nsub + 1):
        pre = {}
        prep_gen = prepare(starts[idx], pre) if idx < nsub else iter(())
        scan_gen = scan(starts[idx - 1], pre_prev, state) if idx > 0 else iter(())
        prep_live = scan_live = True
        while prep_live or scan_live:
            if prep_live:
                prep_live = next(prep_gen, _DONE) is not _DONE
                if prep_live:
                    prep_live = next(prep_gen, _DONE) is not _DONE
            if scan_live:
                scan_live = next(scan_gen, _DONE) is not _DONE
        pre_prev = pre
    for h in heads:
        s_ref[h] = state[h]


_DONE = object()


def _dn_call(q, k, v, gb, extra, reverse, seq_len):
    n = q.shape[0]
    block = DN_BLOCK
    nblk = n // block
    imap = (lambda i: (nblk - 1 - i, 0)) if reverse else (lambda i: (i, 0))
    tok = lambda w: pl.BlockSpec((block, w), imap)
    in_specs = [tok(512), tok(512), tok(512), tok(LANES)]
    args = [q, k, v, gb]
    if reverse:
        o_f, z, ng = extra
        in_specs += [tok(512), tok(512), pl.BlockSpec(ng.shape, lambda i: (0, 0))]
        args += [o_f, z, ng]
    kern = functools.partial(_dn_kernel, block=block, sub=DN_SUB, reverse=reverse,
                             seq_len=seq_len, nblk=nblk)
    return pl.pallas_call(
        kern, grid=(nblk,), in_specs=in_specs, out_specs=tok(512),
        out_shape=jax.ShapeDtypeStruct((n, 512), BF16 if reverse else F32),
        scratch_shapes=[pltpu.VMEM((DN_HEADS, DN_DK, DN_DK), F32)],
        compiler_params=_cparams(("arbitrary",)),
        name="deltanet_bwd" if reverse else "deltanet_fwd")(*args)


def _route_kernel(x_ref, ya_ref, yb_ref, wo_ref, g2_ref, rw_ref, rb_ref, tri_ref,
                  x2_ref, xa_ref, xb_ref, info_ref, infot_ref, cnt_ref, run_ref, *, tile):
    i = pl.program_id(0)

    @pl.when(i == 0)
    def _():
        run_ref[...] = jnp.zeros_like(run_ref)

    sub = tri_ref.shape[0]
    parts = [slice(r, r + sub) for r in range(0, tile, sub)]
    each = lambda f, *lists: [f(*a) for a in zip(*lists)]
    half = wo_ref.shape[0] // 2
    x2 = [x_ref[rows, :] + _dot(ya_ref[rows, :], wo_ref[0:half, :])
          + _dot(yb_ref[rows, :], wo_ref[half:, :]) for rows in parts]
    for rows, v in zip(parts, x2):
        x2_ref[rows, :] = v.astype(x2_ref.dtype)
    xn = [_rms_rows(v, g2_ref[...]) for v in x2]
    for rows, v in zip(parts, xn):
        words = _pack_pairs(v)
        xa_ref[rows, :] = words[:, :ROW_WORDS]
        xb_ref[rows, :] = words[:, ROW_WORDS:]
    logits = [_dot(v.astype(BF16), rw_ref[...]) + rb_ref[...] for v in xn]
    lane = lax.broadcasted_iota(jnp.int32, (sub, LANES), 1).astype(F32)
    neg = jnp.float32(-jnp.inf)
    row_max = lambda v: jnp.max(v, axis=-1, keepdims=True)
    first_at = lambda v, m: jnp.min(jnp.where(v == m, lane, float(LANES)), axis=-1, keepdims=True)

    gl = [jnp.where(lane < N_GROUPS, v, neg) for v in logits]
    gmax = each(row_max, gl)
    gidx = each(first_at, gl, gmax)
    g_w = [1.0 / jnp.sum(jnp.exp(v - m), axis=-1, keepdims=True) for v, m in zip(gl, gmax)]
    lo = [N_GROUPS + g * EXPERTS_PER_GROUP for g in gidx]
    el = [jnp.where((lane >= a) & (lane < a + EXPERTS_PER_GROUP), v, neg) for v, a in zip(logits, lo)]
    m1 = each(row_max, el)
    i1 = each(first_at, el, m1)
    el2 = [jnp.where(lane == a, neg, v) for v, a in zip(el, i1)]
    m2 = each(row_max, el2)
    i2 = each(first_at, el2, m2)
    hit1 = [lane == a - N_GROUPS for a in i1]
    hit2 = [lane == a - N_GROUPS for a in i2]
    onehot = [(a | b).astype(BF16) for a, b in zip(hit1, hit2)]
    inside = [_dot(tri_ref[...], v) for v in onehot]
    run = run_ref[...]
    for k, rows in enumerate(parts):
        before = inside[k] + run
        pos1 = jnp.sum(jnp.where(hit1[k], before, 0.0), axis=-1, keepdims=True)
        pos2 = jnp.sum(jnp.where(hit2[k], before, 0.0), axis=-1, keepdims=True)
        run = run + jnp.sum(onehot[k].astype(F32), axis=0, keepdims=True)
        r = jnp.exp(m2[k] - m1[k])
        gate1 = g_w[k] / (1.0 + r)
        info = jnp.where(lane == 0, i1[k] - N_GROUPS,
               jnp.where(lane == 1, i2[k] - N_GROUPS,
               jnp.where(lane == 2, gate1,
               jnp.where(lane == 3, gate1 * r,
               jnp.where(lane == 4, pos1,
               jnp.where(lane == 5, pos2, 0.0))))))
        info_ref[rows, :] = info
        infot_ref[:, rows] = info.T[0:HALO, :]
    run_ref[...] = run
    cnt_ref[...] = run


def _route_call(x, ya, yb, wts):
    n, d = x.shape
    tile = ROUTE_TILE
    const = lambda i: (0, 0)
    tok = lambda w: pl.BlockSpec((tile, w), lambda i: (i, 0))
    in_specs = [tok(d), tok(512), tok(512)] + [pl.BlockSpec(w.shape, const) for w in wts]
    out_shape = [jax.ShapeDtypeStruct((n, d), BF16), jax.ShapeDtypeStruct((n, ROW_WORDS), U32),
                 jax.ShapeDtypeStruct((n, ROW_WORDS), U32),
                 jax.ShapeDtypeStruct((n, LANES), F32), jax.ShapeDtypeStruct((HALO, n), F32),
                 jax.ShapeDtypeStruct((1, LANES), F32)]
    out_specs = [tok(d), tok(ROW_WORDS), tok(ROW_WORDS), tok(LANES),
                 pl.BlockSpec((HALO, tile), lambda i: (0, i)), pl.BlockSpec((1, LANES), const)]
    kern = functools.partial(_route_kernel, tile=tile)
    return pl.pallas_call(
        kern, grid=(n // tile,), in_specs=in_specs, out_specs=out_specs, out_shape=out_shape,
        scratch_shapes=[pltpu.VMEM((1, LANES), F32)],
        compiler_params=_cparams(("arbitrary",)), name="oproj_router")(x, ya, yb, *wts)


def _sc_mesh():
    return plsc.VectorSubcoreMesh(core_axis_name="core", subcore_axis_name="subcore")


def _sc_scatter_rows(x, idx0, idx1, rows):
    n, w = x.shape

    @pl.kernel(out_type=jax.ShapeDtypeStruct((rows, w), x.dtype), mesh=_sc_mesh(), scratch_types=[])
    def scatter(x_hbm, i0_hbm, i1_hbm, o_hbm):
        def body(x_vmem, i0_vmem, i1_vmem):
            pltpu.sync_copy(x_vmem, o_hbm.at[i0_vmem.at[0]])
            pltpu.sync_copy(x_vmem, o_hbm.at[i1_vmem.at[0]])

        pltpu.emit_pipeline(
            body, grid=(n // SC_WINDOW,),
            in_specs=[pl.BlockSpec((SC_WINDOW, w), lambda i: (i, 0)),
                      pl.BlockSpec((1, SC_WINDOW), lambda i: (0, i)),
                      pl.BlockSpec((1, SC_WINDOW), lambda i: (0, i))],
            out_specs=[], core_axis_name=("core", "subcore"),
            dimension_semantics=(pltpu.PARALLEL,))(x_hbm, i0_hbm, i1_hbm)

    return scatter(x, idx0, idx1)


def _sc_gather_rows(y, idx):
    m = idx.shape[1]
    w = y.shape[1]

    @pl.kernel(out_type=jax.ShapeDtypeStruct((m, w), y.dtype), mesh=_sc_mesh(), scratch_types=[])
    def gather(y_hbm, i_hbm, o_hbm):
        def body(i_vmem, o_vmem):
            pltpu.sync_copy(y_hbm.at[i_vmem.at[0]], o_vmem)

        pltpu.emit_pipeline(
            body, grid=(m // SC_WINDOW,),
            in_specs=[pl.BlockSpec((1, SC_WINDOW), lambda i: (0, i))],
            out_specs=[pl.BlockSpec((SC_WINDOW, w), lambda i: (i, 0))],
            core_axis_name=("core", "subcore"),
            dimension_semantics=(pltpu.PARALLEL,))(i_hbm, o_hbm)

    return gather(y, idx)


def _join_rows(wa, wb):
    lo_a, hi_a = _unpack_pairs(wa)
    lo_b, hi_b = _unpack_pairs(wb)
    return jnp.concatenate([lo_a, lo_b, hi_a, hi_b], axis=1)


def _expert_kernel(te_ref, nx_ref, nv_ref, nu_ref, xa_ref, xb_ref, w1_hbm, w3_hbm, w2_hbm,
                   ya_ref, yb_ref, stage1, stage3, stage2, w1_ref, w3_ref, w2_ref, sem):
    j = pl.program_id(0)

    def fetch(e):
        return [pltpu.make_async_copy(src.at[e], dst, sem.at[k]) for k, (src, dst) in
                enumerate(((w1_hbm, stage1), (w3_hbm, stage3), (w2_hbm, stage2)))]

    @pl.when(j == 0)
    def _():
        for cp in fetch(te_ref[0]):
            cp.start()

    @pl.when((j < nu_ref[0]) & (nx_ref[j] > -2))
    def _():
        for cp in fetch(te_ref[j]):
            cp.wait()
        w1_ref[...] = stage1[...].astype(BF16)
        w3_ref[...] = stage3[...].astype(BF16)
        w2_ref[...] = stage2[...].astype(BF16)

        @pl.when(nx_ref[j] >= 0)
        def _():
            for cp in fetch(nx_ref[j]):
                cp.start()

    @pl.when(j < nu_ref[0])
    def _():
        live = lax.broadcasted_iota(jnp.int32, xa_ref.shape, 0) < nv_ref[j]
        zero = jnp.zeros(xa_ref.shape, U32)
        x = _join_rows(jnp.where(live, xa_ref[...], zero),
                       jnp.where(live, xb_ref[...], zero)).astype(BF16)
        h1 = _dot(x, w1_ref[...])
        h3 = _dot(x, w3_ref[...])
        hdn = (h1 * _sigmoid(h1) * h3).astype(BF16)
        words = _pack_pairs(_dot(hdn, w2_ref[...]))
        ya_ref[...] = words[:, :ROW_WORDS]
        yb_ref[...] = words[:, ROW_WORDS:]


def _expert_call(tile_expert, tile_next, tile_valid, n_used, xa, xb, w1, w3, w2):
    rows, rw = xa.shape
    tm = EXPERT_TILE
    d, ff = w1.shape[1], w1.shape[2]
    row_blk = lambda j, te, nx, nv, nu: (jnp.minimum(j, nu[0] - 1), 0)
    hbm = pl.BlockSpec(memory_space=pl.ANY)
    grid_spec = pltpu.PrefetchScalarGridSpec(
        num_scalar_prefetch=4, grid=(rows // tm,),
        in_specs=[pl.BlockSpec((tm, rw), row_blk), pl.BlockSpec((tm, rw), row_blk), hbm, hbm, hbm],
        out_specs=[pl.BlockSpec((tm, rw), row_blk), pl.BlockSpec((tm, rw), row_blk)],
        scratch_shapes=[pltpu.VMEM((d, ff), w1.dtype), pltpu.VMEM((d, ff), w3.dtype),
                        pltpu.VMEM((ff, d), w2.dtype),
                        pltpu.VMEM((d, ff), BF16), pltpu.VMEM((d, ff), BF16), pltpu.VMEM((ff, d), BF16),
                        pltpu.SemaphoreType.DMA((3,))])
    return pl.pallas_call(
        _expert_kernel, grid_spec=grid_spec,
        out_shape=[jax.ShapeDtypeStruct((rows, rw), U32), jax.ShapeDtypeStruct((rows, rw), U32)],
        compiler_params=_cparams(("arbitrary",)), name="moe_experts")(
            tile_expert, tile_next, tile_valid, n_used, xa, xb, w1, w3, w2)


def _combine_kernel(x2_ref, info_ref, gf_ref, a0_ref, b0_ref, a1_ref, b1_ref, y_ref):
    info = info_ref[...]
    moe = (info[:, 2:3] * _join_rows(a0_ref[...], b0_ref[...])
           + info[:, 3:4] * _join_rows(a1_ref[...], b1_ref[...]))
    y_ref[...] = _rms_rows(x2_ref[...].astype(F32) + moe, gf_ref[...])


def _combine_call(x2, info, gf, ga, gb):
    n, d = x2.shape
    tile = COMBINE_TILE
    nt = n // tile
    first = pl.BlockSpec((tile, ROW_WORDS), lambda i: (i, 0))
    second = pl.BlockSpec((tile, ROW_WORDS), lambda i: (i + nt, 0))
    return pl.pallas_call(
        _combine_kernel, grid=(nt,),
        in_specs=[pl.BlockSpec((tile, d), lambda i: (i, 0)),
                  pl.BlockSpec((tile, LANES), lambda i: (i, 0)),
                  pl.BlockSpec(gf.shape, lambda i: (0, 0)),
                  first, first, second, second],
        out_specs=pl.BlockSpec((tile, d), lambda i: (i, 0)),
        out_shape=jax.ShapeDtypeStruct((n, d), F32),
        compiler_params=_cparams(("arbitrary",)), name="moe_combine")(
            x2, info, gf, ga, gb, ga, gb)


def _row(a):
    return a.reshape(1, -1).astype(F32)


def _prep_weights(norm1_g, w_in, conv_a_w, conv_a_norm_g, dn_conv_w, dn_a_log, dn_dt_bias, w_o,
                  norm2_g, rg_w, rg_b, re_w, re_b):
    row = _row
    n_ab = len(_GB_COLS)
    w_main = jnp.concatenate([w_in[:, :7 * CONV_WIDTH], w_in[:, 7 * CONV_WIDTH + _GB_COLS]], axis=1)
    w_main = jnp.pad(w_main, ((0, 0), (0, LANES - n_ab))).astype(BF16)
    cw = jnp.concatenate([conv_a_w, dn_conv_w], axis=1).astype(F32)
    grp = jnp.arange(CONV_WIDTH) // (CONV_WIDTH // CONV_GROUPS)
    gmat = ((grp[:, None] == grp[None, :]).astype(F32) / (CONV_WIDTH // CONV_GROUPS)).astype(BF16)
    is_a = (_GB_COLS < 2 * DN_HEADS)
    a_idx = jnp.where(is_a, _GB_COLS, 0)
    padl = lambda a: jnp.pad(a.reshape(1, -1).astype(F32), ((0, 0), (0, LANES - n_ab)))
    nea = padl(jnp.where(is_a, -jnp.exp(dn_a_log.astype(F32)).reshape(-1)[a_idx], 0.0))
    dtb = padl(jnp.where(is_a, dn_dt_bias.astype(F32).reshape(-1)[a_idx], 0.0))
    proj_w = [row(norm1_g), w_main, cw, row(conv_a_norm_g), gmat, nea, dtb]
    rw = jnp.pad(jnp.concatenate([rg_w, re_w], axis=1).astype(F32),
                 ((0, 0), (0, LANES - N_GROUPS - N_EXPERTS)))
    rb = jnp.pad(jnp.concatenate([rg_b, re_b]).reshape(1, -1).astype(F32),
                 ((0, 0), (0, LANES - N_GROUPS - N_EXPERTS)))
    tri = (jnp.arange(ROUTE_SUB)[:, None] > jnp.arange(ROUTE_SUB)[None, :]).astype(BF16)
    route_w = [w_o.astype(BF16), row(norm2_g), rw.astype(BF16), rb, tri]
    return proj_w, route_w


def _mix_and_route(x, seq_len, proj_w, dn_norm_g, route_w):
    ya, q, k, v, z, gb = _proj_call(x, seq_len, proj_w)
    o_f = _dn_call(q, k, v, gb, None, False, seq_len)
    yb = _dn_call(q, k, v, gb, (o_f, z, _row(dn_norm_g)), True, seq_len)
    return _route_call(x, ya, yb, route_w)


def _moe(routed, w1, w3, w2, final_g):
    x2, xna, xnb, info, info_t, cnt = routed
    n = x2.shape[0]
    tm = EXPERT_TILE
    counts = cnt[0, :N_EXPERTS].astype(jnp.int32)
    pcounts = (counts + tm - 1) // tm * tm
    pends = jnp.cumsum(pcounts)
    pstarts = pends - pcounts
    eid = jnp.arange(N_EXPERTS, dtype=jnp.int32)
    e12 = info_t[0:2].astype(jnp.int32)
    start12 = jnp.sum(jnp.where(e12[:, None, :] == eid[None, :, None], pstarts[None, :, None], 0),
                      axis=1)
    dest = start12 + info_t[4:6].astype(jnp.int32)
    dest0 = dest[0:1]
    dest1 = dest[1:2]
    n_tiles = 2 * n // tm + N_EXPERTS
    tile_start = jnp.arange(n_tiles, dtype=jnp.int32) * tm
    tile_expert = jnp.minimum(jnp.sum(pends[None, :] <= tile_start[:, None], axis=1),
                              N_EXPERTS - 1).astype(jnp.int32)
    of_tile = lambda table: jnp.sum(jnp.where(tile_expert[:, None] == eid, table, 0), axis=-1)
    tile_valid = jnp.clip(of_tile(pstarts + counts) - tile_start, 0, tm).astype(jnp.int32)
    n_used = (pends[-1:] // tm).astype(jnp.int32)
    later = (eid[None, :] > eid[:, None]) & (pcounts[None, :] > 0)
    next_e = jnp.min(jnp.where(later, eid[None, :], N_EXPERTS), axis=1)
    next_e = jnp.where(next_e == N_EXPERTS, -1, next_e)
    tile_next = jnp.where(tile_start == of_tile(pstarts), of_tile(next_e), -2).astype(jnp.int32)

    rows = n_tiles * tm
    xa = _sc_scatter_rows(xna, dest0, dest1, rows)
    xb = _sc_scatter_rows(xnb, dest0, dest1, rows)
    ya_e, yb_e = _expert_call(tile_expert, tile_next, tile_valid, n_used, xa, xb, w1, w3, w2)
    both = jnp.concatenate([dest0, dest1], axis=1)
    return _combine_call(x2, info, _row(final_g), _sc_gather_rows(ya_e, both),
                         _sc_gather_rows(yb_e, both))


def kernel(x_prompt, x_sample, norm1_g, w_in, conv_a_w, conv_a_norm_g, dn_conv_w, dn_a_log,
           dn_dt_bias, dn_norm_g, w_o, norm2_g, router_group_w, router_group_b, router_expert_w,
           router_expert_b, w1, w3, w2, final_norm_g):
    assert norm1_g.shape[0] == 1, "single-layer trunk"
    bp, sp, d = x_prompt.shape
    bs, ss, _ = x_sample.shape
    step = max(TOKEN_TILE, ROUTE_TILE, COMBINE_TILE, DN_BLOCK)
    assert sp % step == 0 and ss % step == 0, "sequence lengths must be whole grid steps"
    proj_w, route_w = _prep_weights(
        norm1_g[0], w_in[0], conv_a_w[0], conv_a_norm_g[0], dn_conv_w[0], dn_a_log[0],
        dn_dt_bias[0], w_o[0], norm2_g[0], router_group_w[0], router_group_b[0],
        router_expert_w[0], router_expert_b[0])
    routed = [_mix_and_route(x.reshape(-1, d), seq_len, proj_w, dn_norm_g[0], route_w)
              for x, seq_len in ((x_prompt, sp), (x_sample, ss))]
    y_p, y_s = [_moe(r, w1[0], w3[0], w2[0], final_norm_g) for r in routed]
    return y_p.reshape(bp, sp, d), y_s.reshape(bs, ss, d)
```

```python
import functools

import jax
import jax.numpy as jnp
import numpy as np
from jax import lax
from jax.experimental import pallas as pl
from jax.experimental.pallas import tpu as pltpu
from jax.experimental.pallas import tpu_sc as plsc

F32 = jnp.float32
BF16 = jnp.bfloat16
U32 = jnp.uint32

EPS = 1e-6
CONV_WIDTH = 512
CONV_GROUPS = 8
DN_HEADS = 4
DN_DK = 128
DN_V = 512
CHUNK = 64
N_GROUPS = 4
EXPERTS_PER_GROUP = 8
N_EXPERTS = N_GROUPS * EXPERTS_PER_GROUP
EXPERT_FF = 512
LANES = 128
HALO = 8

TOKEN_TILE = 512
COMBINE_TILE = 1024
ROUTE_TILE = 1024
ROUTE_SUB = 256
DN_BLOCK = 2048
DN_SUB = 256
EXPERT_TILE = 512
EXPERT_TILE_MIN = 256
ROW_WORDS = 256
SC_WINDOW = 128
VMEM_LIMIT = 56 * 1024 * 1024

_GB_COLS = np.concatenate([np.arange(4 * DN_HEADS), np.arange(2 * DN_HEADS), np.arange(2 * DN_HEADS)])


def _cparams(sem):
    return pltpu.CompilerParams(dimension_semantics=sem, vmem_limit_bytes=VMEM_LIMIT)


def _sigmoid(x):
    return 1.0 / (1.0 + jnp.exp(-x))


def _rms_rows(x, g):
    return x * lax.rsqrt(jnp.mean(x * x, axis=-1, keepdims=True) + EPS) * g


def _dot(a, b):
    return jnp.dot(a, b, preferred_element_type=F32)


def _pack_pairs(x):
    half = x.shape[1] // 2
    bits = lambda a: lax.bitcast_convert_type(a.astype(BF16).astype(F32), U32)
    return (bits(x[:, half:]) & jnp.uint32(0xFFFF0000)) | (bits(x[:, :half]) >> 16)


def _unpack_pairs(w):
    lo = lax.bitcast_convert_type(w << 16, F32)
    hi = lax.bitcast_convert_type(w & jnp.uint32(0xFFFF0000), F32)
    return lo, hi


def _proj_kernel(x_ref, xprev_ref, xnext_ref,
                 g1_ref, w_ref, cw_ref, cng_ref, gmat_ref, nea_ref, dtb_ref,
                 ya_ref, q_ref, k_ref, v_ref, z_ref, gb_ref,
                 hs_ref, su_ref, p_ref, *, tile, seq_len):
    tok0 = pl.program_id(0) * tile
    not_start = (tok0 % seq_len != 0).astype(F32)
    not_end = ((tok0 + tile) % seq_len != 0).astype(F32)
    g1 = g1_ref[...]
    hs_ref[0:HALO, :] = _rms_rows(xprev_ref[...], g1)
    hs_ref[HALO:HALO + tile, :] = _rms_rows(x_ref[...], g1)
    hs_ref[HALO + tile:, :] = _rms_rows(xnext_ref[...], g1)

    p_ref[...] = _dot(hs_ref[...].astype(BF16), w_ref[...])
    for stage in _FINISH_STAGES:
        stage(p_ref, su_ref, cw_ref, cng_ref, gmat_ref, nea_ref, dtb_ref,
              ya_ref, q_ref, k_ref, v_ref, z_ref, gb_ref, tile, not_start, not_end)


def _conv_cols(p_ref, su_ref, cw_ref, lo, width, tile, not_start, not_end):
    cw = CONV_WIDTH
    cols = slice(lo, lo + width)
    if lo < cw:
        su_ref[:, cols] = p_ref[:, cw + lo:cw + lo + width] * p_ref[:, 2 * cw + lo:2 * cw + lo + width]
    else:
        su_ref[:, cols] = p_ref[:, 2 * cw + lo:2 * cw + lo + width]
    su_ref[0:HALO, cols] = su_ref[0:HALO, cols] * not_start
    su_ref[HALO + tile:, cols] = su_ref[HALO + tile:, cols] * not_end
    u = su_ref[:, cols]
    total = tile + 2 * HALO
    return (pltpu.roll(u, 1, axis=0)[HALO:HALO + tile] * cw_ref[0:1, cols]
            + u[HALO:HALO + tile] * cw_ref[1:2, cols]
            + pltpu.roll(u, total - 1, axis=0)[HALO:HALO + tile] * cw_ref[2:3, cols])


def _stage_mixer_a(half):
    width = CONV_WIDTH // 2
    lo = half * width

    def stage(p_ref, su_ref, cw_ref, cng_ref, gmat_ref, nea_ref, dtb_ref,
              ya_ref, q_ref, k_ref, v_ref, z_ref, gb_ref, tile, not_start, not_end):
        conv = _conv_cols(p_ref, su_ref, cw_ref, lo, width, tile, not_start, not_end)
        ya = p_ref[HALO:HALO + tile, lo:lo + width] * conv
        ms = _dot((ya * ya).astype(BF16), gmat_ref[lo:lo + width, lo:lo + width])
        ya_ref[:, lo:lo + width] = (ya * lax.rsqrt(ms + EPS) * cng_ref[:, lo:lo + width]).astype(ya_ref.dtype)
    return stage


def _stage_mixer_b(which, half):
    width = DN_V // 2
    lo_out = half * width
    lo = CONV_WIDTH + which * DN_V + lo_out

    def stage(p_ref, su_ref, cw_ref, cng_ref, gmat_ref, nea_ref, dtb_ref,
              ya_ref, q_ref, k_ref, v_ref, z_ref, gb_ref, tile, not_start, not_end):
        x = _conv_cols(p_ref, su_ref, cw_ref, lo, width, tile, not_start, not_end)
        x = x * _sigmoid(x)
        out_ref = (q_ref, k_ref, v_ref)[which]
        if which == 2:
            out_ref[:, lo_out:lo_out + width] = x.astype(out_ref.dtype)
            return
        scale = DN_DK ** -0.5 if which == 0 else 1.0
        for h in range(width // DN_DK):
            xh = x[:, h * DN_DK:(h + 1) * DN_DK]
            xh = xh * (lax.rsqrt(jnp.sum(xh * xh, axis=-1, keepdims=True) + EPS) * scale)
            out_ref[:, lo_out + h * DN_DK:lo_out + (h + 1) * DN_DK] = xh.astype(out_ref.dtype)
    return stage


def _stage_gates(inner):
    def stage(p_ref, su_ref, cw_ref, cng_ref, gmat_ref, nea_ref, dtb_ref,
              ya_ref, q_ref, k_ref, v_ref, z_ref, gb_ref, tile, not_start, not_end):
        inner(p_ref, su_ref, cw_ref, cng_ref, gmat_ref, nea_ref, dtb_ref,
              ya_ref, q_ref, k_ref, v_ref, z_ref, gb_ref, tile, not_start, not_end)
        cw = CONV_WIDTH
        z_ref[...] = p_ref[HALO:HALO + tile, 6 * cw:7 * cw].astype(z_ref.dtype)
        ab = p_ref[HALO:HALO + tile, 7 * cw:7 * cw + LANES]
        xs = ab + dtb_ref[...]
        softplus = jnp.maximum(xs, 0.0) + jnp.log(1.0 + jnp.exp(-jnp.abs(xs)))
        g = nea_ref[...] * softplus
        beta = _sigmoid(ab)
        r = lax.broadcasted_iota(jnp.int32, (tile, LANES), 0) % CHUNK
        pre = g
        suf = g
        s = 1
        while s < CHUNK:
            pre = pre + jnp.where(r >= s, pltpu.roll(pre, s, axis=0), 0.0)
            suf = suf + jnp.where(r < CHUNK - s, pltpu.roll(suf, tile - s, axis=0), 0.0)
            s *= 2
        lane = lax.broadcasted_iota(jnp.int32, (tile, LANES), 1) // DN_HEADS
        gb_ref[...] = jnp.where(lane == 0, pre,
                      jnp.where(lane == 1, suf,
                      jnp.where(lane < 4, beta,
                      jnp.where(lane == 4, suf - g,
                      jnp.where(lane == 5, pre - g, pre + suf - g)))))
    return stage


_FINISH_STAGES = (_stage_mixer_a(0), _stage_mixer_a(1), _stage_mixer_b(0, 0), _stage_mixer_b(0, 1),
                  _stage_mixer_b(1, 0), _stage_mixer_b(1, 1), _stage_mixer_b(2, 0),
                  _stage_gates(_stage_mixer_b(2, 1)))


def _proj_call(x, seq_len, wts):
    n, d = x.shape
    tile = TOKEN_TILE
    nt = n // tile
    tb = tile // HALO
    kern = functools.partial(_proj_kernel, tile=tile, seq_len=seq_len)
    const = lambda i: (0, 0)
    in_specs = [pl.BlockSpec((tile, d), lambda i: (i, 0)),
                pl.BlockSpec((HALO, d), lambda i: (jnp.maximum(i * tb - 1, 0), 0)),
                pl.BlockSpec((HALO, d), lambda i: (jnp.minimum((i + 1) * tb, n // HALO - 1), 0))]
    in_specs += [pl.BlockSpec(w.shape, const, pipeline_mode=pl.Buffered(1)) for w in wts]
    widths = (512, 512, 512, 512, 512, LANES)
    dtypes = (BF16, BF16, BF16, BF16, BF16, F32)
    rows = tile + 2 * HALO
    return pl.pallas_call(
        kern, grid=(nt,), in_specs=in_specs,
        out_specs=[pl.BlockSpec((tile, w), lambda i: (i, 0)) for w in widths],
        out_shape=[jax.ShapeDtypeStruct((n, w), dt) for w, dt in zip(widths, dtypes)],
        scratch_shapes=[pltpu.VMEM((rows, d), F32), pltpu.VMEM((rows, 4 * CONV_WIDTH), F32),
                        pltpu.VMEM((rows, wts[1].shape[1]), F32)],
        compiler_params=_cparams(("arbitrary",)), name="proj_conv")(x, x, x, *wts)


def _dn_kernel(*refs, block, sub, reverse, seq_len, nblk):
    if reverse:
        (q_ref, k_ref, v_ref, gb_ref, of_ref, z_ref, ng_ref, o_ref, s_ref) = refs
    else:
        (q_ref, k_ref, v_ref, gb_ref, o_ref, s_ref) = refs
    i = pl.program_id(0)
    b = nblk - 1 - i if reverse else i
    edge = (b + 1) * block if reverse else b * block

    @pl.when(edge % seq_len == 0)
    def _():
        s_ref[...] = jnp.zeros_like(s_ref)

    nchunk = sub // CHUNK
    c = CHUNK
    heads = range(DN_HEADS)
    ri = lax.broadcasted_iota(jnp.int32, (sub, sub), 0)
    ci = lax.broadcasted_iota(jnp.int32, (sub, sub), 1)
    same = (ri // c) == (ci // c)
    incl = same & ((ri <= ci) if reverse else (ri >= ci))
    strict = same & ((ri < ci) if reverse else (ri > ci))
    pr = lax.broadcasted_iota(jnp.int32, (c, sub), 0)
    pc = lax.broadcasted_iota(jnp.int32, (c, sub), 1)
    eye_p = (pr == pc % c).astype(F32)
    lane_chunk = pc // c

    def fold(bd):
        out = bd[0:c]
        for r in range(1, nchunk):
            out = out + bd[r * c:(r + 1) * c]
        return out

    def unfold(p):
        zero = jnp.zeros_like(p)
        return jnp.concatenate([jnp.where(lane_chunk == r, p, zero) for r in range(nchunk)], axis=0)

    grp = lambda g, h: slice(g * DN_HEADS + h, g * DN_HEADS + h + 1)
    g_cum, g_beta, g_rest, g_tot = (1, 3, 5, 7) if reverse else (0, 2, 4, 6)
    sls = [slice(h * DN_DK, (h + 1) * DN_DK) for h in heads]

    def prepare(r0, out):
        rows = slice(r0, r0 + sub)
        gbv = gb_ref[rows, :]
        gbt = gbv.T
        kb = [k_ref[rows, sl] for sl in sls]
        qb = [q_ref[rows, sl] for sl in sls]
        gcol = [gbv[:, grp(g_cum, h)] for h in heads]
        beta = [gbv[:, grp(g_beta, h)] for h in heads]
        kq = [lax.dot_general(jnp.concatenate([kb[h], qb[h]], axis=0), kb[h],
                              (((1,), (1,)), ((), ())), preferred_element_type=F32) for h in heads]
        yield
        decay = [jnp.exp(jnp.minimum(gcol[h] - gbt[grp(g_cum, h), :], 0.0)) for h in heads]
        bdl = [jnp.where(strict, kq[h][:sub] * decay[h], 0.0) * beta[h] for h in heads]
        bda = [jnp.where(incl, kq[h][sub:] * decay[h], 0.0).astype(BF16) for h in heads]
        yield
        lp = [fold(bdl[h]) for h in heads]
        xp = [eye_p - lp[h] for h in heads]
        bd = [bdl[h].astype(BF16) for h in heads]
        p = 1
        while p < c:
            if p == 1:
                lp = [_dot(lp[h].astype(BF16), bd[h]) for h in heads]
            elif 2 * p < c:
                res = [_dot(jnp.concatenate([lp[h], xp[h]], axis=0).astype(BF16), bd[h]) for h in heads]
                lp = [r[:c] for r in res]
                xp = [xp[h] + res[h][c:] for h in heads]
            else:
                xp = [xp[h] + _dot(xp[h].astype(BF16), bd[h]) for h in heads]
            p *= 2
            if p < c:
                bd = [unfold(lp[h].astype(BF16)) for h in heads]
            yield
        bdt = [unfold(xp[h].astype(BF16)) for h in heads]
        egc = [jnp.exp(gcol[h]) for h in heads]
        kf = [kb[h].astype(F32) for h in heads]
        rhs = [jnp.concatenate([kf[h] * (beta[h] * egc[h]), v_ref[rows, sls[h]].astype(F32) * beta[h]],
                               axis=1).astype(BF16) for h in heads]
        wu = [_dot(bdt[h], rhs[h]) for h in heads]
        yield
        au = [_dot(bda[h], wu[h].astype(BF16)) for h in heads]
        qt = [qb[h].astype(F32) * egc[h] - au[h][:, :DN_DK] for h in heads]
        out["kd"] = [(kf[h] * jnp.exp(gbv[:, grp(g_rest, h)])).astype(BF16) for h in heads]
        out["wqt"] = [jnp.concatenate([wu[h][:, :DN_DK], qt[h]], axis=1).astype(BF16) for h in heads]
        out["u"] = [wu[h][:, DN_DK:] for h in heads]
        out["o0"] = [au[h][:, DN_DK:] for h in heads]
        out["gbt"] = gbt
        yield

    def scan(r0, pre, state):
        order = range(nchunk - 1, -1, -1) if reverse else range(nchunk)
        for cc in order:
            rows = slice(cc * c, (cc + 1) * c)
            out_rows = slice(r0 + cc * c, r0 + (cc + 1) * c)
            for h in heads:
                wqt = pre["wqt"][h]
                lhs = jnp.concatenate([wqt[rows, :DN_DK], wqt[rows, DN_DK:]], axis=0)
                wq = _dot(lhs, state[h].astype(BF16))
                vn = pre["u"][h][rows] - wq[:c]
                o = wq[c:] + pre["o0"][h][rows]
                tot = pre["gbt"][grp(g_tot, h), cc * c:cc * c + 1]
                state[h] = state[h] * jnp.exp(tot) + lax.dot_general(
                    pre["kd"][h][rows], vn.astype(BF16), (((0,), (0,)), ((), ())),
                    preferred_element_type=F32)
                if reverse:
                    o = o + of_ref[out_rows, sls[h]]
                    o = o * lax.rsqrt(jnp.mean(o * o, axis=-1, keepdims=True) + EPS) * ng_ref[...]
                    zc = z_ref[out_rows, sls[h]].astype(F32)
                    o = o * (zc * _sigmoid(zc))
                o_ref[out_rows, sls[h]] = o.astype(o_ref.dtype)
            yield

    nsub = block // sub
    starts = [r * sub for r in (range(nsub - 1, -1, -1) if reverse else range(nsub))]
    state = [s_ref[h] for h in heads]
    pre_prev = None
    for idx in range(nsub + 1):
        pre = {}
        prep_gen = prepare(starts[idx], pre) if idx < nsub else iter(())
        scan_gen = scan(starts[idx - 1], pre_prev, state) if idx > 0 else iter(())
        prep_live = scan_live = True
        while prep_live or scan_live:
            if prep_live:
                prep_live = next(prep_gen, _DONE) is not _DONE
                if prep_live:
                    prep_live = next(prep_gen, _DONE) is not _DONE
            if scan_live:
                scan_live = next(scan_gen, _DONE) is not _DONE
        pre_prev = pre
    for h in heads:
        s_ref[h] = state[h]


_DONE = object()


def _dn_call(q, k, v, gb, extra, reverse, seq_len):
    n = q.shape[0]
    block = DN_BLOCK
    nblk = n // block
    imap = (lambda i: (nblk - 1 - i, 0)) if reverse else (lambda i: (i, 0))
    tok = lambda w: pl.BlockSpec((block, w), imap)
    in_specs = [tok(512), tok(512), tok(512), tok(LANES)]
    args = [q, k, v, gb]
    if reverse:
        o_f, z, ng = extra
        in_specs += [tok(512), tok(512), pl.BlockSpec(ng.shape, lambda i: (0, 0))]
        args += [o_f, z, ng]
    kern = functools.partial(_dn_kernel, block=block, sub=DN_SUB, reverse=reverse,
                             seq_len=seq_len, nblk=nblk)
    return pl.pallas_call(
        kern, grid=(nblk,), in_specs=in_specs, out_specs=tok(512),
        out_shape=jax.ShapeDtypeStruct((n, 512), BF16 if reverse else F32),
        scratch_shapes=[pltpu.VMEM((DN_HEADS, DN_DK, DN_DK), F32)],
        compiler_params=_cparams(("arbitrary",)),
        name="deltanet_bwd" if reverse else "deltanet_fwd")(*args)


def _route_kernel(x_ref, ya_ref, yb_ref, wo_ref, g2_ref, rw_ref, rb_ref, tri_ref,
                  x2_ref, xa_ref, xb_ref, info_ref, infot_ref, cnt_ref, run_ref, *, tile):
    i = pl.program_id(0)

    @pl.when(i == 0)
    def _():
        run_ref[...] = jnp.zeros_like(run_ref)

    sub = tri_ref.shape[0]
    parts = [slice(r, r + sub) for r in range(0, tile, sub)]
    each = lambda f, *lists: [f(*a) for a in zip(*lists)]
    half = wo_ref.shape[0] // 2
    x2 = [x_ref[rows, :] + _dot(ya_ref[rows, :], wo_ref[0:half, :])
          + _dot(yb_ref[rows, :], wo_ref[half:, :]) for rows in parts]
    for rows, v in zip(parts, x2):
        x2_ref[rows, :] = v.astype(x2_ref.dtype)
    xn = [_rms_rows(v, g2_ref[...]) for v in x2]
    for rows, v in zip(parts, xn):
        words = _pack_pairs(v)
        xa_ref[rows, :] = words[:, :ROW_WORDS]
        xb_ref[rows, :] = words[:, ROW_WORDS:]
    logits = [_dot(v.astype(BF16), rw_ref[...]) + rb_ref[...] for v in xn]
    lane = lax.broadcasted_iota(jnp.int32, (sub, LANES), 1).astype(F32)
    neg = jnp.float32(-jnp.inf)
    row_max = lambda v: jnp.max(v, axis=-1, keepdims=True)
    first_at = lambda v, m: jnp.min(jnp.where(v == m, lane, float(LANES)), axis=-1, keepdims=True)

    gl = [jnp.where(lane < N_GROUPS, v, neg) for v in logits]
    gmax = each(row_max, gl)
    gidx = each(first_at, gl, gmax)
    g_w = [1.0 / jnp.sum(jnp.exp(v - m), axis=-1, keepdims=True) for v, m in zip(gl, gmax)]
    lo = [N_GROUPS + g * EXPERTS_PER_GROUP for g in gidx]
    el = [jnp.where((lane >= a) & (lane < a + EXPERTS_PER_GROUP), v, neg) for v, a in zip(logits, lo)]
    m1 = each(row_max, el)
    i1 = each(first_at, el, m1)
    el2 = [jnp.where(lane == a, neg, v) for v, a in zip(el, i1)]
    m2 = each(row_max, el2)
    i2 = each(first_at, el2, m2)
    hit1 = [lane == a - N_GROUPS for a in i1]
    hit2 = [lane == a - N_GROUPS for a in i2]
    onehot = [(a | b).astype(BF16) for a, b in zip(hit1, hit2)]
    inside = [_dot(tri_ref[...], v) for v in onehot]
    run = run_ref[...]
    for k, rows in enumerate(parts):
        before = inside[k] + run
        pos1 = jnp.sum(jnp.where(hit1[k], before, 0.0), axis=-1, keepdims=True)
        pos2 = jnp.sum(jnp.where(hit2[k], before, 0.0), axis=-1, keepdims=True)
        run = run + jnp.sum(onehot[k].astype(F32), axis=0, keepdims=True)
        r = jnp.exp(m2[k] - m1[k])
        gate1 = g_w[k] / (1.0 + r)
        info = jnp.where(lane == 0, i1[k] - N_GROUPS,
               jnp.where(lane == 1, i2[k] - N_GROUPS,
               jnp.where(lane == 2, gate1,
               jnp.where(lane == 3, gate1 * r,
               jnp.where(lane == 4, pos1,
               jnp.where(lane == 5, pos2, 0.0))))))
        info_ref[rows, :] = info
        infot_ref[:, rows] = info.T[0:HALO, :]
    run_ref[...] = run
    cnt_ref[...] = run


def _route_call(x, ya, yb, wts):
    n, d = x.shape
    tile = ROUTE_TILE
    const = lambda i: (0, 0)
    tok = lambda w: pl.BlockSpec((tile, w), lambda i: (i, 0))
    in_specs = [tok(d), tok(512), tok(512)] + [pl.BlockSpec(w.shape, const) for w in wts]
    out_shape = [jax.ShapeDtypeStruct((n, d), BF16), jax.ShapeDtypeStruct((n, ROW_WORDS), U32),
                 jax.ShapeDtypeStruct((n, ROW_WORDS), U32),
                 jax.ShapeDtypeStruct((n, LANES), F32), jax.ShapeDtypeStruct((HALO, n), F32),
                 jax.ShapeDtypeStruct((1, LANES), F32)]
    out_specs = [tok(d), tok(ROW_WORDS), tok(ROW_WORDS), tok(LANES),
                 pl.BlockSpec((HALO, tile), lambda i: (0, i)), pl.BlockSpec((1, LANES), const)]
    kern = functools.partial(_route_kernel, tile=tile)
    return pl.pallas_call(
        kern, grid=(n // tile,), in_specs=in_specs, out_specs=out_specs, out_shape=out_shape,
        scratch_shapes=[pltpu.VMEM((1, LANES), F32)],
        compiler_params=_cparams(("arbitrary",)), name="oproj_router")(x, ya, yb, *wts)


def _sc_mesh():
    return plsc.VectorSubcoreMesh(core_axis_name="core", subcore_axis_name="subcore")


def _sc_scatter_rows(x, idx0, idx1, rows):
    n, w = x.shape

    @pl.kernel(out_type=jax.ShapeDtypeStruct((rows, w), x.dtype), mesh=_sc_mesh(), scratch_types=[])
    def scatter(x_hbm, i0_hbm, i1_hbm, o_hbm):
        def body(x_vmem, i0_vmem, i1_vmem):
            pltpu.sync_copy(x_vmem, o_hbm.at[i0_vmem.at[0]])
            pltpu.sync_copy(x_vmem, o_hbm.at[i1_vmem.at[0]])

        pltpu.emit_pipeline(
            body, grid=(n // SC_WINDOW,),
            in_specs=[pl.BlockSpec((SC_WINDOW, w), lambda i: (i, 0)),
                      pl.BlockSpec((1, SC_WINDOW), lambda i: (0, i)),
                      pl.BlockSpec((1, SC_WINDOW), lambda i: (0, i))],
            out_specs=[], core_axis_name=("core", "subcore"),
            dimension_semantics=(pltpu.PARALLEL,))(x_hbm, i0_hbm, i1_hbm)

    return scatter(x, idx0, idx1)


def _sc_gather_rows(y, idx):
    m = idx.shape[1]
    w = y.shape[1]

    @pl.kernel(out_type=jax.ShapeDtypeStruct((m, w), y.dtype), mesh=_sc_mesh(), scratch_types=[])
    def gather(y_hbm, i_hbm, o_hbm):
        def body(i_vmem, o_vmem):
            pltpu.sync_copy(y_hbm.at[i_vmem.at[0]], o_vmem)

        pltpu.emit_pipeline(
            body, grid=(m // SC_WINDOW,),
            in_specs=[pl.BlockSpec((1, SC_WINDOW), lambda i: (0, i))],
            out_specs=[pl.BlockSpec((SC_WINDOW, w), lambda i: (i, 0))],
            core_axis_name=("core", "subcore"),
            dimension_semantics=(pltpu.PARALLEL,))(i_hbm, o_hbm)

    return gather(y, idx)


def _join_rows(wa, wb):
    lo_a, hi_a = _unpack_pairs(wa)
    lo_b, hi_b = _unpack_pairs(wb)
    return jnp.concatenate([lo_a, lo_b, hi_a, hi_b], axis=1)


def _expert_kernel(te_ref, nx_ref, nv_ref, nu_ref, xa_ref, xb_ref, w1_hbm, w3_hbm, w2_hbm,
                   ya_ref, yb_ref, stage1, stage3, stage2, w1_ref, w3_ref, w2_ref, sem):
    j = pl.program_id(0)

    def fetch(e):
        return [pltpu.make_async_copy(src.at[e], dst, sem.at[k]) for k, (src, dst) in
                enumerate(((w1_hbm, stage1), (w3_hbm, stage3), (w2_hbm, stage2)))]

    @pl.when(j == 0)
    def _():
        for cp in fetch(te_ref[0]):
            cp.start()

    @pl.when((j < nu_ref[0]) & (nx_ref[j] > -2))
    def _():
        for cp in fetch(te_ref[j]):
            cp.wait()
        w1_ref[...] = stage1[...].astype(BF16)
        w3_ref[...] = stage3[...].astype(BF16)
        w2_ref[...] = stage2[...].astype(BF16)

        @pl.when(nx_ref[j] >= 0)
        def _():
            for cp in fetch(nx_ref[j]):
                cp.start()

    @pl.when(j < nu_ref[0])
    def _():
        live = lax.broadcasted_iota(jnp.int32, xa_ref.shape, 0) < nv_ref[j]
        zero = jnp.zeros(xa_ref.shape, U32)
        x = _join_rows(jnp.where(live, xa_ref[...], zero),
                       jnp.where(live, xb_ref[...], zero)).astype(BF16)
        h1 = _dot(x, w1_ref[...])
        h3 = _dot(x, w3_ref[...])
        hdn = (h1 * _sigmoid(h1) * h3).astype(BF16)
        words = _pack_pairs(_dot(hdn, w2_ref[...]))
        ya_ref[...] = words[:, :ROW_WORDS]
        yb_ref[...] = words[:, ROW_WORDS:]


def _expert_call(tm, tile_expert, tile_next, tile_valid, n_used, xa, xb, w1, w3, w2):
    rows, rw = xa.shape
    d, ff = w1.shape[1], w1.shape[2]
    row_blk = lambda j, te, nx, nv, nu: (jnp.minimum(j, nu[0] - 1), 0)
    hbm = pl.BlockSpec(memory_space=pl.ANY)
    grid_spec = pltpu.PrefetchScalarGridSpec(
        num_scalar_prefetch=4, grid=(rows // tm,),
        in_specs=[pl.BlockSpec((tm, rw), row_blk), pl.BlockSpec((tm, rw), row_blk), hbm, hbm, hbm],
        out_specs=[pl.BlockSpec((tm, rw), row_blk), pl.BlockSpec((tm, rw), row_blk)],
        scratch_shapes=[pltpu.VMEM((d, ff), w1.dtype), pltpu.VMEM((d, ff), w3.dtype),
                        pltpu.VMEM((ff, d), w2.dtype),
                        pltpu.VMEM((d, ff), BF16), pltpu.VMEM((d, ff), BF16), pltpu.VMEM((ff, d), BF16),
                        pltpu.SemaphoreType.DMA((3,))])
    return pl.pallas_call(
        _expert_kernel, grid_spec=grid_spec,
        out_shape=[jax.ShapeDtypeStruct((rows, rw), U32), jax.ShapeDtypeStruct((rows, rw), U32)],
        compiler_params=_cparams(("arbitrary",)), name="moe_experts")(
            tile_expert, tile_next, tile_valid, n_used, xa, xb, w1, w3, w2)


def _combine_kernel(x2_ref, info_ref, gf_ref, a0_ref, b0_ref, a1_ref, b1_ref, y_ref):
    info = info_ref[...]
    moe = (info[:, 2:3] * _join_rows(a0_ref[...], b0_ref[...])
           + info[:, 3:4] * _join_rows(a1_ref[...], b1_ref[...]))
    y_ref[...] = _rms_rows(x2_ref[...].astype(F32) + moe, gf_ref[...])


def _combine_call(x2, info, gf, ga, gb):
    n, d = x2.shape
    tile = COMBINE_TILE
    nt = n // tile
    first = pl.BlockSpec((tile, ROW_WORDS), lambda i: (i, 0))
    second = pl.BlockSpec((tile, ROW_WORDS), lambda i: (i + nt, 0))
    return pl.pallas_call(
        _combine_kernel, grid=(nt,),
        in_specs=[pl.BlockSpec((tile, d), lambda i: (i, 0)),
                  pl.BlockSpec((tile, LANES), lambda i: (i, 0)),
                  pl.BlockSpec(gf.shape, lambda i: (0, 0)),
                  first, first, second, second],
        out_specs=pl.BlockSpec((tile, d), lambda i: (i, 0)),
        out_shape=jax.ShapeDtypeStruct((n, d), F32),
        compiler_params=_cparams(("arbitrary",)), name="moe_combine")(
            x2, info, gf, ga, gb, ga, gb)


def _row(a):
    return a.reshape(1, -1).astype(F32)


def _prep_weights(norm1_g, w_in, conv_a_w, conv_a_norm_g, dn_conv_w, dn_a_log, dn_dt_bias, w_o,
                  norm2_g, rg_w, rg_b, re_w, re_b):
    row = _row
    n_ab = len(_GB_COLS)
    w_main = jnp.concatenate([w_in[:, :7 * CONV_WIDTH], w_in[:, 7 * CONV_WIDTH + _GB_COLS]], axis=1)
    w_main = jnp.pad(w_main, ((0, 0), (0, LANES - n_ab))).astype(BF16)
    cw = jnp.concatenate([conv_a_w, dn_conv_w], axis=1).astype(F32)
    grp = jnp.arange(CONV_WIDTH) // (CONV_WIDTH // CONV_GROUPS)
    gmat = ((grp[:, None] == grp[None, :]).astype(F32) / (CONV_WIDTH // CONV_GROUPS)).astype(BF16)
    is_a = (_GB_COLS < 2 * DN_HEADS)
    a_idx = jnp.where(is_a, _GB_COLS, 0)
    padl = lambda a: jnp.pad(a.reshape(1, -1).astype(F32), ((0, 0), (0, LANES - n_ab)))
    nea = padl(jnp.where(is_a, -jnp.exp(dn_a_log.astype(F32)).reshape(-1)[a_idx], 0.0))
    dtb = padl(jnp.where(is_a, dn_dt_bias.astype(F32).reshape(-1)[a_idx], 0.0))
    proj_w = [row(norm1_g), w_main, cw, row(conv_a_norm_g), gmat, nea, dtb]
    rw = jnp.pad(jnp.concatenate([rg_w, re_w], axis=1).astype(F32),
                 ((0, 0), (0, LANES - N_GROUPS - N_EXPERTS)))
    rb = jnp.pad(jnp.concatenate([rg_b, re_b]).reshape(1, -1).astype(F32),
                 ((0, 0), (0, LANES - N_GROUPS - N_EXPERTS)))
    tri = (jnp.arange(ROUTE_SUB)[:, None] > jnp.arange(ROUTE_SUB)[None, :]).astype(BF16)
    route_w = [w_o.astype(BF16), row(norm2_g), rw.astype(BF16), rb, tri]
    return proj_w, route_w


def _mix_and_route(x, seq_len, proj_w, dn_norm_g, route_w):
    ya, q, k, v, z, gb = _proj_call(x, seq_len, proj_w)
    o_f = _dn_call(q, k, v, gb, None, False, seq_len)
    yb = _dn_call(q, k, v, gb, (o_f, z, _row(dn_norm_g)), True, seq_len)
    return _route_call(x, ya, yb, route_w)


def _moe(routed, w1, w3, w2, final_g):
    x2, xna, xnb, info, info_t, cnt = routed
    n = x2.shape[0]
    tm = min(EXPERT_TILE, max(EXPERT_TILE_MIN, n // (2 * N_EXPERTS)))
    counts = cnt[0, :N_EXPERTS].astype(jnp.int32)
    pcounts = (counts + tm - 1) // tm * tm
    pends = jnp.cumsum(pcounts)
    pstarts = pends - pcounts
    eid = jnp.arange(N_EXPERTS, dtype=jnp.int32)
    e12 = info_t[0:2].astype(jnp.int32)
    start12 = jnp.sum(jnp.where(e12[:, None, :] == eid[None, :, None], pstarts[None, :, None], 0),
                      axis=1)
    dest = start12 + info_t[4:6].astype(jnp.int32)
    dest0 = dest[0:1]
    dest1 = dest[1:2]
    n_tiles = 2 * n // tm + N_EXPERTS
    tile_start = jnp.arange(n_tiles, dtype=jnp.int32) * tm
    tile_expert = jnp.minimum(jnp.sum(pends[None, :] <= tile_start[:, None], axis=1),
                              N_EXPERTS - 1).astype(jnp.int32)
    of_tile = lambda table: jnp.sum(jnp.where(tile_expert[:, None] == eid, table, 0), axis=-1)
    tile_valid = jnp.clip(of_tile(pstarts + counts) - tile_start, 0, tm).astype(jnp.int32)
    n_used = (pends[-1:] // tm).astype(jnp.int32)
    later = (eid[None, :] > eid[:, None]) & (pcounts[None, :] > 0)
    next_e = jnp.min(jnp.where(later, eid[None, :], N_EXPERTS), axis=1)
    next_e = jnp.where(next_e == N_EXPERTS, -1, next_e)
    tile_next = jnp.where(tile_start == of_tile(pstarts), of_tile(next_e), -2).astype(jnp.int32)

    rows = n_tiles * tm
    xa = _sc_scatter_rows(xna, dest0, dest1, rows)
    xb = _sc_scatter_rows(xnb, dest0, dest1, rows)
    ya_e, yb_e = _expert_call(tm, tile_expert, tile_next, tile_valid, n_used, xa, xb, w1, w3, w2)
    both = jnp.concatenate([dest0, dest1], axis=1)
    return _combine_call(x2, info, _row(final_g), _sc_gather_rows(ya_e, both),
                         _sc_gather_rows(yb_e, both))


def kernel(x_prompt, x_sample, norm1_g, w_in, conv_a_w, conv_a_norm_g, dn_conv_w, dn_a_log,
           dn_dt_bias, dn_norm_g, w_o, norm2_g, router_group_w, router_group_b, router_expert_w,
           router_expert_b, w1, w3, w2, final_norm_g):
    assert norm1_g.shape[0] == 1, "single-layer trunk"
    bp, sp, d = x_prompt.shape
    bs, ss, _ = x_sample.shape
    step = max(TOKEN_TILE, ROUTE_TILE, COMBINE_TILE, DN_BLOCK)
    assert sp % step == 0 and ss % step == 0, "sequence lengths must be whole grid steps"
    proj_w, route_w = _prep_weights(
        norm1_g[0], w_in[0], conv_a_w[0], conv_a_norm_g[0], dn_conv_w[0], dn_a_log[0],
        dn_dt_bias[0], w_o[0], norm2_g[0], router_group_w[0], router_group_b[0],
        router_expert_w[0], router_expert_b[0])
    routed = [_mix_and_route(x.reshape(-1, d), seq_len, proj_w, dn_norm_g[0], route_w)
              for x, seq_len in ((x_prompt, sp), (x_sample, ss))]
    y_p, y_s = [_moe(r, w1[0], w3[0], w2[0], final_norm_g) for r in routed]
    return y_p.reshape(bp, sp, d), y_s.reshape(bs, ss, d)
```

```python
import functools

import jax
import jax.numpy as jnp
import numpy as np
from jax import lax
from jax.experimental import pallas as pl
from jax.experimental.pallas import tpu as pltpu
from jax.experimental.pallas import tpu_sc as plsc

F32 = jnp.float32
BF16 = jnp.bfloat16
U32 = jnp.uint32

EPS = 1e-6
CONV_WIDTH = 512
CONV_GROUPS = 8
DN_HEADS = 4
DN_DK = 128
DN_V = 512
CHUNK = 64
N_GROUPS = 4
EXPERTS_PER_GROUP = 8
N_EXPERTS = N_GROUPS * EXPERTS_PER_GROUP
EXPERT_FF = 512
LANES = 128
HALO = 8

TOKEN_TILE = 512
COMBINE_TILE = 1024
ROUTE_TILE = 1024
ROUTE_SUB = 256
DN_BLOCK = 2048
DN_SUB = 256
EXPERT_TILE = 512
ROW_WORDS = 256
SC_WINDOW = 128
VMEM_LIMIT = 56 * 1024 * 1024

_GB_COLS = np.concatenate([np.arange(4 * DN_HEADS), np.arange(2 * DN_HEADS), np.arange(2 * DN_HEADS)])


def _cparams(sem):
    return pltpu.CompilerParams(dimension_semantics=sem, vmem_limit_bytes=VMEM_LIMIT)


def _sigmoid(x):
    return 1.0 / (1.0 + jnp.exp(-x))


def _rms_rows(x, g):
    return x * lax.rsqrt(jnp.mean(x * x, axis=-1, keepdims=True) + EPS) * g


def _dot(a, b):
    return jnp.dot(a, b, preferred_element_type=F32)


def _pack_pairs(x):
    half = x.shape[1] // 2
    bits = lambda a: lax.bitcast_convert_type(a.astype(BF16).astype(F32), U32)
    return (bits(x[:, half:]) & jnp.uint32(0xFFFF0000)) | (bits(x[:, :half]) >> 16)


def _unpack_pairs(w):
    lo = lax.bitcast_convert_type(w << 16, F32)
    hi = lax.bitcast_convert_type(w & jnp.uint32(0xFFFF0000), F32)
    return lo, hi


def _proj_kernel(x_ref, xprev_ref, xnext_ref,
                 g1_ref, w_ref, cw_ref, cng_ref, gmat_ref, nea_ref, dtb_ref,
                 ya_ref, q_ref, k_ref, v_ref, z_ref, gb_ref,
                 hs_ref, su_ref, p_ref, *, tile, seq_len):
    tok0 = pl.program_id(0) * tile
    not_start = (tok0 % seq_len != 0).astype(F32)
    not_end = ((tok0 + tile) % seq_len != 0).astype(F32)
    g1 = g1_ref[...]
    hs_ref[0:HALO, :] = _rms_rows(xprev_ref[...], g1)
    hs_ref[HALO:HALO + tile, :] = _rms_rows(x_ref[...], g1)
    hs_ref[HALO + tile:, :] = _rms_rows(xnext_ref[...], g1)

    p_ref[...] = _dot(hs_ref[...].astype(BF16), w_ref[...])
    for stage in _FINISH_STAGES:
        stage(p_ref, su_ref, cw_ref, cng_ref, gmat_ref, nea_ref, dtb_ref,
              ya_ref, q_ref, k_ref, v_ref, z_ref, gb_ref, tile, not_start, not_end)


def _conv_cols(p_ref, su_ref, cw_ref, lo, width, tile, not_start, not_end):
    cw = CONV_WIDTH
    cols = slice(lo, lo + width)
    if lo < cw:
        su_ref[:, cols] = p_ref[:, cw + lo:cw + lo + width] * p_ref[:, 2 * cw + lo:2 * cw + lo + width]
    else:
        su_ref[:, cols] = p_ref[:, 2 * cw + lo:2 * cw + lo + width]
    su_ref[0:HALO, cols] = su_ref[0:HALO, cols] * not_start
    su_ref[HALO + tile:, cols] = su_ref[HALO + tile:, cols] * not_end
    u = su_ref[:, cols]
    total = tile + 2 * HALO
    return (pltpu.roll(u, 1, axis=0)[HALO:HALO + tile] * cw_ref[0:1, cols]
            + u[HALO:HALO + tile] * cw_ref[1:2, cols]
            + pltpu.roll(u, total - 1, axis=0)[HALO:HALO + tile] * cw_ref[2:3, cols])


def _stage_mixer_a(half):
    width = CONV_WIDTH // 2
    lo = half * width

    def stage(p_ref, su_ref, cw_ref, cng_ref, gmat_ref, nea_ref, dtb_ref,
              ya_ref, q_ref, k_ref, v_ref, z_ref, gb_ref, tile, not_start, not_end):
        conv = _conv_cols(p_ref, su_ref, cw_ref, lo, width, tile, not_start, not_end)
        ya = p_ref[HALO:HALO + tile, lo:lo + width] * conv
        ms = _dot((ya * ya).astype(BF16), gmat_ref[lo:lo + width, lo:lo + width])
        ya_ref[:, lo:lo + width] = (ya * lax.rsqrt(ms + EPS) * cng_ref[:, lo:lo + width]).astype(ya_ref.dtype)
    return stage


def _stage_mixer_b(which, half):
    width = DN_V // 2
    lo_out = half * width
    lo = CONV_WIDTH + which * DN_V + lo_out

    def stage(p_ref, su_ref, cw_ref, cng_ref, gmat_ref, nea_ref, dtb_ref,
              ya_ref, q_ref, k_ref, v_ref, z_ref, gb_ref, tile, not_start, not_end):
        x = _conv_cols(p_ref, su_ref, cw_ref, lo, width, tile, not_start, not_end)
        x = x * _sigmoid(x)
        out_ref = (q_ref, k_ref, v_ref)[which]
        if which == 2:
            out_ref[:, lo_out:lo_out + width] = x.astype(out_ref.dtype)
            return
        scale = DN_DK ** -0.5 if which == 0 else 1.0
        for h in range(width // DN_DK):
            xh = x[:, h * DN_DK:(h + 1) * DN_DK]
            xh = xh * (lax.rsqrt(jnp.sum(xh * xh, axis=-1, keepdims=True) + EPS) * scale)
            out_ref[:, lo_out + h * DN_DK:lo_out + (h + 1) * DN_DK] = xh.astype(out_ref.dtype)
    return stage


def _stage_gates(inner):
    def stage(p_ref, su_ref, cw_ref, cng_ref, gmat_ref, nea_ref, dtb_ref,
              ya_ref, q_ref, k_ref, v_ref, z_ref, gb_ref, tile, not_start, not_end):
        inner(p_ref, su_ref, cw_ref, cng_ref, gmat_ref, nea_ref, dtb_ref,
              ya_ref, q_ref, k_ref, v_ref, z_ref, gb_ref, tile, not_start, not_end)
        cw = CONV_WIDTH
        z_ref[...] = p_ref[HALO:HALO + tile, 6 * cw:7 * cw].astype(z_ref.dtype)
        ab = p_ref[HALO:HALO + tile, 7 * cw:7 * cw + LANES]
        xs = ab + dtb_ref[...]
        softplus = jnp.maximum(xs, 0.0) + jnp.log(1.0 + jnp.exp(-jnp.abs(xs)))
        g = nea_ref[...] * softplus
        beta = _sigmoid(ab)
        r = lax.broadcasted_iota(jnp.int32, (tile, LANES), 0) % CHUNK
        pre = g
        suf = g
        s = 1
        while s < CHUNK:
            pre = pre + jnp.where(r >= s, pltpu.roll(pre, s, axis=0), 0.0)
            suf = suf + jnp.where(r < CHUNK - s, pltpu.roll(suf, tile - s, axis=0), 0.0)
            s *= 2
        lane = lax.broadcasted_iota(jnp.int32, (tile, LANES), 1) // DN_HEADS
        gb_ref[...] = jnp.where(lane == 0, pre,
                      jnp.where(lane == 1, suf,
                      jnp.where(lane < 4, beta,
                      jnp.where(lane == 4, suf - g,
                      jnp.where(lane == 5, pre - g, pre + suf - g)))))
    return stage


_FINISH_STAGES = (_stage_mixer_a(0), _stage_mixer_a(1), _stage_mixer_b(0, 0), _stage_mixer_b(0, 1),
                  _stage_mixer_b(1, 0), _stage_mixer_b(1, 1), _stage_mixer_b(2, 0),
                  _stage_gates(_stage_mixer_b(2, 1)))


def _proj_call(x, seq_len, wts):
    n, d = x.shape
    tile = TOKEN_TILE
    nt = n // tile
    tb = tile // HALO
    kern = functools.partial(_proj_kernel, tile=tile, seq_len=seq_len)
    const = lambda i: (0, 0)
    in_specs = [pl.BlockSpec((tile, d), lambda i: (i, 0)),
                pl.BlockSpec((HALO, d), lambda i: (jnp.maximum(i * tb - 1, 0), 0)),
                pl.BlockSpec((HALO, d), lambda i: (jnp.minimum((i + 1) * tb, n // HALO - 1), 0))]
    in_specs += [pl.BlockSpec(w.shape, const, pipeline_mode=pl.Buffered(1)) for w in wts]
    widths = (512, 512, 512, 512, 512, LANES)
    dtypes = (BF16, BF16, BF16, BF16, BF16, F32)
    rows = tile + 2 * HALO
    return pl.pallas_call(
        kern, grid=(nt,), in_specs=in_specs,
        out_specs=[pl.BlockSpec((tile, w), lambda i: (i, 0)) for w in widths],
        out_shape=[jax.ShapeDtypeStruct((n, w), dt) for w, dt in zip(widths, dtypes)],
        scratch_shapes=[pltpu.VMEM((rows, d), F32), pltpu.VMEM((rows, 4 * CONV_WIDTH), F32),
                        pltpu.VMEM((rows, wts[1].shape[1]), F32)],
        compiler_params=_cparams(("arbitrary",)), name="proj_conv")(x, x, x, *wts)


def _dn_kernel(*refs, block, sub, reverse, seq_len, nblk):
    if reverse:
        (q_ref, k_ref, v_ref, gb_ref, of_ref, z_ref, ng_ref, o_ref, s_ref) = refs
    else:
        (q_ref, k_ref, v_ref, gb_ref, o_ref, s_ref) = refs
    i = pl.program_id(0)
    b = nblk - 1 - i if reverse else i
    edge = (b + 1) * block if reverse else b * block

    @pl.when(edge % seq_len == 0)
    def _():
        s_ref[...] = jnp.zeros_like(s_ref)

    nchunk = sub // CHUNK
    c = CHUNK
    heads = range(DN_HEADS)
    ri = lax.broadcasted_iota(jnp.int32, (sub, sub), 0)
    ci = lax.broadcasted_iota(jnp.int32, (sub, sub), 1)
    same = (ri // c) == (ci // c)
    incl = same & ((ri <= ci) if reverse else (ri >= ci))
    strict = same & ((ri < ci) if reverse else (ri > ci))
    pr = lax.broadcasted_iota(jnp.int32, (c, sub), 0)
    pc = lax.broadcasted_iota(jnp.int32, (c, sub), 1)
    eye_p = (pr == pc % c).astype(F32)
    lane_chunk = pc // c

    def fold(bd):
        out = bd[0:c]
        for r in range(1, nchunk):
            out = out + bd[r * c:(r + 1) * c]
        return out

    def unfold(p):
        zero = jnp.zeros_like(p)
        return jnp.concatenate([jnp.where(lane_chunk == r, p, zero) for r in range(nchunk)], axis=0)

    grp = lambda g, h: slice(g * DN_HEADS + h, g * DN_HEADS + h + 1)
    g_cum, g_beta, g_rest, g_tot = (1, 3, 5, 7) if reverse else (0, 2, 4, 6)
    sls = [slice(h * DN_DK, (h + 1) * DN_DK) for h in heads]

    def prepare(r0, out):
        rows = slice(r0, r0 + sub)
        gbv = gb_ref[rows, :]
        gbt = gbv.T
        kb = [k_ref[rows, sl] for sl in sls]
        qb = [q_ref[rows, sl] for sl in sls]
        gcol = [gbv[:, grp(g_cum, h)] for h in heads]
        beta = [gbv[:, grp(g_beta, h)] for h in heads]
        kq = [lax.dot_general(jnp.concatenate([kb[h], qb[h]], axis=0), kb[h],
                              (((1,), (1,)), ((), ())), preferred_element_type=F32) for h in heads]
        yield
        decay = [jnp.exp(jnp.minimum(gcol[h] - gbt[grp(g_cum, h), :], 0.0)) for h in heads]
        bdl = [jnp.where(strict, kq[h][:sub] * decay[h], 0.0) * beta[h] for h in heads]
        bda = [jnp.where(incl, kq[h][sub:] * decay[h], 0.0).astype(BF16) for h in heads]
        yield
        lp = [fold(bdl[h]) for h in heads]
        xp = [eye_p - lp[h] for h in heads]
        bd = [bdl[h].astype(BF16) for h in heads]
        p = 1
        while p < c:
            if p == 1:
                lp = [_dot(lp[h].astype(BF16), bd[h]) for h in heads]
            elif 2 * p < c:
                res = [_dot(jnp.concatenate([lp[h], xp[h]], axis=0).astype(BF16), bd[h]) for h in heads]
                lp = [r[:c] for r in res]
                xp = [xp[h] + res[h][c:] for h in heads]
            else:
                xp = [xp[h] + _dot(xp[h].astype(BF16), bd[h]) for h in heads]
            p *= 2
            if p < c:
                bd = [unfold(lp[h].astype(BF16)) for h in heads]
            yield
        bdt = [unfold(xp[h].astype(BF16)) for h in heads]
        egc = [jnp.exp(gcol[h]) for h in heads]
        kf = [kb[h].astype(F32) for h in heads]
        rhs = [jnp.concatenate([kf[h] * (beta[h] * egc[h]), v_ref[rows, sls[h]].astype(F32) * beta[h]],
                               axis=1).astype(BF16) for h in heads]
        wu = [_dot(bdt[h], rhs[h]) for h in heads]
        yield
        au = [_dot(bda[h], wu[h].astype(BF16)) for h in heads]
        qt = [qb[h].astype(F32) * egc[h] - au[h][:, :DN_DK] for h in heads]
        out["kd"] = [(kf[h] * jnp.exp(gbv[:, grp(g_rest, h)])).astype(BF16) for h in heads]
        out["wqt"] = [jnp.concatenate([wu[h][:, :DN_DK], qt[h]], axis=1).astype(BF16) for h in heads]
        out["u"] = [wu[h][:, DN_DK:] for h in heads]
        out["o0"] = [au[h][:, DN_DK:] for h in heads]
        out["gbt"] = gbt
        yield

    def scan(r0, pre, state):
        order = range(nchunk - 1, -1, -1) if reverse else range(nchunk)
        for cc in order:
            rows = slice(cc * c, (cc + 1) * c)
            out_rows = slice(r0 + cc * c, r0 + (cc + 1) * c)
            for h in heads:
                wqt = pre["wqt"][h]
                lhs = jnp.concatenate([wqt[rows, :DN_DK], wqt[rows, DN_DK:]], axis=0)
                wq = _dot(lhs, state[h].astype(BF16))
                vn = pre["u"][h][rows] - wq[:c]
                o = wq[c:] + pre["o0"][h][rows]
                tot = pre["gbt"][grp(g_tot, h), cc * c:cc * c + 1]
                state[h] = state[h] * jnp.exp(tot) + lax.dot_general(
                    pre["kd"][h][rows], vn.astype(BF16), (((0,), (0,)), ((), ())),
                    preferred_element_type=F32)
                if reverse:
                    o = o + of_ref[out_rows, sls[h]]
                    o = o * lax.rsqrt(jnp.mean(o * o, axis=-1, keepdims=True) + EPS) * ng_ref[...]
                    zc = z_ref[out_rows, sls[h]].astype(F32)
                    o = o * (zc * _sigmoid(zc))
                o_ref[out_rows, sls[h]] = o.astype(o_ref.dtype)
            yield

    nsub = block // sub
    starts = [r * sub for r in (range(nsub - 1, -1, -1) if reverse else range(nsub))]
    state = [s_ref[h] for h in heads]
    pre_prev = None
    for idx in range(nsub + 1):
        pre = {}
        prep_gen = prepare(starts[idx], pre) if idx < nsub else iter(())
        scan_gen = scan(starts[idx - 1], pre_prev, state) if idx > 0 else iter(())
        prep_live = scan_live = True
        while prep_live or scan_live:
            if prep_live:
                prep_live = next(prep_gen, _DONE) is not _DONE
                if prep_live:
                    prep_live = next(prep_gen, _DONE) is not _DONE
            if scan_live:
                scan_live = next(scan_gen, _DONE) is not _DONE
        pre_prev = pre
    for h in heads:
        s_ref[h] = state[h]


_DONE = object()


def _dn_call(q, k, v, gb, extra, reverse, seq_len):
    n = q.shape[0]
    block = DN_BLOCK
    nblk = n // block
    imap = (lambda i: (nblk - 1 - i, 0)) if reverse else (lambda i: (i, 0))
    tok = lambda w: pl.BlockSpec((block, w), imap)
    in_specs = [tok(512), tok(512), tok(512), tok(LANES)]
    args = [q, k, v, gb]
    if reverse:
        o_f, z, ng = extra
        in_specs += [tok(512), tok(512), pl.BlockSpec(ng.shape, lambda i: (0, 0))]
        args += [o_f, z, ng]
    kern = functools.partial(_dn_kernel, block=block, sub=DN_SUB, reverse=reverse,
                             seq_len=seq_len, nblk=nblk)
    return pl.pallas_call(
        kern, grid=(nblk,), in_specs=in_specs, out_specs=tok(512),
        out_shape=jax.ShapeDtypeStruct((n, 512), BF16 if reverse else F32),
        scratch_shapes=[pltpu.VMEM((DN_HEADS, DN_DK, DN_DK), F32)],
        compiler_params=_cparams(("arbitrary",)),
        name="deltanet_bwd" if reverse else "deltanet_fwd")(*args)


def _route_kernel(x_ref, ya_ref, yb_ref, wo_ref, g2_ref, rw_ref, rb_ref, tri_ref,
                  x2_ref, xa_ref, xb_ref, info_ref, infot_ref, cnt_ref, run_ref, *, tile):
    i = pl.program_id(0)

    @pl.when(i == 0)
    def _():
        run_ref[...] = jnp.zeros_like(run_ref)

    sub = tri_ref.shape[0]
    parts = [slice(r, r + sub) for r in range(0, tile, sub)]
    each = lambda f, *lists: [f(*a) for a in zip(*lists)]
    half = wo_ref.shape[0] // 2
    x2 = [x_ref[rows, :] + _dot(ya_ref[rows, :], wo_ref[0:half, :])
          + _dot(yb_ref[rows, :], wo_ref[half:, :]) for rows in parts]
    for rows, v in zip(parts, x2):
        x2_ref[rows, :] = v.astype(x2_ref.dtype)
    xn = [_rms_rows(v, g2_ref[...]) for v in x2]
    for rows, v in zip(parts, xn):
        words = _pack_pairs(v)
        xa_ref[rows, :] = words[:, :ROW_WORDS]
        xb_ref[rows, :] = words[:, ROW_WORDS:]
    logits = [_dot(v.astype(BF16), rw_ref[...]) + rb_ref[...] for v in xn]
    lane = lax.broadcasted_iota(jnp.int32, (sub, LANES), 1).astype(F32)
    neg = jnp.float32(-jnp.inf)
    row_max = lambda v: jnp.max(v, axis=-1, keepdims=True)
    first_at = lambda v, m: jnp.min(jnp.where(v == m, lane, float(LANES)), axis=-1, keepdims=True)

    gl = [jnp.where(lane < N_GROUPS, v, neg) for v in logits]
    gmax = each(row_max, gl)
    gidx = each(first_at, gl, gmax)
    g_w = [1.0 / jnp.sum(jnp.exp(v - m), axis=-1, keepdims=True) for v, m in zip(gl, gmax)]
    lo = [N_GROUPS + g * EXPERTS_PER_GROUP for g in gidx]
    el = [jnp.where((lane >= a) & (lane < a + EXPERTS_PER_GROUP), v, neg) for v, a in zip(logits, lo)]
    m1 = each(row_max, el)
    i1 = each(first_at, el, m1)
    el2 = [jnp.where(lane == a, neg, v) for v, a in zip(el, i1)]
    m2 = each(row_max, el2)
    i2 = each(first_at, el2, m2)
    hit1 = [lane == a - N_GROUPS for a in i1]
    hit2 = [lane == a - N_GROUPS for a in i2]
    onehot = [(a | b).astype(BF16) for a, b in zip(hit1, hit2)]
    inside = [_dot(tri_ref[...], v) for v in onehot]
    run = run_ref[...]
    for k, rows in enumerate(parts):
        before = inside[k] + run
        pos1 = jnp.sum(jnp.where(hit1[k], before, 0.0), axis=-1, keepdims=True)
        pos2 = jnp.sum(jnp.where(hit2[k], before, 0.0), axis=-1, keepdims=True)
        run = run + jnp.sum(onehot[k].astype(F32), axis=0, keepdims=True)
        r = jnp.exp(m2[k] - m1[k])
        gate1 = g_w[k] / (1.0 + r)
        info = jnp.where(lane == 0, i1[k] - N_GROUPS,
               jnp.where(lane == 1, i2[k] - N_GROUPS,
               jnp.where(lane == 2, gate1,
               jnp.where(lane == 3, gate1 * r,
               jnp.where(lane == 4, pos1,
               jnp.where(lane == 5, pos2, 0.0))))))
        info_ref[rows, :] = info
        infot_ref[:, rows] = info.T[0:HALO, :]
    run_ref[...] = run
    cnt_ref[...] = run


def _route_call(x, ya, yb, wts):
    n, d = x.shape
    tile = ROUTE_TILE
    const = lambda i: (0, 0)
    tok = lambda w: pl.BlockSpec((tile, w), lambda i: (i, 0))
    in_specs = [tok(d), tok(512), tok(512)] + [pl.BlockSpec(w.shape, const) for w in wts]
    out_shape = [jax.ShapeDtypeStruct((n, d), BF16), jax.ShapeDtypeStruct((n, ROW_WORDS), U32),
                 jax.ShapeDtypeStruct((n, ROW_WORDS), U32),
                 jax.ShapeDtypeStruct((n, LANES), F32), jax.ShapeDtypeStruct((HALO, n), F32),
                 jax.ShapeDtypeStruct((1, LANES), F32)]
    out_specs = [tok(d), tok(ROW_WORDS), tok(ROW_WORDS), tok(LANES),
                 pl.BlockSpec((HALO, tile), lambda i: (0, i)), pl.BlockSpec((1, LANES), const)]
    kern = functools.partial(_route_kernel, tile=tile)
    return pl.pallas_call(
        kern, grid=(n // tile,), in_specs=in_specs, out_specs=out_specs, out_shape=out_shape,
        scratch_shapes=[pltpu.VMEM((1, LANES), F32)],
        compiler_params=_cparams(("arbitrary",)), name="oproj_router")(x, ya, yb, *wts)


def _sc_mesh():
    return plsc.VectorSubcoreMesh(core_axis_name="core", subcore_axis_name="subcore")


def _sc_scatter_rows(x, idx0, idx1, rows):
    n, w = x.shape

    @pl.kernel(out_type=jax.ShapeDtypeStruct((rows, w), x.dtype), mesh=_sc_mesh(), scratch_types=[])
    def scatter(x_hbm, i0_hbm, i1_hbm, o_hbm):
        def body(x_vmem, i0_vmem, i1_vmem):
            pltpu.sync_copy(x_vmem, o_hbm.at[i0_vmem.at[0]])
            pltpu.sync_copy(x_vmem, o_hbm.at[i1_vmem.at[0]])

        pltpu.emit_pipeline(
            body, grid=(n // SC_WINDOW,),
            in_specs=[pl.BlockSpec((SC_WINDOW, w), lambda i: (i, 0)),
                      pl.BlockSpec((1, SC_WINDOW), lambda i: (0, i)),
                      pl.BlockSpec((1, SC_WINDOW), lambda i: (0, i))],
            out_specs=[], core_axis_name=("core", "subcore"),
            dimension_semantics=(pltpu.PARALLEL,))(x_hbm, i0_hbm, i1_hbm)

    return scatter(x, idx0, idx1)


def _sc_gather_rows(y, idx):
    m = idx.shape[1]
    w = y.shape[1]

    @pl.kernel(out_type=jax.ShapeDtypeStruct((m, w), y.dtype), mesh=_sc_mesh(), scratch_types=[])
    def gather(y_hbm, i_hbm, o_hbm):
        def body(i_vmem, o_vmem):
            pltpu.sync_copy(y_hbm.at[i_vmem.at[0]], o_vmem)

        pltpu.emit_pipeline(
            body, grid=(m // SC_WINDOW,),
            in_specs=[pl.BlockSpec((1, SC_WINDOW), lambda i: (0, i))],
            out_specs=[pl.BlockSpec((SC_WINDOW, w), lambda i: (i, 0))],
            core_axis_name=("core", "subcore"),
            dimension_semantics=(pltpu.PARALLEL,))(i_hbm, o_hbm)

    return gather(y, idx)


def _join_rows(wa, wb):
    lo_a, hi_a = _unpack_pairs(wa)
    lo_b, hi_b = _unpack_pairs(wb)
    return jnp.concatenate([lo_a, lo_b, hi_a, hi_b], axis=1)


def _expert_kernel(te_ref, nx_ref, nv_ref, nu_ref, xa_ref, xb_ref, w1_hbm, w3_hbm, w2_hbm,
                   ya_ref, yb_ref, stage1, stage3, stage2, w1_ref, w3_ref, w2_ref, sem):
    j = pl.program_id(0)

    def fetch(e):
        return [pltpu.make_async_copy(src.at[e], dst, sem.at[k]) for k, (src, dst) in
                enumerate(((w1_hbm, stage1), (w3_hbm, stage3), (w2_hbm, stage2)))]

    @pl.when(j == 0)
    def _():
        for cp in fetch(te_ref[0]):
            cp.start()

    @pl.when((j < nu_ref[0]) & (nx_ref[j] > -2))
    def _():
        for cp in fetch(te_ref[j]):
            cp.wait()
        w1_ref[...] = stage1[...].astype(BF16)
        w3_ref[...] = stage3[...].astype(BF16)
        w2_ref[...] = stage2[...].astype(BF16)

        @pl.when(nx_ref[j] >= 0)
        def _():
            for cp in fetch(nx_ref[j]):
                cp.start()

    def ffn(m):
        live = lax.broadcasted_iota(jnp.int32, (m, xa_ref.shape[1]), 0) < nv_ref[j]
        zero = jnp.zeros((m, xa_ref.shape[1]), U32)
        x = _join_rows(jnp.where(live, xa_ref[0:m, :], zero),
                       jnp.where(live, xb_ref[0:m, :], zero)).astype(BF16)
        h1 = _dot(x, w1_ref[...])
        h3 = _dot(x, w3_ref[...])
        hdn = (h1 * _sigmoid(h1) * h3).astype(BF16)
        words = _pack_pairs(_dot(hdn, w2_ref[...]))
        ya_ref[0:m, :] = words[:, :ROW_WORDS]
        yb_ref[0:m, :] = words[:, ROW_WORDS:]

    full = xa_ref.shape[0]
    pl.when((j < nu_ref[0]) & (nv_ref[j] > full // 2))(lambda: ffn(full))
    pl.when((j < nu_ref[0]) & (nv_ref[j] <= full // 2))(lambda: ffn(full // 2))


def _expert_call(tile_expert, tile_next, tile_valid, n_used, xa, xb, w1, w3, w2):
    rows, rw = xa.shape
    tm = EXPERT_TILE
    d, ff = w1.shape[1], w1.shape[2]
    row_blk = lambda j, te, nx, nv, nu: (jnp.minimum(j, nu[0] - 1), 0)
    hbm = pl.BlockSpec(memory_space=pl.ANY)
    grid_spec = pltpu.PrefetchScalarGridSpec(
        num_scalar_prefetch=4, grid=(rows // tm,),
        in_specs=[pl.BlockSpec((tm, rw), row_blk), pl.BlockSpec((tm, rw), row_blk), hbm, hbm, hbm],
        out_specs=[pl.BlockSpec((tm, rw), row_blk), pl.BlockSpec((tm, rw), row_blk)],
        scratch_shapes=[pltpu.VMEM((d, ff), w1.dtype), pltpu.VMEM((d, ff), w3.dtype),
                        pltpu.VMEM((ff, d), w2.dtype),
                        pltpu.VMEM((d, ff), BF16), pltpu.VMEM((d, ff), BF16), pltpu.VMEM((ff, d), BF16),
                        pltpu.SemaphoreType.DMA((3,))])
    return pl.pallas_call(
        _expert_kernel, grid_spec=grid_spec,
        out_shape=[jax.ShapeDtypeStruct((rows, rw), U32), jax.ShapeDtypeStruct((rows, rw), U32)],
        compiler_params=_cparams(("arbitrary",)), name="moe_experts")(
            tile_expert, tile_next, tile_valid, n_used, xa, xb, w1, w3, w2)


def _combine_kernel(x2_ref, info_ref, gf_ref, a0_ref, b0_ref, a1_ref, b1_ref, y_ref):
    info = info_ref[...]
    moe = (info[:, 2:3] * _join_rows(a0_ref[...], b0_ref[...])
           + info[:, 3:4] * _join_rows(a1_ref[...], b1_ref[...]))
    y_ref[...] = _rms_rows(x2_ref[...].astype(F32) + moe, gf_ref[...])


def _combine_call(x2, info, gf, ga, gb):
    n, d = x2.shape
    tile = COMBINE_TILE
    nt = n // tile
    first = pl.BlockSpec((tile, ROW_WORDS), lambda i: (i, 0))
    second = pl.BlockSpec((tile, ROW_WORDS), lambda i: (i + nt, 0))
    return pl.pallas_call(
        _combine_kernel, grid=(nt,),
        in_specs=[pl.BlockSpec((tile, d), lambda i: (i, 0)),
                  pl.BlockSpec((tile, LANES), lambda i: (i, 0)),
                  pl.BlockSpec(gf.shape, lambda i: (0, 0)),
                  first, first, second, second],
        out_specs=pl.BlockSpec((tile, d), lambda i: (i, 0)),
        out_shape=jax.ShapeDtypeStruct((n, d), F32),
        compiler_params=_cparams(("arbitrary",)), name="moe_combine")(
            x2, info, gf, ga, gb, ga, gb)


def _row(a):
    return a.reshape(1, -1).astype(F32)


def _prep_weights(norm1_g, w_in, conv_a_w, conv_a_norm_g, dn_conv_w, dn_a_log, dn_dt_bias, w_o,
                  norm2_g, rg_w, rg_b, re_w, re_b):
    row = _row
    n_ab = len(_GB_COLS)
    w_main = jnp.concatenate([w_in[:, :7 * CONV_WIDTH], w_in[:, 7 * CONV_WIDTH + _GB_COLS]], axis=1)
    w_main = jnp.pad(w_main, ((0, 0), (0, LANES - n_ab))).astype(BF16)
    cw = jnp.concatenate([conv_a_w, dn_conv_w], axis=1).astype(F32)
    grp = jnp.arange(CONV_WIDTH) // (CONV_WIDTH // CONV_GROUPS)
    gmat = ((grp[:, None] == grp[None, :]).astype(F32) / (CONV_WIDTH // CONV_GROUPS)).astype(BF16)
    is_a = (_GB_COLS < 2 * DN_HEADS)
    a_idx = jnp.where(is_a, _GB_COLS, 0)
    padl = lambda a: jnp.pad(a.reshape(1, -1).astype(F32), ((0, 0), (0, LANES - n_ab)))
    nea = padl(jnp.where(is_a, -jnp.exp(dn_a_log.astype(F32)).reshape(-1)[a_idx], 0.0))
    dtb = padl(jnp.where(is_a, dn_dt_bias.astype(F32).reshape(-1)[a_idx], 0.0))
    proj_w = [row(norm1_g), w_main, cw, row(conv_a_norm_g), gmat, nea, dtb]
    rw = jnp.pad(jnp.concatenate([rg_w, re_w], axis=1).astype(F32),
                 ((0, 0), (0, LANES - N_GROUPS - N_EXPERTS)))
    rb = jnp.pad(jnp.concatenate([rg_b, re_b]).reshape(1, -1).astype(F32),
                 ((0, 0), (0, LANES - N_GROUPS - N_EXPERTS)))
    tri = (jnp.arange(ROUTE_SUB)[:, None] > jnp.arange(ROUTE_SUB)[None, :]).astype(BF16)
    route_w = [w_o.astype(BF16), row(norm2_g), rw.astype(BF16), rb, tri]
    return proj_w, route_w


def _mix_and_route(x, seq_len, proj_w, dn_norm_g, route_w):
    ya, q, k, v, z, gb = _proj_call(x, seq_len, proj_w)
    o_f = _dn_call(q, k, v, gb, None, False, seq_len)
    yb = _dn_call(q, k, v, gb, (o_f, z, _row(dn_norm_g)), True, seq_len)
    return _route_call(x, ya, yb, route_w)


def _moe(routed, w1, w3, w2, final_g):
    x2, xna, xnb, info, info_t, cnt = routed
    n = x2.shape[0]
    tm = EXPERT_TILE
    counts = cnt[0, :N_EXPERTS].astype(jnp.int32)
    pcounts = (counts + tm - 1) // tm * tm
    pends = jnp.cumsum(pcounts)
    pstarts = pends - pcounts
    eid = jnp.arange(N_EXPERTS, dtype=jnp.int32)
    e12 = info_t[0:2].astype(jnp.int32)
    start12 = jnp.sum(jnp.where(e12[:, None, :] == eid[None, :, None], pstarts[None, :, None], 0),
                      axis=1)
    dest = start12 + info_t[4:6].astype(jnp.int32)
    dest0 = dest[0:1]
    dest1 = dest[1:2]
    n_tiles = 2 * n // tm + N_EXPERTS
    tile_start = jnp.arange(n_tiles, dtype=jnp.int32) * tm
    tile_expert = jnp.minimum(jnp.sum(pends[None, :] <= tile_start[:, None], axis=1),
                              N_EXPERTS - 1).astype(jnp.int32)
    of_tile = lambda table: jnp.sum(jnp.where(tile_expert[:, None] == eid, table, 0), axis=-1)
    tile_valid = jnp.clip(of_tile(pstarts + counts) - tile_start, 0, tm).astype(jnp.int32)
    n_used = (pends[-1:] // tm).astype(jnp.int32)
    later = (eid[None, :] > eid[:, None]) & (pcounts[None, :] > 0)
    next_e = jnp.min(jnp.where(later, eid[None, :], N_EXPERTS), axis=1)
    next_e = jnp.where(next_e == N_EXPERTS, -1, next_e)
    tile_next = jnp.where(tile_start == of_tile(pstarts), of_tile(next_e), -2).astype(jnp.int32)

    rows = n_tiles * tm
    xa = _sc_scatter_rows(xna, dest0, dest1, rows)
    xb = _sc_scatter_rows(xnb, dest0, dest1, rows)
    ya_e, yb_e = _expert_call(tile_expert, tile_next, tile_valid, n_used, xa, xb, w1, w3, w2)
    both = jnp.concatenate([dest0, dest1], axis=1)
    return _combine_call(x2, info, _row(final_g), _sc_gather_rows(ya_e, both),
                         _sc_gather_rows(yb_e, both))


def kernel(x_prompt, x_sample, norm1_g, w_in, conv_a_w, conv_a_norm_g, dn_conv_w, dn_a_log,
           dn_dt_bias, dn_norm_g, w_o, norm2_g, router_group_w, router_group_b, router_expert_w,
           router_expert_b, w1, w3, w2, final_norm_g):
    assert norm1_g.shape[0] == 1, "single-layer trunk"
    bp, sp, d = x_prompt.shape
    bs, ss, _ = x_sample.shape
    step = max(TOKEN_TILE, ROUTE_TILE, COMBINE_TILE, DN_BLOCK)
    assert sp % step == 0 and ss % step == 0, "sequence lengths must be whole grid steps"
    proj_w, route_w = _prep_weights(
        norm1_g[0], w_in[0], conv_a_w[0], conv_a_norm_g[0], dn_conv_w[0], dn_a_log[0],
        dn_dt_bias[0], w_o[0], norm2_g[0], router_group_w[0], router_group_b[0],
        router_expert_w[0], router_expert_b[0])
    routed = [_mix_and_route(x.reshape(-1, d), seq_len, proj_w, dn_norm_g[0], route_w)
              for x, seq_len in ((x_prompt, sp), (x_sample, ss))]
    y_p, y_s = [_moe(r, w1[0], w3[0], w2[0], final_norm_g) for r in routed]
    return y_p.reshape(bp, sp, d), y_s.reshape(bs, ss, d)
```

```python
import functools

import jax
import jax.numpy as jnp
import numpy as np
from jax import lax
from jax.experimental import pallas as pl
from jax.experimental.pallas import tpu as pltpu
from jax.experimental.pallas import tpu_sc as plsc

F32 = jnp.float32
BF16 = jnp.bfloat16
U32 = jnp.uint32

EPS = 1e-6
CONV_WIDTH = 512
CONV_GROUPS = 8
DN_HEADS = 4
DN_DK = 128
DN_V = 512
CHUNK = 64
INV_BASE = 8
N_GROUPS = 4
EXPERTS_PER_GROUP = 8
N_EXPERTS = N_GROUPS * EXPERTS_PER_GROUP
EXPERT_FF = 512
LANES = 128
HALO = 8

TOKEN_TILE = 512
COMBINE_TILE = 1024
ROUTE_TILE = 1024
ROUTE_SUB = 256
DN_BLOCK = 2048
DN_SUB = 256
EXPERT_TILE = 512
ROW_WORDS = 256
SC_WINDOW = 128
VMEM_LIMIT = 56 * 1024 * 1024

_GB_COLS = np.concatenate([np.arange(4 * DN_HEADS), np.arange(2 * DN_HEADS), np.arange(2 * DN_HEADS)])


def _cparams(sem):
    return pltpu.CompilerParams(dimension_semantics=sem, vmem_limit_bytes=VMEM_LIMIT)


def _sigmoid(x):
    return 1.0 / (1.0 + jnp.exp(-x))


def _rms_rows(x, g):
    return x * lax.rsqrt(jnp.mean(x * x, axis=-1, keepdims=True) + EPS) * g


def _dot(a, b):
    return jnp.dot(a, b, preferred_element_type=F32)


def _pack_pairs(x):
    half = x.shape[1] // 2
    bits = lambda a: lax.bitcast_convert_type(a.astype(BF16).astype(F32), U32)
    return (bits(x[:, half:]) & jnp.uint32(0xFFFF0000)) | (bits(x[:, :half]) >> 16)


def _unpack_pairs(w):
    lo = lax.bitcast_convert_type(w << 16, F32)
    hi = lax.bitcast_convert_type(w & jnp.uint32(0xFFFF0000), F32)
    return lo, hi


def _proj_kernel(x_ref, xprev_ref, xnext_ref,
                 g1_ref, w_ref, cw_ref, cng_ref, gmat_ref, nea_ref, dtb_ref,
                 ya_ref, q_ref, k_ref, v_ref, z_ref, gb_ref,
                 hs_ref, su_ref, p_ref, *, tile, seq_len):
    tok0 = pl.program_id(0) * tile
    not_start = (tok0 % seq_len != 0).astype(F32)
    not_end = ((tok0 + tile) % seq_len != 0).astype(F32)
    g1 = g1_ref[...]
    hs_ref[0:HALO, :] = _rms_rows(xprev_ref[...], g1)
    hs_ref[HALO:HALO + tile, :] = _rms_rows(x_ref[...], g1)
    hs_ref[HALO + tile:, :] = _rms_rows(xnext_ref[...], g1)

    p_ref[...] = _dot(hs_ref[...].astype(BF16), w_ref[...])
    for stage in _FINISH_STAGES:
        stage(p_ref, su_ref, cw_ref, cng_ref, gmat_ref, nea_ref, dtb_ref,
              ya_ref, q_ref, k_ref, v_ref, z_ref, gb_ref, tile, not_start, not_end)


def _conv_cols(p_ref, su_ref, cw_ref, lo, width, tile, not_start, not_end):
    cw = CONV_WIDTH
    cols = slice(lo, lo + width)
    if lo < cw:
        su_ref[:, cols] = p_ref[:, cw + lo:cw + lo + width] * p_ref[:, 2 * cw + lo:2 * cw + lo + width]
    else:
        su_ref[:, cols] = p_ref[:, 2 * cw + lo:2 * cw + lo + width]
    su_ref[0:HALO, cols] = su_ref[0:HALO, cols] * not_start
    su_ref[HALO + tile:, cols] = su_ref[HALO + tile:, cols] * not_end
    u = su_ref[:, cols]
    total = tile + 2 * HALO
    return (pltpu.roll(u, 1, axis=0)[HALO:HALO + tile] * cw_ref[0:1, cols]
            + u[HALO:HALO + tile] * cw_ref[1:2, cols]
            + pltpu.roll(u, total - 1, axis=0)[HALO:HALO + tile] * cw_ref[2:3, cols])


def _stage_mixer_a(half):
    width = CONV_WIDTH // 2
    lo = half * width

    def stage(p_ref, su_ref, cw_ref, cng_ref, gmat_ref, nea_ref, dtb_ref,
              ya_ref, q_ref, k_ref, v_ref, z_ref, gb_ref, tile, not_start, not_end):
        conv = _conv_cols(p_ref, su_ref, cw_ref, lo, width, tile, not_start, not_end)
        ya = p_ref[HALO:HALO + tile, lo:lo + width] * conv
        ms = _dot((ya * ya).astype(BF16), gmat_ref[lo:lo + width, lo:lo + width])
        ya_ref[:, lo:lo + width] = (ya * lax.rsqrt(ms + EPS) * cng_ref[:, lo:lo + width]).astype(ya_ref.dtype)
    return stage


def _stage_mixer_b(which, half):
    width = DN_V // 2
    lo_out = half * width
    lo = CONV_WIDTH + which * DN_V + lo_out

    def stage(p_ref, su_ref, cw_ref, cng_ref, gmat_ref, nea_ref, dtb_ref,
              ya_ref, q_ref, k_ref, v_ref, z_ref, gb_ref, tile, not_start, not_end):
        x = _conv_cols(p_ref, su_ref, cw_ref, lo, width, tile, not_start, not_end)
        x = x * _sigmoid(x)
        out_ref = (q_ref, k_ref, v_ref)[which]
        if which == 2:
            out_ref[:, lo_out:lo_out + width] = x.astype(out_ref.dtype)
            return
        scale = DN_DK ** -0.5 if which == 0 else 1.0
        for h in range(width // DN_DK):
            xh = x[:, h * DN_DK:(h + 1) * DN_DK]
            xh = xh * (lax.rsqrt(jnp.sum(xh * xh, axis=-1, keepdims=True) + EPS) * scale)
            out_ref[:, lo_out + h * DN_DK:lo_out + (h + 1) * DN_DK] = xh.astype(out_ref.dtype)
    return stage


def _stage_gates(inner):
    def stage(p_ref, su_ref, cw_ref, cng_ref, gmat_ref, nea_ref, dtb_ref,
              ya_ref, q_ref, k_ref, v_ref, z_ref, gb_ref, tile, not_start, not_end):
        inner(p_ref, su_ref, cw_ref, cng_ref, gmat_ref, nea_ref, dtb_ref,
              ya_ref, q_ref, k_ref, v_ref, z_ref, gb_ref, tile, not_start, not_end)
        cw = CONV_WIDTH
        z_ref[...] = p_ref[HALO:HALO + tile, 6 * cw:7 * cw].astype(z_ref.dtype)
        ab = p_ref[HALO:HALO + tile, 7 * cw:7 * cw + LANES]
        xs = ab + dtb_ref[...]
        softplus = jnp.maximum(xs, 0.0) + jnp.log(1.0 + jnp.exp(-jnp.abs(xs)))
        g = nea_ref[...] * softplus
        beta = _sigmoid(ab)
        r = lax.broadcasted_iota(jnp.int32, (tile, LANES), 0) % CHUNK
        pre = g
        suf = g
        s = 1
        while s < CHUNK:
            pre = pre + jnp.where(r >= s, pltpu.roll(pre, s, axis=0), 0.0)
            suf = suf + jnp.where(r < CHUNK - s, pltpu.roll(suf, tile - s, axis=0), 0.0)
            s *= 2
        lane = lax.broadcasted_iota(jnp.int32, (tile, LANES), 1) // DN_HEADS
        gb_ref[...] = jnp.where(lane == 0, pre,
                      jnp.where(lane == 1, suf,
                      jnp.where(lane < 4, beta,
                      jnp.where(lane == 4, suf - g,
                      jnp.where(lane == 5, pre - g, pre + suf - g)))))
    return stage


_FINISH_STAGES = (_stage_mixer_a(0), _stage_mixer_a(1), _stage_mixer_b(0, 0), _stage_mixer_b(0, 1),
                  _stage_mixer_b(1, 0), _stage_mixer_b(1, 1), _stage_mixer_b(2, 0),
                  _stage_gates(_stage_mixer_b(2, 1)))


def _proj_call(x, seq_len, wts):
    n, d = x.shape
    tile = TOKEN_TILE
    nt = n // tile
    tb = tile // HALO
    kern = functools.partial(_proj_kernel, tile=tile, seq_len=seq_len)
    const = lambda i: (0, 0)
    in_specs = [pl.BlockSpec((tile, d), lambda i: (i, 0)),
                pl.BlockSpec((HALO, d), lambda i: (jnp.maximum(i * tb - 1, 0), 0)),
                pl.BlockSpec((HALO, d), lambda i: (jnp.minimum((i + 1) * tb, n // HALO - 1), 0))]
    in_specs += [pl.BlockSpec(w.shape, const, pipeline_mode=pl.Buffered(1)) for w in wts]
    widths = (512, 512, 512, 512, 512, LANES)
    dtypes = (BF16, BF16, BF16, BF16, BF16, F32)
    rows = tile + 2 * HALO
    return pl.pallas_call(
        kern, grid=(nt,), in_specs=in_specs,
        out_specs=[pl.BlockSpec((tile, w), lambda i: (i, 0)) for w in widths],
        out_shape=[jax.ShapeDtypeStruct((n, w), dt) for w, dt in zip(widths, dtypes)],
        scratch_shapes=[pltpu.VMEM((rows, d), F32), pltpu.VMEM((rows, 4 * CONV_WIDTH), F32),
                        pltpu.VMEM((rows, wts[1].shape[1]), F32)],
        compiler_params=_cparams(("arbitrary",)), name="proj_conv")(x, x, x, *wts)


def _dn_kernel(*refs, block, sub, reverse, seq_len, nblk):
    if reverse:
        (q_ref, k_ref, v_ref, gb_ref, of_ref, z_ref, ng_ref, o_ref, s_ref) = refs
    else:
        (q_ref, k_ref, v_ref, gb_ref, o_ref, s_ref) = refs
    i = pl.program_id(0)
    b = nblk - 1 - i if reverse else i
    edge = (b + 1) * block if reverse else b * block

    @pl.when(edge % seq_len == 0)
    def _():
        s_ref[...] = jnp.zeros_like(s_ref)

    nchunk = sub // CHUNK
    c = CHUNK
    heads = range(DN_HEADS)
    ri = lax.broadcasted_iota(jnp.int32, (sub, sub), 0)
    ci = lax.broadcasted_iota(jnp.int32, (sub, sub), 1)
    same = (ri // c) == (ci // c)
    incl = same & ((ri <= ci) if reverse else (ri >= ci))
    strict = same & ((ri < ci) if reverse else (ri > ci))
    near = [(ri // bs) == (ci // bs) for bs in (INV_BASE << l for l in range(8)) if bs <= c]
    pr = lax.broadcasted_iota(jnp.int32, (c, sub), 0)
    pc = lax.broadcasted_iota(jnp.int32, (c, sub), 1)
    eye_p = (pr == pc % c).astype(F32)
    lane_chunk = pc // c

    def fold(bd):
        out = bd[0:c]
        for r in range(1, nchunk):
            out = out + bd[r * c:(r + 1) * c]
        return out

    def unfold(p):
        zero = jnp.zeros_like(p)
        return jnp.concatenate([jnp.where(lane_chunk == r, p, zero) for r in range(nchunk)], axis=0)

    grp = lambda g, h: slice(g * DN_HEADS + h, g * DN_HEADS + h + 1)
    g_cum, g_beta, g_rest, g_tot = (1, 3, 5, 7) if reverse else (0, 2, 4, 6)
    sls = [slice(h * DN_DK, (h + 1) * DN_DK) for h in heads]

    def prepare(r0, out):
        rows = slice(r0, r0 + sub)
        gbv = gb_ref[rows, :]
        gbt = gbv.T
        kb = [k_ref[rows, sl] for sl in sls]
        qb = [q_ref[rows, sl] for sl in sls]
        gcol = [gbv[:, grp(g_cum, h)] for h in heads]
        beta = [gbv[:, grp(g_beta, h)] for h in heads]
        kq = [lax.dot_general(jnp.concatenate([kb[h], qb[h]], axis=0), kb[h],
                              (((1,), (1,)), ((), ())), preferred_element_type=F32) for h in heads]
        yield
        decay = [jnp.exp(jnp.minimum(gcol[h] - gbt[grp(g_cum, h), :], 0.0)) for h in heads]
        bdl = [jnp.where(strict, kq[h][:sub] * decay[h], 0.0) * beta[h] for h in heads]
        bda = [jnp.where(incl, kq[h][sub:] * decay[h], 0.0).astype(BF16) for h in heads]
        yield
        d0 = [jnp.where(near[0], bdl[h], 0.0) for h in heads]
        lp = [fold(d0[h]) for h in heads]
        xp = [eye_p - lp[h] for h in heads]
        bd = [d0[h].astype(BF16) for h in heads]
        p = 1
        while p < INV_BASE:
            if p == 1:
                lp = [_dot(lp[h].astype(BF16), bd[h]) for h in heads]
            elif 2 * p < INV_BASE:
                res = [_dot(jnp.concatenate([lp[h], xp[h]], axis=0).astype(BF16), bd[h]) for h in heads]
                lp = [r[:c] for r in res]
                xp = [xp[h] + res[h][c:] for h in heads]
            else:
                xp = [xp[h] + _dot(xp[h].astype(BF16), bd[h]) for h in heads]
            p *= 2
            if p < INV_BASE:
                bd = [unfold(lp[h].astype(BF16)) for h in heads]
            yield
        for lvl in range(1, len(near)):
            e = [jnp.where(near[lvl] & ~near[lvl - 1], bdl[h], 0.0).astype(BF16) for h in heads]
            xe = [_dot(xp[h].astype(BF16), e[h]) for h in heads]
            xp = [xp[h] - _dot(xe[h].astype(BF16), unfold(xp[h].astype(BF16))) for h in heads]
            yield
        bdt = [unfold(xp[h].astype(BF16)) for h in heads]
        egc = [jnp.exp(gcol[h]) for h in heads]
        kf = [kb[h].astype(F32) for h in heads]
        rhs = [jnp.concatenate([kf[h] * (beta[h] * egc[h]), v_ref[rows, sls[h]].astype(F32) * beta[h]],
                               axis=1).astype(BF16) for h in heads]
        wu = [_dot(bdt[h], rhs[h]) for h in heads]
        yield
        au = [_dot(bda[h], wu[h].astype(BF16)) for h in heads]
        qt = [qb[h].astype(F32) * egc[h] - au[h][:, :DN_DK] for h in heads]
        out["kd"] = [(kf[h] * jnp.exp(gbv[:, grp(g_rest, h)])).astype(BF16) for h in heads]
        out["wqt"] = [jnp.concatenate([wu[h][:, :DN_DK], qt[h]], axis=1).astype(BF16) for h in heads]
        out["u"] = [wu[h][:, DN_DK:] for h in heads]
        out["o0"] = [au[h][:, DN_DK:] for h in heads]
        out["gbt"] = gbt
        yield

    def scan(r0, pre, state):
        order = range(nchunk - 1, -1, -1) if reverse else range(nchunk)
        for cc in order:
            rows = slice(cc * c, (cc + 1) * c)
            out_rows = slice(r0 + cc * c, r0 + (cc + 1) * c)
            for h in heads:
                wqt = pre["wqt"][h]
                lhs = jnp.concatenate([wqt[rows, :DN_DK], wqt[rows, DN_DK:]], axis=0)
                wq = _dot(lhs, state[h].astype(BF16))
                vn = pre["u"][h][rows] - wq[:c]
                o = wq[c:] + pre["o0"][h][rows]
                tot = pre["gbt"][grp(g_tot, h), cc * c:cc * c + 1]
                state[h] = state[h] * jnp.exp(tot) + lax.dot_general(
                    pre["kd"][h][rows], vn.astype(BF16), (((0,), (0,)), ((), ())),
                    preferred_element_type=F32)
                if reverse:
                    o = o + of_ref[out_rows, sls[h]]
                    o = o * lax.rsqrt(jnp.mean(o * o, axis=-1, keepdims=True) + EPS) * ng_ref[...]
                    zc = z_ref[out_rows, sls[h]].astype(F32)
                    o = o * (zc * _sigmoid(zc))
                o_ref[out_rows, sls[h]] = o.astype(o_ref.dtype)
            yield

    nsub = block // sub
    starts = [r * sub for r in (range(nsub - 1, -1, -1) if reverse else range(nsub))]
    state = [s_ref[h] for h in heads]
    pre_prev = None
    for idx in range(nsub + 1):
        pre = {}
        prep_gen = prepare(starts[idx], pre) if idx < nsub else iter(())
        scan_gen = scan(starts[idx - 1], pre_prev, state) if idx > 0 else iter(())
        prep_live = scan_live = True
        while prep_live or scan_live:
            if prep_live:
                prep_live = next(prep_gen, _DONE) is not _DONE
                if prep_live:
                    prep_live = next(prep_gen, _DONE) is not _DONE
            if scan_live:
                scan_live = next(scan_gen, _DONE) is not _DONE
        pre_prev = pre
    for h in heads:
        s_ref[h] = state[h]


_DONE = object()


def _dn_call(q, k, v, gb, extra, reverse, seq_len):
    n = q.shape[0]
    block = DN_BLOCK
    nblk = n // block
    imap = (lambda i: (nblk - 1 - i, 0)) if reverse else (lambda i: (i, 0))
    tok = lambda w: pl.BlockSpec((block, w), imap)
    in_specs = [tok(512), tok(512), tok(512), tok(LANES)]
    args = [q, k, v, gb]
    if reverse:
        o_f, z, ng = extra
        in_specs += [tok(512), tok(512), pl.BlockSpec(ng.shape, lambda i: (0, 0))]
        args += [o_f, z, ng]
    kern = functools.partial(_dn_kernel, block=block, sub=DN_SUB, reverse=reverse,
                             seq_len=seq_len, nblk=nblk)
    return pl.pallas_call(
        kern, grid=(nblk,), in_specs=in_specs, out_specs=tok(512),
        out_shape=jax.ShapeDtypeStruct((n, 512), BF16 if reverse else F32),
        scratch_shapes=[pltpu.VMEM((DN_HEADS, DN_DK, DN_DK), F32)],
        compiler_params=_cparams(("arbitrary",)),
        name="deltanet_bwd" if reverse else "deltanet_fwd")(*args)


def _route_kernel(x_ref, ya_ref, yb_ref, wo_ref, g2_ref, rw_ref, rb_ref, tri_ref,
                  x2_ref, xa_ref, xb_ref, info_ref, infot_ref, cnt_ref, run_ref, *, tile):
    i = pl.program_id(0)

    @pl.when(i == 0)
    def _():
        run_ref[...] = jnp.zeros_like(run_ref)

    sub = tri_ref.shape[0]
    parts = [slice(r, r + sub) for r in range(0, tile, sub)]
    each = lambda f, *lists: [f(*a) for a in zip(*lists)]
    half = wo_ref.shape[0] // 2
    x2 = [x_ref[rows, :] + _dot(ya_ref[rows, :], wo_ref[0:half, :])
          + _dot(yb_ref[rows, :], wo_ref[half:, :]) for rows in parts]
    for rows, v in zip(parts, x2):
        x2_ref[rows, :] = v.astype(x2_ref.dtype)
    xn = [_rms_rows(v, g2_ref[...]) for v in x2]
    for rows, v in zip(parts, xn):
        words = _pack_pairs(v)
        xa_ref[rows, :] = words[:, :ROW_WORDS]
        xb_ref[rows, :] = words[:, ROW_WORDS:]
    logits = [_dot(v.astype(BF16), rw_ref[...]) + rb_ref[...] for v in xn]
    lane = lax.broadcasted_iota(jnp.int32, (sub, LANES), 1).astype(F32)
    neg = jnp.float32(-jnp.inf)
    row_max = lambda v: jnp.max(v, axis=-1, keepdims=True)
    first_at = lambda v, m: jnp.min(jnp.where(v == m, lane, float(LANES)), axis=-1, keepdims=True)

    gl = [jnp.where(lane < N_GROUPS, v, neg) for v in logits]
    gmax = each(row_max, gl)
    gidx = each(first_at, gl, gmax)
    g_w = [1.0 / jnp.sum(jnp.exp(v - m), axis=-1, keepdims=True) for v, m in zip(gl, gmax)]
    lo = [N_GROUPS + g * EXPERTS_PER_GROUP for g in gidx]
    el = [jnp.where((lane >= a) & (lane < a + EXPERTS_PER_GROUP), v, neg) for v, a in zip(logits, lo)]
    m1 = each(row_max, el)
    i1 = each(first_at, el, m1)
    el2 = [jnp.where(lane == a, neg, v) for v, a in zip(el, i1)]
    m2 = each(row_max, el2)
    i2 = each(first_at, el2, m2)
    hit1 = [lane == a - N_GROUPS for a in i1]
    hit2 = [lane == a - N_GROUPS for a in i2]
    onehot = [(a | b).astype(BF16) for a, b in zip(hit1, hit2)]
    inside = [_dot(tri_ref[...], v) for v in onehot]
    run = run_ref[...]
    for k, rows in enumerate(parts):
        before = inside[k] + run
        pos1 = jnp.sum(jnp.where(hit1[k], before, 0.0), axis=-1, keepdims=True)
        pos2 = jnp.sum(jnp.where(hit2[k], before, 0.0), axis=-1, keepdims=True)
        run = run + jnp.sum(onehot[k].astype(F32), axis=0, keepdims=True)
        r = jnp.exp(m2[k] - m1[k])
        gate1 = g_w[k] / (1.0 + r)
        info = jnp.where(lane == 0, i1[k] - N_GROUPS,
               jnp.where(lane == 1, i2[k] - N_GROUPS,
               jnp.where(lane == 2, gate1,
               jnp.where(lane == 3, gate1 * r,
               jnp.where(lane == 4, pos1,
               jnp.where(lane == 5, pos2, 0.0))))))
        info_ref[rows, :] = info
        infot_ref[:, rows] = info.T[0:HALO, :]
    run_ref[...] = run
    cnt_ref[...] = run


def _route_call(x, ya, yb, wts):
    n, d = x.shape
    tile = ROUTE_TILE
    const = lambda i: (0, 0)
    tok = lambda w: pl.BlockSpec((tile, w), lambda i: (i, 0))
    in_specs = [tok(d), tok(512), tok(512)] + [pl.BlockSpec(w.shape, const) for w in wts]
    out_shape = [jax.ShapeDtypeStruct((n, d), BF16), jax.ShapeDtypeStruct((n, ROW_WORDS), U32),
                 jax.ShapeDtypeStruct((n, ROW_WORDS), U32),
                 jax.ShapeDtypeStruct((n, LANES), F32), jax.ShapeDtypeStruct((HALO, n), F32),
                 jax.ShapeDtypeStruct((1, LANES), F32)]
    out_specs = [tok(d), tok(ROW_WORDS), tok(ROW_WORDS), tok(LANES),
                 pl.BlockSpec((HALO, tile), lambda i: (0, i)), pl.BlockSpec((1, LANES), const)]
    kern = functools.partial(_route_kernel, tile=tile)
    return pl.pallas_call(
        kern, grid=(n // tile,), in_specs=in_specs, out_specs=out_specs, out_shape=out_shape,
        scratch_shapes=[pltpu.VMEM((1, LANES), F32)],
        compiler_params=_cparams(("arbitrary",)), name="oproj_router")(x, ya, yb, *wts)


def _sc_mesh():
    return plsc.VectorSubcoreMesh(core_axis_name="core", subcore_axis_name="subcore")


def _sc_scatter_rows(x, idx0, idx1, rows):
    n, w = x.shape

    @pl.kernel(out_type=jax.ShapeDtypeStruct((rows, w), x.dtype), mesh=_sc_mesh(), scratch_types=[])
    def scatter(x_hbm, i0_hbm, i1_hbm, o_hbm):
        def body(x_vmem, i0_vmem, i1_vmem):
            pltpu.sync_copy(x_vmem, o_hbm.at[i0_vmem.at[0]])
            pltpu.sync_copy(x_vmem, o_hbm.at[i1_vmem.at[0]])

        pltpu.emit_pipeline(
            body, grid=(n // SC_WINDOW,),
            in_specs=[pl.BlockSpec((SC_WINDOW, w), lambda i: (i, 0)),
                      pl.BlockSpec((1, SC_WINDOW), lambda i: (0, i)),
                      pl.BlockSpec((1, SC_WINDOW), lambda i: (0, i))],
            out_specs=[], core_axis_name=("core", "subcore"),
            dimension_semantics=(pltpu.PARALLEL,))(x_hbm, i0_hbm, i1_hbm)

    return scatter(x, idx0, idx1)


def _sc_gather_rows(y, idx):
    m = idx.shape[1]
    w = y.shape[1]

    @pl.kernel(out_type=jax.ShapeDtypeStruct((m, w), y.dtype), mesh=_sc_mesh(), scratch_types=[])
    def gather(y_hbm, i_hbm, o_hbm):
        def body(i_vmem, o_vmem):
            pltpu.sync_copy(y_hbm.at[i_vmem.at[0]], o_vmem)

        pltpu.emit_pipeline(
            body, grid=(m // SC_WINDOW,),
            in_specs=[pl.BlockSpec((1, SC_WINDOW), lambda i: (0, i))],
            out_specs=[pl.BlockSpec((SC_WINDOW, w), lambda i: (i, 0))],
            core_axis_name=("core", "subcore"),
            dimension_semantics=(pltpu.PARALLEL,))(i_hbm, o_hbm)

    return gather(y, idx)


def _join_rows(wa, wb):
    lo_a, hi_a = _unpack_pairs(wa)
    lo_b, hi_b = _unpack_pairs(wb)
    return jnp.concatenate([lo_a, lo_b, hi_a, hi_b], axis=1)


def _expert_kernel(te_ref, nx_ref, nv_ref, nu_ref, xa_ref, xb_ref, w1_hbm, w3_hbm, w2_hbm,
                   ya_ref, yb_ref, stage1, stage3, stage2, w1_ref, w3_ref, w2_ref, sem):
    j = pl.program_id(0)

    def fetch(e):
        return [pltpu.make_async_copy(src.at[e], dst, sem.at[k]) for k, (src, dst) in
                enumerate(((w1_hbm, stage1), (w3_hbm, stage3), (w2_hbm, stage2)))]

    @pl.when(j == 0)
    def _():
        for cp in fetch(te_ref[0]):
            cp.start()

    @pl.when((j < nu_ref[0]) & (nx_ref[j] > -2))
    def _():
        for cp in fetch(te_ref[j]):
            cp.wait()
        w1_ref[...] = stage1[...].astype(BF16)
        w3_ref[...] = stage3[...].astype(BF16)
        w2_ref[...] = stage2[...].astype(BF16)

        @pl.when(nx_ref[j] >= 0)
        def _():
            for cp in fetch(nx_ref[j]):
                cp.start()

    @pl.when(j < nu_ref[0])
    def _():
        live = lax.broadcasted_iota(jnp.int32, xa_ref.shape, 0) < nv_ref[j]
        zero = jnp.zeros(xa_ref.shape, U32)
        x = _join_rows(jnp.where(live, xa_ref[...], zero),
                       jnp.where(live, xb_ref[...], zero)).astype(BF16)
        h1 = _dot(x, w1_ref[...])
        h3 = _dot(x, w3_ref[...])
        hdn = (h1 * _sigmoid(h1) * h3).astype(BF16)
        words = _pack_pairs(_dot(hdn, w2_ref[...]))
        ya_ref[...] = words[:, :ROW_WORDS]
        yb_ref[...] = words[:, ROW_WORDS:]


def _expert_call(tile_expert, tile_next, tile_valid, n_used, xa, xb, w1, w3, w2):
    rows, rw = xa.shape
    tm = EXPERT_TILE
    d, ff = w1.shape[1], w1.shape[2]
    row_blk = lambda j, te, nx, nv, nu: (jnp.minimum(j, nu[0] - 1), 0)
    hbm = pl.BlockSpec(memory_space=pl.ANY)
    grid_spec = pltpu.PrefetchScalarGridSpec(
        num_scalar_prefetch=4, grid=(rows // tm,),
        in_specs=[pl.BlockSpec((tm, rw), row_blk), pl.BlockSpec((tm, rw), row_blk), hbm, hbm, hbm],
        out_specs=[pl.BlockSpec((tm, rw), row_blk), pl.BlockSpec((tm, rw), row_blk)],
        scratch_shapes=[pltpu.VMEM((d, ff), w1.dtype), pltpu.VMEM((d, ff), w3.dtype),
                        pltpu.VMEM((ff, d), w2.dtype),
                        pltpu.VMEM((d, ff), BF16), pltpu.VMEM((d, ff), BF16), pltpu.VMEM((ff, d), BF16),
                        pltpu.SemaphoreType.DMA((3,))])
    return pl.pallas_call(
        _expert_kernel, grid_spec=grid_spec,
        out_shape=[jax.ShapeDtypeStruct((rows, rw), U32), jax.ShapeDtypeStruct((rows, rw), U32)],
        compiler_params=_cparams(("arbitrary",)), name="moe_experts")(
            tile_expert, tile_next, tile_valid, n_used, xa, xb, w1, w3, w2)


def _combine_kernel(x2_ref, info_ref, gf_ref, a0_ref, b0_ref, a1_ref, b1_ref, y_ref):
    info = info_ref[...]
    moe = (info[:, 2:3] * _join_rows(a0_ref[...], b0_ref[...])
           + info[:, 3:4] * _join_rows(a1_ref[...], b1_ref[...]))
    y_ref[...] = _rms_rows(x2_ref[...].astype(F32) + moe, gf_ref[...])


def _combine_call(x2, info, gf, ga, gb):
    n, d = x2.shape
    tile = COMBINE_TILE
    nt = n // tile
    first = pl.BlockSpec((tile, ROW_WORDS), lambda i: (i, 0))
    second = pl.BlockSpec((tile, ROW_WORDS), lambda i: (i + nt, 0))
    return pl.pallas_call(
        _combine_kernel, grid=(nt,),
        in_specs=[pl.BlockSpec((tile, d), lambda i: (i, 0)),
                  pl.BlockSpec((tile, LANES), lambda i: (i, 0)),
                  pl.BlockSpec(gf.shape, lambda i: (0, 0)),
                  first, first, second, second],
        out_specs=pl.BlockSpec((tile, d), lambda i: (i, 0)),
        out_shape=jax.ShapeDtypeStruct((n, d), F32),
        compiler_params=_cparams(("arbitrary",)), name="moe_combine")(
            x2, info, gf, ga, gb, ga, gb)


def _row(a):
    return a.reshape(1, -1).astype(F32)


def _prep_weights(norm1_g, w_in, conv_a_w, conv_a_norm_g, dn_conv_w, dn_a_log, dn_dt_bias, w_o,
                  norm2_g, rg_w, rg_b, re_w, re_b):
    row = _row
    n_ab = len(_GB_COLS)
    w_main = jnp.concatenate([w_in[:, :7 * CONV_WIDTH], w_in[:, 7 * CONV_WIDTH + _GB_COLS]], axis=1)
    w_main = jnp.pad(w_main, ((0, 0), (0, LANES - n_ab))).astype(BF16)
    cw = jnp.concatenate([conv_a_w, dn_conv_w], axis=1).astype(F32)
    grp = jnp.arange(CONV_WIDTH) // (CONV_WIDTH // CONV_GROUPS)
    gmat = ((grp[:, None] == grp[None, :]).astype(F32) / (CONV_WIDTH // CONV_GROUPS)).astype(BF16)
    is_a = (_GB_COLS < 2 * DN_HEADS)
    a_idx = jnp.where(is_a, _GB_COLS, 0)
    padl = lambda a: jnp.pad(a.reshape(1, -1).astype(F32), ((0, 0), (0, LANES - n_ab)))
    nea = padl(jnp.where(is_a, -jnp.exp(dn_a_log.astype(F32)).reshape(-1)[a_idx], 0.0))
    dtb = padl(jnp.where(is_a, dn_dt_bias.astype(F32).reshape(-1)[a_idx], 0.0))
    proj_w = [row(norm1_g), w_main, cw, row(conv_a_norm_g), gmat, nea, dtb]
    rw = jnp.pad(jnp.concatenate([rg_w, re_w], axis=1).astype(F32),
                 ((0, 0), (0, LANES - N_GROUPS - N_EXPERTS)))
    rb = jnp.pad(jnp.concatenate([rg_b, re_b]).reshape(1, -1).astype(F32),
                 ((0, 0), (0, LANES - N_GROUPS - N_EXPERTS)))
    tri = (jnp.arange(ROUTE_SUB)[:, None] > jnp.arange(ROUTE_SUB)[None, :]).astype(BF16)
    route_w = [w_o.astype(BF16), row(norm2_g), rw.astype(BF16), rb, tri]
    return proj_w, route_w


def _mix_and_route(x, seq_len, proj_w, dn_norm_g, route_w):
    ya, q, k, v, z, gb = _proj_call(x, seq_len, proj_w)
    o_f = _dn_call(q, k, v, gb, None, False, seq_len)
    yb = _dn_call(q, k, v, gb, (o_f, z, _row(dn_norm_g)), True, seq_len)
    return _route_call(x, ya, yb, route_w)


def _moe(routed, w1, w3, w2, final_g):
    x2, xna, xnb, info, info_t, cnt = routed
    n = x2.shape[0]
    tm = EXPERT_TILE
    counts = cnt[0, :N_EXPERTS].astype(jnp.int32)
    pcounts = (counts + tm - 1) // tm * tm
    pends = jnp.cumsum(pcounts)
    pstarts = pends - pcounts
    eid = jnp.arange(N_EXPERTS, dtype=jnp.int32)
    e12 = info_t[0:2].astype(jnp.int32)
    start12 = jnp.sum(jnp.where(e12[:, None, :] == eid[None, :, None], pstarts[None, :, None], 0),
                      axis=1)
    dest = start12 + info_t[4:6].astype(jnp.int32)
    dest0 = dest[0:1]
    dest1 = dest[1:2]
    n_tiles = 2 * n // tm + N_EXPERTS
    tile_start = jnp.arange(n_tiles, dtype=jnp.int32) * tm
    tile_expert = jnp.minimum(jnp.sum(pends[None, :] <= tile_start[:, None], axis=1),
                              N_EXPERTS - 1).astype(jnp.int32)
    of_tile = lambda table: jnp.sum(jnp.where(tile_expert[:, None] == eid, table, 0), axis=-1)
    tile_valid = jnp.clip(of_tile(pstarts + counts) - tile_start, 0, tm).astype(jnp.int32)
    n_used = (pends[-1:] // tm).astype(jnp.int32)
    later = (eid[None, :] > eid[:, None]) & (pcounts[None, :] > 0)
    next_e = jnp.min(jnp.where(later, eid[None, :], N_EXPERTS), axis=1)
    next_e = jnp.where(next_e == N_EXPERTS, -1, next_e)
    tile_next = jnp.where(tile_start == of_tile(pstarts), of_tile(next_e), -2).astype(jnp.int32)

    rows = n_tiles * tm
    xa = _sc_scatter_rows(xna, dest0, dest1, rows)
    xb = _sc_scatter_rows(xnb, dest0, dest1, rows)
    ya_e, yb_e = _expert_call(tile_expert, tile_next, tile_valid, n_used, xa, xb, w1, w3, w2)
    both = jnp.concatenate([dest0, dest1], axis=1)
    return _combine_call(x2, info, _row(final_g), _sc_gather_rows(ya_e, both),
                         _sc_gather_rows(yb_e, both))


def kernel(x_prompt, x_sample, norm1_g, w_in, conv_a_w, conv_a_norm_g, dn_conv_w, dn_a_log,
           dn_dt_bias, dn_norm_g, w_o, norm2_g, router_group_w, router_group_b, router_expert_w,
           router_expert_b, w1, w3, w2, final_norm_g):
    assert norm1_g.shape[0] == 1, "single-layer trunk"
    bp, sp, d = x_prompt.shape
    bs, ss, _ = x_sample.shape
    step = max(TOKEN_TILE, ROUTE_TILE, COMBINE_TILE, DN_BLOCK)
    assert sp % step == 0 and ss % step == 0, "sequence lengths must be whole grid steps"
    proj_w, route_w = _prep_weights(
        norm1_g[0], w_in[0], conv_a_w[0], conv_a_norm_g[0], dn_conv_w[0], dn_a_log[0],
        dn_dt_bias[0], w_o[0], norm2_g[0], router_group_w[0], router_group_b[0],
        router_expert_w[0], router_expert_b[0])
    routed = [_mix_and_route(x.reshape(-1, d), seq_len, proj_w, dn_norm_g[0], route_w)
              for x, seq_len in ((x_prompt, sp), (x_sample, ss))]
    y_p, y_s = [_moe(r, w1[0], w3[0], w2[0], final_norm_g) for r in routed]
    return y_p.reshape(bp, sp, d), y_s.reshape(bs, ss, d)
```

```python
import functools

import jax
import jax.numpy as jnp
import numpy as np
from jax import lax
from jax.experimental import pallas as pl
from jax.experimental.pallas import tpu as pltpu
from jax.experimental.pallas import tpu_sc as plsc

F32 = jnp.float32
BF16 = jnp.bfloat16
U32 = jnp.uint32

EPS = 1e-6
CONV_WIDTH = 512
CONV_GROUPS = 8
DN_HEADS = 4
DN_DK = 128
DN_V = 512
CHUNK = 64
INV_BASE = 8
N_GROUPS = 4
EXPERTS_PER_GROUP = 8
N_EXPERTS = N_GROUPS * EXPERTS_PER_GROUP
EXPERT_FF = 512
LANES = 128
HALO = 8

TOKEN_TILE = 512
COMBINE_TILE = 2048
ROUTE_TILE = 2048
ROUTE_SUB = 256
DN_BLOCK = 2048
DN_SUB = 256
EXPERT_TILE = 512
ROW_WORDS = 256
SC_WINDOW = 128
VMEM_LIMIT = 56 * 1024 * 1024

_GB_COLS = np.concatenate([np.arange(4 * DN_HEADS), np.arange(2 * DN_HEADS), np.arange(2 * DN_HEADS)])


def _cparams(sem):
    return pltpu.CompilerParams(dimension_semantics=sem, vmem_limit_bytes=VMEM_LIMIT)


def _sigmoid(x):
    return 1.0 / (1.0 + jnp.exp(-x))


def _rms_rows(x, g):
    return x * lax.rsqrt(jnp.mean(x * x, axis=-1, keepdims=True) + EPS) * g


def _dot(a, b):
    return jnp.dot(a, b, preferred_element_type=F32)


def _pack_pairs(x):
    half = x.shape[1] // 2
    bits = lambda a: lax.bitcast_convert_type(a.astype(BF16).astype(F32), U32)
    return (bits(x[:, half:]) & jnp.uint32(0xFFFF0000)) | (bits(x[:, :half]) >> 16)


def _unpack_pairs(w):
    lo = lax.bitcast_convert_type(w << 16, F32)
    hi = lax.bitcast_convert_type(w & jnp.uint32(0xFFFF0000), F32)
    return lo, hi


def _proj_kernel(x_ref, xprev_ref, xnext_ref,
                 g1_ref, w_ref, cw_ref, cng_ref, gmat_ref, nea_ref, dtb_ref,
                 ya_ref, q_ref, k_ref, v_ref, z_ref, gb_ref,
                 hs_ref, su_ref, p_ref, *, tile, seq_len):
    tok0 = pl.program_id(0) * tile
    not_start = (tok0 % seq_len != 0).astype(F32)
    not_end = ((tok0 + tile) % seq_len != 0).astype(F32)
    g1 = g1_ref[...]
    hs_ref[0:HALO, :] = _rms_rows(xprev_ref[...], g1)
    hs_ref[HALO:HALO + tile, :] = _rms_rows(x_ref[...], g1)
    hs_ref[HALO + tile:, :] = _rms_rows(xnext_ref[...], g1)

    p_ref[...] = _dot(hs_ref[...].astype(BF16), w_ref[...])
    for stage in _FINISH_STAGES:
        stage(p_ref, su_ref, cw_ref, cng_ref, gmat_ref, nea_ref, dtb_ref,
              ya_ref, q_ref, k_ref, v_ref, z_ref, gb_ref, tile, not_start, not_end)


def _conv_cols(p_ref, su_ref, cw_ref, lo, width, tile, not_start, not_end):
    cw = CONV_WIDTH
    cols = slice(lo, lo + width)
    if lo < cw:
        su_ref[:, cols] = p_ref[:, cw + lo:cw + lo + width] * p_ref[:, 2 * cw + lo:2 * cw + lo + width]
    else:
        su_ref[:, cols] = p_ref[:, 2 * cw + lo:2 * cw + lo + width]
    su_ref[0:HALO, cols] = su_ref[0:HALO, cols] * not_start
    su_ref[HALO + tile:, cols] = su_ref[HALO + tile:, cols] * not_end
    u = su_ref[:, cols]
    total = tile + 2 * HALO
    return (pltpu.roll(u, 1, axis=0)[HALO:HALO + tile] * cw_ref[0:1, cols]
            + u[HALO:HALO + tile] * cw_ref[1:2, cols]
            + pltpu.roll(u, total - 1, axis=0)[HALO:HALO + tile] * cw_ref[2:3, cols])


def _stage_mixer_a(half):
    width = CONV_WIDTH // 2
    lo = half * width

    def stage(p_ref, su_ref, cw_ref, cng_ref, gmat_ref, nea_ref, dtb_ref,
              ya_ref, q_ref, k_ref, v_ref, z_ref, gb_ref, tile, not_start, not_end):
        conv = _conv_cols(p_ref, su_ref, cw_ref, lo, width, tile, not_start, not_end)
        ya = p_ref[HALO:HALO + tile, lo:lo + width] * conv
        ms = _dot((ya * ya).astype(BF16), gmat_ref[lo:lo + width, lo:lo + width])
        ya_ref[:, lo:lo + width] = (ya * lax.rsqrt(ms + EPS) * cng_ref[:, lo:lo + width]).astype(ya_ref.dtype)
    return stage


def _stage_mixer_b(which, half):
    width = DN_V // 2
    lo_out = half * width
    lo = CONV_WIDTH + which * DN_V + lo_out

    def stage(p_ref, su_ref, cw_ref, cng_ref, gmat_ref, nea_ref, dtb_ref,
              ya_ref, q_ref, k_ref, v_ref, z_ref, gb_ref, tile, not_start, not_end):
        x = _conv_cols(p_ref, su_ref, cw_ref, lo, width, tile, not_start, not_end)
        x = x * _sigmoid(x)
        out_ref = (q_ref, k_ref, v_ref)[which]
        if which == 2:
            out_ref[:, lo_out:lo_out + width] = x.astype(out_ref.dtype)
            return
        scale = DN_DK ** -0.5 if which == 0 else 1.0
        for h in range(width // DN_DK):
            xh = x[:, h * DN_DK:(h + 1) * DN_DK]
            xh = xh * (lax.rsqrt(jnp.sum(xh * xh, axis=-1, keepdims=True) + EPS) * scale)
            out_ref[:, lo_out + h * DN_DK:lo_out + (h + 1) * DN_DK] = xh.astype(out_ref.dtype)
    return stage


def _stage_gates(inner):
    def stage(p_ref, su_ref, cw_ref, cng_ref, gmat_ref, nea_ref, dtb_ref,
              ya_ref, q_ref, k_ref, v_ref, z_ref, gb_ref, tile, not_start, not_end):
        inner(p_ref, su_ref, cw_ref, cng_ref, gmat_ref, nea_ref, dtb_ref,
              ya_ref, q_ref, k_ref, v_ref, z_ref, gb_ref, tile, not_start, not_end)
        cw = CONV_WIDTH
        z_ref[...] = p_ref[HALO:HALO + tile, 6 * cw:7 * cw].astype(z_ref.dtype)
        ab = p_ref[HALO:HALO + tile, 7 * cw:7 * cw + LANES]
        xs = ab + dtb_ref[...]
        softplus = jnp.maximum(xs, 0.0) + jnp.log(1.0 + jnp.exp(-jnp.abs(xs)))
        g = nea_ref[...] * softplus
        beta = _sigmoid(ab)
        r = lax.broadcasted_iota(jnp.int32, (tile, LANES), 0) % CHUNK
        pre = g
        suf = g
        s = 1
        while s < CHUNK:
            pre = pre + jnp.where(r >= s, pltpu.roll(pre, s, axis=0), 0.0)
            suf = suf + jnp.where(r < CHUNK - s, pltpu.roll(suf, tile - s, axis=0), 0.0)
            s *= 2
        lane = lax.broadcasted_iota(jnp.int32, (tile, LANES), 1) // DN_HEADS
        gb_ref[...] = jnp.where(lane == 0, pre,
                      jnp.where(lane == 1, suf,
                      jnp.where(lane < 4, beta,
                      jnp.where(lane == 4, suf - g,
                      jnp.where(lane == 5, pre - g, pre + suf - g)))))
    return stage


_FINISH_STAGES = (_stage_mixer_a(0), _stage_mixer_a(1), _stage_mixer_b(0, 0), _stage_mixer_b(0, 1),
                  _stage_mixer_b(1, 0), _stage_mixer_b(1, 1), _stage_mixer_b(2, 0),
                  _stage_gates(_stage_mixer_b(2, 1)))


def _proj_call(x, seq_len, wts):
    n, d = x.shape
    tile = TOKEN_TILE
    nt = n // tile
    tb = tile // HALO
    kern = functools.partial(_proj_kernel, tile=tile, seq_len=seq_len)
    const = lambda i: (0, 0)
    in_specs = [pl.BlockSpec((tile, d), lambda i: (i, 0)),
                pl.BlockSpec((HALO, d), lambda i: (jnp.maximum(i * tb - 1, 0), 0)),
                pl.BlockSpec((HALO, d), lambda i: (jnp.minimum((i + 1) * tb, n // HALO - 1), 0))]
    in_specs += [pl.BlockSpec(w.shape, const, pipeline_mode=pl.Buffered(1)) for w in wts]
    widths = (512, 512, 512, 512, 512, LANES)
    dtypes = (BF16, BF16, BF16, BF16, BF16, F32)
    rows = tile + 2 * HALO
    return pl.pallas_call(
        kern, grid=(nt,), in_specs=in_specs,
        out_specs=[pl.BlockSpec((tile, w), lambda i: (i, 0)) for w in widths],
        out_shape=[jax.ShapeDtypeStruct((n, w), dt) for w, dt in zip(widths, dtypes)],
        scratch_shapes=[pltpu.VMEM((rows, d), F32), pltpu.VMEM((rows, 4 * CONV_WIDTH), F32),
                        pltpu.VMEM((rows, wts[1].shape[1]), F32)],
        compiler_params=_cparams(("arbitrary",)), name="proj_conv")(x, x, x, *wts)


def _dn_kernel(*refs, block, sub, reverse, seq_len, nblk):
    if reverse:
        (q_ref, k_ref, v_ref, gb_ref, of_ref, z_ref, ng_ref, o_ref, s_ref) = refs
    else:
        (q_ref, k_ref, v_ref, gb_ref, o_ref, s_ref) = refs
    i = pl.program_id(0)
    b = nblk - 1 - i if reverse else i
    edge = (b + 1) * block if reverse else b * block

    @pl.when(edge % seq_len == 0)
    def _():
        s_ref[...] = jnp.zeros_like(s_ref)

    nchunk = sub // CHUNK
    c = CHUNK
    heads = range(DN_HEADS)
    ri = lax.broadcasted_iota(jnp.int32, (sub, sub), 0)
    ci = lax.broadcasted_iota(jnp.int32, (sub, sub), 1)
    same = (ri // c) == (ci // c)
    incl = same & ((ri <= ci) if reverse else (ri >= ci))
    strict = same & ((ri < ci) if reverse else (ri > ci))
    near = [(ri // bs) == (ci // bs) for bs in (INV_BASE << l for l in range(8)) if bs <= c]
    pr = lax.broadcasted_iota(jnp.int32, (c, sub), 0)
    pc = lax.broadcasted_iota(jnp.int32, (c, sub), 1)
    eye_p = (pr == pc % c).astype(F32)
    lane_chunk = pc // c

    def fold(bd):
        out = bd[0:c]
        for r in range(1, nchunk):
            out = out + bd[r * c:(r + 1) * c]
        return out

    def unfold(p):
        zero = jnp.zeros_like(p)
        return jnp.concatenate([jnp.where(lane_chunk == r, p, zero) for r in range(nchunk)], axis=0)

    grp = lambda g, h: slice(g * DN_HEADS + h, g * DN_HEADS + h + 1)
    g_cum, g_beta, g_rest, g_tot = (1, 3, 5, 7) if reverse else (0, 2, 4, 6)
    sls = [slice(h * DN_DK, (h + 1) * DN_DK) for h in heads]

    def prepare(r0, out):
        rows = slice(r0, r0 + sub)
        gbv = gb_ref[rows, :]
        gbt = gbv.T
        kb = [k_ref[rows, sl] for sl in sls]
        qb = [q_ref[rows, sl] for sl in sls]
        gcol = [gbv[:, grp(g_cum, h)] for h in heads]
        beta = [gbv[:, grp(g_beta, h)] for h in heads]
        kq = [lax.dot_general(jnp.concatenate([kb[h], qb[h]], axis=0), kb[h],
                              (((1,), (1,)), ((), ())), preferred_element_type=F32) for h in heads]
        yield
        decay = [jnp.exp(jnp.minimum(gcol[h] - gbt[grp(g_cum, h), :], 0.0)) for h in heads]
        bdl = [jnp.where(strict, kq[h][:sub] * decay[h], 0.0) * beta[h] for h in heads]
        bda = [jnp.where(incl, kq[h][sub:] * decay[h], 0.0).astype(BF16) for h in heads]
        yield
        d0 = [jnp.where(near[0], bdl[h], 0.0) for h in heads]
        lp = [fold(d0[h]) for h in heads]
        xp = [eye_p - lp[h] for h in heads]
        bd = [d0[h].astype(BF16) for h in heads]
        p = 1
        while p < INV_BASE:
            if p == 1:
                lp = [_dot(lp[h].astype(BF16), bd[h]) for h in heads]
            elif 2 * p < INV_BASE:
                res = [_dot(jnp.concatenate([lp[h], xp[h]], axis=0).astype(BF16), bd[h]) for h in heads]
                lp = [r[:c] for r in res]
                xp = [xp[h] + res[h][c:] for h in heads]
            else:
                xp = [xp[h] + _dot(xp[h].astype(BF16), bd[h]) for h in heads]
            p *= 2
            if p < INV_BASE:
                bd = [unfold(lp[h].astype(BF16)) for h in heads]
            yield
        for lvl in range(1, len(near)):
            e = [jnp.where(near[lvl] & ~near[lvl - 1], bdl[h], 0.0).astype(BF16) for h in heads]
            xe = [_dot(xp[h].astype(BF16), e[h]) for h in heads]
            xp = [xp[h] - _dot(xe[h].astype(BF16), unfold(xp[h].astype(BF16))) for h in heads]
            yield
        bdt = [unfold(xp[h].astype(BF16)) for h in heads]
        egc = [jnp.exp(gcol[h]) for h in heads]
        kf = [kb[h].astype(F32) for h in heads]
        rhs = [jnp.concatenate([kf[h] * (beta[h] * egc[h]), v_ref[rows, sls[h]].astype(F32) * beta[h]],
                               axis=1).astype(BF16) for h in heads]
        wu = [_dot(bdt[h], rhs[h]) for h in heads]
        yield
        au = [_dot(bda[h], wu[h].astype(BF16)) for h in heads]
        qt = [qb[h].astype(F32) * egc[h] - au[h][:, :DN_DK] for h in heads]
        out["kd"] = [(kf[h] * jnp.exp(gbv[:, grp(g_rest, h)])).astype(BF16) for h in heads]
        out["wqt"] = [jnp.concatenate([wu[h][:, :DN_DK], qt[h]], axis=1).astype(BF16) for h in heads]
        out["u"] = [wu[h][:, DN_DK:] for h in heads]
        out["o0"] = [au[h][:, DN_DK:] for h in heads]
        out["gbt"] = gbt
        yield

    def scan(r0, pre, state):
        order = range(nchunk - 1, -1, -1) if reverse else range(nchunk)
        for cc in order:
            rows = slice(cc * c, (cc + 1) * c)
            out_rows = slice(r0 + cc * c, r0 + (cc + 1) * c)
            for h in heads:
                wqt = pre["wqt"][h]
                lhs = jnp.concatenate([wqt[rows, :DN_DK], wqt[rows, DN_DK:]], axis=0)
                wq = _dot(lhs, state[h].astype(BF16))
                vn = pre["u"][h][rows] - wq[:c]
                o = wq[c:] + pre["o0"][h][rows]
                tot = pre["gbt"][grp(g_tot, h), cc * c:cc * c + 1]
                state[h] = state[h] * jnp.exp(tot) + lax.dot_general(
                    pre["kd"][h][rows], vn.astype(BF16), (((0,), (0,)), ((), ())),
                    preferred_element_type=F32)
                if reverse:
                    o = o + of_ref[out_rows, sls[h]]
                    o = o * lax.rsqrt(jnp.mean(o * o, axis=-1, keepdims=True) + EPS) * ng_ref[...]
                    zc = z_ref[out_rows, sls[h]].astype(F32)
                    o = o * (zc * _sigmoid(zc))
                o_ref[out_rows, sls[h]] = o.astype(o_ref.dtype)
            yield

    nsub = block // sub
    starts = [r * sub for r in (range(nsub - 1, -1, -1) if reverse else range(nsub))]
    state = [s_ref[h] for h in heads]
    pre_prev = None
    for idx in range(nsub + 1):
        pre = {}
        prep_gen = prepare(starts[idx], pre) if idx < nsub else iter(())
        scan_gen = scan(starts[idx - 1], pre_prev, state) if idx > 0 else iter(())
        prep_live = scan_live = True
        while prep_live or scan_live:
            if prep_live:
                prep_live = next(prep_gen, _DONE) is not _DONE
                if prep_live:
                    prep_live = next(prep_gen, _DONE) is not _DONE
            if scan_live:
                scan_live = next(scan_gen, _DONE) is not _DONE
        pre_prev = pre
    for h in heads:
        s_ref[h] = state[h]


_DONE = object()


def _dn_call(q, k, v, gb, extra, reverse, seq_len):
    n = q.shape[0]
    block = DN_BLOCK
    nblk = n // block
    imap = (lambda i: (nblk - 1 - i, 0)) if reverse else (lambda i: (i, 0))
    tok = lambda w: pl.BlockSpec((block, w), imap)
    in_specs = [tok(512), tok(512), tok(512), tok(LANES)]
    args = [q, k, v, gb]
    if reverse:
        o_f, z, ng = extra
        in_specs += [tok(512), tok(512), pl.BlockSpec(ng.shape, lambda i: (0, 0))]
        args += [o_f, z, ng]
    kern = functools.partial(_dn_kernel, block=block, sub=DN_SUB, reverse=reverse,
                             seq_len=seq_len, nblk=nblk)
    return pl.pallas_call(
        kern, grid=(nblk,), in_specs=in_specs, out_specs=tok(512),
        out_shape=jax.ShapeDtypeStruct((n, 512), BF16 if reverse else F32),
        scratch_shapes=[pltpu.VMEM((DN_HEADS, DN_DK, DN_DK), F32)],
        compiler_params=_cparams(("arbitrary",)),
        name="deltanet_bwd" if reverse else "deltanet_fwd")(*args)


def _route_kernel(x_ref, ya_ref, yb_ref, wo_ref, g2_ref, rw_ref, rb_ref, tri_ref,
                  x2_ref, xa_ref, xb_ref, info_ref, infot_ref, cnt_ref, run_ref, *, tile):
    i = pl.program_id(0)

    @pl.when(i == 0)
    def _():
        run_ref[...] = jnp.zeros_like(run_ref)

    sub = tri_ref.shape[0]
    parts = [slice(r, r + sub) for r in range(0, tile, sub)]
    each = lambda f, *lists: [f(*a) for a in zip(*lists)]
    half = wo_ref.shape[0] // 2
    x2 = [x_ref[rows, :] + _dot(ya_ref[rows, :], wo_ref[0:half, :])
          + _dot(yb_ref[rows, :], wo_ref[half:, :]) for rows in parts]
    for rows, v in zip(parts, x2):
        x2_ref[rows, :] = v.astype(x2_ref.dtype)
    xn = [_rms_rows(v, g2_ref[...]) for v in x2]
    for rows, v in zip(parts, xn):
        words = _pack_pairs(v)
        xa_ref[rows, :] = words[:, :ROW_WORDS]
        xb_ref[rows, :] = words[:, ROW_WORDS:]
    logits = [_dot(v.astype(BF16), rw_ref[...]) + rb_ref[...] for v in xn]
    lane = lax.broadcasted_iota(jnp.int32, (sub, LANES), 1).astype(F32)
    neg = jnp.float32(-jnp.inf)
    row_max = lambda v: jnp.max(v, axis=-1, keepdims=True)
    first_at = lambda v, m: jnp.min(jnp.where(v == m, lane, float(LANES)), axis=-1, keepdims=True)

    gl = [jnp.where(lane < N_GROUPS, v, neg) for v in logits]
    gmax = each(row_max, gl)
    gidx = each(first_at, gl, gmax)
    g_w = [1.0 / jnp.sum(jnp.exp(v - m), axis=-1, keepdims=True) for v, m in zip(gl, gmax)]
    lo = [N_GROUPS + g * EXPERTS_PER_GROUP for g in gidx]
    el = [jnp.where((lane >= a) & (lane < a + EXPERTS_PER_GROUP), v, neg) for v, a in zip(logits, lo)]
    m1 = each(row_max, el)
    i1 = each(first_at, el, m1)
    el2 = [jnp.where(lane == a, neg, v) for v, a in zip(el, i1)]
    m2 = each(row_max, el2)
    i2 = each(first_at, el2, m2)
    hit1 = [lane == a - N_GROUPS for a in i1]
    hit2 = [lane == a - N_GROUPS for a in i2]
    onehot = [(a | b).astype(BF16) for a, b in zip(hit1, hit2)]
    inside = [_dot(tri_ref[...], v) for v in onehot]
    run = run_ref[...]
    for k, rows in enumerate(parts):
        before = inside[k] + run
        pos1 = jnp.sum(jnp.where(hit1[k], before, 0.0), axis=-1, keepdims=True)
        pos2 = jnp.sum(jnp.where(hit2[k], before, 0.0), axis=-1, keepdims=True)
        run = run + jnp.sum(onehot[k].astype(F32), axis=0, keepdims=True)
        r = jnp.exp(m2[k] - m1[k])
        gate1 = g_w[k] / (1.0 + r)
        info = jnp.where(lane == 0, i1[k] - N_GROUPS,
               jnp.where(lane == 1, i2[k] - N_GROUPS,
               jnp.where(lane == 2, gate1,
               jnp.where(lane == 3, gate1 * r,
               jnp.where(lane == 4, pos1,
               jnp.where(lane == 5, pos2, 0.0))))))
        info_ref[rows, :] = info
        infot_ref[:, rows] = info.T[0:HALO, :]
    run_ref[...] = run
    cnt_ref[...] = run


def _route_call(x, ya, yb, wts):
    n, d = x.shape
    tile = ROUTE_TILE
    const = lambda i: (0, 0)
    tok = lambda w: pl.BlockSpec((tile, w), lambda i: (i, 0))
    in_specs = [tok(d), tok(512), tok(512)] + [pl.BlockSpec(w.shape, const) for w in wts]
    out_shape = [jax.ShapeDtypeStruct((n, d), BF16), jax.ShapeDtypeStruct((n, ROW_WORDS), U32),
                 jax.ShapeDtypeStruct((n, ROW_WORDS), U32),
                 jax.ShapeDtypeStruct((n, LANES), F32), jax.ShapeDtypeStruct((HALO, n), F32),
                 jax.ShapeDtypeStruct((1, LANES), F32)]
    out_specs = [tok(d), tok(ROW_WORDS), tok(ROW_WORDS), tok(LANES),
                 pl.BlockSpec((HALO, tile), lambda i: (0, i)), pl.BlockSpec((1, LANES), const)]
    kern = functools.partial(_route_kernel, tile=tile)
    return pl.pallas_call(
        kern, grid=(n // tile,), in_specs=in_specs, out_specs=out_specs, out_shape=out_shape,
        scratch_shapes=[pltpu.VMEM((1, LANES), F32)],
        compiler_params=_cparams(("arbitrary",)), name="oproj_router")(x, ya, yb, *wts)


def _sc_mesh():
    return plsc.VectorSubcoreMesh(core_axis_name="core", subcore_axis_name="subcore")


def _sc_scatter_rows(x, idx0, idx1, rows):
    n, w = x.shape

    @pl.kernel(out_type=jax.ShapeDtypeStruct((rows, w), x.dtype), mesh=_sc_mesh(), scratch_types=[])
    def scatter(x_hbm, i0_hbm, i1_hbm, o_hbm):
        def body(x_vmem, i0_vmem, i1_vmem):
            pltpu.sync_copy(x_vmem, o_hbm.at[i0_vmem.at[0]])
            pltpu.sync_copy(x_vmem, o_hbm.at[i1_vmem.at[0]])

        pltpu.emit_pipeline(
            body, grid=(n // SC_WINDOW,),
            in_specs=[pl.BlockSpec((SC_WINDOW, w), lambda i: (i, 0)),
                      pl.BlockSpec((1, SC_WINDOW), lambda i: (0, i)),
                      pl.BlockSpec((1, SC_WINDOW), lambda i: (0, i))],
            out_specs=[], core_axis_name=("core", "subcore"),
            dimension_semantics=(pltpu.PARALLEL,))(x_hbm, i0_hbm, i1_hbm)

    return scatter(x, idx0, idx1)


def _sc_gather_rows(y, idx):
    m = idx.shape[1]
    w = y.shape[1]

    @pl.kernel(out_type=jax.ShapeDtypeStruct((m, w), y.dtype), mesh=_sc_mesh(), scratch_types=[])
    def gather(y_hbm, i_hbm, o_hbm):
        def body(i_vmem, o_vmem):
            pltpu.sync_copy(y_hbm.at[i_vmem.at[0]], o_vmem)

        pltpu.emit_pipeline(
            body, grid=(m // SC_WINDOW,),
            in_specs=[pl.BlockSpec((1, SC_WINDOW), lambda i: (0, i))],
            out_specs=[pl.BlockSpec((SC_WINDOW, w), lambda i: (i, 0))],
            core_axis_name=("core", "subcore"),
            dimension_semantics=(pltpu.PARALLEL,))(i_hbm, o_hbm)

    return gather(y, idx)


def _join_rows(wa, wb):
    lo_a, hi_a = _unpack_pairs(wa)
    lo_b, hi_b = _unpack_pairs(wb)
    return jnp.concatenate([lo_a, lo_b, hi_a, hi_b], axis=1)


def _expert_kernel(te_ref, nx_ref, nv_ref, nu_ref, xa_ref, xb_ref, w1_hbm, w3_hbm, w2_hbm,
                   ya_ref, yb_ref, stage1, stage3, stage2, w1_ref, w3_ref, w2_ref, sem):
    j = pl.program_id(0)

    def fetch(e):
        return [pltpu.make_async_copy(src.at[e], dst, sem.at[k]) for k, (src, dst) in
                enumerate(((w1_hbm, stage1), (w3_hbm, stage3), (w2_hbm, stage2)))]

    @pl.when(j == 0)
    def _():
        for cp in fetch(te_ref[0]):
            cp.start()

    @pl.when((j < nu_ref[0]) & (nx_ref[j] > -2))
    def _():
        for cp in fetch(te_ref[j]):
            cp.wait()
        w1_ref[...] = stage1[...].astype(BF16)
        w3_ref[...] = stage3[...].astype(BF16)
        w2_ref[...] = stage2[...].astype(BF16)

        @pl.when(nx_ref[j] >= 0)
        def _():
            for cp in fetch(nx_ref[j]):
                cp.start()

    @pl.when(j < nu_ref[0])
    def _():
        live = lax.broadcasted_iota(jnp.int32, xa_ref.shape, 0) < nv_ref[j]
        zero = jnp.zeros(xa_ref.shape, U32)
        x = _join_rows(jnp.where(live, xa_ref[...], zero),
                       jnp.where(live, xb_ref[...], zero)).astype(BF16)
        h1 = _dot(x, w1_ref[...])
        h3 = _dot(x, w3_ref[...])
        hdn = (h1 * _sigmoid(h1) * h3).astype(BF16)
        words = _pack_pairs(_dot(hdn, w2_ref[...]))
        ya_ref[...] = words[:, :ROW_WORDS]
        yb_ref[...] = words[:, ROW_WORDS:]


def _expert_call(tile_expert, tile_next, tile_valid, n_used, xa, xb, w1, w3, w2):
    rows, rw = xa.shape
    tm = EXPERT_TILE
    d, ff = w1.shape[1], w1.shape[2]
    row_blk = lambda j, te, nx, nv, nu: (jnp.minimum(j, nu[0] - 1), 0)
    hbm = pl.BlockSpec(memory_space=pl.ANY)
    grid_spec = pltpu.PrefetchScalarGridSpec(
        num_scalar_prefetch=4, grid=(rows // tm,),
        in_specs=[pl.BlockSpec((tm, rw), row_blk), pl.BlockSpec((tm, rw), row_blk), hbm, hbm, hbm],
        out_specs=[pl.BlockSpec((tm, rw), row_blk), pl.BlockSpec((tm, rw), row_blk)],
        scratch_shapes=[pltpu.VMEM((d, ff), w1.dtype), pltpu.VMEM((d, ff), w3.dtype),
                        pltpu.VMEM((ff, d), w2.dtype),
                        pltpu.VMEM((d, ff), BF16), pltpu.VMEM((d, ff), BF16), pltpu.VMEM((ff, d), BF16),
                        pltpu.SemaphoreType.DMA((3,))])
    return pl.pallas_call(
        _expert_kernel, grid_spec=grid_spec,
        out_shape=[jax.ShapeDtypeStruct((rows, rw), U32), jax.ShapeDtypeStruct((rows, rw), U32)],
        compiler_params=_cparams(("arbitrary",)), name="moe_experts")(
            tile_expert, tile_next, tile_valid, n_used, xa, xb, w1, w3, w2)


def _combine_kernel(x2_ref, info_ref, gf_ref, a0_ref, b0_ref, a1_ref, b1_ref, y_ref):
    info = info_ref[...]
    moe = (info[:, 2:3] * _join_rows(a0_ref[...], b0_ref[...])
           + info[:, 3:4] * _join_rows(a1_ref[...], b1_ref[...]))
    y_ref[...] = _rms_rows(x2_ref[...].astype(F32) + moe, gf_ref[...])


def _combine_call(x2, info, gf, ga, gb):
    n, d = x2.shape
    tile = COMBINE_TILE
    nt = n // tile
    first = pl.BlockSpec((tile, ROW_WORDS), lambda i: (i, 0))
    second = pl.BlockSpec((tile, ROW_WORDS), lambda i: (i + nt, 0))
    return pl.pallas_call(
        _combine_kernel, grid=(nt,),
        in_specs=[pl.BlockSpec((tile, d), lambda i: (i, 0)),
                  pl.BlockSpec((tile, LANES), lambda i: (i, 0)),
                  pl.BlockSpec(gf.shape, lambda i: (0, 0)),
                  first, first, second, second],
        out_specs=pl.BlockSpec((tile, d), lambda i: (i, 0)),
        out_shape=jax.ShapeDtypeStruct((n, d), F32),
        compiler_params=_cparams(("arbitrary",)), name="moe_combine")(
            x2, info, gf, ga, gb, ga, gb)


def _row(a):
    return a.reshape(1, -1).astype(F32)


def _prep_weights(norm1_g, w_in, conv_a_w, conv_a_norm_g, dn_conv_w, dn_a_log, dn_dt_bias, w_o,
                  norm2_g, rg_w, rg_b, re_w, re_b):
    row = _row
    n_ab = len(_GB_COLS)
    w_main = jnp.concatenate([w_in[:, :7 * CONV_WIDTH], w_in[:, 7 * CONV_WIDTH + _GB_COLS]], axis=1)
    w_main = jnp.pad(w_main, ((0, 0), (0, LANES - n_ab))).astype(BF16)
    cw = jnp.concatenate([conv_a_w, dn_conv_w], axis=1).astype(F32)
    grp = jnp.arange(CONV_WIDTH) // (CONV_WIDTH // CONV_GROUPS)
    gmat = ((grp[:, None] == grp[None, :]).astype(F32) / (CONV_WIDTH // CONV_GROUPS)).astype(BF16)
    is_a = (_GB_COLS < 2 * DN_HEADS)
    a_idx = jnp.where(is_a, _GB_COLS, 0)
    padl = lambda a: jnp.pad(a.reshape(1, -1).astype(F32), ((0, 0), (0, LANES - n_ab)))
    nea = padl(jnp.where(is_a, -jnp.exp(dn_a_log.astype(F32)).reshape(-1)[a_idx], 0.0))
    dtb = padl(jnp.where(is_a, dn_dt_bias.astype(F32).reshape(-1)[a_idx], 0.0))
    proj_w = [row(norm1_g), w_main, cw, row(conv_a_norm_g), gmat, nea, dtb]
    rw = jnp.pad(jnp.concatenate([rg_w, re_w], axis=1).astype(F32),
                 ((0, 0), (0, LANES - N_GROUPS - N_EXPERTS)))
    rb = jnp.pad(jnp.concatenate([rg_b, re_b]).reshape(1, -1).astype(F32),
                 ((0, 0), (0, LANES - N_GROUPS - N_EXPERTS)))
    tri = (jnp.arange(ROUTE_SUB)[:, None] > jnp.arange(ROUTE_SUB)[None, :]).astype(BF16)
    route_w = [w_o.astype(BF16), row(norm2_g), rw.astype(BF16), rb, tri]
    return proj_w, route_w


def _mix_and_route(x, seq_len, proj_w, dn_norm_g, route_w):
    ya, q, k, v, z, gb = _proj_call(x, seq_len, proj_w)
    o_f = _dn_call(q, k, v, gb, None, False, seq_len)
    yb = _dn_call(q, k, v, gb, (o_f, z, _row(dn_norm_g)), True, seq_len)
    return _route_call(x, ya, yb, route_w)


def _moe(routed, w1, w3, w2, final_g):
    x2, xna, xnb, info, info_t, cnt = routed
    n = x2.shape[0]
    tm = EXPERT_TILE
    counts = cnt[0, :N_EXPERTS].astype(jnp.int32)
    pcounts = (counts + tm - 1) // tm * tm
    pends = jnp.cumsum(pcounts)
    pstarts = pends - pcounts
    eid = jnp.arange(N_EXPERTS, dtype=jnp.int32)
    e12 = info_t[0:2].astype(jnp.int32)
    start12 = jnp.sum(jnp.where(e12[:, None, :] == eid[None, :, None], pstarts[None, :, None], 0),
                      axis=1)
    dest = start12 + info_t[4:6].astype(jnp.int32)
    dest0 = dest[0:1]
    dest1 = dest[1:2]
    n_tiles = 2 * n // tm + N_EXPERTS
    tile_start = jnp.arange(n_tiles, dtype=jnp.int32) * tm
    tile_expert = jnp.minimum(jnp.sum(pends[None, :] <= tile_start[:, None], axis=1),
                              N_EXPERTS - 1).astype(jnp.int32)
    of_tile = lambda table: jnp.sum(jnp.where(tile_expert[:, None] == eid, table, 0), axis=-1)
    tile_valid = jnp.clip(of_tile(pstarts + counts) - tile_start, 0, tm).astype(jnp.int32)
    n_used = (pends[-1:] // tm).astype(jnp.int32)
    later = (eid[None, :] > eid[:, None]) & (pcounts[None, :] > 0)
    next_e = jnp.min(jnp.where(later, eid[None, :], N_EXPERTS), axis=1)
    next_e = jnp.where(next_e == N_EXPERTS, -1, next_e)
    tile_next = jnp.where(tile_start == of_tile(pstarts), of_tile(next_e), -2).astype(jnp.int32)

    rows = n_tiles * tm
    xa = _sc_scatter_rows(xna, dest0, dest1, rows)
    xb = _sc_scatter_rows(xnb, dest0, dest1, rows)
    ya_e, yb_e = _expert_call(tile_expert, tile_next, tile_valid, n_used, xa, xb, w1, w3, w2)
    both = jnp.concatenate([dest0, dest1], axis=1)
    return _combine_call(x2, info, _row(final_g), _sc_gather_rows(ya_e, both),
                         _sc_gather_rows(yb_e, both))


def kernel(x_prompt, x_sample, norm1_g, w_in, conv_a_w, conv_a_norm_g, dn_conv_w, dn_a_log,
           dn_dt_bias, dn_norm_g, w_o, norm2_g, router_group_w, router_group_b, router_expert_w,
           router_expert_b, w1, w3, w2, final_norm_g):
    assert norm1_g.shape[0] == 1, "single-layer trunk"
    bp, sp, d = x_prompt.shape
    bs, ss, _ = x_sample.shape
    step = max(TOKEN_TILE, ROUTE_TILE, COMBINE_TILE, DN_BLOCK)
    assert sp % step == 0 and ss % step == 0, "sequence lengths must be whole grid steps"
    proj_w, route_w = _prep_weights(
        norm1_g[0], w_in[0], conv_a_w[0], conv_a_norm_g[0], dn_conv_w[0], dn_a_log[0],
        dn_dt_bias[0], w_o[0], norm2_g[0], router_group_w[0], router_group_b[0],
        router_expert_w[0], router_expert_b[0])
    routed = [_mix_and_route(x.reshape(-1, d), seq_len, proj_w, dn_norm_g[0], route_w)
              for x, seq_len in ((x_prompt, sp), (x_sample, ss))]
    y_p, y_s = [_moe(r, w1[0], w3[0], w2[0], final_norm_g) for r in routed]
    return y_p.reshape(bp, sp, d), y_s.reshape(bs, ss, d)
```

```python
import functools

import jax
import jax.numpy as jnp
import numpy as np
from jax import lax
from jax.experimental import pallas as pl
from jax.experimental.pallas import tpu as pltpu
from jax.experimental.pallas import tpu_sc as plsc

F32 = jnp.float32
BF16 = jnp.bfloat16
U32 = jnp.uint32

EPS = 1e-6
CONV_WIDTH = 512
CONV_GROUPS = 8
DN_HEADS = 4
DN_DK = 128
DN_V = 512
CHUNK = 64
INV_BASE = 8
N_GROUPS = 4
EXPERTS_PER_GROUP = 8
N_EXPERTS = N_GROUPS * EXPERTS_PER_GROUP
EXPERT_FF = 512
LANES = 128
HALO = 8

TOKEN_TILE = 512
COMBINE_TILE = 2048
ROUTE_TILE = 2048
ROUTE_SUB = 256
DN_BLOCK = 2048
DN_SUB = 256
EXPERT_TILE = 512
ROW_WORDS = 256
SC_WINDOW = 128
VMEM_LIMIT = 56 * 1024 * 1024

_GB_COLS = np.concatenate([np.arange(4 * DN_HEADS), np.arange(2 * DN_HEADS), np.arange(2 * DN_HEADS)])


def _cparams(sem):
    return pltpu.CompilerParams(dimension_semantics=sem, vmem_limit_bytes=VMEM_LIMIT)


def _sigmoid(x):
    return 1.0 / (1.0 + jnp.exp(-x))


def _rms_rows(x, g):
    return x * lax.rsqrt(jnp.mean(x * x, axis=-1, keepdims=True) + EPS) * g


def _dot(a, b):
    return jnp.dot(a, b, preferred_element_type=F32)


def _pack_pairs(x):
    half = x.shape[1] // 2
    bits = lambda a: lax.bitcast_convert_type(a.astype(BF16).astype(F32), U32)
    return (bits(x[:, half:]) & jnp.uint32(0xFFFF0000)) | (bits(x[:, :half]) >> 16)


def _unpack_pairs(w):
    lo = lax.bitcast_convert_type(w << 16, F32)
    hi = lax.bitcast_convert_type(w & jnp.uint32(0xFFFF0000), F32)
    return lo, hi


def _proj_kernel(x_ref, xprev_ref, xnext_ref,
                 g1_ref, w_ref, cw_ref, cng_ref, gmat_ref, nea_ref, dtb_ref,
                 ya_ref, q_ref, k_ref, v_ref, z_ref, gb_ref,
                 hs_ref, su_ref, p_ref, *, tile, seq_len):
    tok0 = pl.program_id(0) * tile
    not_start = (tok0 % seq_len != 0).astype(F32)
    not_end = ((tok0 + tile) % seq_len != 0).astype(F32)
    g1 = g1_ref[...]
    hs_ref[0:HALO, :] = _rms_rows(xprev_ref[...], g1)
    hs_ref[HALO:HALO + tile, :] = _rms_rows(x_ref[...], g1)
    hs_ref[HALO + tile:, :] = _rms_rows(xnext_ref[...], g1)

    p_ref[...] = _dot(hs_ref[...].astype(BF16), w_ref[...])
    for stage in _FINISH_STAGES:
        stage(p_ref, su_ref, cw_ref, cng_ref, gmat_ref, nea_ref, dtb_ref,
              ya_ref, q_ref, k_ref, v_ref, z_ref, gb_ref, tile, not_start, not_end)


def _conv_cols(p_ref, su_ref, cw_ref, lo, width, tile, not_start, not_end):
    cw = CONV_WIDTH
    cols = slice(lo, lo + width)
    if lo < cw:
        su_ref[:, cols] = p_ref[:, cw + lo:cw + lo + width] * p_ref[:, 2 * cw + lo:2 * cw + lo + width]
    else:
        su_ref[:, cols] = p_ref[:, 2 * cw + lo:2 * cw + lo + width]
    su_ref[0:HALO, cols] = su_ref[0:HALO, cols] * not_start
    su_ref[HALO + tile:, cols] = su_ref[HALO + tile:, cols] * not_end
    u = su_ref[:, cols]
    total = tile + 2 * HALO
    return (pltpu.roll(u, 1, axis=0)[HALO:HALO + tile] * cw_ref[0:1, cols]
            + u[HALO:HALO + tile] * cw_ref[1:2, cols]
            + pltpu.roll(u, total - 1, axis=0)[HALO:HALO + tile] * cw_ref[2:3, cols])


def _stage_mixer_a(half):
    width = CONV_WIDTH // 2
    lo = half * width

    def stage(p_ref, su_ref, cw_ref, cng_ref, gmat_ref, nea_ref, dtb_ref,
              ya_ref, q_ref, k_ref, v_ref, z_ref, gb_ref, tile, not_start, not_end):
        conv = _conv_cols(p_ref, su_ref, cw_ref, lo, width, tile, not_start, not_end)
        ya = p_ref[HALO:HALO + tile, lo:lo + width] * conv
        ms = _dot((ya * ya).astype(BF16), gmat_ref[lo:lo + width, lo:lo + width])
        ya_ref[:, lo:lo + width] = (ya * lax.rsqrt(ms + EPS) * cng_ref[:, lo:lo + width]).astype(ya_ref.dtype)
    return stage


def _stage_mixer_b(which, half):
    width = DN_V // 2
    lo_out = half * width
    lo = CONV_WIDTH + which * DN_V + lo_out

    def stage(p_ref, su_ref, cw_ref, cng_ref, gmat_ref, nea_ref, dtb_ref,
              ya_ref, q_ref, k_ref, v_ref, z_ref, gb_ref, tile, not_start, not_end):
        x = _conv_cols(p_ref, su_ref, cw_ref, lo, width, tile, not_start, not_end)
        x = x * _sigmoid(x)
        out_ref = (q_ref, k_ref, v_ref)[which]
        if which == 2:
            out_ref[:, lo_out:lo_out + width] = x.astype(out_ref.dtype)
            return
        scale = DN_DK ** -0.5 if which == 0 else 1.0
        for h in range(width // DN_DK):
            xh = x[:, h * DN_DK:(h + 1) * DN_DK]
            xh = xh * (lax.rsqrt(jnp.sum(xh * xh, axis=-1, keepdims=True) + EPS) * scale)
            out_ref[:, lo_out + h * DN_DK:lo_out + (h + 1) * DN_DK] = xh.astype(out_ref.dtype)
    return stage


def _stage_gates(inner):
    def stage(p_ref, su_ref, cw_ref, cng_ref, gmat_ref, nea_ref, dtb_ref,
              ya_ref, q_ref, k_ref, v_ref, z_ref, gb_ref, tile, not_start, not_end):
        inner(p_ref, su_ref, cw_ref, cng_ref, gmat_ref, nea_ref, dtb_ref,
              ya_ref, q_ref, k_ref, v_ref, z_ref, gb_ref, tile, not_start, not_end)
        cw = CONV_WIDTH
        z_ref[...] = p_ref[HALO:HALO + tile, 6 * cw:7 * cw].astype(z_ref.dtype)
        ab = p_ref[HALO:HALO + tile, 7 * cw:7 * cw + LANES]
        xs = ab + dtb_ref[...]
        softplus = jnp.maximum(xs, 0.0) + jnp.log(1.0 + jnp.exp(-jnp.abs(xs)))
        g = nea_ref[...] * softplus
        beta = _sigmoid(ab)
        r = lax.broadcasted_iota(jnp.int32, (tile, LANES), 0) % CHUNK
        pre = g
        suf = g
        s = 1
        while s < CHUNK:
            pre = pre + jnp.where(r >= s, pltpu.roll(pre, s, axis=0), 0.0)
            suf = suf + jnp.where(r < CHUNK - s, pltpu.roll(suf, tile - s, axis=0), 0.0)
            s *= 2
        lane = lax.broadcasted_iota(jnp.int32, (tile, LANES), 1) // DN_HEADS
        gb_ref[...] = jnp.where(lane == 0, pre,
                      jnp.where(lane == 1, suf,
                      jnp.where(lane < 4, beta,
                      jnp.where(lane == 4, suf - g,
                      jnp.where(lane == 5, pre - g, pre + suf - g)))))
    return stage


_FINISH_STAGES = (_stage_mixer_a(0), _stage_mixer_a(1), _stage_mixer_b(0, 0), _stage_mixer_b(0, 1),
                  _stage_mixer_b(1, 0), _stage_mixer_b(1, 1), _stage_mixer_b(2, 0),
                  _stage_gates(_stage_mixer_b(2, 1)))


def _proj_call(x, seq_len, wts):
    n, d = x.shape
    tile = TOKEN_TILE
    nt = n // tile
    tb = tile // HALO
    kern = functools.partial(_proj_kernel, tile=tile, seq_len=seq_len)
    const = lambda i: (0, 0)
    in_specs = [pl.BlockSpec((tile, d), lambda i: (i, 0)),
                pl.BlockSpec((HALO, d), lambda i: (jnp.maximum(i * tb - 1, 0), 0)),
                pl.BlockSpec((HALO, d), lambda i: (jnp.minimum((i + 1) * tb, n // HALO - 1), 0))]
    in_specs += [pl.BlockSpec(w.shape, const, pipeline_mode=pl.Buffered(1)) for w in wts]
    widths = (512, 512, 512, 512, 512, LANES)
    dtypes = (BF16, BF16, BF16, BF16, BF16, F32)
    rows = tile + 2 * HALO
    return pl.pallas_call(
        kern, grid=(nt,), in_specs=in_specs,
        out_specs=[pl.BlockSpec((tile, w), lambda i: (i, 0)) for w in widths],
        out_shape=[jax.ShapeDtypeStruct((n, w), dt) for w, dt in zip(widths, dtypes)],
        scratch_shapes=[pltpu.VMEM((rows, d), F32), pltpu.VMEM((rows, 4 * CONV_WIDTH), F32),
                        pltpu.VMEM((rows, wts[1].shape[1]), F32)],
        compiler_params=_cparams(("arbitrary",)), name="proj_conv")(x, x, x, *wts)


def _dn_kernel(*refs, block, sub, reverse, seq_len, nblk):
    if reverse:
        (q_ref, k_ref, v_ref, gb_ref, of_ref, z_ref, ng_ref, o_ref, s_ref) = refs
    else:
        (q_ref, k_ref, v_ref, gb_ref, o_ref, s_ref) = refs
    i = pl.program_id(0)
    b = nblk - 1 - i if reverse else i
    edge = (b + 1) * block if reverse else b * block

    @pl.when(edge % seq_len == 0)
    def _():
        s_ref[...] = jnp.zeros_like(s_ref)

    nchunk = sub // CHUNK
    c = CHUNK
    heads = range(DN_HEADS)
    ri = lax.broadcasted_iota(jnp.int32, (sub, sub), 0)
    ci = lax.broadcasted_iota(jnp.int32, (sub, sub), 1)
    same = (ri // c) == (ci // c)
    incl = same & ((ri <= ci) if reverse else (ri >= ci))
    strict = same & ((ri < ci) if reverse else (ri > ci))
    near = [(ri // bs) == (ci // bs) for bs in (INV_BASE << l for l in range(8)) if bs <= c]
    pr = lax.broadcasted_iota(jnp.int32, (c, sub), 0)
    pc = lax.broadcasted_iota(jnp.int32, (c, sub), 1)
    eye_p = (pr == pc % c).astype(F32)
    lane_chunk = pc // c

    def fold(bd):
        out = bd[0:c]
        for r in range(1, nchunk):
            out = out + bd[r * c:(r + 1) * c]
        return out

    def unfold(p):
        zero = jnp.zeros_like(p)
        return jnp.concatenate([jnp.where(lane_chunk == r, p, zero) for r in range(nchunk)], axis=0)

    grp = lambda g, h: slice(g * DN_HEADS + h, g * DN_HEADS + h + 1)
    g_cum, g_beta, g_rest, g_tot = (1, 3, 5, 7) if reverse else (0, 2, 4, 6)
    sls = [slice(h * DN_DK, (h + 1) * DN_DK) for h in heads]

    def prepare(r0, out):
        rows = slice(r0, r0 + sub)
        gbv = gb_ref[rows, :]
        gbt = gbv.T
        kb = [k_ref[rows, sl] for sl in sls]
        qb = [q_ref[rows, sl] for sl in sls]
        gcol = [gbv[:, grp(g_cum, h)] for h in heads]
        beta = [gbv[:, grp(g_beta, h)] for h in heads]
        kq = [lax.dot_general(jnp.concatenate([kb[h], qb[h]], axis=0), kb[h],
                              (((1,), (1,)), ((), ())), preferred_element_type=F32) for h in heads]
        yield
        decay = [jnp.exp(jnp.minimum(gcol[h] - gbt[grp(g_cum, h), :], 0.0)) for h in heads]
        bdl = [jnp.where(strict, kq[h][:sub] * decay[h], 0.0) * beta[h] for h in heads]
        bda = [jnp.where(incl, kq[h][sub:] * decay[h], 0.0).astype(BF16) for h in heads]
        yield
        d0 = [jnp.where(near[0], bdl[h], 0.0) for h in heads]
        lp = [fold(d0[h]) for h in heads]
        xp = [eye_p - lp[h] for h in heads]
        bd = [d0[h].astype(BF16) for h in heads]
        p = 1
        while p < INV_BASE:
            if p == 1:
                lp = [_dot(lp[h].astype(BF16), bd[h]) for h in heads]
            elif 2 * p < INV_BASE:
                res = [_dot(jnp.concatenate([lp[h], xp[h]], axis=0).astype(BF16), bd[h]) for h in heads]
                lp = [r[:c] for r in res]
                xp = [xp[h] + res[h][c:] for h in heads]
            else:
                xp = [xp[h] + _dot(xp[h].astype(BF16), bd[h]) for h in heads]
            p *= 2
            if p < INV_BASE:
                bd = [unfold(lp[h].astype(BF16)) for h in heads]
            yield
        for lvl in range(1, len(near)):
            e = [jnp.where(near[lvl] & ~near[lvl - 1], bdl[h], 0.0).astype(BF16) for h in heads]
            xe = [_dot(xp[h].astype(BF16), e[h]) for h in heads]
            xp = [xp[h] - _dot(xe[h].astype(BF16), unfold(xp[h].astype(BF16))) for h in heads]
            yield
        bdt = [unfold(xp[h].astype(BF16)) for h in heads]
        egc = [jnp.exp(gcol[h]) for h in heads]
        kf = [k_ref[rows, sls[h]].astype(F32) for h in heads]
        rhs = [jnp.concatenate([kf[h] * (beta[h] * egc[h]), v_ref[rows, sls[h]].astype(F32) * beta[h]],
                               axis=1).astype(BF16) for h in heads]
        wu = [_dot(bdt[h], rhs[h]) for h in heads]
        yield
        au = [_dot(bda[h], wu[h].astype(BF16)) for h in heads]
        qt = [q_ref[rows, sls[h]].astype(F32) * egc[h] - au[h][:, :DN_DK] for h in heads]
        out["kd"] = [(kf[h] * jnp.exp(gbv[:, grp(g_rest, h)])).astype(BF16) for h in heads]
        out["wqt"] = [jnp.concatenate([wu[h][:, :DN_DK], qt[h]], axis=1).astype(BF16) for h in heads]
        out["u"] = [wu[h][:, DN_DK:] for h in heads]
        out["o0"] = [au[h][:, DN_DK:] for h in heads]
        out["gbt"] = gbt
        yield

    def scan(r0, pre, state):
        order = range(nchunk - 1, -1, -1) if reverse else range(nchunk)
        for cc in order:
            rows = slice(cc * c, (cc + 1) * c)
            out_rows = slice(r0 + cc * c, r0 + (cc + 1) * c)
            for h in heads:
                wqt = pre["wqt"][h]
                lhs = jnp.concatenate([wqt[rows, :DN_DK], wqt[rows, DN_DK:]], axis=0)
                wq = _dot(lhs, state[h].astype(BF16))
                vn = pre["u"][h][rows] - wq[:c]
                o = wq[c:] + pre["o0"][h][rows]
                tot = pre["gbt"][grp(g_tot, h), cc * c:cc * c + 1]
                state[h] = state[h] * jnp.exp(tot) + lax.dot_general(
                    pre["kd"][h][rows], vn.astype(BF16), (((0,), (0,)), ((), ())),
                    preferred_element_type=F32)
                if reverse:
                    o = o + of_ref[out_rows, sls[h]]
                    o = o * lax.rsqrt(jnp.mean(o * o, axis=-1, keepdims=True) + EPS) * ng_ref[...]
                    zc = z_ref[out_rows, sls[h]].astype(F32)
                    o = o * (zc * _sigmoid(zc))
                o_ref[out_rows, sls[h]] = o.astype(o_ref.dtype)
            yield

    nsub = block // sub
    starts = [r * sub for r in (range(nsub - 1, -1, -1) if reverse else range(nsub))]
    state = [s_ref[h] for h in heads]
    pre_prev = None
    for idx in range(nsub + 1):
        pre = {}
        prep_gen = prepare(starts[idx], pre) if idx < nsub else iter(())
        scan_gen = scan(starts[idx - 1], pre_prev, state) if idx > 0 else iter(())
        prep_live = scan_live = True
        while prep_live or scan_live:
            if prep_live:
                prep_live = next(prep_gen, _DONE) is not _DONE
                if prep_live:
                    prep_live = next(prep_gen, _DONE) is not _DONE
            if scan_live:
                scan_live = next(scan_gen, _DONE) is not _DONE
        pre_prev = pre
    for h in heads:
        s_ref[h] = state[h]


_DONE = object()


def _dn_call(q, k, v, gb, extra, reverse, seq_len):
    n = q.shape[0]
    block = DN_BLOCK
    nblk = n // block
    imap = (lambda i: (nblk - 1 - i, 0)) if reverse else (lambda i: (i, 0))
    tok = lambda w: pl.BlockSpec((block, w), imap)
    in_specs = [tok(512), tok(512), tok(512), tok(LANES)]
    args = [q, k, v, gb]
    if reverse:
        o_f, z, ng = extra
        in_specs += [tok(512), tok(512), pl.BlockSpec(ng.shape, lambda i: (0, 0))]
        args += [o_f, z, ng]
    kern = functools.partial(_dn_kernel, block=block, sub=DN_SUB, reverse=reverse,
                             seq_len=seq_len, nblk=nblk)
    return pl.pallas_call(
        kern, grid=(nblk,), in_specs=in_specs, out_specs=tok(512),
        out_shape=jax.ShapeDtypeStruct((n, 512), BF16 if reverse else F32),
        scratch_shapes=[pltpu.VMEM((DN_HEADS, DN_DK, DN_DK), F32)],
        compiler_params=_cparams(("arbitrary",)),
        name="deltanet_bwd" if reverse else "deltanet_fwd")(*args)


def _route_kernel(x_ref, ya_ref, yb_ref, wo_ref, g2_ref, rw_ref, rb_ref, tri_ref,
                  x2_ref, xa_ref, xb_ref, info_ref, infot_ref, cnt_ref, run_ref, *, tile):
    i = pl.program_id(0)

    @pl.when(i == 0)
    def _():
        run_ref[...] = jnp.zeros_like(run_ref)

    sub = tri_ref.shape[0]
    parts = [slice(r, r + sub) for r in range(0, tile, sub)]
    each = lambda f, *lists: [f(*a) for a in zip(*lists)]
    half = wo_ref.shape[0] // 2
    x2 = [x_ref[rows, :] + _dot(ya_ref[rows, :], wo_ref[0:half, :])
          + _dot(yb_ref[rows, :], wo_ref[half:, :]) for rows in parts]
    for rows, v in zip(parts, x2):
        x2_ref[rows, :] = v.astype(x2_ref.dtype)
    xn = [_rms_rows(v, g2_ref[...]) for v in x2]
    for rows, v in zip(parts, xn):
        words = _pack_pairs(v)
        xa_ref[rows, :] = words[:, :ROW_WORDS]
        xb_ref[rows, :] = words[:, ROW_WORDS:]
    logits = [_dot(v.astype(BF16), rw_ref[...]) + rb_ref[...] for v in xn]
    lane = lax.broadcasted_iota(jnp.int32, (sub, LANES), 1).astype(F32)
    neg = jnp.float32(-jnp.inf)
    row_max = lambda v: jnp.max(v, axis=-1, keepdims=True)
    first_at = lambda v, m: jnp.min(jnp.where(v == m, lane, float(LANES)), axis=-1, keepdims=True)

    gl = [jnp.where(lane < N_GROUPS, v, neg) for v in logits]
    gmax = each(row_max, gl)
    gidx = each(first_at, gl, gmax)
    g_w = [1.0 / jnp.sum(jnp.exp(v - m), axis=-1, keepdims=True) for v, m in zip(gl, gmax)]
    lo = [N_GROUPS + g * EXPERTS_PER_GROUP for g in gidx]
    el = [jnp.where((lane >= a) & (lane < a + EXPERTS_PER_GROUP), v, neg) for v, a in zip(logits, lo)]
    m1 = each(row_max, el)
    i1 = each(first_at, el, m1)
    el2 = [jnp.where(lane == a, neg, v) for v, a in zip(el, i1)]
    m2 = each(row_max, el2)
    i2 = each(first_at, el2, m2)
    hit1 = [lane == a - N_GROUPS for a in i1]
    hit2 = [lane == a - N_GROUPS for a in i2]
    onehot = [(a | b).astype(BF16) for a, b in zip(hit1, hit2)]
    inside = [_dot(tri_ref[...], v) for v in onehot]
    run = run_ref[...]
    for k, rows in enumerate(parts):
        before = inside[k] + run
        pos1 = jnp.sum(jnp.where(hit1[k], before, 0.0), axis=-1, keepdims=True)
        pos2 = jnp.sum(jnp.where(hit2[k], before, 0.0), axis=-1, keepdims=True)
        run = run + jnp.sum(onehot[k].astype(F32), axis=0, keepdims=True)
        r = jnp.exp(m2[k] - m1[k])
        gate1 = g_w[k] / (1.0 + r)
        info = jnp.where(lane == 0, i1[k] - N_GROUPS,
               jnp.where(lane == 1, i2[k] - N_GROUPS,
               jnp.where(lane == 2, gate1,
               jnp.where(lane == 3, gate1 * r,
               jnp.where(lane == 4, pos1,
               jnp.where(lane == 5, pos2, 0.0))))))
        info_ref[rows, :] = info
        infot_ref[:, rows] = info.T[0:HALO, :]
    run_ref[...] = run
    cnt_ref[...] = run


def _route_call(x, ya, yb, wts):
    n, d = x.shape
    tile = ROUTE_TILE
    const = lambda i: (0, 0)
    tok = lambda w: pl.BlockSpec((tile, w), lambda i: (i, 0))
    in_specs = [tok(d), tok(512), tok(512)] + [pl.BlockSpec(w.shape, const) for w in wts]
    out_shape = [jax.ShapeDtypeStruct((n, d), BF16), jax.ShapeDtypeStruct((n, ROW_WORDS), U32),
                 jax.ShapeDtypeStruct((n, ROW_WORDS), U32),
                 jax.ShapeDtypeStruct((n, LANES), F32), jax.ShapeDtypeStruct((HALO, n), F32),
                 jax.ShapeDtypeStruct((1, LANES), F32)]
    out_specs = [tok(d), tok(ROW_WORDS), tok(ROW_WORDS), tok(LANES),
                 pl.BlockSpec((HALO, tile), lambda i: (0, i)), pl.BlockSpec((1, LANES), const)]
    kern = functools.partial(_route_kernel, tile=tile)
    return pl.pallas_call(
        kern, grid=(n // tile,), in_specs=in_specs, out_specs=out_specs, out_shape=out_shape,
        scratch_shapes=[pltpu.VMEM((1, LANES), F32)],
        compiler_params=_cparams(("arbitrary",)), name="oproj_router")(x, ya, yb, *wts)


def _sc_mesh():
    return plsc.VectorSubcoreMesh(core_axis_name="core", subcore_axis_name="subcore")


def _sc_scatter_rows(x, idx0, idx1, rows):
    n, w = x.shape

    @pl.kernel(out_type=jax.ShapeDtypeStruct((rows, w), x.dtype), mesh=_sc_mesh(), scratch_types=[])
    def scatter(x_hbm, i0_hbm, i1_hbm, o_hbm):
        def body(x_vmem, i0_vmem, i1_vmem):
            pltpu.sync_copy(x_vmem, o_hbm.at[i0_vmem.at[0]])
            pltpu.sync_copy(x_vmem, o_hbm.at[i1_vmem.at[0]])

        pltpu.emit_pipeline(
            body, grid=(n // SC_WINDOW,),
            in_specs=[pl.BlockSpec((SC_WINDOW, w), lambda i: (i, 0)),
                      pl.BlockSpec((1, SC_WINDOW), lambda i: (0, i)),
                      pl.BlockSpec((1, SC_WINDOW), lambda i: (0, i))],
            out_specs=[], core_axis_name=("core", "subcore"),
            dimension_semantics=(pltpu.PARALLEL,))(x_hbm, i0_hbm, i1_hbm)

    return scatter(x, idx0, idx1)


def _sc_gather_rows(y, idx):
    m = idx.shape[1]
    w = y.shape[1]

    @pl.kernel(out_type=jax.ShapeDtypeStruct((m, w), y.dtype), mesh=_sc_mesh(), scratch_types=[])
    def gather(y_hbm, i_hbm, o_hbm):
        def body(i_vmem, o_vmem):
            pltpu.sync_copy(y_hbm.at[i_vmem.at[0]], o_vmem)

        pltpu.emit_pipeline(
            body, grid=(m // SC_WINDOW,),
            in_specs=[pl.BlockSpec((1, SC_WINDOW), lambda i: (0, i))],
            out_specs=[pl.BlockSpec((SC_WINDOW, w), lambda i: (i, 0))],
            core_axis_name=("core", "subcore"),
            dimension_semantics=(pltpu.PARALLEL,))(i_hbm, o_hbm)

    return gather(y, idx)


def _join_rows(wa, wb):
    lo_a, hi_a = _unpack_pairs(wa)
    lo_b, hi_b = _unpack_pairs(wb)
    return jnp.concatenate([lo_a, lo_b, hi_a, hi_b], axis=1)


def _expert_kernel(te_ref, nx_ref, nv_ref, nu_ref, xa_ref, xb_ref, w1_hbm, w3_hbm, w2_hbm,
                   ya_ref, yb_ref, stage1, stage3, stage2, w1_ref, w3_ref, w2_ref, sem):
    j = pl.program_id(0)

    def fetch(e):
        return [pltpu.make_async_copy(src.at[e], dst, sem.at[k]) for k, (src, dst) in
                enumerate(((w1_hbm, stage1), (w3_hbm, stage3), (w2_hbm, stage2)))]

    @pl.when(j == 0)
    def _():
        for cp in fetch(te_ref[0]):
            cp.start()

    @pl.when((j < nu_ref[0]) & (nx_ref[j] > -2))
    def _():
        for cp in fetch(te_ref[j]):
            cp.wait()
        w1_ref[...] = stage1[...].astype(BF16)
        w3_ref[...] = stage3[...].astype(BF16)
        w2_ref[...] = stage2[...].astype(BF16)

        @pl.when(nx_ref[j] >= 0)
        def _():
            for cp in fetch(nx_ref[j]):
                cp.start()

    @pl.when(j < nu_ref[0])
    def _():
        live = lax.broadcasted_iota(jnp.int32, xa_ref.shape, 0) < nv_ref[j]
        zero = jnp.zeros(xa_ref.shape, U32)
        x = _join_rows(jnp.where(live, xa_ref[...], zero),
                       jnp.where(live, xb_ref[...], zero)).astype(BF16)
        h1 = _dot(x, w1_ref[...])
        h3 = _dot(x, w3_ref[...])
        hdn = (h1 * _sigmoid(h1) * h3).astype(BF16)
        words = _pack_pairs(_dot(hdn, w2_ref[...]))
        ya_ref[...] = words[:, :ROW_WORDS]
        yb_ref[...] = words[:, ROW_WORDS:]


def _expert_call(tile_expert, tile_next, tile_valid, n_used, xa, xb, w1, w3, w2):
    rows, rw = xa.shape
    tm = EXPERT_TILE
    d, ff = w1.shape[1], w1.shape[2]
    row_blk = lambda j, te, nx, nv, nu: (jnp.minimum(j, nu[0] - 1), 0)
    hbm = pl.BlockSpec(memory_space=pl.ANY)
    grid_spec = pltpu.PrefetchScalarGridSpec(
        num_scalar_prefetch=4, grid=(rows // tm,),
        in_specs=[pl.BlockSpec((tm, rw), row_blk), pl.BlockSpec((tm, rw), row_blk), hbm, hbm, hbm],
        out_specs=[pl.BlockSpec((tm, rw), row_blk), pl.BlockSpec((tm, rw), row_blk)],
        scratch_shapes=[pltpu.VMEM((d, ff), w1.dtype), pltpu.VMEM((d, ff), w3.dtype),
                        pltpu.VMEM((ff, d), w2.dtype),
                        pltpu.VMEM((d, ff), BF16), pltpu.VMEM((d, ff), BF16), pltpu.VMEM((ff, d), BF16),
                        pltpu.SemaphoreType.DMA((3,))])
    return pl.pallas_call(
        _expert_kernel, grid_spec=grid_spec,
        out_shape=[jax.ShapeDtypeStruct((rows, rw), U32), jax.ShapeDtypeStruct((rows, rw), U32)],
        compiler_params=_cparams(("arbitrary",)), name="moe_experts")(
            tile_expert, tile_next, tile_valid, n_used, xa, xb, w1, w3, w2)


def _combine_kernel(x2_ref, info_ref, gf_ref, a0_ref, b0_ref, a1_ref, b1_ref, y_ref):
    info = info_ref[...]
    moe = (info[:, 2:3] * _join_rows(a0_ref[...], b0_ref[...])
           + info[:, 3:4] * _join_rows(a1_ref[...], b1_ref[...]))
    y_ref[...] = _rms_rows(x2_ref[...].astype(F32) + moe, gf_ref[...])


def _combine_call(x2, info, gf, ga, gb):
    n, d = x2.shape
    tile = COMBINE_TILE
    nt = n // tile
    first = pl.BlockSpec((tile, ROW_WORDS), lambda i: (i, 0))
    second = pl.BlockSpec((tile, ROW_WORDS), lambda i: (i + nt, 0))
    return pl.pallas_call(
        _combine_kernel, grid=(nt,),
        in_specs=[pl.BlockSpec((tile, d), lambda i: (i, 0)),
                  pl.BlockSpec((tile, LANES), lambda i: (i, 0)),
                  pl.BlockSpec(gf.shape, lambda i: (0, 0)),
                  first, first, second, second],
        out_specs=pl.BlockSpec((tile, d), lambda i: (i, 0)),
        out_shape=jax.ShapeDtypeStruct((n, d), F32),
        compiler_params=_cparams(("arbitrary",)), name="moe_combine")(
            x2, info, gf, ga, gb, ga, gb)


def _row(a):
    return a.reshape(1, -1).astype(F32)


def _prep_weights(norm1_g, w_in, conv_a_w, conv_a_norm_g, dn_conv_w, dn_a_log, dn_dt_bias, w_o,
                  norm2_g, rg_w, rg_b, re_w, re_b):
    row = _row
    n_ab = len(_GB_COLS)
    w_main = jnp.concatenate([w_in[:, :7 * CONV_WIDTH], w_in[:, 7 * CONV_WIDTH + _GB_COLS]], axis=1)
    w_main = jnp.pad(w_main, ((0, 0), (0, LANES - n_ab))).astype(BF16)
    cw = jnp.concatenate([conv_a_w, dn_conv_w], axis=1).astype(F32)
    grp = jnp.arange(CONV_WIDTH) // (CONV_WIDTH // CONV_GROUPS)
    gmat = ((grp[:, None] == grp[None, :]).astype(F32) / (CONV_WIDTH // CONV_GROUPS)).astype(BF16)
    is_a = (_GB_COLS < 2 * DN_HEADS)
    a_idx = jnp.where(is_a, _GB_COLS, 0)
    padl = lambda a: jnp.pad(a.reshape(1, -1).astype(F32), ((0, 0), (0, LANES - n_ab)))
    nea = padl(jnp.where(is_a, -jnp.exp(dn_a_log.astype(F32)).reshape(-1)[a_idx], 0.0))
    dtb = padl(jnp.where(is_a, dn_dt_bias.astype(F32).reshape(-1)[a_idx], 0.0))
    proj_w = [row(norm1_g), w_main, cw, row(conv_a_norm_g), gmat, nea, dtb]
    rw = jnp.pad(jnp.concatenate([rg_w, re_w], axis=1).astype(F32),
                 ((0, 0), (0, LANES - N_GROUPS - N_EXPERTS)))
    rb = jnp.pad(jnp.concatenate([rg_b, re_b]).reshape(1, -1).astype(F32),
                 ((0, 0), (0, LANES - N_GROUPS - N_EXPERTS)))
    tri = (jnp.arange(ROUTE_SUB)[:, None] > jnp.arange(ROUTE_SUB)[None, :]).astype(BF16)
    route_w = [w_o.astype(BF16), row(norm2_g), rw.astype(BF16), rb, tri]
    return proj_w, route_w


def _mix_and_route(x, seq_len, proj_w, dn_norm_g, route_w):
    ya, q, k, v, z, gb = _proj_call(x, seq_len, proj_w)
    o_f = _dn_call(q, k, v, gb, None, False, seq_len)
    yb = _dn_call(q, k, v, gb, (o_f, z, _row(dn_norm_g)), True, seq_len)
    return _route_call(x, ya, yb, route_w)


def _moe(routed, w1, w3, w2, final_g):
    x2, xna, xnb, info, info_t, cnt = routed
    n = x2.shape[0]
    tm = EXPERT_TILE
    counts = cnt[0, :N_EXPERTS].astype(jnp.int32)
    pcounts = (counts + tm - 1) // tm * tm
    pends = jnp.cumsum(pcounts)
    pstarts = pends - pcounts
    eid = jnp.arange(N_EXPERTS, dtype=jnp.int32)
    e12 = info_t[0:2].astype(jnp.int32)
    start12 = jnp.sum(jnp.where(e12[:, None, :] == eid[None, :, None], pstarts[None, :, None], 0),
                      axis=1)
    dest = start12 + info_t[4:6].astype(jnp.int32)
    dest0 = dest[0:1]
    dest1 = dest[1:2]
    n_tiles = 2 * n // tm + N_EXPERTS
    tile_start = jnp.arange(n_tiles, dtype=jnp.int32) * tm
    tile_expert = jnp.minimum(jnp.sum(pends[None, :] <= tile_start[:, None], axis=1),
                              N_EXPERTS - 1).astype(jnp.int32)
    of_tile = lambda table: jnp.sum(jnp.where(tile_expert[:, None] == eid, table, 0), axis=-1)
    tile_valid = jnp.clip(of_tile(pstarts + counts) - tile_start, 0, tm).astype(jnp.int32)
    n_used = (pends[-1:] // tm).astype(jnp.int32)
    later = (eid[None, :] > eid[:, None]) & (pcounts[None, :] > 0)
    next_e = jnp.min(jnp.where(later, eid[None, :], N_EXPERTS), axis=1)
    next_e = jnp.where(next_e == N_EXPERTS, -1, next_e)
    tile_next = jnp.where(tile_start == of_tile(pstarts), of_tile(next_e), -2).astype(jnp.int32)

    rows = n_tiles * tm
    xa = _sc_scatter_rows(xna, dest0, dest1, rows)
    xb = _sc_scatter_rows(xnb, dest0, dest1, rows)
    ya_e, yb_e = _expert_call(tile_expert, tile_next, tile_valid, n_used, xa, xb, w1, w3, w2)
    both = jnp.concatenate([dest0, dest1], axis=1)
    return _combine_call(x2, info, _row(final_g), _sc_gather_rows(ya_e, both),
                         _sc_gather_rows(yb_e, both))


def kernel(x_prompt, x_sample, norm1_g, w_in, conv_a_w, conv_a_norm_g, dn_conv_w, dn_a_log,
           dn_dt_bias, dn_norm_g, w_o, norm2_g, router_group_w, router_group_b, router_expert_w,
           router_expert_b, w1, w3, w2, final_norm_g):
    assert norm1_g.shape[0] == 1, "single-layer trunk"
    bp, sp, d = x_prompt.shape
    bs, ss, _ = x_sample.shape
    step = max(TOKEN_TILE, ROUTE_TILE, COMBINE_TILE, DN_BLOCK)
    assert sp % step == 0 and ss % step == 0, "sequence lengths must be whole grid steps"
    proj_w, route_w = _prep_weights(
        norm1_g[0], w_in[0], conv_a_w[0], conv_a_norm_g[0], dn_conv_w[0], dn_a_log[0],
        dn_dt_bias[0], w_o[0], norm2_g[0], router_group_w[0], router_group_b[0],
        router_expert_w[0], router_expert_b[0])
    routed = [_mix_and_route(x.reshape(-1, d), seq_len, proj_w, dn_norm_g[0], route_w)
              for x, seq_len in ((x_prompt, sp), (x_sample, ss))]
    y_p, y_s = [_moe(r, w1[0], w3[0], w2[0], final_norm_g) for r in routed]
    return y_p.reshape(bp, sp, d), y_s.reshape(bs, ss, d)
```
